```python
import jax, jax.numpy as jnp
from jax import lax
import numpy as np

D_MODEL = 2048
BATCH = 4
SEQ = 2048
DEPTH = 2

N_Q_HEADS = 16
N_KV_HEADS = 4
HEAD_DIM = 64
Q_PER_KV = N_Q_HEADS // N_KV_HEADS
NSA_WIDTH = N_Q_HEADS * HEAD_DIM
KV_WIDTH = N_KV_HEADS * HEAD_DIM
CMP_BLOCK = 32
CMP_STRIDE = 16
CMP_HIDDEN = 256
SLC_BLOCK = 64
N_SELECT = 16
SLC_QUERY_BLOCK = 64
WINDOW = 512
WIN_QUERY_BLOCK = 128
FORCED_SCORE = 1e4
SGU_WIDTH = D_MODEL // 2
SGU_GROUPS = 8
SGU_GROUP_DIM = SGU_WIDTH // SGU_GROUPS
SGU_CHUNK = 128
D_FF_DENSE = 256 * ((8 * D_MODEL // 3 + 255) // 256)
N_EXPERTS = 8
TOP_K = 2
D_FF_EXPERT = 7 * D_MODEL // 2
NORM_EPS = 1e-5

IN_SIZES = (NSA_WIDTH, KV_WIDTH, KV_WIDTH, KV_WIDTH, KV_WIDTH, KV_WIDTH, KV_WIDTH,
            3 * N_Q_HEADS, SGU_WIDTH, SGU_WIDTH, D_MODEL, D_MODEL)
IN_WIDTH = sum(IN_SIZES)

kernel_name = "hybrid_nsa_gmlp_moe_trunk"


def rms_norm(x, g):
    xf = x.astype(jnp.float32)
    y = xf * lax.rsqrt(jnp.mean(xf * xf, axis=-1, keepdims=True) + NORM_EPS)
    return (y * g.astype(jnp.float32)).astype(x.dtype)


def layer_norm(x, g, b):
    xf = x.astype(jnp.float32)
    mu = jnp.mean(xf, axis=-1, keepdims=True)
    var = jnp.mean(jnp.square(xf - mu), axis=-1, keepdims=True)
    y = (xf - mu) * lax.rsqrt(var + NORM_EPS)
    return (y * g.astype(jnp.float32) + b.astype(jnp.float32)).astype(x.dtype)


def masked_softmax(s, mask):
    s = jnp.where(mask, s.astype(jnp.float32), -jnp.inf)
    m = jnp.max(s, axis=-1, keepdims=True)
    m = jnp.where(jnp.isfinite(m), m, 0.0)
    e = jnp.where(mask, jnp.exp(s - m), 0.0)
    return e / jnp.maximum(jnp.sum(e, axis=-1, keepdims=True), 1e-30)


def alibi_slopes():
    sl = np.array([2.0 ** (-8.0 * (h + 1) / N_Q_HEADS) for h in range(N_Q_HEADS)], dtype=np.float32)
    return jnp.asarray(sl.reshape(N_KV_HEADS, Q_PER_KV))


def split_in(z):
    points = np.cumsum(np.array(IN_SIZES))[:-1].tolist()
    return jnp.split(z, points, axis=-1)


def cmp_to_slc_overlap(n_cmp, n_slc):
    cs = np.arange(n_cmp)[:, None] * CMP_STRIDE
    ss = np.arange(n_slc)[None, :] * SLC_BLOCK
    ov = np.clip(np.minimum(cs + CMP_BLOCK, ss + SLC_BLOCK) - np.maximum(cs, ss), 0, None)
    return jnp.asarray((ov / CMP_STRIDE).astype(np.float32))


def compress_blocks(kv, pe, w1, w2):
    B, T, G, dh = kv.shape
    r = CMP_BLOCK // CMP_STRIDE
    n_chunks = T // CMP_STRIDE
    n_cmp = n_chunks - r + 1
    chunks = kv.reshape(B, n_chunks, CMP_STRIDE, G, dh)
    blocks = jnp.concatenate([chunks[:, j:j + n_cmp] for j in range(r)], axis=2)
    blocks = blocks + pe[None, None, :, None, :]
    flat = blocks.transpose(0, 1, 3, 2, 4).reshape(B, n_cmp, G, CMP_BLOCK * dh)
    return jax.nn.gelu(flat @ w1) @ w2


def nsa_attention(q, k_cmp, v_cmp, k_slc, v_slc, k_win, v_win, gates, pe_k, pe_v, kw1, kw2, vw1, vw2):
    B, T, G, hg, dh = q.shape
    scale = dh ** -0.5
    slopes = alibi_slopes()
    t = jnp.arange(T)

    kc = compress_blocks(k_cmp, pe_k, kw1, kw2)
    vc = compress_blocks(v_cmp, pe_v, vw1, vw2)
    n_cmp = kc.shape[1]
    blk_end = jnp.arange(n_cmp) * CMP_STRIDE + (CMP_BLOCK - 1)
    dist_c = t[:, None] - blk_end[None, :]
    s_c = jnp.einsum('btgjd,bcgd->bgjtc', q, kc).astype(jnp.float32) * scale
    s_c = s_c - slopes[None, :, :, None, None] * dist_c.astype(jnp.float32)
    p_c = masked_softmax(s_c, dist_c >= 0)
    o_c = jnp.einsum('bgjtc,bcgd->btgjd', p_c.astype(vc.dtype), vc)

    n_slc = T // SLC_BLOCK
    overlap = cmp_to_slc_overlap(n_cmp, n_slc)
    imp = jnp.einsum('bgjtc,cn->bgtn', p_c, overlap)
    blk = jnp.arange(n_slc)
    cur = t // SLC_BLOCK
    valid = blk[None, :] * SLC_BLOCK <= t[:, None]
    forced = (blk[None, :] == 0) | (blk[None, :] == cur[:, None]) | (blk[None, :] == cur[:, None] - 1)
    score = jnp.where(valid, imp, -1.0)
    score = jnp.where(forced, FORCED_SCORE, score)
    n_top = min(N_SELECT, n_slc)
    _, idx = lax.top_k(score, n_top)

    k_blocks = k_slc.reshape(B, n_slc, SLC_BLOCK, G, dh).transpose(0, 3, 1, 2, 4)
    v_blocks = v_slc.reshape(B, n_slc, SLC_BLOCK, G, dh).transpose(0, 3, 1, 2, 4)
    QB = SLC_QUERY_BLOCK
    nq = T // QB
    q_b = q.reshape(B, nq, QB, G, hg, dh).transpose(1, 0, 3, 4, 2, 5)
    idx_b = idx.reshape(B, G, nq, QB, n_top).transpose(2, 0, 1, 3, 4)
    t_b = t.reshape(nq, QB)
    b_i = jnp.arange(B)[:, None, None, None]
    g_i = jnp.arange(G)[None, :, None, None]
    n_keys = n_top * SLC_BLOCK

    def slc_step(args):
        q_i, idx_i, t_i = args
        k_sel = k_blocks[b_i, g_i, idx_i].reshape(B, G, QB, n_keys, dh)
        v_sel = v_blocks[b_i, g_i, idx_i].reshape(B, G, QB, n_keys, dh)
        kpos = (idx_i[..., None] * SLC_BLOCK + jnp.arange(SLC_BLOCK)).reshape(B, G, QB, n_keys)
        dist = (t_i[None, None, :, None] - kpos)[:, :, None]
        s = jnp.einsum('bgjqd,bgqkd->bgjqk', q_i, k_sel).astype(jnp.float32) * scale
        s = s - slopes[None, :, :, None, None] * dist.astype(jnp.float32)
        p = masked_softmax(s, dist >= 0)
        return jnp.einsum('bgjqk,bgqkd->bgjqd', p.astype(v_sel.dtype), v_sel)

    o_s = lax.map(slc_step, (q_b, idx_b, t_b))
    o_s = o_s.transpose(1, 0, 4, 2, 3, 5).reshape(B, T, G, hg, dh)

    WQ = WIN_QUERY_BLOCK
    nw = T // WQ
    r = WINDOW // WQ
    n_band = (r + 1) * WQ
    k_pad = jnp.pad(k_win, ((0, 0), (WINDOW, 0), (0, 0), (0, 0))).reshape(B, nw + r, WQ, G, dh)
    v_pad = jnp.pad(v_win, ((0, 0), (WINDOW, 0), (0, 0), (0, 0))).reshape(B, nw + r, WQ, G, dh)
    k_band = jnp.concatenate([k_pad[:, j:j + nw] for j in range(r + 1)], axis=2)
    v_band = jnp.concatenate([v_pad[:, j:j + nw] for j in range(r + 1)], axis=2)
    q_w = q.reshape(B, nw, WQ, G, hg, dh)
    tq = t.reshape(nw, WQ)
    kpos_w = jnp.arange(nw)[:, None] * WQ - WINDOW + jnp.arange(n_band)[None, :]
    dist_w = tq[:, :, None] - kpos_w[:, None, :]
    mask_w = (dist_w >= 0) & (dist_w < WINDOW) & (kpos_w[:, None, :] >= 0)
    s_w = jnp.einsum('bnqgjd,bnkgd->bgjnqk', q_w, k_band).astype(jnp.float32) * scale
    s_w = s_w - slopes[None, :, :, None, None, None] * dist_w.astype(jnp.float32)
    p_w = masked_softmax(s_w, mask_w)
    o_w = jnp.einsum('bgjnqk,bnkgd->bnqgjd', p_w.astype(v_band.dtype), v_band).reshape(B, T, G, hg, dh)

    g = jax.nn.sigmoid(gates.astype(jnp.float32)).astype(q.dtype)
    o = g[..., 0:1] * o_c + g[..., 1:2] * o_s + g[..., 2:3] * o_w
    return o.reshape(B, T, G * hg * dh)


def spatial_gating(u, v, ln_g, ln_b, w_s, b_s):
    B, T, _ = u.shape
    u = jax.nn.gelu(u)
    v = layer_norm(jax.nn.gelu(v), ln_g, ln_b)
    vg = v.reshape(B, T // SGU_CHUNK, SGU_CHUNK, SGU_GROUPS, SGU_GROUP_DIM)
    w_causal = jnp.tril(w_s)
    vm = jnp.einsum('gts,bnsgd->bntgd', w_causal, vg) + b_s.T[None, None, :, :, None]
    return u * vm.reshape(B, T, SGU_WIDTH)


def hybrid_mixer(h, w_in, pe_k, pe_v, kw1, kw2, vw1, vw2, ln_g, ln_b, w_s, b_s, w_a, w_b, w_out):
    B, T, _ = h.shape
    z = h @ w_in
    (q, kc, vc, ks, vs, kw, vw, ng, u, v, ga, gb) = split_in(z)
    kv = lambda a: a.reshape(B, T, N_KV_HEADS, HEAD_DIM)
    o_a = nsa_attention(q.reshape(B, T, N_KV_HEADS, Q_PER_KV, HEAD_DIM),
                        kv(kc), kv(vc), kv(ks), kv(vs), kv(kw), kv(vw),
                        ng.reshape(B, T, N_KV_HEADS, Q_PER_KV, 3),
                        pe_k, pe_v, kw1, kw2, vw1, vw2)
    o_b = spatial_gating(u, v, ln_g, ln_b, w_s, b_s)
    gate_a = jax.nn.sigmoid(ga.astype(jnp.float32)).astype(h.dtype)
    gate_b = jax.nn.sigmoid(gb.astype(jnp.float32)).astype(h.dtype)
    y = gate_a * (o_a @ w_a) + gate_b * (o_b @ w_b)
    return y @ w_out


def swiglu(h, w1, w3, w2):
    return (jax.nn.silu(h @ w1) * (h @ w3)) @ w2


def moe_swiglu(h, router_w, router_b, w1, w3, w2):
    logits = (h @ router_w).astype(jnp.float32) + router_b.astype(jnp.float32)
    top_v, top_i = lax.top_k(logits, TOP_K)
    wts = jax.nn.softmax(top_v, axis=-1)
    out = jnp.zeros_like(h)
    for e in range(N_EXPERTS):
        gate_e = jnp.sum(jnp.where(top_i == e, wts, 0.0), axis=-1).astype(h.dtype)
        out = out + gate_e[..., None] * swiglu(h, w1[e], w3[e], w2[e])
    return out


def setup_inputs(seed: int = 0) -> dict:
    key = jax.random.key(seed)
    ks = iter(jax.random.split(key, 32))
    n_dense = (DEPTH + 1) // 2
    n_moe = DEPTH // 2
    f32 = jnp.float32

    def nrm(shape, scale):
        return jax.random.normal(next(ks), shape, f32) * scale

    def gain(shape):
        return 1.0 + 0.02 * jax.random.normal(next(ks), shape, f32)

    return {
        "x": jax.random.normal(next(ks), (BATCH, SEQ, D_MODEL), f32),
        "norm_mix": gain((DEPTH, D_MODEL)),
        "w_in": nrm((DEPTH, D_MODEL, IN_WIDTH), D_MODEL ** -0.5),
        "cmp_pe_k": nrm((DEPTH, CMP_BLOCK, HEAD_DIM), 0.02),
        "cmp_pe_v": nrm((DEPTH, CMP_BLOCK, HEAD_DIM), 0.02),
        "cmp_k_w1": nrm((DEPTH, CMP_BLOCK * HEAD_DIM, CMP_HIDDEN), (CMP_BLOCK * HEAD_DIM) ** -0.5),
        "cmp_k_w2": nrm((DEPTH, CMP_HIDDEN, HEAD_DIM), CMP_HIDDEN ** -0.5),
        "cmp_v_w1": nrm((DEPTH, CMP_BLOCK * HEAD_DIM, CMP_HIDDEN), (CMP_BLOCK * HEAD_DIM) ** -0.5),
        "cmp_v_w2": nrm((DEPTH, CMP_HIDDEN, HEAD_DIM), CMP_HIDDEN ** -0.5),
        "sgu_ln_g": gain((DEPTH, SGU_WIDTH)),
        "sgu_ln_b": nrm((DEPTH, SGU_WIDTH), 0.02),
        "sgu_w": nrm((DEPTH, SGU_GROUPS, SGU_CHUNK, SGU_CHUNK), SGU_CHUNK ** -0.5),
        "sgu_b": gain((DEPTH, SGU_GROUPS, SGU_CHUNK)),
        "w_branch_a": nrm((DEPTH, NSA_WIDTH, D_MODEL), NSA_WIDTH ** -0.5),
        "w_branch_b": nrm((DEPTH, SGU_WIDTH, D_MODEL), SGU_WIDTH ** -0.5),
        "w_out": nrm((DEPTH, D_MODEL, D_MODEL), D_MODEL ** -0.5),
        "norm_ffn": gain((DEPTH, D_MODEL)),
        "ffn_w1": nrm((n_dense, D_MODEL, D_FF_DENSE), D_MODEL ** -0.5),
        "ffn_w3": nrm((n_dense, D_MODEL, D_FF_DENSE), D_MODEL ** -0.5),
        "ffn_w2": nrm((n_dense, D_FF_DENSE, D_MODEL), D_FF_DENSE ** -0.5),
        "router_w": nrm((n_moe, D_MODEL, N_EXPERTS), D_MODEL ** -0.5),
        "router_b": nrm((n_moe, N_EXPERTS), 0.01),
        "moe_w1": nrm((n_moe, N_EXPERTS, D_MODEL, D_FF_EXPERT), D_MODEL ** -0.5),
        "moe_w3": nrm((n_moe, N_EXPERTS, D_MODEL, D_FF_EXPERT), D_MODEL ** -0.5),
        "moe_w2": nrm((n_moe, N_EXPERTS, D_FF_EXPERT, D_MODEL), D_FF_EXPERT ** -0.5),
        "norm_f": gain((D_MODEL,)),
    }


def reference(x, norm_mix, w_in, cmp_pe_k, cmp_pe_v, cmp_k_w1, cmp_k_w2, cmp_v_w1, cmp_v_w2,
              sgu_ln_g, sgu_ln_b, sgu_w, sgu_b, w_branch_a, w_branch_b, w_out, norm_ffn,
              ffn_w1, ffn_w3, ffn_w2, router_w, router_b, moe_w1, moe_w3, moe_w2, norm_f):
    for layer in range(DEPTH):
        h = rms_norm(x, norm_mix[layer])
        x = x + hybrid_mixer(h, w_in[layer], cmp_pe_k[layer], cmp_pe_v[layer],
                             cmp_k_w1[layer], cmp_k_w2[layer], cmp_v_w1[layer], cmp_v_w2[layer],
                             sgu_ln_g[layer], sgu_ln_b[layer], sgu_w[layer], sgu_b[layer],
                             w_branch_a[layer], w_branch_b[layer], w_out[layer])
        h = rms_norm(x, norm_ffn[layer])
        j = layer // 2
        if layer % 2 == 0:
            x = x + swiglu(h, ffn_w1[j], ffn_w3[j], ffn_w2[j])
        else:
            x = x + moe_swiglu(h, router_w[j], router_b[j], moe_w1[j], moe_w3[j], moe_w2[j])
    return rms_norm(x, norm_f)
```

```python
import functools

import numpy as np
import jax
import jax.numpy as jnp
from jax import lax
from jax.experimental import pallas as pl
from jax.experimental.pallas import tpu as pltpu

F32 = jnp.float32
BF16 = jnp.bfloat16

D_MODEL = 2048
N_Q_HEADS = 16
N_KV_HEADS = 4
HEAD_DIM = 64
Q_PER_KV = N_Q_HEADS // N_KV_HEADS
NSA_WIDTH = N_Q_HEADS * HEAD_DIM
KV_WIDTH = N_KV_HEADS * HEAD_DIM
GROUP_WIDTH = Q_PER_KV * HEAD_DIM
CMP_BLOCK = 32
CMP_STRIDE = 16
CMP_HIDDEN = 256
SLC_BLOCK = 64
N_SELECT = 16
WINDOW = 512
FORCED_SCORE = 1e4
SGU_WIDTH = D_MODEL // 2
SGU_GROUPS = 8
SGU_CHUNK = 128
N_EXPERTS = 8
NORM_EPS = 1e-5
NEG = -1e30
LANES = 128
VMEM_LIMIT = 56 * 1024 * 1024

Q_OFF = 0
KV_OFF = NSA_WIDTH
NG_OFF = KV_OFF + 6 * KV_WIDTH
NG_WIDTH = 3 * N_Q_HEADS
REST_OFF = NG_OFF + NG_WIDTH


def _params(*sem):
    return pltpu.CompilerParams(dimension_semantics=sem, vmem_limit_bytes=VMEM_LIMIT)


def _dot(a, b):
    return jnp.dot(a, b, preferred_element_type=F32)


def _dot_nt(a, b):
    return lax.dot_general(a, b, (((1,), (1,)), ((), ())), preferred_element_type=F32)


def _dot_split(a_f32, b_bf16):
    hi = a_f32.astype(BF16)
    lo = (a_f32 - hi.astype(F32)).astype(BF16)
    return _dot(hi, b_bf16) + _dot(lo, b_bf16)


def _sigmoid(x):
    return 1.0 / (1.0 + jnp.exp(-x))


def _gelu(x):
    return x * (0.5 * (1.0 + jnp.tanh(0.7978845608028654 * (x + 0.044715 * (x * x * x)))))


def _rms(x, g):
    return x * lax.rsqrt(jnp.mean(x * x, axis=-1, keepdims=True) + NORM_EPS) * g


def _rmsnorm_kernel(x_ref, g_ref, o_ref):
    o_ref[...] = _rms(x_ref[...], g_ref[...]).astype(o_ref.dtype)


def rmsnorm(x, g, out_dtype, tm=512):
    m, d = x.shape
    return pl.pallas_call(
        _rmsnorm_kernel,
        grid=(m // tm,),
        in_specs=[pl.BlockSpec((tm, d), lambda i: (i, 0)),
                  pl.BlockSpec((1, d), lambda i: (0, 0))],
        out_specs=pl.BlockSpec((tm, d), lambda i: (i, 0)),
        out_shape=jax.ShapeDtypeStruct((m, d), out_dtype),
        compiler_params=_params("parallel"),
        name="rmsnorm",
    )(x, g.reshape(1, d))


def _norm_router_kernel(x_ref, g_ref, rw_ref, rb_ref, h_ref, gate_ref):
    h = _rms(x_ref[...], g_ref[...])
    h_ref[...] = h.astype(h_ref.dtype)
    logits = jnp.dot(h, rw_ref[...], preferred_element_type=F32,
                     precision=lax.Precision.HIGHEST) + rb_ref[...]
    lane = lax.broadcasted_iota(jnp.int32, logits.shape, 1)
    m1 = jnp.max(logits, axis=-1, keepdims=True)
    i1 = jnp.min(jnp.where(logits == m1, lane, LANES), axis=-1, keepdims=True)
    rest = jnp.where(lane == i1, -jnp.inf, logits)
    m2 = jnp.max(rest, axis=-1, keepdims=True)
    i2 = jnp.min(jnp.where(rest == m2, lane, LANES), axis=-1, keepdims=True)
    e2 = jnp.exp(m2 - m1)
    w1 = 1.0 / (1.0 + e2)
    w2 = e2 / (1.0 + e2)
    gate_ref[...] = jnp.where(lane == i1, w1, 0.0) + jnp.where(lane == i2, w2, 0.0)


def norm_router(x, g, router_w, router_b, tm=512):
    m, d = x.shape
    rw = jnp.pad(router_w, ((0, 0), (0, LANES - N_EXPERTS)))
    rb = jnp.pad(router_b, (0, LANES - N_EXPERTS), constant_values=-jnp.inf).reshape(1, LANES)
    return pl.pallas_call(
        _norm_router_kernel,
        grid=(m // tm,),
        in_specs=[pl.BlockSpec((tm, d), lambda i: (i, 0)),
                  pl.BlockSpec((1, d), lambda i: (0, 0)),
                  pl.BlockSpec((d, LANES), lambda i: (0, 0)),
                  pl.BlockSpec((1, LANES), lambda i: (0, 0))],
        out_specs=[pl.BlockSpec((tm, d), lambda i: (i, 0)),
                   pl.BlockSpec((tm, LANES), lambda i: (i, 0))],
        out_shape=[jax.ShapeDtypeStruct((m, d), BF16),
                   jax.ShapeDtypeStruct((m, LANES), F32)],
        compiler_params=_params("parallel"),
        name="norm_router",
    )(x, g.reshape(1, d), rw, rb)


def _mm_full_kernel(*refs, n_a, n_b, n_e, n_o, pairs, inner_axis, epilogue):
    a_refs = refs[:n_a]
    b_refs = refs[n_a:n_a + n_b]
    e_refs = refs[n_a + n_b:n_a + n_b + n_e]
    o_refs = refs[n_a + n_b + n_e:n_a + n_b + n_e + n_o]
    w_refs = refs[n_a + n_b + n_e + n_o:]

    @pl.when(pl.program_id(inner_axis) == 0)
    def _():
        for b_ref, w_ref in zip(b_refs, w_refs):
            w_ref[...] = b_ref[...].astype(BF16)

    accs = [_dot(a_refs[ia][...], w_refs[ib][...]) for ia, ib in pairs]
    epilogue(accs, e_refs, o_refs)


def mm_full(name, grid, a_ops, b_ops, e_ops, outs, pairs, epilogue):
    ops = a_ops + b_ops + e_ops
    kernel = functools.partial(
        _mm_full_kernel, n_a=len(a_ops), n_b=len(b_ops), n_e=len(e_ops), n_o=len(outs),
        pairs=pairs, inner_axis=len(grid) - 1, epilogue=epilogue)
    scratch = [pltpu.VMEM(tuple(s for s in blk if s is not None), BF16) for _, blk, _ in b_ops]
    res = pl.pallas_call(
        kernel,
        grid=grid,
        in_specs=[pl.BlockSpec(blk, imap) for _, blk, imap in ops],
        out_specs=[pl.BlockSpec(blk, imap) for _, blk, imap in outs],
        out_shape=[sds for sds, _, _ in outs],
        scratch_shapes=scratch,
        compiler_params=_params(*(("arbitrary",) * len(grid))),
        name=name,
    )(*[arr for arr, _, _ in ops])
    return res


def _mm_acc_kernel(a_ref, b_ref, r_ref, o_ref):
    d = _dot(a_ref[...], b_ref[...].astype(BF16))

    @pl.when(pl.program_id(2) == 0)
    def _():
        o_ref[...] = r_ref[...] + d

    @pl.when(pl.program_id(2) > 0)
    def _():
        o_ref[...] += d


def mm_acc(name, a, b, res, tm, tn, tk):
    m, k = a.shape
    n = b.shape[1]
    return pl.pallas_call(
        _mm_acc_kernel,
        grid=(m // tm, n // tn, k // tk),
        in_specs=[pl.BlockSpec((tm, tk), lambda i, j, kk: (i, kk)),
                  pl.BlockSpec((tk, tn), lambda i, j, kk: (kk, j)),
                  pl.BlockSpec((tm, tn), lambda i, j, kk: (i, j))],
        out_specs=pl.BlockSpec((tm, tn), lambda i, j, kk: (i, j)),
        out_shape=jax.ShapeDtypeStruct((m, n), F32),
        compiler_params=_params("parallel", "parallel", "arbitrary"),
        name=name,
    )(a, b, res)


def in_projection(h, w_in):
    m, d = h.shape
    tm = 1024
    a_op = [(h, (tm, d), lambda j, i: (i, 0))]

    def q_epi(accs, e_refs, o_refs):
        o_refs[0][...] = (accs[0] * (HEAD_DIM ** -0.5)).astype(BF16)

    tq = 512
    (q,) = mm_full(
        "in_proj_q", (NSA_WIDTH // tq, m // tm), a_op,
        [(w_in, (d, tq), lambda j, i: (0, j))], [],
        [(jax.ShapeDtypeStruct((m, NSA_WIDTH), BF16), (tm, tq), lambda j, i: (i, j))],
        [(0, 0)], q_epi)

    tkv = 512
    heads_per_tile = tkv // HEAD_DIM

    def kv_epi(accs, e_refs, o_refs):
        for c in range(heads_per_tile):
            o_refs[0][c] = accs[0][:, c * HEAD_DIM:(c + 1) * HEAD_DIM].astype(BF16)

    n_kv_heads_total = 6 * N_KV_HEADS
    (kv,) = mm_full(
        "in_proj_kv", (6 * KV_WIDTH // tkv, m // tm), a_op,
        [(w_in, (d, tkv), lambda j, i: (0, j + KV_OFF // tkv))], [],
        [(jax.ShapeDtypeStruct((n_kv_heads_total, m, HEAD_DIM), BF16),
          (heads_per_tile, tm, HEAD_DIM), lambda j, i: (j, i, 0))],
        [(0, 0)], kv_epi)

    def f32_epi(accs, e_refs, o_refs):
        o_refs[0][...] = accs[0]

    w_ng = jnp.pad(w_in[:, NG_OFF:REST_OFF], ((0, 0), (0, LANES - NG_WIDTH)))
    (ng,) = mm_full(
        "in_proj_ng", (1, m // tm), a_op,
        [(w_ng, (d, LANES), lambda j, i: (0, 0))], [],
        [(jax.ShapeDtypeStruct((m, LANES), F32), (tm, LANES), lambda j, i: (i, 0))],
        [(0, 0)], f32_epi)

    w_rest = w_in[:, REST_OFF:]
    n_rest = w_rest.shape[1]
    tr = 512
    (rest,) = mm_full(
        "in_proj_rest", (n_rest // tr, m // tm), a_op,
        [(w_rest, (d, tr), lambda j, i: (0, j))], [],
        [(jax.ShapeDtypeStruct((m, n_rest), F32), (tm, tr), lambda j, i: (i, j))],
        [(0, 0)], f32_epi)
    return q, kv, ng, rest


def _cmp_kernel(a_ref, w1_ref, w2_ref, pe_ref, o_ref):
    a = a_ref[...]
    rows = a.shape[0]
    half = a.shape[1]
    w1 = w1_ref[...].astype(BF16)
    p0 = _dot(a, w1[:half])
    p1 = _dot(a, w1[half:])
    bias = _dot(pe_ref[...].astype(BF16), w1)[0:1]
    hid = p0 + pltpu.roll(p1, rows - 1, 0) + bias
    o_ref[...] = _dot(_gelu(hid).astype(BF16), w2_ref[...].astype(BF16))


def compress(kv, batch, pe_k, pe_v, kw1, kw2, vw1, vw2):
    m = kv.shape[1]
    n_chunks = m // batch // CMP_STRIDE
    rows = batch * n_chunks
    feat = CMP_STRIDE * HEAD_DIM
    a = kv[:2 * N_KV_HEADS].reshape(2, N_KV_HEADS, rows, feat)
    w1 = jnp.stack([kw1, vw1])
    w2 = jnp.stack([kw2, vw2])
    pe = jnp.stack([pe_k, pe_v]).reshape(2, 1, CMP_BLOCK * HEAD_DIM)
    pe = jnp.broadcast_to(pe, (2, 8, CMP_BLOCK * HEAD_DIM))
    return pl.pallas_call(
        _cmp_kernel,
        grid=(2, N_KV_HEADS),
        in_specs=[pl.BlockSpec((None, None, rows, feat), lambda s, g: (s, g, 0, 0)),
                  pl.BlockSpec((None, 2 * feat, CMP_HIDDEN), lambda s, g: (s, 0, 0)),
                  pl.BlockSpec((None, CMP_HIDDEN, HEAD_DIM), lambda s, g: (s, 0, 0)),
                  pl.BlockSpec((None, 8, 2 * feat), lambda s, g: (s, 0, 0))],
        out_specs=pl.BlockSpec((None, None, rows, HEAD_DIM), lambda s, g: (s, g, 0, 0)),
        out_shape=jax.ShapeDtypeStruct((2, N_KV_HEADS, rows, HEAD_DIM), F32),
        compiler_params=_params("parallel", "parallel"),
        name="nsa_compress",
    )(a, w1, w2, pe)


def _alibi_slopes():
    return np.array([2.0 ** (-8.0 * (h + 1) / N_Q_HEADS) for h in range(N_Q_HEADS)], dtype=np.float32)


def _overlap_t(n_cmp_pad, n_slc):
    cs = np.arange(n_cmp_pad)[None, :] * CMP_STRIDE
    ss = np.arange(n_slc)[:, None] * SLC_BLOCK
    ov = np.clip(np.minimum(cs + CMP_BLOCK, ss + SLC_BLOCK) - np.maximum(cs, ss), 0, None)
    return (ov / CMP_STRIDE).astype(np.float32)


def _sel_kernel(slopes_ref, q_ref, kc_ref, vc_ref, ovt_ref, oc_ref, sel_ref, *, n_cmp):
    g = pl.program_id(1)
    i = pl.program_id(2)
    tq = q_ref.shape[0]
    ncp = kc_ref.shape[0]
    n_slc = ovt_ref.shape[0]
    q = q_ref[...]
    kc = kc_ref[...].astype(BF16)
    vc = vc_ref[...].astype(BF16)

    t = i * tq + lax.broadcasted_iota(jnp.int32, (tq, ncp), 0)
    c = lax.broadcasted_iota(jnp.int32, (tq, ncp), 1)
    dist = t - (c * CMP_STRIDE + (CMP_BLOCK - 1))
    mask = (dist >= 0) & (c < n_cmp)
    distf = dist.astype(F32)

    p_sum = jnp.zeros((tq, ncp), F32)
    for j in range(Q_PER_KV):
        s = _dot_nt(q[:, j * HEAD_DIM:(j + 1) * HEAD_DIM], kc)
        s = s - slopes_ref[g * Q_PER_KV + j] * distf
        s = jnp.where(mask, s, NEG)
        mx = jnp.max(s, axis=-1, keepdims=True)
        e = jnp.where(mask, jnp.exp(s - mx), 0.0)
        p = e / jnp.maximum(jnp.sum(e, axis=-1, keepdims=True), 1e-30)
        p_sum = p_sum + p
        oc_ref[:, j * HEAD_DIM:(j + 1) * HEAD_DIM] = _dot(p.astype(BF16), vc)

    ovt = ovt_ref[...]
    hi = p_sum.astype(BF16)
    lo = (p_sum - hi.astype(F32)).astype(BF16)
    imp = _dot_nt(ovt, hi) + _dot_nt(ovt, lo)
    tt = i * tq + lax.broadcasted_iota(jnp.int32, (n_slc, tq), 1)
    blk = lax.broadcasted_iota(jnp.int32, (n_slc, tq), 0)
    cur = tt // SLC_BLOCK
    valid = blk * SLC_BLOCK <= tt
    forced = (blk == 0) | (blk == cur) | (blk == cur - 1)
    score = jnp.where(valid, imp, -1.0)
    score = jnp.where(forced, FORCED_SCORE, score)
    rank = jnp.zeros((n_slc, tq), F32)
    for mrow in range(n_slc):
        other = jnp.broadcast_to(score[mrow:mrow + 1, :], (n_slc, tq))
        beats = (other > score) | ((other == score) & (blk > mrow))
        rank = rank + jnp.where(beats, 1.0, 0.0)
    sel_t = jnp.where(rank < float(min(N_SELECT, n_slc)), 1.0, 0.0)
    sel_ref[...] = sel_t.T


def compressed_and_select(q, cmp_kv, batch, tq=256):
    m = q.shape[0]
    seq = m // batch
    nq = seq // tq
    ncp = cmp_kv.shape[2] // batch
    n_slc = seq // SLC_BLOCK
    ovt = jnp.asarray(_overlap_t(ncp, n_slc), BF16)
    slopes = jnp.asarray(_alibi_slopes())
    kernel = functools.partial(_sel_kernel, n_cmp=ncp - 1)
    return pl.pallas_call(
        kernel,
        grid=(batch, N_KV_HEADS, nq),
        in_specs=[pl.BlockSpec(memory_space=pltpu.SMEM),
                  pl.BlockSpec((tq, GROUP_WIDTH), lambda b, g, i: (b * nq + i, g)),
                  pl.BlockSpec((None, None, ncp, HEAD_DIM), lambda b, g, i: (0, g, b, 0)),
                  pl.BlockSpec((None, None, ncp, HEAD_DIM), lambda b, g, i: (1, g, b, 0)),
                  pl.BlockSpec((n_slc, ncp), lambda b, g, i: (0, 0))],
        out_specs=[pl.BlockSpec((tq, GROUP_WIDTH), lambda b, g, i: (b * nq + i, g)),
                   pl.BlockSpec((None, None, tq, n_slc), lambda b, g, i: (b, g, i, 0))],
        out_shape=[jax.ShapeDtypeStruct((m, NSA_WIDTH), F32),
                   jax.ShapeDtypeStruct((batch, N_KV_HEADS, seq, n_slc), F32)],
        compiler_params=_params("parallel", "parallel", "parallel"),
        name="nsa_compressed_select",
    )(slopes, q, cmp_kv, cmp_kv, ovt)


def _attn_kernel(slopes_ref, q_ref, ks_ref, vs_ref, kw_ref, vw_ref, sel_ref, exp_ref, oc_ref,
                 ng_ref, ge_ref, o_ref, selb_ref, m_ref, l_ref, acc_ref, os_ref):
    g = pl.program_id(1)
    i = pl.program_id(2)
    tq = q_ref.shape[0]
    tk = tq
    q0 = i * tq
    q = q_ref[...]
    row = lax.broadcasted_iota(jnp.int32, (tq, tk), 0)
    col = lax.broadcasted_iota(jnp.int32, (tq, tk), 1)
    dloc = (row - col).astype(F32)
    causal = jnp.where(col <= row, 0.0, NEG)

    selb_ref[...] = (_dot(sel_ref[...].astype(BF16), exp_ref[...]) - 1.0) * (-NEG)

    def chunk(k_ref, v_ref, k0, bias, first):
        kc = k_ref[pl.ds(k0, tk), :]
        vc = v_ref[pl.ds(k0, tk), :]
        distf = dloc + (q0 - k0).astype(F32)
        for j in range(Q_PER_KV):
            s = _dot_nt(q[:, j * HEAD_DIM:(j + 1) * HEAD_DIM], kc)
            s = s - slopes_ref[g * Q_PER_KV + j] * distf
            if bias is not None:
                s = s + bias
            if first:
                m_new = jnp.max(s, axis=-1, keepdims=True)
                p = jnp.exp(s - m_new)
                l_ref[j] = jnp.sum(p, axis=-1, keepdims=True)
                acc_ref[j] = _dot(p.astype(BF16), vc)
            else:
                m_old = m_ref[j]
                m_new = jnp.maximum(m_old, jnp.max(s, axis=-1, keepdims=True))
                alpha = jnp.exp(m_old - m_new)
                p = jnp.exp(s - m_new)
                l_ref[j] = alpha * l_ref[j] + jnp.sum(p, axis=-1, keepdims=True)
                acc_ref[j] = alpha * acc_ref[j] + _dot(p.astype(BF16), vc)
            m_ref[j] = m_new

    k_diag = pl.multiple_of(q0, tk)
    chunk(ks_ref, vs_ref, k_diag, selb_ref[:, pl.ds(k_diag, tk)] + causal, True)

    def slc_body(c, carry):
        k0 = pl.multiple_of(c * tk, tk)
        chunk(ks_ref, vs_ref, k0, selb_ref[:, pl.ds(k0, tk)], False)
        return carry

    lax.fori_loop(0, i, slc_body, 0)
    for j in range(Q_PER_KV):
        os_ref[:, j * HEAD_DIM:(j + 1) * HEAD_DIM] = acc_ref[j] / l_ref[j]

    chunk(kw_ref, vw_ref, k_diag, causal, True)
    n_back = WINDOW // tk
    for back in range(1, n_back + 1):
        @pl.when(i >= back)
        def _(back=back):
            k0 = pl.multiple_of(q0 - back * tk, tk)
            bias = jnp.where(col > row, 0.0, NEG) if back == n_back else None
            chunk(kw_ref, vw_ref, k0, bias, False)

    gx = _dot_split(_sigmoid(ng_ref[...]), ge_ref[...])
    gw = GROUP_WIDTH
    for j in range(Q_PER_KV):
        sl = slice(j * HEAD_DIM, (j + 1) * HEAD_DIM)
        o_w = acc_ref[j] / l_ref[j]
        out = (gx[:, j * HEAD_DIM:(j + 1) * HEAD_DIM] * oc_ref[:, sl]
               + gx[:, gw + j * HEAD_DIM:gw + (j + 1) * HEAD_DIM] * os_ref[:, sl]
               + gx[:, 2 * gw + j * HEAD_DIM:2 * gw + (j + 1) * HEAD_DIM] * o_w)
        o_ref[:, sl] = out.astype(o_ref.dtype)


def _gate_expand():
    ge = np.zeros((N_KV_HEADS, LANES, 3 * GROUP_WIDTH), np.float32)
    for g in range(N_KV_HEADS):
        for j in range(Q_PER_KV):
            for br in range(3):
                src = (g * Q_PER_KV + j) * 3 + br
                ge[g, src, br * GROUP_WIDTH + j * HEAD_DIM:br * GROUP_WIDTH + (j + 1) * HEAD_DIM] = 1.0
    return ge


def selected_window_attention(q, kv, sel, oc, ng, batch, tq=256):
    m = q.shape[0]
    seq = m // batch
    nq = seq // tq
    n_slc = seq // SLC_BLOCK
    expand = np.zeros((n_slc, seq), np.float32)
    expand[np.arange(seq) // SLC_BLOCK, np.arange(seq)] = 1.0
    expand = jnp.asarray(expand, BF16)
    ge = jnp.asarray(_gate_expand(), BF16)
    slopes = jnp.asarray(_alibi_slopes())
    g_off = N_KV_HEADS

    def kv_spec(section):
        return pl.BlockSpec((None, seq, HEAD_DIM), lambda b, g, i: (section * g_off + g, b, 0))

    return pl.pallas_call(
        _attn_kernel,
        grid=(batch, N_KV_HEADS, nq),
        in_specs=[pl.BlockSpec(memory_space=pltpu.SMEM),
                  pl.BlockSpec((tq, GROUP_WIDTH), lambda b, g, i: (b * nq + i, g)),
                  kv_spec(2), kv_spec(3), kv_spec(4), kv_spec(5),
                  pl.BlockSpec((None, None, tq, n_slc), lambda b, g, i: (b, g, i, 0)),
                  pl.BlockSpec((n_slc, seq), lambda b, g, i: (0, 0)),
                  pl.BlockSpec((tq, GROUP_WIDTH), lambda b, g, i: (b * nq + i, g)),
                  pl.BlockSpec((tq, LANES), lambda b, g, i: (b * nq + i, 0)),
                  pl.BlockSpec((None, LANES, 3 * GROUP_WIDTH), lambda b, g, i: (g, 0, 0))],
        out_specs=pl.BlockSpec((tq, GROUP_WIDTH), lambda b, g, i: (b * nq + i, g)),
        out_shape=jax.ShapeDtypeStruct((m, NSA_WIDTH), BF16),
        scratch_shapes=[pltpu.VMEM((tq, seq), F32),
                        pltpu.VMEM((Q_PER_KV, tq, 1), F32),
                        pltpu.VMEM((Q_PER_KV, tq, 1), F32),
                        pltpu.VMEM((Q_PER_KV, tq, HEAD_DIM), F32),
                        pltpu.VMEM((tq, GROUP_WIDTH), F32)],
        compiler_params=_params("parallel", "parallel", "parallel"),
        name="nsa_selected_window",
    )(slopes, q, kv, kv, kv, kv, sel, expand, oc, ng, ge)


def _sgu_kernel(u_ref, v_ref, lg_ref, lb_ref, ws_ref, bs_ref, o_ref):
    rows = u_ref.shape[0]
    v = _gelu(v_ref[...])
    mu = jnp.mean(v, axis=-1, keepdims=True)
    var = jnp.mean(jnp.square(v - mu), axis=-1, keepdims=True)
    vl = ((v - mu) * lax.rsqrt(var + NORM_EPS) * lg_ref[...] + lb_ref[...]).astype(BF16)
    r = lax.broadcasted_iota(jnp.int32, (SGU_CHUNK, SGU_CHUNK), 0)
    c = lax.broadcasted_iota(jnp.int32, (SGU_CHUNK, SGU_CHUNK), 1)
    gd = SGU_WIDTH // SGU_GROUPS
    for grp in range(SGU_GROUPS):
        w = jnp.where(c <= r, ws_ref[grp], 0.0).astype(BF16)
        lanes = slice(grp * gd, (grp + 1) * gd)
        for n in range(rows // SGU_CHUNK):
            rs = slice(n * SGU_CHUNK, (n + 1) * SGU_CHUNK)
            vm = _dot(w, vl[rs, lanes]) + bs_ref[:, lanes]
            o_ref[rs, lanes] = (_gelu(u_ref[rs, lanes]) * vm).astype(o_ref.dtype)


def spatial_gating(rest, ln_g, ln_b, w_s, b_s, tm=512):
    m = rest.shape[0]
    gd = SGU_WIDTH // SGU_GROUPS
    bias = jnp.repeat(b_s.T, gd, axis=1)
    return pl.pallas_call(
        _sgu_kernel,
        grid=(m // tm,),
        in_specs=[pl.BlockSpec((tm, SGU_WIDTH), lambda i: (i, 0)),
                  pl.BlockSpec((tm, SGU_WIDTH), lambda i: (i, 1)),
                  pl.BlockSpec((1, SGU_WIDTH), lambda i: (0, 0)),
                  pl.BlockSpec((1, SGU_WIDTH), lambda i: (0, 0)),
                  pl.BlockSpec((SGU_GROUPS, SGU_CHUNK, SGU_CHUNK), lambda i: (0, 0, 0)),
                  pl.BlockSpec((SGU_CHUNK, SGU_WIDTH), lambda i: (0, 0))],
        out_specs=pl.BlockSpec((tm, SGU_WIDTH), lambda i: (i, 0)),
        out_shape=jax.ShapeDtypeStruct((m, SGU_WIDTH), BF16),
        compiler_params=_params("parallel"),
        name="spatial_gating",
    )(rest, rest, ln_g.reshape(1, -1), ln_b.reshape(1, -1), w_s, bias)


def merge_branches(oa, ob, rest, w_a, w_b, tm=1024, tn=512):
    m = oa.shape[0]
    d = w_a.shape[1]
    ga_blk = 2 * SGU_WIDTH // tn
    gb_blk = (2 * SGU_WIDTH + d) // tn

    def epi(accs, e_refs, o_refs):
        y = _sigmoid(e_refs[0][...]) * accs[0] + _sigmoid(e_refs[1][...]) * accs[1]
        o_refs[0][...] = y.astype(BF16)

    (y,) = mm_full(
        "merge_branches", (d // tn, m // tm),
        [(oa, (tm, oa.shape[1]), lambda j, i: (i, 0)), (ob, (tm, ob.shape[1]), lambda j, i: (i, 0))],
        [(w_a, (w_a.shape[0], tn), lambda j, i: (0, j)), (w_b, (w_b.shape[0], tn), lambda j, i: (0, j))],
        [(rest, (tm, tn), lambda j, i: (i, ga_blk + j)), (rest, (tm, tn), lambda j, i: (i, gb_blk + j))],
        [(jax.ShapeDtypeStruct((m, d), BF16), (tm, tn), lambda j, i: (i, j))],
        [(0, 0), (1, 1)], epi)
    return y


def residual_matmul(name, a, w, x, tm=1024, tn=512):
    m, k = a.shape
    n = w.shape[1]

    def epi(accs, e_refs, o_refs):
        o_refs[0][...] = e_refs[0][...] + accs[0]

    (out,) = mm_full(
        name, (n // tn, m // tm),
        [(a, (tm, k), lambda j, i: (i, 0))],
        [(w, (k, tn), lambda j, i: (0, j))],
        [(x, (tm, tn), lambda j, i: (i, j))],
        [(jax.ShapeDtypeStruct((m, n), F32), (tm, tn), lambda j, i: (i, j))],
        [(0, 0)], epi)
    return out


def swiglu_up(h, w1, w3, tm=1024, tn=512):
    m, d = h.shape
    f = w1.shape[1]

    def epi(accs, e_refs, o_refs):
        o_refs[0][...] = (accs[0] * _sigmoid(accs[0]) * accs[1]).astype(BF16)

    (out,) = mm_full(
        "swiglu_up", (f // tn, m // tm),
        [(h, (tm, d), lambda j, i: (i, 0))],
        [(w1, (d, tn), lambda j, i: (0, j)), (w3, (d, tn), lambda j, i: (0, j))],
        [],
        [(jax.ShapeDtypeStruct((m, f), BF16), (tm, tn), lambda j, i: (i, j))],
        [(0, 0), (0, 1)], epi)
    return out


def moe_up_dense(h, gates, w1, w3, tm=1024, tn=512):
    m, d = h.shape
    n_e, _, f = w1.shape
    nf = f // tn

    def epi(accs, e_refs, o_refs):
        o_refs[0][...] = (accs[0] * _sigmoid(accs[0]) * accs[1] * e_refs[0][...]).astype(BF16)

    (out,) = mm_full(
        "moe_up", (n_e, nf, m // tm),
        [(h, (tm, d), lambda e, j, i: (i, 0))],
        [(w1, (None, d, tn), lambda e, j, i: (e, 0, j)), (w3, (None, d, tn), lambda e, j, i: (e, 0, j))],
        [(gates, (None, tm, 1), lambda e, j, i: (e, i, 0))],
        [(jax.ShapeDtypeStruct((m, n_e * f), BF16), (tm, tn), lambda e, j, i: (i, e * nf + j))],
        [(0, 0), (0, 1)], epi)
    return out


def _mixer(x, batch, norm_g, w_in, pe_k, pe_v, kw1, kw2, vw1, vw2, ln_g, ln_b, w_s, b_s, w_a, w_b, w_out):
    h = rmsnorm(x, norm_g, BF16)
    q, kv, ng, rest = in_projection(h, w_in)
    cmp_kv = compress(kv, batch, pe_k, pe_v, kw1, kw2, vw1, vw2)
    oc, sel = compressed_and_select(q, cmp_kv, batch)
    oa = selected_window_attention(q, kv, sel, oc, ng, batch)
    ob = spatial_gating(rest, ln_g, ln_b, w_s, b_s)
    y = merge_branches(oa, ob, rest, w_a, w_b)
    return residual_matmul("out_proj", y, w_out, x)


def kernel(x, norm_mix, w_in, cmp_pe_k, cmp_pe_v, cmp_k_w1, cmp_k_w2, cmp_v_w1, cmp_v_w2, sgu_ln_g, sgu_ln_b, sgu_w, sgu_b, w_branch_a, w_branch_b, w_out, norm_ffn, ffn_w1, ffn_w3, ffn_w2, router_w, router_b, moe_w1, moe_w3, moe_w2, norm_f):
    batch, seq, d = x.shape
    depth = norm_mix.shape[0]
    xf = x.reshape(batch * seq, d)
    for layer in range(depth):
        xf = _mixer(xf, batch, norm_mix[layer], w_in[layer], cmp_pe_k[layer], cmp_pe_v[layer],
                    cmp_k_w1[layer], cmp_k_w2[layer], cmp_v_w1[layer], cmp_v_w2[layer],
                    sgu_ln_g[layer], sgu_ln_b[layer], sgu_w[layer], sgu_b[layer],
                    w_branch_a[layer], w_branch_b[layer], w_out[layer])
        j = layer // 2
        if layer % 2 == 0:
            h = rmsnorm(xf, norm_ffn[layer], BF16)
            up = swiglu_up(h, ffn_w1[j], ffn_w3[j])
            xf = mm_acc("ffn_down", up, ffn_w2[j], xf, 1024, 1024, 512)
        else:
            h, gates = norm_router(xf, norm_ffn[layer], router_w[j], router_b[j])
            gates_e = gates[:, :N_EXPERTS].T.reshape(N_EXPERTS, batch * seq, 1)
            up = moe_up_dense(h, gates_e, moe_w1[j], moe_w3[j])
            w2 = moe_w2[j].reshape(-1, d)
            xf = mm_acc("moe_down", up, w2, xf, 1024, 1024, 512)
    return rmsnorm(xf, norm_f, F32).reshape(batch, seq, d)
```

```python
import functools

import numpy as np
import jax
import jax.numpy as jnp
from jax import lax
from jax.experimental import pallas as pl
from jax.experimental.pallas import tpu as pltpu

F32 = jnp.float32
BF16 = jnp.bfloat16

D_MODEL = 2048
N_Q_HEADS = 16
N_KV_HEADS = 4
HEAD_DIM = 64
Q_PER_KV = N_Q_HEADS // N_KV_HEADS
NSA_WIDTH = N_Q_HEADS * HEAD_DIM
KV_WIDTH = N_KV_HEADS * HEAD_DIM
GROUP_WIDTH = Q_PER_KV * HEAD_DIM
CMP_BLOCK = 32
CMP_STRIDE = 16
CMP_HIDDEN = 256
SLC_BLOCK = 64
N_SELECT = 16
WINDOW = 512
FORCED_SCORE = 1e4
SGU_WIDTH = D_MODEL // 2
SGU_GROUPS = 8
SGU_CHUNK = 128
N_EXPERTS = 8
NORM_EPS = 1e-5
NEG = -1e30
LANES = 128
VMEM_LIMIT = 56 * 1024 * 1024

Q_OFF = 0
KV_OFF = NSA_WIDTH
NG_OFF = KV_OFF + 6 * KV_WIDTH
NG_WIDTH = 3 * N_Q_HEADS
REST_OFF = NG_OFF + NG_WIDTH


def _params(*sem):
    return pltpu.CompilerParams(dimension_semantics=sem, vmem_limit_bytes=VMEM_LIMIT)


def _dot(a, b):
    return jnp.dot(a, b, preferred_element_type=F32)


def _dot_nt(a, b):
    return lax.dot_general(a, b, (((1,), (1,)), ((), ())), preferred_element_type=F32)


def _dot_split(a_f32, b_bf16):
    hi = a_f32.astype(BF16)
    lo = (a_f32 - hi.astype(F32)).astype(BF16)
    return _dot(hi, b_bf16) + _dot(lo, b_bf16)


def _sigmoid(x):
    return 1.0 / (1.0 + jnp.exp(-x))


def _gelu(x):
    return x * (0.5 * (1.0 + jnp.tanh(0.7978845608028654 * (x + 0.044715 * (x * x * x)))))


def _rms(x, g):
    return x * lax.rsqrt(jnp.mean(x * x, axis=-1, keepdims=True) + NORM_EPS) * g


def _rmsnorm_kernel(x_ref, g_ref, o_ref):
    o_ref[...] = _rms(x_ref[...], g_ref[...]).astype(o_ref.dtype)


def rmsnorm(x, g, out_dtype, tm=512):
    m, d = x.shape
    return pl.pallas_call(
        _rmsnorm_kernel,
        grid=(m // tm,),
        in_specs=[pl.BlockSpec((tm, d), lambda i: (i, 0)),
                  pl.BlockSpec((1, d), lambda i: (0, 0))],
        out_specs=pl.BlockSpec((tm, d), lambda i: (i, 0)),
        out_shape=jax.ShapeDtypeStruct((m, d), out_dtype),
        compiler_params=_params("parallel"),
        name="rmsnorm",
    )(x, g.reshape(1, d))


def _norm_router_kernel(x_ref, g_ref, rwt_ref, rb_ref, h_ref, sel_ref, w_ref):
    h = _rms(x_ref[...], g_ref[...])
    h_ref[...] = h
    logits = lax.dot_general(rwt_ref[...], h, (((1,), (1,)), ((), ())), preferred_element_type=F32,
                             precision=lax.Precision.HIGHEST) + rb_ref[...]
    n_e = logits.shape[0]
    e = lax.broadcasted_iota(jnp.int32, logits.shape, 0)
    m1 = jnp.max(logits, axis=0, keepdims=True)
    i1 = jnp.min(jnp.where(logits == m1, e, n_e), axis=0, keepdims=True)
    rest = jnp.where(e == i1, -jnp.inf, logits)
    m2 = jnp.max(rest, axis=0, keepdims=True)
    i2 = jnp.min(jnp.where(rest == m2, e, n_e), axis=0, keepdims=True)
    e2 = jnp.exp(m2 - m1)
    w1 = 1.0 / (1.0 + e2)
    w2 = e2 / (1.0 + e2)
    sel_ref[...] = jnp.where((e == i1) | (e == i2), 1.0, 0.0)
    w_ref[...] = jnp.where(e == i1, w1, 0.0) + jnp.where(e == i2, w2, 0.0)


def norm_router(x, g, router_w, router_b, tm=512):
    m, d = x.shape
    n_e = router_w.shape[1]
    return pl.pallas_call(
        _norm_router_kernel,
        grid=(m // tm,),
        in_specs=[pl.BlockSpec((tm, d), lambda i: (i, 0)),
                  pl.BlockSpec((1, d), lambda i: (0, 0)),
                  pl.BlockSpec((n_e, d), lambda i: (0, 0)),
                  pl.BlockSpec((n_e, 1), lambda i: (0, 0))],
        out_specs=[pl.BlockSpec((tm, d), lambda i: (i, 0)),
                   pl.BlockSpec((n_e, tm), lambda i: (0, i)),
                   pl.BlockSpec((n_e, tm), lambda i: (0, i))],
        out_shape=[jax.ShapeDtypeStruct((m, d), F32),
                   jax.ShapeDtypeStruct((n_e, m), F32),
                   jax.ShapeDtypeStruct((n_e, m), F32)],
        compiler_params=_params("parallel"),
        name="norm_router",
    )(x, g.reshape(1, d), router_w.T, router_b.reshape(n_e, 1))


def _mm_full_kernel(*refs, n_a, n_b, n_e, n_o, pairs, inner_axis, epilogue):
    a_refs = refs[:n_a]
    b_refs = refs[n_a:n_a + n_b]
    e_refs = refs[n_a + n_b:n_a + n_b + n_e]
    o_refs = refs[n_a + n_b + n_e:n_a + n_b + n_e + n_o]
    w_refs = refs[n_a + n_b + n_e + n_o:]

    @pl.when(pl.program_id(inner_axis) == 0)
    def _():
        for b_ref, w_ref in zip(b_refs, w_refs):
            w_ref[...] = b_ref[...].astype(BF16)

    accs = [_dot(a_refs[ia][...], w_refs[ib][...]) for ia, ib in pairs]
    epilogue(accs, e_refs, o_refs)


def mm_full(name, grid, a_ops, b_ops, e_ops, outs, pairs, epilogue):
    ops = a_ops + b_ops + e_ops
    kernel = functools.partial(
        _mm_full_kernel, n_a=len(a_ops), n_b=len(b_ops), n_e=len(e_ops), n_o=len(outs),
        pairs=pairs, inner_axis=len(grid) - 1, epilogue=epilogue)
    scratch = [pltpu.VMEM(tuple(s for s in blk if s is not None), BF16) for _, blk, _ in b_ops]
    res = pl.pallas_call(
        kernel,
        grid=grid,
        in_specs=[pl.BlockSpec(blk, imap) for _, blk, imap in ops],
        out_specs=[pl.BlockSpec(blk, imap) for _, blk, imap in outs],
        out_shape=[sds for sds, _, _ in outs],
        scratch_shapes=scratch,
        compiler_params=_params(*(("arbitrary",) * len(grid))),
        name=name,
    )(*[arr for arr, _, _ in ops])
    return res


def _mm_acc_kernel(a_ref, b_ref, r_ref, o_ref):
    d = _dot(a_ref[...], b_ref[...].astype(BF16))

    @pl.when(pl.program_id(2) == 0)
    def _():
        o_ref[...] = r_ref[...] + d

    @pl.when(pl.program_id(2) > 0)
    def _():
        o_ref[...] += d


def mm_acc(name, a, b, res, tm, tn, tk):
    m, k = a.shape
    n = b.shape[1]
    return pl.pallas_call(
        _mm_acc_kernel,
        grid=(m // tm, n // tn, k // tk),
        in_specs=[pl.BlockSpec((tm, tk), lambda i, j, kk: (i, kk)),
                  pl.BlockSpec((tk, tn), lambda i, j, kk: (kk, j)),
                  pl.BlockSpec((tm, tn), lambda i, j, kk: (i, j))],
        out_specs=pl.BlockSpec((tm, tn), lambda i, j, kk: (i, j)),
        out_shape=jax.ShapeDtypeStruct((m, n), F32),
        compiler_params=_params("parallel", "parallel", "arbitrary"),
        name=name,
    )(a, b, res)


def in_projection(h, w_in):
    m, d = h.shape
    tm = 1024
    a_op = [(h, (tm, d), lambda j, i: (i, 0))]

    def q_epi(accs, e_refs, o_refs):
        o_refs[0][...] = (accs[0] * (HEAD_DIM ** -0.5)).astype(BF16)

    tq = 512
    (q,) = mm_full(
        "in_proj_q", (NSA_WIDTH // tq, m // tm), a_op,
        [(w_in, (d, tq), lambda j, i: (0, j))], [],
        [(jax.ShapeDtypeStruct((m, NSA_WIDTH), BF16), (tm, tq), lambda j, i: (i, j))],
        [(0, 0)], q_epi)

    tkv = 512
    heads_per_tile = tkv // HEAD_DIM

    def kv_epi(accs, e_refs, o_refs):
        for c in range(heads_per_tile):
            o_refs[0][c] = accs[0][:, c * HEAD_DIM:(c + 1) * HEAD_DIM].astype(BF16)

    n_kv_heads_total = 6 * N_KV_HEADS
    (kv,) = mm_full(
        "in_proj_kv", (6 * KV_WIDTH // tkv, m // tm), a_op,
        [(w_in, (d, tkv), lambda j, i: (0, j + KV_OFF // tkv))], [],
        [(jax.ShapeDtypeStruct((n_kv_heads_total, m, HEAD_DIM), BF16),
          (heads_per_tile, tm, HEAD_DIM), lambda j, i: (j, i, 0))],
        [(0, 0)], kv_epi)

    def f32_epi(accs, e_refs, o_refs):
        o_refs[0][...] = accs[0]

    w_ng = jnp.pad(w_in[:, NG_OFF:REST_OFF], ((0, 0), (0, LANES - NG_WIDTH)))
    (ng,) = mm_full(
        "in_proj_ng", (1, m // tm), a_op,
        [(w_ng, (d, LANES), lambda j, i: (0, 0))], [],
        [(jax.ShapeDtypeStruct((m, LANES), F32), (tm, LANES), lambda j, i: (i, 0))],
        [(0, 0)], f32_epi)

    w_rest = w_in[:, REST_OFF:]
    n_rest = w_rest.shape[1]
    tr = 512
    (rest,) = mm_full(
        "in_proj_rest", (n_rest // tr, m // tm), a_op,
        [(w_rest, (d, tr), lambda j, i: (0, j))], [],
        [(jax.ShapeDtypeStruct((m, n_rest), F32), (tm, tr), lambda j, i: (i, j))],
        [(0, 0)], f32_epi)
    return q, kv, ng, rest


def _cmp_kernel(a_ref, w1_ref, w2_ref, pe_ref, o_ref):
    a = a_ref[...]
    rows = a.shape[0]
    half = a.shape[1]
    w1 = w1_ref[...].astype(BF16)
    p0 = _dot(a, w1[:half])
    p1 = _dot(a, w1[half:])
    bias = _dot(pe_ref[...].astype(BF16), w1)[0:1]
    hid = p0 + pltpu.roll(p1, rows - 1, 0) + bias
    o_ref[...] = _dot(_gelu(hid).astype(BF16), w2_ref[...].astype(BF16))


def compress(kv, batch, pe_k, pe_v, kw1, kw2, vw1, vw2):
    m = kv.shape[1]
    n_chunks = m // batch // CMP_STRIDE
    rows = batch * n_chunks
    feat = CMP_STRIDE * HEAD_DIM
    a = kv[:2 * N_KV_HEADS].reshape(2, N_KV_HEADS, rows, feat)
    w1 = jnp.stack([kw1, vw1])
    w2 = jnp.stack([kw2, vw2])
    pe = jnp.stack([pe_k, pe_v]).reshape(2, 1, CMP_BLOCK * HEAD_DIM)
    pe = jnp.broadcast_to(pe, (2, 8, CMP_BLOCK * HEAD_DIM))
    return pl.pallas_call(
        _cmp_kernel,
        grid=(2, N_KV_HEADS),
        in_specs=[pl.BlockSpec((None, None, rows, feat), lambda s, g: (s, g, 0, 0)),
                  pl.BlockSpec((None, 2 * feat, CMP_HIDDEN), lambda s, g: (s, 0, 0)),
                  pl.BlockSpec((None, CMP_HIDDEN, HEAD_DIM), lambda s, g: (s, 0, 0)),
                  pl.BlockSpec((None, 8, 2 * feat), lambda s, g: (s, 0, 0))],
        out_specs=pl.BlockSpec((None, None, rows, HEAD_DIM), lambda s, g: (s, g, 0, 0)),
        out_shape=jax.ShapeDtypeStruct((2, N_KV_HEADS, rows, HEAD_DIM), F32),
        compiler_params=_params("parallel", "parallel"),
        name="nsa_compress",
    )(a, w1, w2, pe)


def _alibi_slopes():
    return np.array([2.0 ** (-8.0 * (h + 1) / N_Q_HEADS) for h in range(N_Q_HEADS)], dtype=np.float32)


def _overlap_t(n_cmp_pad, n_slc):
    cs = np.arange(n_cmp_pad)[None, :] * CMP_STRIDE
    ss = np.arange(n_slc)[:, None] * SLC_BLOCK
    ov = np.clip(np.minimum(cs + CMP_BLOCK, ss + SLC_BLOCK) - np.maximum(cs, ss), 0, None)
    return (ov / CMP_STRIDE).astype(np.float32)


def _sel_kernel(slopes_ref, q_ref, kc_ref, vc_ref, ovt_ref, oc_ref, sel_ref, *, n_cmp):
    g = pl.program_id(1)
    i = pl.program_id(2)
    tq = q_ref.shape[0]
    ncp = kc_ref.shape[0]
    n_slc = ovt_ref.shape[0]
    q = q_ref[...]
    kc = kc_ref[...].astype(BF16)
    vc = vc_ref[...].astype(BF16)

    t = i * tq + lax.broadcasted_iota(jnp.int32, (tq, ncp), 0)
    c = lax.broadcasted_iota(jnp.int32, (tq, ncp), 1)
    dist = t - (c * CMP_STRIDE + (CMP_BLOCK - 1))
    mask = (dist >= 0) & (c < n_cmp)
    distf = dist.astype(F32)

    p_sum = jnp.zeros((tq, ncp), F32)
    for j in range(Q_PER_KV):
        s = _dot_nt(q[:, j * HEAD_DIM:(j + 1) * HEAD_DIM], kc)
        s = s - slopes_ref[g * Q_PER_KV + j] * distf
        s = jnp.where(mask, s, NEG)
        mx = jnp.max(s, axis=-1, keepdims=True)
        e = jnp.where(mask, jnp.exp(s - mx), 0.0)
        p = e / jnp.maximum(jnp.sum(e, axis=-1, keepdims=True), 1e-30)
        p_sum = p_sum + p
        oc_ref[:, j * HEAD_DIM:(j + 1) * HEAD_DIM] = _dot(p.astype(BF16), vc)

    ovt = ovt_ref[...]
    hi = p_sum.astype(BF16)
    lo = (p_sum - hi.astype(F32)).astype(BF16)
    imp = _dot_nt(ovt, hi) + _dot_nt(ovt, lo)
    tt = i * tq + lax.broadcasted_iota(jnp.int32, (n_slc, tq), 1)
    blk = lax.broadcasted_iota(jnp.int32, (n_slc, tq), 0)
    cur = tt // SLC_BLOCK
    valid = blk * SLC_BLOCK <= tt
    forced = (blk == 0) | (blk == cur) | (blk == cur - 1)
    score = jnp.where(valid, imp, -1.0)
    score = jnp.where(forced, FORCED_SCORE, score)
    rank = jnp.zeros((n_slc, tq), F32)
    for mrow in range(n_slc):
        other = jnp.broadcast_to(score[mrow:mrow + 1, :], (n_slc, tq))
        beats = (other > score) | ((other == score) & (blk > mrow))
        rank = rank + jnp.where(beats, 1.0, 0.0)
    sel_t = jnp.where(rank < float(min(N_SELECT, n_slc)), 1.0, 0.0)
    sel_ref[...] = sel_t.T


def compressed_and_select(q, cmp_kv, batch, tq=256):
    m = q.shape[0]
    seq = m // batch
    nq = seq // tq
    ncp = cmp_kv.shape[2] // batch
    n_slc = seq // SLC_BLOCK
    ovt = jnp.asarray(_overlap_t(ncp, n_slc), BF16)
    slopes = jnp.asarray(_alibi_slopes())
    kernel = functools.partial(_sel_kernel, n_cmp=ncp - 1)
    return pl.pallas_call(
        kernel,
        grid=(batch, N_KV_HEADS, nq),
        in_specs=[pl.BlockSpec(memory_space=pltpu.SMEM),
                  pl.BlockSpec((tq, GROUP_WIDTH), lambda b, g, i: (b * nq + i, g)),
                  pl.BlockSpec((None, None, ncp, HEAD_DIM), lambda b, g, i: (0, g, b, 0)),
                  pl.BlockSpec((None, None, ncp, HEAD_DIM), lambda b, g, i: (1, g, b, 0)),
                  pl.BlockSpec((n_slc, ncp), lambda b, g, i: (0, 0))],
        out_specs=[pl.BlockSpec((tq, GROUP_WIDTH), lambda b, g, i: (b * nq + i, g)),
                   pl.BlockSpec((None, None, tq, n_slc), lambda b, g, i: (b, g, i, 0))],
        out_shape=[jax.ShapeDtypeStruct((m, NSA_WIDTH), F32),
                   jax.ShapeDtypeStruct((batch, N_KV_HEADS, seq, n_slc), F32)],
        compiler_params=_params("parallel", "parallel", "parallel"),
        name="nsa_compressed_select",
    )(slopes, q, cmp_kv, cmp_kv, ovt)


def _attn_kernel(slopes_ref, q_ref, ks_ref, vs_ref, kw_ref, vw_ref, sel_ref, exp_ref, oc_ref,
                 ng_ref, ge_ref, o_ref, selb_ref, m_ref, l_ref, acc_ref, os_ref):
    g = pl.program_id(1)
    i = pl.program_id(2)
    tq = q_ref.shape[0]
    tk = tq
    q0 = i * tq
    q = q_ref[...]
    row = lax.broadcasted_iota(jnp.int32, (tq, tk), 0)
    col = lax.broadcasted_iota(jnp.int32, (tq, tk), 1)
    dloc = (row - col).astype(F32)
    causal = jnp.where(col <= row, 0.0, NEG)

    selb_ref[...] = (_dot(sel_ref[...].astype(BF16), exp_ref[...]) - 1.0) * (-NEG)

    def chunk(k_ref, v_ref, k0, bias, first):
        kc = k_ref[pl.ds(k0, tk), :]
        vc = v_ref[pl.ds(k0, tk), :]
        distf = dloc + (q0 - k0).astype(F32)
        for j in range(Q_PER_KV):
            s = _dot_nt(q[:, j * HEAD_DIM:(j + 1) * HEAD_DIM], kc)
            s = s - slopes_ref[g * Q_PER_KV + j] * distf
            if bias is not None:
                s = s + bias
            if first:
                m_new = jnp.max(s, axis=-1, keepdims=True)
                p = jnp.exp(s - m_new)
                l_ref[j] = jnp.sum(p, axis=-1, keepdims=True)
                acc_ref[j] = _dot(p.astype(BF16), vc)
            else:
                m_old = m_ref[j]
                m_new = jnp.maximum(m_old, jnp.max(s, axis=-1, keepdims=True))
                alpha = jnp.exp(m_old - m_new)
                p = jnp.exp(s - m_new)
                l_ref[j] = alpha * l_ref[j] + jnp.sum(p, axis=-1, keepdims=True)
                acc_ref[j] = alpha * acc_ref[j] + _dot(p.astype(BF16), vc)
            m_ref[j] = m_new

    k_diag = pl.multiple_of(q0, tk)
    chunk(ks_ref, vs_ref, k_diag, selb_ref[:, pl.ds(k_diag, tk)] + causal, True)

    def slc_body(c, carry):
        k0 = pl.multiple_of(c * tk, tk)
        chunk(ks_ref, vs_ref, k0, selb_ref[:, pl.ds(k0, tk)], False)
        return carry

    lax.fori_loop(0, i, slc_body, 0)
    for j in range(Q_PER_KV):
        os_ref[:, j * HEAD_DIM:(j + 1) * HEAD_DIM] = acc_ref[j] / l_ref[j]

    chunk(kw_ref, vw_ref, k_diag, causal, True)
    n_back = WINDOW // tk
    for back in range(1, n_back + 1):
        @pl.when(i >= back)
        def _(back=back):
            k0 = pl.multiple_of(q0 - back * tk, tk)
            bias = jnp.where(col > row, 0.0, NEG) if back == n_back else None
            chunk(kw_ref, vw_ref, k0, bias, False)

    gx = _dot_split(_sigmoid(ng_ref[...]), ge_ref[...])
    gw = GROUP_WIDTH
    for j in range(Q_PER_KV):
        sl = slice(j * HEAD_DIM, (j + 1) * HEAD_DIM)
        o_w = acc_ref[j] / l_ref[j]
        out = (gx[:, j * HEAD_DIM:(j + 1) * HEAD_DIM] * oc_ref[:, sl]
               + gx[:, gw + j * HEAD_DIM:gw + (j + 1) * HEAD_DIM] * os_ref[:, sl]
               + gx[:, 2 * gw + j * HEAD_DIM:2 * gw + (j + 1) * HEAD_DIM] * o_w)
        o_ref[:, sl] = out.astype(o_ref.dtype)


def _gate_expand():
    ge = np.zeros((N_KV_HEADS, LANES, 3 * GROUP_WIDTH), np.float32)
    for g in range(N_KV_HEADS):
        for j in range(Q_PER_KV):
            for br in range(3):
                src = (g * Q_PER_KV + j) * 3 + br
                ge[g, src, br * GROUP_WIDTH + j * HEAD_DIM:br * GROUP_WIDTH + (j + 1) * HEAD_DIM] = 1.0
    return ge


def selected_window_attention(q, kv, sel, oc, ng, batch, tq=256):
    m = q.shape[0]
    seq = m // batch
    nq = seq // tq
    n_slc = seq // SLC_BLOCK
    expand = np.zeros((n_slc, seq), np.float32)
    expand[np.arange(seq) // SLC_BLOCK, np.arange(seq)] = 1.0
    expand = jnp.asarray(expand, BF16)
    ge = jnp.asarray(_gate_expand(), BF16)
    slopes = jnp.asarray(_alibi_slopes())
    g_off = N_KV_HEADS

    def kv_spec(section):
        return pl.BlockSpec((None, seq, HEAD_DIM), lambda b, g, i: (section * g_off + g, b, 0))

    return pl.pallas_call(
        _attn_kernel,
        grid=(batch, N_KV_HEADS, nq),
        in_specs=[pl.BlockSpec(memory_space=pltpu.SMEM),
                  pl.BlockSpec((tq, GROUP_WIDTH), lambda b, g, i: (b * nq + i, g)),
                  kv_spec(2), kv_spec(3), kv_spec(4), kv_spec(5),
                  pl.BlockSpec((None, None, tq, n_slc), lambda b, g, i: (b, g, i, 0)),
                  pl.BlockSpec((n_slc, seq), lambda b, g, i: (0, 0)),
                  pl.BlockSpec((tq, GROUP_WIDTH), lambda b, g, i: (b * nq + i, g)),
                  pl.BlockSpec((tq, LANES), lambda b, g, i: (b * nq + i, 0)),
                  pl.BlockSpec((None, LANES, 3 * GROUP_WIDTH), lambda b, g, i: (g, 0, 0))],
        out_specs=pl.BlockSpec((tq, GROUP_WIDTH), lambda b, g, i: (b * nq + i, g)),
        out_shape=jax.ShapeDtypeStruct((m, NSA_WIDTH), BF16),
        scratch_shapes=[pltpu.VMEM((tq, seq), F32),
                        pltpu.VMEM((Q_PER_KV, tq, 1), F32),
                        pltpu.VMEM((Q_PER_KV, tq, 1), F32),
                        pltpu.VMEM((Q_PER_KV, tq, HEAD_DIM), F32),
                        pltpu.VMEM((tq, GROUP_WIDTH), F32)],
        compiler_params=_params("parallel", "parallel", "parallel"),
        name="nsa_selected_window",
    )(slopes, q, kv, kv, kv, kv, sel, expand, oc, ng, ge)


def _sgu_kernel(u_ref, v_ref, lg_ref, lb_ref, ws_ref, bs_ref, o_ref):
    rows = u_ref.shape[0]
    v = _gelu(v_ref[...])
    mu = jnp.mean(v, axis=-1, keepdims=True)
    var = jnp.mean(jnp.square(v - mu), axis=-1, keepdims=True)
    vl = ((v - mu) * lax.rsqrt(var + NORM_EPS) * lg_ref[...] + lb_ref[...]).astype(BF16)
    r = lax.broadcasted_iota(jnp.int32, (SGU_CHUNK, SGU_CHUNK), 0)
    c = lax.broadcasted_iota(jnp.int32, (SGU_CHUNK, SGU_CHUNK), 1)
    gd = SGU_WIDTH // SGU_GROUPS
    for grp in range(SGU_GROUPS):
        w = jnp.where(c <= r, ws_ref[grp], 0.0).astype(BF16)
        lanes = slice(grp * gd, (grp + 1) * gd)
        for n in range(rows // SGU_CHUNK):
            rs = slice(n * SGU_CHUNK, (n + 1) * SGU_CHUNK)
            vm = _dot(w, vl[rs, lanes]) + bs_ref[:, lanes]
            o_ref[rs, lanes] = (_gelu(u_ref[rs, lanes]) * vm).astype(o_ref.dtype)


def spatial_gating(rest, ln_g, ln_b, w_s, b_s, tm=512):
    m = rest.shape[0]
    gd = SGU_WIDTH // SGU_GROUPS
    bias = jnp.repeat(b_s.T, gd, axis=1)
    return pl.pallas_call(
        _sgu_kernel,
        grid=(m // tm,),
        in_specs=[pl.BlockSpec((tm, SGU_WIDTH), lambda i: (i, 0)),
                  pl.BlockSpec((tm, SGU_WIDTH), lambda i: (i, 1)),
                  pl.BlockSpec((1, SGU_WIDTH), lambda i: (0, 0)),
                  pl.BlockSpec((1, SGU_WIDTH), lambda i: (0, 0)),
                  pl.BlockSpec((SGU_GROUPS, SGU_CHUNK, SGU_CHUNK), lambda i: (0, 0, 0)),
                  pl.BlockSpec((SGU_CHUNK, SGU_WIDTH), lambda i: (0, 0))],
        out_specs=pl.BlockSpec((tm, SGU_WIDTH), lambda i: (i, 0)),
        out_shape=jax.ShapeDtypeStruct((m, SGU_WIDTH), BF16),
        compiler_params=_params("parallel"),
        name="spatial_gating",
    )(rest, rest, ln_g.reshape(1, -1), ln_b.reshape(1, -1), w_s, bias)


def merge_branches(oa, ob, rest, w_a, w_b, tm=1024, tn=512):
    m = oa.shape[0]
    d = w_a.shape[1]
    ga_blk = 2 * SGU_WIDTH // tn
    gb_blk = (2 * SGU_WIDTH + d) // tn

    def epi(accs, e_refs, o_refs):
        y = _sigmoid(e_refs[0][...]) * accs[0] + _sigmoid(e_refs[1][...]) * accs[1]
        o_refs[0][...] = y.astype(BF16)

    (y,) = mm_full(
        "merge_branches", (d // tn, m // tm),
        [(oa, (tm, oa.shape[1]), lambda j, i: (i, 0)), (ob, (tm, ob.shape[1]), lambda j, i: (i, 0))],
        [(w_a, (w_a.shape[0], tn), lambda j, i: (0, j)), (w_b, (w_b.shape[0], tn), lambda j, i: (0, j))],
        [(rest, (tm, tn), lambda j, i: (i, ga_blk + j)), (rest, (tm, tn), lambda j, i: (i, gb_blk + j))],
        [(jax.ShapeDtypeStruct((m, d), BF16), (tm, tn), lambda j, i: (i, j))],
        [(0, 0), (1, 1)], epi)
    return y


def residual_matmul(name, a, w, x, tm=1024, tn=512):
    m, k = a.shape
    n = w.shape[1]

    def epi(accs, e_refs, o_refs):
        o_refs[0][...] = e_refs[0][...] + accs[0]

    (out,) = mm_full(
        name, (n // tn, m // tm),
        [(a, (tm, k), lambda j, i: (i, 0))],
        [(w, (k, tn), lambda j, i: (0, j))],
        [(x, (tm, tn), lambda j, i: (i, j))],
        [(jax.ShapeDtypeStruct((m, n), F32), (tm, tn), lambda j, i: (i, j))],
        [(0, 0)], epi)
    return out


def swiglu_up(h, w1, w3, tm=1024, tn=512):
    m, d = h.shape
    f = w1.shape[1]

    def epi(accs, e_refs, o_refs):
        o_refs[0][...] = (accs[0] * _sigmoid(accs[0]) * accs[1]).astype(BF16)

    (out,) = mm_full(
        "swiglu_up", (f // tn, m // tm),
        [(h, (tm, d), lambda j, i: (i, 0))],
        [(w1, (d, tn), lambda j, i: (0, j)), (w3, (d, tn), lambda j, i: (0, j))],
        [],
        [(jax.ShapeDtypeStruct((m, f), BF16), (tm, tn), lambda j, i: (i, j))],
        [(0, 0), (0, 1)], epi)
    return out


MOE_TILE = 512


def _route_kernel(sel_ref, w_ref, tri_ref, pos_ref, wt_ref, meta_ref, cum_ref, *, tile):
    n_e, m = sel_ref.shape
    ck = tri_ref.shape[0]
    carry = jnp.zeros((n_e, 1), F32)
    for c in range(m // ck):
        sl = slice(c * ck, (c + 1) * ck)
        cs = _dot(sel_ref[:, sl].astype(BF16), tri_ref[...]) + carry
        cum_ref[:, sl] = cs
        carry = cs[:, ck - 1:ck]
    padded = jnp.ceil(carry / tile) * tile
    sub = lax.broadcasted_iota(jnp.int32, (n_e, 1), 0)
    start = jnp.zeros((n_e, 1), F32)
    run = jnp.zeros((1, 1), F32)
    for ex in range(n_e):
        start = jnp.where(sub == ex, run, start)
        run = run + padded[ex:ex + 1, :]
    sel = sel_ref[...] > 0.5
    pos = start + cum_ref[...] - 1.0
    eidx = lax.broadcasted_iota(jnp.int32, (n_e, m), 0)
    e_lo = jnp.min(jnp.where(sel, eidx, n_e), axis=0, keepdims=True)
    e_hi = jnp.max(jnp.where(sel, eidx, -1), axis=0, keepdims=True)
    is_lo = eidx == e_lo
    is_hi = eidx == e_hi
    w = w_ref[...]
    pos_a = jnp.sum(jnp.where(is_lo, pos, 0.0), axis=0, keepdims=True)
    pos_b = jnp.sum(jnp.where(is_hi, pos, 0.0), axis=0, keepdims=True)
    w_a = jnp.sum(jnp.where(is_lo, w, 0.0), axis=0, keepdims=True)
    w_b = jnp.sum(jnp.where(is_hi, w, 0.0), axis=0, keepdims=True)
    pos_ref[...] = jnp.where(eidx == 0, pos_a, jnp.where(eidx == 1, pos_b, 0.0)).astype(jnp.int32)
    wt_ref[...] = jnp.where(eidx == 0, w_a, jnp.where(eidx == 1, w_b, 0.0)).T
    tile_lo = lax.broadcasted_iota(jnp.int32, (n_e, LANES), 1).astype(F32) * tile
    t_exp = jnp.sum(jnp.where(start + padded <= tile_lo, 1.0, 0.0), axis=0, keepdims=True)
    t_exp = jnp.minimum(t_exp, n_e - 1.0)
    r8 = lax.broadcasted_iota(jnp.int32, (n_e, LANES), 0)
    meta_ref[...] = jnp.where(r8 == 0, t_exp, jnp.where(r8 == 1, run / tile, 0.0)).astype(jnp.int32)


def moe_route(sel, w, tile):
    n_e, m = sel.shape
    ck = 256
    tri = jnp.asarray(np.triu(np.ones((ck, ck), np.float32)), BF16)
    full = lambda shape: pl.BlockSpec(shape, lambda: tuple(0 for _ in shape))
    return pl.pallas_call(
        functools.partial(_route_kernel, tile=tile),
        in_specs=[full((n_e, m)), full((n_e, m)), full((ck, ck))],
        out_specs=[full((n_e, m)), full((m, n_e)), full((n_e, LANES))],
        out_shape=[jax.ShapeDtypeStruct((n_e, m), jnp.int32),
                   jax.ShapeDtypeStruct((m, n_e), F32),
                   jax.ShapeDtypeStruct((n_e, LANES), jnp.int32)],
        scratch_shapes=[pltpu.VMEM((n_e, m), F32)],
        compiler_params=pltpu.CompilerParams(vmem_limit_bytes=VMEM_LIMIT),
        name="moe_route",
    )(sel, w, tri)


def _row_gather_start(src_hbm, dst, rows_ref, base, n_rows, sem, unroll=8):
    def body(r, carry):
        src_row = rows_ref[base + r]
        pltpu.make_async_copy(src_hbm.at[pl.ds(src_row, 1)], dst.at[pl.ds(r, 1)], sem).start()
        return carry
    lax.fori_loop(0, n_rows, body, 0, unroll=unroll)


def _row_gather_wait(src_hbm, dst, sem):
    pltpu.make_async_copy(src_hbm.at[pl.ds(0, dst.shape[0])], dst, sem).wait()


def _moe_gather_kernel(pos_ref, h_hbm, o_ref, tok_ref, buf_ref, sem_ref):
    i = pl.program_id(0)
    n = pl.num_programs(0)
    tile = o_ref.shape[0]
    m = pos_ref.shape[0] // 2

    @pl.when(i == 0)
    def _():
        def zero(r, carry):
            tok_ref[r] = 0
            return carry
        lax.fori_loop(0, tok_ref.shape[0], zero, 0, unroll=8)

        def fill(t, carry):
            tok_ref[pos_ref[t]] = t
            tok_ref[pos_ref[m + t]] = t
            return carry
        lax.fori_loop(0, m, fill, 0, unroll=8)
        _row_gather_start(h_hbm, buf_ref.at[0], tok_ref, 0, tile, sem_ref.at[0])

    @pl.when(i + 1 < n)
    def _():
        nxt = (i + 1) % 2
        _row_gather_start(h_hbm, buf_ref.at[nxt], tok_ref, (i + 1) * tile, tile, sem_ref.at[nxt])

    cur = i % 2
    _row_gather_wait(h_hbm, buf_ref.at[cur], sem_ref.at[cur])
    o_ref[...] = buf_ref[cur].astype(o_ref.dtype)


def moe_gather(pos2, h, n_tiles, tile):
    m, d = h.shape
    rows = n_tiles * tile
    return pl.pallas_call(
        _moe_gather_kernel,
        grid_spec=pltpu.PrefetchScalarGridSpec(
            num_scalar_prefetch=1,
            grid=(n_tiles,),
            in_specs=[pl.BlockSpec(memory_space=pl.ANY)],
            out_specs=pl.BlockSpec((tile, d), lambda i, pos: (i, 0)),
            scratch_shapes=[pltpu.SMEM((rows,), jnp.int32),
                            pltpu.VMEM((2, tile, d), F32),
                            pltpu.SemaphoreType.DMA((2,))]),
        out_shape=jax.ShapeDtypeStruct((rows, d), BF16),
        compiler_params=_params("arbitrary"),
        name="moe_gather",
    )(pos2, h)


def _moe_up_kernel(meta_ref, x_ref, w1_ref, w3_ref, o_ref, c1_ref, c3_ref):
    i = pl.program_id(1)
    prev = meta_ref[0, jnp.maximum(i - 1, 0)]
    changed = (i == 0) | (meta_ref[0, i] != prev)

    @pl.when(changed)
    def _():
        c1_ref[...] = w1_ref[...].astype(BF16)
        c3_ref[...] = w3_ref[...].astype(BF16)

    @pl.when(i < meta_ref[1, 0])
    def _():
        x = x_ref[...]
        a1 = _dot(x, c1_ref[...])
        a3 = _dot(x, c3_ref[...])
        o_ref[...] = (a1 * _sigmoid(a1) * a3).astype(o_ref.dtype)

    @pl.when(i >= meta_ref[1, 0])
    def _():
        o_ref[...] = jnp.zeros_like(o_ref)


def moe_up(meta, xs, w1, w3, tile, tn=512):
    rows, d = xs.shape
    f = w1.shape[2]
    n_tiles = rows // tile
    last = lambda i, meta: jnp.minimum(i, meta[1, 0] - 1)
    return pl.pallas_call(
        _moe_up_kernel,
        grid_spec=pltpu.PrefetchScalarGridSpec(
            num_scalar_prefetch=1,
            grid=(f // tn, n_tiles),
            in_specs=[pl.BlockSpec((tile, d), lambda j, i, meta: (last(i, meta), 0)),
                      pl.BlockSpec((None, d, tn), lambda j, i, meta: (meta[0, i], 0, j)),
                      pl.BlockSpec((None, d, tn), lambda j, i, meta: (meta[0, i], 0, j))],
            out_specs=pl.BlockSpec((tile, tn), lambda j, i, meta: (i, j)),
            scratch_shapes=[pltpu.VMEM((d, tn), BF16), pltpu.VMEM((d, tn), BF16)]),
        out_shape=jax.ShapeDtypeStruct((rows, f), BF16),
        compiler_params=_params("arbitrary", "arbitrary"),
        name="moe_up",
    )(meta, xs, w1, w3)


def _moe_down_kernel(meta_ref, a_ref, w_ref, o_ref):
    i = pl.program_id(0)
    k = pl.program_id(1)

    @pl.when(i < meta_ref[1, 0])
    def _():
        d = _dot(a_ref[...], w_ref[...].astype(BF16))

        @pl.when(k == 0)
        def _():
            o_ref[...] = d

        @pl.when(k > 0)
        def _():
            o_ref[...] += d

    @pl.when((i >= meta_ref[1, 0]) & (k == 0))
    def _():
        o_ref[...] = jnp.zeros_like(o_ref)


def moe_down(meta, up, w2, tile, tk=512):
    rows, f = up.shape
    d = w2.shape[2]
    n_tiles = rows // tile
    last = lambda i, meta: jnp.minimum(i, meta[1, 0] - 1)
    return pl.pallas_call(
        _moe_down_kernel,
        grid_spec=pltpu.PrefetchScalarGridSpec(
            num_scalar_prefetch=1,
            grid=(n_tiles, f // tk),
            in_specs=[pl.BlockSpec((tile, tk), lambda i, k, meta: (last(i, meta), k)),
                      pl.BlockSpec((None, tk, d), lambda i, k, meta: (meta[0, i], k, 0))],
            out_specs=pl.BlockSpec((tile, d), lambda i, k, meta: (i, 0))),
        out_shape=jax.ShapeDtypeStruct((rows, d), F32),
        compiler_params=_params("arbitrary", "arbitrary"),
        name="moe_down",
    )(meta, up, w2)


def _moe_combine_kernel(pos_ref, x_ref, wt_ref, y_hbm, o_ref, buf_ref, sem_ref):
    i = pl.program_id(0)
    n = pl.num_programs(0)
    tc = x_ref.shape[0]
    m = pos_ref.shape[0] // 2

    def start(step, slot):
        for s in range(2):
            _row_gather_start(y_hbm, buf_ref.at[slot, s], pos_ref, s * m + step * tc, tc, sem_ref.at[slot, s])

    @pl.when(i == 0)
    def _():
        start(0, 0)

    @pl.when(i + 1 < n)
    def _():
        start(i + 1, (i + 1) % 2)

    cur = i % 2
    for s in range(2):
        _row_gather_wait(y_hbm, buf_ref.at[cur, s], sem_ref.at[cur, s])
    wt = wt_ref[...]
    o_ref[...] = x_ref[...] + wt[:, 0:1] * buf_ref[cur, 0] + wt[:, 1:2] * buf_ref[cur, 1]


def moe_combine(pos2, x, wt, y, tc=256):
    m, d = x.shape
    return pl.pallas_call(
        _moe_combine_kernel,
        grid_spec=pltpu.PrefetchScalarGridSpec(
            num_scalar_prefetch=1,
            grid=(m // tc,),
            in_specs=[pl.BlockSpec((tc, d), lambda i, pos: (i, 0)),
                      pl.BlockSpec((tc, wt.shape[1]), lambda i, pos: (i, 0)),
                      pl.BlockSpec(memory_space=pl.ANY)],
            out_specs=pl.BlockSpec((tc, d), lambda i, pos: (i, 0)),
            scratch_shapes=[pltpu.VMEM((2, 2, tc, d), F32),
                            pltpu.SemaphoreType.DMA((2, 2))]),
        out_shape=jax.ShapeDtypeStruct((m, d), F32),
        compiler_params=_params("arbitrary"),
        name="moe_combine",
    )(pos2, x, wt, y)


def moe_layer(x, norm_g, router_w, router_b, w1, w3, w2):
    m, d = x.shape
    n_e = router_w.shape[1]
    tile = MOE_TILE
    n_tiles = 2 * m // tile + n_e
    h, sel, w = norm_router(x, norm_g, router_w, router_b)
    pos, wt, meta = moe_route(sel, w, tile)
    pos2 = pos[:2].reshape(-1)
    xs = moe_gather(pos2, h, n_tiles, tile)
    up = moe_up(meta, xs, w1, w3, tile)
    y = moe_down(meta, up, w2, tile)
    return moe_combine(pos2, x, wt, y)


def _mixer(x, batch, norm_g, w_in, pe_k, pe_v, kw1, kw2, vw1, vw2, ln_g, ln_b, w_s, b_s, w_a, w_b, w_out):
    h = rmsnorm(x, norm_g, BF16)
    q, kv, ng, rest = in_projection(h, w_in)
    cmp_kv = compress(kv, batch, pe_k, pe_v, kw1, kw2, vw1, vw2)
    oc, sel = compressed_and_select(q, cmp_kv, batch)
    oa = selected_window_attention(q, kv, sel, oc, ng, batch)
    ob = spatial_gating(rest, ln_g, ln_b, w_s, b_s)
    y = merge_branches(oa, ob, rest, w_a, w_b)
    return residual_matmul("out_proj", y, w_out, x)


def kernel(x, norm_mix, w_in, cmp_pe_k, cmp_pe_v, cmp_k_w1, cmp_k_w2, cmp_v_w1, cmp_v_w2, sgu_ln_g, sgu_ln_b, sgu_w, sgu_b, w_branch_a, w_branch_b, w_out, norm_ffn, ffn_w1, ffn_w3, ffn_w2, router_w, router_b, moe_w1, moe_w3, moe_w2, norm_f):
    batch, seq, d = x.shape
    depth = norm_mix.shape[0]
    xf = x.reshape(batch * seq, d)
    for layer in range(depth):
        xf = _mixer(xf, batch, norm_mix[layer], w_in[layer], cmp_pe_k[layer], cmp_pe_v[layer],
                    cmp_k_w1[layer], cmp_k_w2[layer], cmp_v_w1[layer], cmp_v_w2[layer],
                    sgu_ln_g[layer], sgu_ln_b[layer], sgu_w[layer], sgu_b[layer],
                    w_branch_a[layer], w_branch_b[layer], w_out[layer])
        j = layer // 2
        if layer % 2 == 0:
            h = rmsnorm(xf, norm_ffn[layer], BF16)
            up = swiglu_up(h, ffn_w1[j], ffn_w3[j])
            xf = mm_acc("ffn_down", up, ffn_w2[j], xf, 1024, 1024, 512)
        else:
            xf = moe_layer(xf, norm_ffn[layer], router_w[j], router_b[j], moe_w1[j], moe_w3[j], moe_w2[j])
    return rmsnorm(xf, norm_f, F32).reshape(batch, seq, d)
```

```python
import functools

import numpy as np
import jax
import jax.numpy as jnp
from jax import lax
from jax.experimental import pallas as pl
from jax.experimental.pallas import tpu as pltpu

F32 = jnp.float32
BF16 = jnp.bfloat16

D_MODEL = 2048
N_Q_HEADS = 16
N_KV_HEADS = 4
HEAD_DIM = 64
Q_PER_KV = N_Q_HEADS // N_KV_HEADS
NSA_WIDTH = N_Q_HEADS * HEAD_DIM
KV_WIDTH = N_KV_HEADS * HEAD_DIM
GROUP_WIDTH = Q_PER_KV * HEAD_DIM
CMP_BLOCK = 32
CMP_STRIDE = 16
CMP_HIDDEN = 256
SLC_BLOCK = 64
N_SELECT = 16
WINDOW = 512
FORCED_SCORE = 1e4
SGU_WIDTH = D_MODEL // 2
SGU_GROUPS = 8
SGU_CHUNK = 128
N_EXPERTS = 8
NORM_EPS = 1e-5
NEG = -1e30
LANES = 128
VMEM_LIMIT = 56 * 1024 * 1024

Q_OFF = 0
KV_OFF = NSA_WIDTH
NG_OFF = KV_OFF + 6 * KV_WIDTH
NG_WIDTH = 3 * N_Q_HEADS
REST_OFF = NG_OFF + NG_WIDTH


def _params(*sem):
    return pltpu.CompilerParams(dimension_semantics=sem, vmem_limit_bytes=VMEM_LIMIT)


def _dot(a, b):
    return jnp.dot(a, b, preferred_element_type=F32)


def _dot_nt(a, b):
    return lax.dot_general(a, b, (((1,), (1,)), ((), ())), preferred_element_type=F32)


def _dot_split(a_f32, b_bf16):
    hi = a_f32.astype(BF16)
    lo = (a_f32 - hi.astype(F32)).astype(BF16)
    return _dot(hi, b_bf16) + _dot(lo, b_bf16)


def _sigmoid(x):
    return 1.0 / (1.0 + jnp.exp(-x))


def _gelu(x):
    return x * (0.5 * (1.0 + jnp.tanh(0.7978845608028654 * (x + 0.044715 * (x * x * x)))))


def _rms(x, g):
    return x * lax.rsqrt(jnp.mean(x * x, axis=-1, keepdims=True) + NORM_EPS) * g


def _rmsnorm_kernel(x_ref, g_ref, o_ref):
    o_ref[...] = _rms(x_ref[...], g_ref[...]).astype(o_ref.dtype)


def rmsnorm(x, g, out_dtype, tm=512):
    m, d = x.shape
    return pl.pallas_call(
        _rmsnorm_kernel,
        grid=(m // tm,),
        in_specs=[pl.BlockSpec((tm, d), lambda i: (i, 0)),
                  pl.BlockSpec((1, d), lambda i: (0, 0))],
        out_specs=pl.BlockSpec((tm, d), lambda i: (i, 0)),
        out_shape=jax.ShapeDtypeStruct((m, d), out_dtype),
        compiler_params=_params("parallel"),
        name="rmsnorm",
    )(x, g.reshape(1, d))


def _norm_router_kernel(x_ref, g_ref, rwt_ref, rb_ref, h_ref, sel_ref, w_ref):
    h = _rms(x_ref[...], g_ref[...])
    h_ref[...] = h
    logits = lax.dot_general(rwt_ref[...], h, (((1,), (1,)), ((), ())), preferred_element_type=F32,
                             precision=lax.Precision.HIGHEST) + rb_ref[...]
    n_e = logits.shape[0]
    e = lax.broadcasted_iota(jnp.int32, logits.shape, 0)
    m1 = jnp.max(logits, axis=0, keepdims=True)
    i1 = jnp.min(jnp.where(logits == m1, e, n_e), axis=0, keepdims=True)
    rest = jnp.where(e == i1, -jnp.inf, logits)
    m2 = jnp.max(rest, axis=0, keepdims=True)
    i2 = jnp.min(jnp.where(rest == m2, e, n_e), axis=0, keepdims=True)
    e2 = jnp.exp(m2 - m1)
    w1 = 1.0 / (1.0 + e2)
    w2 = e2 / (1.0 + e2)
    sel_ref[...] = jnp.where((e == i1) | (e == i2), 1.0, 0.0)
    w_ref[...] = jnp.where(e == i1, w1, 0.0) + jnp.where(e == i2, w2, 0.0)


def norm_router(x, g, router_w, router_b, tm=512):
    m, d = x.shape
    n_e = router_w.shape[1]
    return pl.pallas_call(
        _norm_router_kernel,
        grid=(m // tm,),
        in_specs=[pl.BlockSpec((tm, d), lambda i: (i, 0)),
                  pl.BlockSpec((1, d), lambda i: (0, 0)),
                  pl.BlockSpec((n_e, d), lambda i: (0, 0)),
                  pl.BlockSpec((n_e, 1), lambda i: (0, 0))],
        out_specs=[pl.BlockSpec((tm, d), lambda i: (i, 0)),
                   pl.BlockSpec((n_e, tm), lambda i: (0, i)),
                   pl.BlockSpec((n_e, tm), lambda i: (0, i))],
        out_shape=[jax.ShapeDtypeStruct((m, d), F32),
                   jax.ShapeDtypeStruct((n_e, m), F32),
                   jax.ShapeDtypeStruct((n_e, m), F32)],
        compiler_params=_params("parallel"),
        name="norm_router",
    )(x, g.reshape(1, d), router_w.T, router_b.reshape(n_e, 1))


def _mm_full_kernel(*refs, n_a, n_b, n_e, n_o, pairs, inner_axis, epilogue):
    a_refs = refs[:n_a]
    b_refs = refs[n_a:n_a + n_b]
    e_refs = refs[n_a + n_b:n_a + n_b + n_e]
    o_refs = refs[n_a + n_b + n_e:n_a + n_b + n_e + n_o]
    w_refs = refs[n_a + n_b + n_e + n_o:]

    @pl.when(pl.program_id(inner_axis) == 0)
    def _():
        for b_ref, w_ref in zip(b_refs, w_refs):
            w_ref[...] = b_ref[...].astype(BF16)

    accs = [_dot(a_refs[ia][...], w_refs[ib][...]) for ia, ib in pairs]
    epilogue(accs, e_refs, o_refs)


def mm_full(name, grid, a_ops, b_ops, e_ops, outs, pairs, epilogue):
    ops = a_ops + b_ops + e_ops
    kernel = functools.partial(
        _mm_full_kernel, n_a=len(a_ops), n_b=len(b_ops), n_e=len(e_ops), n_o=len(outs),
        pairs=pairs, inner_axis=len(grid) - 1, epilogue=epilogue)
    scratch = [pltpu.VMEM(tuple(s for s in blk if s is not None), BF16) for _, blk, _ in b_ops]
    res = pl.pallas_call(
        kernel,
        grid=grid,
        in_specs=[pl.BlockSpec(blk, imap) for _, blk, imap in ops],
        out_specs=[pl.BlockSpec(blk, imap) for _, blk, imap in outs],
        out_shape=[sds for sds, _, _ in outs],
        scratch_shapes=scratch,
        compiler_params=_params(*(("arbitrary",) * len(grid))),
        name=name,
    )(*[arr for arr, _, _ in ops])
    return res


def _mm_acc_kernel(a_ref, b_ref, r_ref, o_ref):
    d = _dot(a_ref[...], b_ref[...].astype(BF16))

    @pl.when(pl.program_id(2) == 0)
    def _():
        o_ref[...] = r_ref[...] + d

    @pl.when(pl.program_id(2) > 0)
    def _():
        o_ref[...] += d


def mm_acc(name, a, b, res, tm, tn, tk):
    m, k = a.shape
    n = b.shape[1]
    return pl.pallas_call(
        _mm_acc_kernel,
        grid=(m // tm, n // tn, k // tk),
        in_specs=[pl.BlockSpec((tm, tk), lambda i, j, kk: (i, kk)),
                  pl.BlockSpec((tk, tn), lambda i, j, kk: (kk, j)),
                  pl.BlockSpec((tm, tn), lambda i, j, kk: (i, j))],
        out_specs=pl.BlockSpec((tm, tn), lambda i, j, kk: (i, j)),
        out_shape=jax.ShapeDtypeStruct((m, n), F32),
        compiler_params=_params("parallel", "parallel", "arbitrary"),
        name=name,
    )(a, b, res)


def in_projection(h, w_in):
    m, d = h.shape
    tm = 1024
    a_op = [(h, (tm, d), lambda j, i: (i, 0))]

    tq = 512
    q_heads = tq // HEAD_DIM

    def q_epi(accs, e_refs, o_refs):
        for c in range(q_heads):
            o_refs[0][c] = (accs[0][:, c * HEAD_DIM:(c + 1) * HEAD_DIM] * (HEAD_DIM ** -0.5)).T.astype(BF16)

    (qt,) = mm_full(
        "in_proj_q", (NSA_WIDTH // tq, m // tm), a_op,
        [(w_in, (d, tq), lambda j, i: (0, j))], [],
        [(jax.ShapeDtypeStruct((N_Q_HEADS, HEAD_DIM, m), BF16), (q_heads, HEAD_DIM, tm), lambda j, i: (j, 0, i))],
        [(0, 0)], q_epi)

    sec0 = KV_OFF // KV_WIDTH

    def rows_epi(accs, e_refs, o_refs):
        for c in range(N_KV_HEADS):
            o_refs[0][c] = accs[0][:, c * HEAD_DIM:(c + 1) * HEAD_DIM].astype(BF16)

    (kvr,) = mm_full(
        "in_proj_kv_rows", (4, m // tm), a_op,
        [(w_in, (d, KV_WIDTH), lambda j, i: (0, sec0 + j + j // 3))], [],
        [(jax.ShapeDtypeStruct((4 * N_KV_HEADS, m, HEAD_DIM), BF16),
          (N_KV_HEADS, tm, HEAD_DIM), lambda j, i: (j, i, 0))],
        [(0, 0)], rows_epi)

    def cols_epi(accs, e_refs, o_refs):
        for c in range(N_KV_HEADS):
            o_refs[0][c] = accs[0][:, c * HEAD_DIM:(c + 1) * HEAD_DIM].T.astype(BF16)

    (vt,) = mm_full(
        "in_proj_v_cols", (2, m // tm), a_op,
        [(w_in, (d, KV_WIDTH), lambda j, i: (0, sec0 + 3 + 2 * j))], [],
        [(jax.ShapeDtypeStruct((2 * N_KV_HEADS, HEAD_DIM, m), BF16),
          (N_KV_HEADS, HEAD_DIM, tm), lambda j, i: (j, 0, i))],
        [(0, 0)], cols_epi)

    def f32_epi(accs, e_refs, o_refs):
        o_refs[0][...] = accs[0]

    def f32_t_epi(accs, e_refs, o_refs):
        o_refs[0][...] = accs[0].T

    w_ng = jnp.pad(w_in[:, NG_OFF:REST_OFF], ((0, 0), (0, LANES - NG_WIDTH)))
    (ngt,) = mm_full(
        "in_proj_ng", (1, m // tm), a_op,
        [(w_ng, (d, LANES), lambda j, i: (0, 0))], [],
        [(jax.ShapeDtypeStruct((LANES, m), F32), (LANES, tm), lambda j, i: (0, i))],
        [(0, 0)], f32_t_epi)

    w_rest = w_in[:, REST_OFF:]
    n_rest = w_rest.shape[1]
    tr = 512
    (rest,) = mm_full(
        "in_proj_rest", (n_rest // tr, m // tm), a_op,
        [(w_rest, (d, tr), lambda j, i: (0, j))], [],
        [(jax.ShapeDtypeStruct((m, n_rest), F32), (tm, tr), lambda j, i: (i, j))],
        [(0, 0)], f32_epi)
    return qt, kvr, vt, ngt, rest


def _cmp_kernel(a_ref, w1_ref, w2_ref, pe_ref, o_ref):
    a = a_ref[...]
    rows = a.shape[0]
    half = a.shape[1]
    w1 = w1_ref[...].astype(BF16)
    p0 = _dot(a, w1[:half])
    p1 = _dot(a, w1[half:])
    bias = _dot(pe_ref[...].astype(BF16), w1)[0:1]
    hid = p0 + pltpu.roll(p1, rows - 1, 0) + bias
    o_ref[...] = _dot(_gelu(hid).astype(BF16), w2_ref[...].astype(BF16))


def compress(kv, batch, pe_k, pe_v, kw1, kw2, vw1, vw2):
    m = kv.shape[1]
    n_chunks = m // batch // CMP_STRIDE
    rows = batch * n_chunks
    feat = CMP_STRIDE * HEAD_DIM
    a = kv[:2 * N_KV_HEADS].reshape(2, N_KV_HEADS, rows, feat)
    w1 = jnp.stack([kw1, vw1])
    w2 = jnp.stack([kw2, vw2])
    pe = jnp.stack([pe_k, pe_v]).reshape(2, 1, CMP_BLOCK * HEAD_DIM)
    pe = jnp.broadcast_to(pe, (2, 8, CMP_BLOCK * HEAD_DIM))
    return pl.pallas_call(
        _cmp_kernel,
        grid=(2, N_KV_HEADS),
        in_specs=[pl.BlockSpec((None, None, rows, feat), lambda s, g: (s, g, 0, 0)),
                  pl.BlockSpec((None, 2 * feat, CMP_HIDDEN), lambda s, g: (s, 0, 0)),
                  pl.BlockSpec((None, CMP_HIDDEN, HEAD_DIM), lambda s, g: (s, 0, 0)),
                  pl.BlockSpec((None, 8, 2 * feat), lambda s, g: (s, 0, 0))],
        out_specs=pl.BlockSpec((None, None, rows, HEAD_DIM), lambda s, g: (s, g, 0, 0)),
        out_shape=jax.ShapeDtypeStruct((2, N_KV_HEADS, rows, HEAD_DIM), F32),
        compiler_params=_params("parallel", "parallel"),
        name="nsa_compress",
    )(a, w1, w2, pe)


def _alibi_slopes():
    return np.array([2.0 ** (-8.0 * (h + 1) / N_Q_HEADS) for h in range(N_Q_HEADS)], dtype=np.float32)


def _overlap_t(n_cmp_pad, n_slc):
    cs = np.arange(n_cmp_pad)[None, :] * CMP_STRIDE
    ss = np.arange(n_slc)[:, None] * SLC_BLOCK
    ov = np.clip(np.minimum(cs + CMP_BLOCK, ss + SLC_BLOCK) - np.maximum(cs, ss), 0, None)
    return (ov / CMP_STRIDE).astype(np.float32)


def _sel_kernel(slopes_ref, q_ref, kc_ref, vc_ref, ovt_ref, oc_ref, sel_ref, *, n_cmp):
    g = pl.program_id(1)
    i = pl.program_id(2)
    tq = q_ref.shape[2]
    ncp = kc_ref.shape[0]
    n_slc = ovt_ref.shape[0]
    kc = kc_ref[...].astype(BF16)
    vct = vc_ref[...].T.astype(BF16)

    t = i * tq + lax.broadcasted_iota(jnp.int32, (ncp, tq), 1)
    c = lax.broadcasted_iota(jnp.int32, (ncp, tq), 0)
    dist = t - (c * CMP_STRIDE + (CMP_BLOCK - 1))
    mask = (dist >= 0) & (c < n_cmp)
    distf = dist.astype(F32)

    p_sum = jnp.zeros((ncp, tq), F32)
    for j in range(Q_PER_KV):
        s = _dot(kc, q_ref[j])
        s = s - slopes_ref[g * Q_PER_KV + j] * distf
        s = jnp.where(mask, s, NEG)
        mx = jnp.max(s, axis=0, keepdims=True)
        e = jnp.where(mask, jnp.exp(s - mx), 0.0)
        p = e / jnp.maximum(jnp.sum(e, axis=0, keepdims=True), 1e-30)
        p_sum = p_sum + p
        oc_ref[j] = _dot(vct, p.astype(BF16))

    ovt = ovt_ref[...]
    hi = p_sum.astype(BF16)
    lo = (p_sum - hi.astype(F32)).astype(BF16)
    imp = _dot(ovt, hi) + _dot(ovt, lo)
    tt = i * tq + lax.broadcasted_iota(jnp.int32, (n_slc, tq), 1)
    blk = lax.broadcasted_iota(jnp.int32, (n_slc, tq), 0)
    cur = tt // SLC_BLOCK
    valid = blk * SLC_BLOCK <= tt
    forced = (blk == 0) | (blk == cur) | (blk == cur - 1)
    score = jnp.where(valid, imp, -1.0)
    score = jnp.where(forced, FORCED_SCORE, score)
    rank = jnp.zeros((n_slc, tq), F32)
    for mrow in range(n_slc):
        other = jnp.broadcast_to(score[mrow:mrow + 1, :], (n_slc, tq))
        beats = (other > score) | ((other == score) & (blk > mrow))
        rank = rank + jnp.where(beats, 1.0, 0.0)
    sel_ref[...] = jnp.where(rank < float(min(N_SELECT, n_slc)), 1.0, 0.0)


def compressed_and_select(qt, cmp_kv, batch, tq=256):
    m = qt.shape[2]
    seq = m // batch
    nq = seq // tq
    ncp = cmp_kv.shape[2] // batch
    n_slc = seq // SLC_BLOCK
    ovt = jnp.asarray(_overlap_t(ncp, n_slc), BF16)
    slopes = jnp.asarray(_alibi_slopes())
    kernel = functools.partial(_sel_kernel, n_cmp=ncp - 1)
    return pl.pallas_call(
        kernel,
        grid=(batch, N_KV_HEADS, nq),
        in_specs=[pl.BlockSpec(memory_space=pltpu.SMEM),
                  pl.BlockSpec((Q_PER_KV, HEAD_DIM, tq), lambda b, g, i: (g, 0, b * nq + i)),
                  pl.BlockSpec((None, None, ncp, HEAD_DIM), lambda b, g, i: (0, g, b, 0)),
                  pl.BlockSpec((None, None, ncp, HEAD_DIM), lambda b, g, i: (1, g, b, 0)),
                  pl.BlockSpec((n_slc, ncp), lambda b, g, i: (0, 0))],
        out_specs=[pl.BlockSpec((Q_PER_KV, HEAD_DIM, tq), lambda b, g, i: (g, 0, b * nq + i)),
                   pl.BlockSpec((None, None, n_slc, tq), lambda b, g, i: (b, g, 0, i))],
        out_shape=[jax.ShapeDtypeStruct((N_Q_HEADS, HEAD_DIM, m), F32),
                   jax.ShapeDtypeStruct((batch, N_KV_HEADS, n_slc, seq), F32)],
        compiler_params=_params("parallel", "parallel", "parallel"),
        name="nsa_compressed_select",
    )(slopes, qt, cmp_kv, cmp_kv, ovt)


AUX_SLOPE = HEAD_DIM
AUX_SEL = HEAD_DIM + 16
ATTN_K = 2 * HEAD_DIM
STRIP = 32


def _attn_kernel(slopes_ref, q_ref, ks_ref, vs_ref, kw_ref, vw_ref, auxs_ref, auxw_ref, sel_ref, oc_ref,
                 ng_ref, o_ref, ka_s_ref, ka_w_ref, qa_ref, s_ref, p_ref, mask_ref, m_ref, l_ref, acc_ref,
                 os_ref, sig_ref):
    g = pl.program_id(1)
    i = pl.program_id(2)
    tq = q_ref.shape[2]
    tk = tq
    q0 = i * tq
    n_slc = sel_ref.shape[0]

    @pl.when(i == 0)
    def _():
        ka_s_ref[:, 0:HEAD_DIM] = ks_ref[...]
        ka_s_ref[:, HEAD_DIM:ATTN_K] = auxs_ref[...]
        ka_w_ref[:, 0:HEAD_DIM] = kw_ref[...]
        ka_w_ref[:, HEAD_DIM:ATTN_K] = auxw_ref[...]
        kr = lax.broadcasted_iota(jnp.int32, (tk, tq), 0)
        qc = lax.broadcasted_iota(jnp.int32, (tk, tq), 1)
        mask_ref[0] = jnp.where(kr <= qc, 0.0, NEG)
        mask_ref[1] = jnp.where(kr > qc, 0.0, NEG)

    sig_ref[...] = _sigmoid(ng_ref[...])
    sel_neg = (sel_ref[...] - 1.0) * (-NEG)
    row16 = lax.broadcasted_iota(jnp.int32, (16, tq), 0)
    pad = jnp.zeros((ATTN_K - AUX_SEL - n_slc, tq), F32)
    for j in range(Q_PER_KV):
        sl = jnp.full((16, tq), slopes_ref[g * Q_PER_KV + j], F32)
        hi = sl.astype(BF16).astype(F32)
        mid = (sl - hi).astype(BF16).astype(F32)
        lo = (sl - hi - mid).astype(BF16).astype(F32)
        pieces = jnp.where(row16 == 0, hi, jnp.where(row16 == 1, mid, jnp.where(row16 == 2, lo, 0.0)))
        qa = jnp.concatenate([q_ref[j].astype(F32), pieces, sel_neg, pad], axis=0)
        qa_ref[j] = qa.astype(BF16)

    def chunk(ka_ref, v_ref, k0, mask_idx, first):
        ka = ka_ref[pl.ds(k0, tk), :]
        vc = v_ref[:, pl.ds(k0, tk)]
        for j in range(Q_PER_KV):
            s_ref[j] = _dot(ka, qa_ref[j])
        for j in range(Q_PER_KV):
            shift = slopes_ref[g * Q_PER_KV + j] * k0.astype(F32)
            mx = None
            for r in range(0, tk, STRIP):
                x = s_ref[j, r:r + STRIP, :]
                if mask_idx is not None:
                    x = x + mask_ref[mask_idx, r:r + STRIP, :]
                    s_ref[j, r:r + STRIP, :] = x
                mx = x if mx is None else jnp.maximum(mx, x)
            m_cur = jnp.max(mx, axis=0, keepdims=True) + shift
            if first:
                m_new = m_cur
            else:
                m_old = m_ref[j]
                m_new = jnp.maximum(m_old, m_cur)
                alpha = jnp.exp(m_old - m_new)
            m_ref[j] = m_new
            m_loc = m_new - shift
            ls = None
            for r in range(0, tk, STRIP):
                p = jnp.exp(s_ref[j, r:r + STRIP, :] - m_loc)
                ls = p if ls is None else ls + p
                p_ref[j, r:r + STRIP, :] = p.astype(BF16)
            l_cur = jnp.sum(ls, axis=0, keepdims=True)
            pv = _dot(vc, p_ref[j])
            if first:
                l_ref[j] = l_cur
                acc_ref[j] = pv
            else:
                l_ref[j] = alpha * l_ref[j] + l_cur
                acc_ref[j] = alpha * acc_ref[j] + pv

    k_diag = pl.multiple_of(q0, tk)
    chunk(ka_s_ref, vs_ref, k_diag, 0, True)

    def slc_body(c, carry):
        chunk(ka_s_ref, vs_ref, pl.multiple_of(c * tk, tk), None, False)
        return carry

    lax.fori_loop(0, i, slc_body, 0)
    for j in range(Q_PER_KV):
        os_ref[j] = acc_ref[j] / l_ref[j]

    chunk(ka_w_ref, vw_ref, k_diag, 0, True)
    n_back = WINDOW // tk
    for back in range(1, n_back + 1):
        @pl.when(i >= back)
        def _(back=back):
            k0 = pl.multiple_of(q0 - back * tk, tk)
            chunk(ka_w_ref, vw_ref, k0, 1 if back == n_back else None, False)

    for j in range(Q_PER_KV):
        base = (g * Q_PER_KV + j) * 3
        out_t = (sig_ref[pl.ds(base, 1), :] * oc_ref[j]
                 + sig_ref[pl.ds(base + 1, 1), :] * os_ref[j]
                 + sig_ref[pl.ds(base + 2, 1), :] * (acc_ref[j] / l_ref[j]))
        o_ref[:, j * HEAD_DIM:(j + 1) * HEAD_DIM] = out_t.T.astype(o_ref.dtype)


def _key_aux(seq, tk, with_blocks):
    aux = np.zeros((seq, ATTN_K - HEAD_DIM), np.float32)
    pos = np.arange(seq)
    aux[:, AUX_SLOPE - HEAD_DIM:AUX_SLOPE - HEAD_DIM + 3] = (pos % tk)[:, None]
    if with_blocks:
        aux[pos, AUX_SEL - HEAD_DIM + pos // SLC_BLOCK] = 1.0
    return aux


def selected_window_attention(qt, kvr, vt, sel, oct, ngt, batch, tq=256):
    m = qt.shape[2]
    seq = m // batch
    nq = seq // tq
    n_slc = seq // SLC_BLOCK
    assert tq <= 256 and AUX_SEL + n_slc <= ATTN_K
    aux_s = jnp.asarray(_key_aux(seq, tq, True), BF16)
    aux_w = jnp.asarray(_key_aux(seq, tq, False), BF16)
    slopes = jnp.asarray(_alibi_slopes())
    n_g = N_KV_HEADS

    def k_spec(section):
        return pl.BlockSpec((None, seq, HEAD_DIM), lambda b, g, i: (section * n_g + g, b, 0))

    def v_spec(section):
        return pl.BlockSpec((None, HEAD_DIM, seq), lambda b, g, i: (section * n_g + g, 0, b))

    head_blk = pl.BlockSpec((Q_PER_KV, HEAD_DIM, tq), lambda b, g, i: (g, 0, b * nq + i))
    aux_blk = pl.BlockSpec((seq, ATTN_K - HEAD_DIM), lambda b, g, i: (0, 0))
    return pl.pallas_call(
        _attn_kernel,
        grid=(batch, N_KV_HEADS, nq),
        in_specs=[pl.BlockSpec(memory_space=pltpu.SMEM),
                  head_blk,
                  k_spec(2), v_spec(0), k_spec(3), v_spec(1),
                  aux_blk, aux_blk,
                  pl.BlockSpec((None, None, n_slc, tq), lambda b, g, i: (b, g, 0, i)),
                  head_blk,
                  pl.BlockSpec((LANES, tq), lambda b, g, i: (0, b * nq + i))],
        out_specs=pl.BlockSpec((tq, GROUP_WIDTH), lambda b, g, i: (b * nq + i, g)),
        out_shape=jax.ShapeDtypeStruct((m, NSA_WIDTH), BF16),
        scratch_shapes=[pltpu.VMEM((seq, ATTN_K), BF16),
                        pltpu.VMEM((seq, ATTN_K), BF16),
                        pltpu.VMEM((Q_PER_KV, ATTN_K, tq), BF16),
                        pltpu.VMEM((Q_PER_KV, tq, tq), F32),
                        pltpu.VMEM((Q_PER_KV, tq, tq), BF16),
                        pltpu.VMEM((2, tq, tq), F32),
                        pltpu.VMEM((Q_PER_KV, 1, tq), F32),
                        pltpu.VMEM((Q_PER_KV, 1, tq), F32),
                        pltpu.VMEM((Q_PER_KV, HEAD_DIM, tq), F32),
                        pltpu.VMEM((Q_PER_KV, HEAD_DIM, tq), F32),
                        pltpu.VMEM((LANES, tq), F32)],
        compiler_params=_params("arbitrary", "arbitrary", "arbitrary"),
        name="nsa_selected_window",
    )(slopes, qt, kvr, vt, kvr, vt, aux_s, aux_w, sel, oct, ngt)


def _sgu_kernel(u_ref, v_ref, lg_ref, lb_ref, ws_ref, bs_ref, o_ref):
    rows = u_ref.shape[0]
    v = _gelu(v_ref[...])
    mu = jnp.mean(v, axis=-1, keepdims=True)
    var = jnp.mean(jnp.square(v - mu), axis=-1, keepdims=True)
    vl = ((v - mu) * lax.rsqrt(var + NORM_EPS) * lg_ref[...] + lb_ref[...]).astype(BF16)
    r = lax.broadcasted_iota(jnp.int32, (SGU_CHUNK, SGU_CHUNK), 0)
    c = lax.broadcasted_iota(jnp.int32, (SGU_CHUNK, SGU_CHUNK), 1)
    gd = SGU_WIDTH // SGU_GROUPS
    for grp in range(SGU_GROUPS):
        w = jnp.where(c <= r, ws_ref[grp], 0.0).astype(BF16)
        lanes = slice(grp * gd, (grp + 1) * gd)
        for n in range(rows // SGU_CHUNK):
            rs = slice(n * SGU_CHUNK, (n + 1) * SGU_CHUNK)
            vm = _dot(w, vl[rs, lanes]) + bs_ref[:, lanes]
            o_ref[rs, lanes] = (_gelu(u_ref[rs, lanes]) * vm).astype(o_ref.dtype)


def spatial_gating(rest, ln_g, ln_b, w_s, b_s, tm=512):
    m = rest.shape[0]
    gd = SGU_WIDTH // SGU_GROUPS
    bias = jnp.repeat(b_s.T, gd, axis=1)
    return pl.pallas_call(
        _sgu_kernel,
        grid=(m // tm,),
        in_specs=[pl.BlockSpec((tm, SGU_WIDTH), lambda i: (i, 0)),
                  pl.BlockSpec((tm, SGU_WIDTH), lambda i: (i, 1)),
                  pl.BlockSpec((1, SGU_WIDTH), lambda i: (0, 0)),
                  pl.BlockSpec((1, SGU_WIDTH), lambda i: (0, 0)),
                  pl.BlockSpec((SGU_GROUPS, SGU_CHUNK, SGU_CHUNK), lambda i: (0, 0, 0)),
                  pl.BlockSpec((SGU_CHUNK, SGU_WIDTH), lambda i: (0, 0))],
        out_specs=pl.BlockSpec((tm, SGU_WIDTH), lambda i: (i, 0)),
        out_shape=jax.ShapeDtypeStruct((m, SGU_WIDTH), BF16),
        compiler_params=_params("parallel"),
        name="spatial_gating",
    )(rest, rest, ln_g.reshape(1, -1), ln_b.reshape(1, -1), w_s, bias)


def merge_branches(oa, ob, rest, w_a, w_b, tm=1024, tn=512):
    m = oa.shape[0]
    d = w_a.shape[1]
    ga_blk = 2 * SGU_WIDTH // tn
    gb_blk = (2 * SGU_WIDTH + d) // tn

    def epi(accs, e_refs, o_refs):
        y = _sigmoid(e_refs[0][...]) * accs[0] + _sigmoid(e_refs[1][...]) * accs[1]
        o_refs[0][...] = y.astype(BF16)

    (y,) = mm_full(
        "merge_branches", (d // tn, m // tm),
        [(oa, (tm, oa.shape[1]), lambda j, i: (i, 0)), (ob, (tm, ob.shape[1]), lambda j, i: (i, 0))],
        [(w_a, (w_a.shape[0], tn), lambda j, i: (0, j)), (w_b, (w_b.shape[0], tn), lambda j, i: (0, j))],
        [(rest, (tm, tn), lambda j, i: (i, ga_blk + j)), (rest, (tm, tn), lambda j, i: (i, gb_blk + j))],
        [(jax.ShapeDtypeStruct((m, d), BF16), (tm, tn), lambda j, i: (i, j))],
        [(0, 0), (1, 1)], epi)
    return y


def residual_matmul(name, a, w, x, tm=1024, tn=512):
    m, k = a.shape
    n = w.shape[1]

    def epi(accs, e_refs, o_refs):
        o_refs[0][...] = e_refs[0][...] + accs[0]

    (out,) = mm_full(
        name, (n // tn, m // tm),
        [(a, (tm, k), lambda j, i: (i, 0))],
        [(w, (k, tn), lambda j, i: (0, j))],
        [(x, (tm, tn), lambda j, i: (i, j))],
        [(jax.ShapeDtypeStruct((m, n), F32), (tm, tn), lambda j, i: (i, j))],
        [(0, 0)], epi)
    return out


def swiglu_up(h, w1, w3, tm=1024, tn=512):
    m, d = h.shape
    f = w1.shape[1]

    def epi(accs, e_refs, o_refs):
        o_refs[0][...] = (accs[0] * _sigmoid(accs[0]) * accs[1]).astype(BF16)

    (out,) = mm_full(
        "swiglu_up", (f // tn, m // tm),
        [(h, (tm, d), lambda j, i: (i, 0))],
        [(w1, (d, tn), lambda j, i: (0, j)), (w3, (d, tn), lambda j, i: (0, j))],
        [],
        [(jax.ShapeDtypeStruct((m, f), BF16), (tm, tn), lambda j, i: (i, j))],
        [(0, 0), (0, 1)], epi)
    return out


MOE_TILE = 512


def _route_kernel(sel_ref, w_ref, tri_ref, pos_ref, wt_ref, meta_ref, cum_ref, *, tile):
    n_e, m = sel_ref.shape
    ck = tri_ref.shape[0]
    carry = jnp.zeros((n_e, 1), F32)
    for c in range(m // ck):
        sl = slice(c * ck, (c + 1) * ck)
        cs = _dot(sel_ref[:, sl].astype(BF16), tri_ref[...]) + carry
        cum_ref[:, sl] = cs
        carry = cs[:, ck - 1:ck]
    padded = jnp.ceil(carry / tile) * tile
    sub = lax.broadcasted_iota(jnp.int32, (n_e, 1), 0)
    start = jnp.zeros((n_e, 1), F32)
    run = jnp.zeros((1, 1), F32)
    for ex in range(n_e):
        start = jnp.where(sub == ex, run, start)
        run = run + padded[ex:ex + 1, :]
    sel = sel_ref[...] > 0.5
    pos = start + cum_ref[...] - 1.0
    eidx = lax.broadcasted_iota(jnp.int32, (n_e, m), 0)
    e_lo = jnp.min(jnp.where(sel, eidx, n_e), axis=0, keepdims=True)
    e_hi = jnp.max(jnp.where(sel, eidx, -1), axis=0, keepdims=True)
    is_lo = eidx == e_lo
    is_hi = eidx == e_hi
    w = w_ref[...]
    pos_a = jnp.sum(jnp.where(is_lo, pos, 0.0), axis=0, keepdims=True)
    pos_b = jnp.sum(jnp.where(is_hi, pos, 0.0), axis=0, keepdims=True)
    w_a = jnp.sum(jnp.where(is_lo, w, 0.0), axis=0, keepdims=True)
    w_b = jnp.sum(jnp.where(is_hi, w, 0.0), axis=0, keepdims=True)
    pos_ref[...] = jnp.where(eidx == 0, pos_a, jnp.where(eidx == 1, pos_b, 0.0)).astype(jnp.int32)
    wt_ref[...] = jnp.where(eidx == 0, w_a, jnp.where(eidx == 1, w_b, 0.0)).T
    tile_lo = lax.broadcasted_iota(jnp.int32, (n_e, LANES), 1).astype(F32) * tile
    t_exp = jnp.sum(jnp.where(start + padded <= tile_lo, 1.0, 0.0), axis=0, keepdims=True)
    t_exp = jnp.minimum(t_exp, n_e - 1.0)
    r8 = lax.broadcasted_iota(jnp.int32, (n_e, LANES), 0)
    meta_ref[...] = jnp.where(r8 == 0, t_exp, jnp.where(r8 == 1, run / tile, 0.0)).astype(jnp.int32)


def moe_route(sel, w, tile):
    n_e, m = sel.shape
    ck = 256
    tri = jnp.asarray(np.triu(np.ones((ck, ck), np.float32)), BF16)
    full = lambda shape: pl.BlockSpec(shape, lambda: tuple(0 for _ in shape))
    return pl.pallas_call(
        functools.partial(_route_kernel, tile=tile),
        in_specs=[full((n_e, m)), full((n_e, m)), full((ck, ck))],
        out_specs=[full((n_e, m)), full((m, n_e)), full((n_e, LANES))],
        out_shape=[jax.ShapeDtypeStruct((n_e, m), jnp.int32),
                   jax.ShapeDtypeStruct((m, n_e), F32),
                   jax.ShapeDtypeStruct((n_e, LANES), jnp.int32)],
        scratch_shapes=[pltpu.VMEM((n_e, m), F32)],
        compiler_params=pltpu.CompilerParams(vmem_limit_bytes=VMEM_LIMIT),
        name="moe_route",
    )(sel, w, tri)


def _row_gather_start(src_hbm, dst, rows_ref, base, n_rows, sem, unroll=8):
    def body(r, carry):
        src_row = rows_ref[base + r]
        pltpu.make_async_copy(src_hbm.at[pl.ds(src_row, 1)], dst.at[pl.ds(r, 1)], sem).start()
        return carry
    lax.fori_loop(0, n_rows, body, 0, unroll=unroll)


def _row_gather_wait(src_hbm, dst, sem):
    pltpu.make_async_copy(src_hbm.at[pl.ds(0, dst.shape[0])], dst, sem).wait()


def _moe_gather_kernel(pos_ref, h_hbm, zeros_hbm, o_ref, tok_ref, buf_ref, sem_ref):
    i = pl.program_id(0)
    n = pl.num_programs(0)
    tile = o_ref.shape[0]
    m = pos_ref.shape[0] // 2

    @pl.when(i == 0)
    def _():
        clear = pltpu.make_async_copy(zeros_hbm, tok_ref, sem_ref.at[2])
        clear.start()
        clear.wait()

        def fill(t, carry):
            tok_ref[pos_ref[t]] = t
            tok_ref[pos_ref[m + t]] = t
            return carry
        lax.fori_loop(0, m, fill, 0, unroll=8)
        _row_gather_start(h_hbm, buf_ref.at[0], tok_ref, 0, tile, sem_ref.at[0])

    @pl.when(i + 1 < n)
    def _():
        nxt = (i + 1) % 2
        _row_gather_start(h_hbm, buf_ref.at[nxt], tok_ref, (i + 1) * tile, tile, sem_ref.at[nxt])

    cur = i % 2
    _row_gather_wait(h_hbm, buf_ref.at[cur], sem_ref.at[cur])
    o_ref[...] = buf_ref[cur].astype(o_ref.dtype)


def moe_gather(pos2, h, n_tiles, tile):
    m, d = h.shape
    rows = n_tiles * tile
    return pl.pallas_call(
        _moe_gather_kernel,
        grid_spec=pltpu.PrefetchScalarGridSpec(
            num_scalar_prefetch=1,
            grid=(n_tiles,),
            in_specs=[pl.BlockSpec(memory_space=pl.ANY), pl.BlockSpec(memory_space=pl.ANY)],
            out_specs=pl.BlockSpec((tile, d), lambda i, pos: (i, 0)),
            scratch_shapes=[pltpu.SMEM((rows,), jnp.int32),
                            pltpu.VMEM((2, tile, d), F32),
                            pltpu.SemaphoreType.DMA((3,))]),
        out_shape=jax.ShapeDtypeStruct((rows, d), BF16),
        compiler_params=_params("arbitrary"),
        name="moe_gather",
    )(pos2, h, jnp.zeros((rows,), jnp.int32))


def _moe_up_kernel(meta_ref, x_ref, w1_ref, w3_ref, o_ref, c1_ref, c3_ref):
    i = pl.program_id(1)
    prev = meta_ref[0, jnp.maximum(i - 1, 0)]
    changed = (i == 0) | (meta_ref[0, i] != prev)

    @pl.when(changed)
    def _():
        c1_ref[...] = w1_ref[...].astype(BF16)
        c3_ref[...] = w3_ref[...].astype(BF16)

    @pl.when(i < meta_ref[1, 0])
    def _():
        x = x_ref[...]
        a1 = _dot(x, c1_ref[...])
        a3 = _dot(x, c3_ref[...])
        o_ref[...] = (a1 * _sigmoid(a1) * a3).astype(o_ref.dtype)

    @pl.when(i >= meta_ref[1, 0])
    def _():
        o_ref[...] = jnp.zeros_like(o_ref)


def moe_up(meta, xs, w1, w3, tile, tn=1024):
    rows, d = xs.shape
    f = w1.shape[2]
    assert f % tn == 0 and rows % tile == 0
    n_tiles = rows // tile
    last = lambda i, meta: jnp.maximum(jnp.minimum(i, meta[1, 0] - 1), 0)
    return pl.pallas_call(
        _moe_up_kernel,
        grid_spec=pltpu.PrefetchScalarGridSpec(
            num_scalar_prefetch=1,
            grid=(f // tn, n_tiles),
            in_specs=[pl.BlockSpec((tile, d), lambda j, i, meta: (last(i, meta), 0)),
                      pl.BlockSpec((None, d, tn), lambda j, i, meta: (meta[0, i], 0, j)),
                      pl.BlockSpec((None, d, tn), lambda j, i, meta: (meta[0, i], 0, j))],
            out_specs=pl.BlockSpec((tile, tn), lambda j, i, meta: (i, j)),
            scratch_shapes=[pltpu.VMEM((d, tn), BF16), pltpu.VMEM((d, tn), BF16)]),
        out_shape=jax.ShapeDtypeStruct((rows, f), BF16),
        compiler_params=_params("arbitrary", "arbitrary"),
        name="moe_up",
    )(meta, xs, w1, w3)


def _moe_down_kernel(meta_ref, a_ref, w_ref, o_ref):
    i = pl.program_id(0)
    k = pl.program_id(1)

    active = i < meta_ref[1, 0]
    cn = 256

    @pl.when(active & (k == 0))
    def _():
        a = a_ref[...]
        for c in range(0, o_ref.shape[1], cn):
            o_ref[:, c:c + cn] = _dot(a, w_ref[:, c:c + cn].astype(BF16))

    @pl.when(active & (k > 0))
    def _():
        a = a_ref[...]
        for c in range(0, o_ref.shape[1], cn):
            o_ref[:, c:c + cn] += _dot(a, w_ref[:, c:c + cn].astype(BF16))

    @pl.when(jnp.logical_not(active) & (k == 0))
    def _():
        o_ref[...] = jnp.zeros_like(o_ref)


def moe_down(meta, up, w2, tile, tk=1792):
    rows, f = up.shape
    d = w2.shape[2]
    assert f % tk == 0 and rows % tile == 0
    n_tiles = rows // tile
    last = lambda i, meta: jnp.maximum(jnp.minimum(i, meta[1, 0] - 1), 0)
    return pl.pallas_call(
        _moe_down_kernel,
        grid_spec=pltpu.PrefetchScalarGridSpec(
            num_scalar_prefetch=1,
            grid=(n_tiles, f // tk),
            in_specs=[pl.BlockSpec((tile, tk), lambda i, k, meta: (last(i, meta), k)),
                      pl.BlockSpec((None, tk, d), lambda i, k, meta: (meta[0, i], k, 0))],
            out_specs=pl.BlockSpec((tile, d), lambda i, k, meta: (i, 0))),
        out_shape=jax.ShapeDtypeStruct((rows, d), F32),
        compiler_params=_params("arbitrary", "arbitrary"),
        name="moe_down",
    )(meta, up, w2)


def _moe_combine_kernel(pos_ref, x_ref, wt_ref, y_hbm, o_ref, buf_ref, sem_ref):
    i = pl.program_id(0)
    n = pl.num_programs(0)
    tc = x_ref.shape[0]
    m = pos_ref.shape[0] // 2

    def start(step, slot):
        for s in range(2):
            _row_gather_start(y_hbm, buf_ref.at[slot, s], pos_ref, s * m + step * tc, tc, sem_ref.at[slot, s])

    @pl.when(i == 0)
    def _():
        start(0, 0)

    @pl.when(i + 1 < n)
    def _():
        start(i + 1, (i + 1) % 2)

    cur = i % 2
    for s in range(2):
        _row_gather_wait(y_hbm, buf_ref.at[cur, s], sem_ref.at[cur, s])
    wt = wt_ref[...]
    o_ref[...] = x_ref[...] + wt[:, 0:1] * buf_ref[cur, 0] + wt[:, 1:2] * buf_ref[cur, 1]


def moe_combine(pos2, x, wt, y, tc=256):
    m, d = x.shape
    return pl.pallas_call(
        _moe_combine_kernel,
        grid_spec=pltpu.PrefetchScalarGridSpec(
            num_scalar_prefetch=1,
            grid=(m // tc,),
            in_specs=[pl.BlockSpec((tc, d), lambda i, pos: (i, 0)),
                      pl.BlockSpec((tc, wt.shape[1]), lambda i, pos: (i, 0)),
                      pl.BlockSpec(memory_space=pl.ANY)],
            out_specs=pl.BlockSpec((tc, d), lambda i, pos: (i, 0)),
            scratch_shapes=[pltpu.VMEM((2, 2, tc, d), F32),
                            pltpu.SemaphoreType.DMA((2, 2))]),
        out_shape=jax.ShapeDtypeStruct((m, d), F32),
        compiler_params=_params("arbitrary"),
        name="moe_combine",
    )(pos2, x, wt, y)


def moe_layer(x, norm_g, router_w, router_b, w1, w3, w2):
    m, d = x.shape
    n_e = router_w.shape[1]
    tile = MOE_TILE
    n_tiles = 2 * m // tile + n_e
    h, sel, w = norm_router(x, norm_g, router_w, router_b)
    pos, wt, meta = moe_route(sel, w, tile)
    pos2 = pos[:2].reshape(-1)
    xs = moe_gather(pos2, h, n_tiles, tile)
    up = moe_up(meta, xs, w1, w3, tile)
    y = moe_down(meta, up, w2, tile)
    return moe_combine(pos2, x, wt, y)


def _mixer(x, batch, norm_g, w_in, pe_k, pe_v, kw1, kw2, vw1, vw2, ln_g, ln_b, w_s, b_s, w_a, w_b, w_out):
    h = rmsnorm(x, norm_g, BF16)
    qt, kvr, vt, ngt, rest = in_projection(h, w_in)
    cmp_kv = compress(kvr, batch, pe_k, pe_v, kw1, kw2, vw1, vw2)
    oct, sel = compressed_and_select(qt, cmp_kv, batch)
    oa = selected_window_attention(qt, kvr, vt, sel, oct, ngt, batch)
    ob = spatial_gating(rest, ln_g, ln_b, w_s, b_s)
    y = merge_branches(oa, ob, rest, w_a, w_b)
    return residual_matmul("out_proj", y, w_out, x)


def kernel(x, norm_mix, w_in, cmp_pe_k, cmp_pe_v, cmp_k_w1, cmp_k_w2, cmp_v_w1, cmp_v_w2, sgu_ln_g, sgu_ln_b, sgu_w, sgu_b, w_branch_a, w_branch_b, w_out, norm_ffn, ffn_w1, ffn_w3, ffn_w2, router_w, router_b, moe_w1, moe_w3, moe_w2, norm_f):
    batch, seq, d = x.shape
    depth = norm_mix.shape[0]
    xf = x.reshape(batch * seq, d)
    for layer in range(depth):
        xf = _mixer(xf, batch, norm_mix[layer], w_in[layer], cmp_pe_k[layer], cmp_pe_v[layer],
                    cmp_k_w1[layer], cmp_k_w2[layer], cmp_v_w1[layer], cmp_v_w2[layer],
                    sgu_ln_g[layer], sgu_ln_b[layer], sgu_w[layer], sgu_b[layer],
                    w_branch_a[layer], w_branch_b[layer], w_out[layer])
        j = layer // 2
        if layer % 2 == 0:
            h = rmsnorm(xf, norm_ffn[layer], BF16)
            up = swiglu_up(h, ffn_w1[j], ffn_w3[j])
            xf = mm_acc("ffn_down", up, ffn_w2[j], xf, 1024, 1024, 512)
        else:
            xf = moe_layer(xf, norm_ffn[layer], router_w[j], router_b[j], moe_w1[j], moe_w3[j], moe_w2[j])
    return rmsnorm(xf, norm_f, F32).reshape(batch, seq, d)
```

```python
import functools

import numpy as np
import jax
import jax.numpy as jnp
from jax import lax
from jax.experimental import pallas as pl
from jax.experimental.pallas import tpu as pltpu

F32 = jnp.float32
BF16 = jnp.bfloat16

D_MODEL = 2048
N_Q_HEADS = 16
N_KV_HEADS = 4
HEAD_DIM = 64
Q_PER_KV = N_Q_HEADS // N_KV_HEADS
NSA_WIDTH = N_Q_HEADS * HEAD_DIM
KV_WIDTH = N_KV_HEADS * HEAD_DIM
GROUP_WIDTH = Q_PER_KV * HEAD_DIM
CMP_BLOCK = 32
CMP_STRIDE = 16
CMP_HIDDEN = 256
SLC_BLOCK = 64
N_SELECT = 16
WINDOW = 512
FORCED_SCORE = 1e4
SGU_WIDTH = D_MODEL // 2
SGU_GROUPS = 8
SGU_CHUNK = 128
N_EXPERTS = 8
NORM_EPS = 1e-5
NEG = -1e30
LANES = 128
VMEM_LIMIT = 56 * 1024 * 1024

Q_OFF = 0
KV_OFF = NSA_WIDTH
NG_OFF = KV_OFF + 6 * KV_WIDTH
NG_WIDTH = 3 * N_Q_HEADS
REST_OFF = NG_OFF + NG_WIDTH


def _params(*sem):
    return pltpu.CompilerParams(dimension_semantics=sem, vmem_limit_bytes=VMEM_LIMIT)


def _dot(a, b):
    return jnp.dot(a, b, preferred_element_type=F32)


def _dot_nt(a, b):
    return lax.dot_general(a, b, (((1,), (1,)), ((), ())), preferred_element_type=F32)


def _dot_split(a_f32, b_bf16):
    hi = a_f32.astype(BF16)
    lo = (a_f32 - hi.astype(F32)).astype(BF16)
    return _dot(hi, b_bf16) + _dot(lo, b_bf16)


def _sigmoid(x):
    return 1.0 / (1.0 + jnp.exp(-x))


def _gelu(x):
    return x * (0.5 * (1.0 + jnp.tanh(0.7978845608028654 * (x + 0.044715 * (x * x * x)))))


def _rms(x, g):
    return x * lax.rsqrt(jnp.mean(x * x, axis=-1, keepdims=True) + NORM_EPS) * g


def _rmsnorm_kernel(x_ref, g_ref, o_ref):
    o_ref[...] = _rms(x_ref[...], g_ref[...]).astype(o_ref.dtype)


def rmsnorm(x, g, out_dtype, tm=512):
    m, d = x.shape
    return pl.pallas_call(
        _rmsnorm_kernel,
        grid=(m // tm,),
        in_specs=[pl.BlockSpec((tm, d), lambda i: (i, 0)),
                  pl.BlockSpec((1, d), lambda i: (0, 0))],
        out_specs=pl.BlockSpec((tm, d), lambda i: (i, 0)),
        out_shape=jax.ShapeDtypeStruct((m, d), out_dtype),
        compiler_params=_params("parallel"),
        name="rmsnorm",
    )(x, g.reshape(1, d))


def _norm_router_kernel(x_ref, g_ref, rwt_ref, rb_ref, h_ref, sel_ref, w_ref):
    h = _rms(x_ref[...], g_ref[...])
    h_ref[...] = h
    logits = lax.dot_general(rwt_ref[...], h, (((1,), (1,)), ((), ())), preferred_element_type=F32,
                             precision=lax.Precision.HIGHEST) + rb_ref[...]
    n_e = logits.shape[0]
    e = lax.broadcasted_iota(jnp.int32, logits.shape, 0)
    m1 = jnp.max(logits, axis=0, keepdims=True)
    i1 = jnp.min(jnp.where(logits == m1, e, n_e), axis=0, keepdims=True)
    rest = jnp.where(e == i1, -jnp.inf, logits)
    m2 = jnp.max(rest, axis=0, keepdims=True)
    i2 = jnp.min(jnp.where(rest == m2, e, n_e), axis=0, keepdims=True)
    e2 = jnp.exp(m2 - m1)
    w1 = 1.0 / (1.0 + e2)
    w2 = e2 / (1.0 + e2)
    sel_ref[...] = jnp.where((e == i1) | (e == i2), 1.0, 0.0)
    w_ref[...] = jnp.where(e == i1, w1, 0.0) + jnp.where(e == i2, w2, 0.0)


def norm_router(x, g, router_w, router_b, tm=512):
    m, d = x.shape
    n_e = router_w.shape[1]
    return pl.pallas_call(
        _norm_router_kernel,
        grid=(m // tm,),
        in_specs=[pl.BlockSpec((tm, d), lambda i: (i, 0)),
                  pl.BlockSpec((1, d), lambda i: (0, 0)),
                  pl.BlockSpec((n_e, d), lambda i: (0, 0)),
                  pl.BlockSpec((n_e, 1), lambda i: (0, 0))],
        out_specs=[pl.BlockSpec((tm, d), lambda i: (i, 0)),
                   pl.BlockSpec((n_e, tm), lambda i: (0, i)),
                   pl.BlockSpec((n_e, tm), lambda i: (0, i))],
        out_shape=[jax.ShapeDtypeStruct((m, d), F32),
                   jax.ShapeDtypeStruct((n_e, m), F32),
                   jax.ShapeDtypeStruct((n_e, m), F32)],
        compiler_params=_params("parallel"),
        name="norm_router",
    )(x, g.reshape(1, d), router_w.T, router_b.reshape(n_e, 1))


def _mm_full_kernel(*refs, n_a, n_b, n_e, n_o, pairs, inner_axis, epilogue):
    a_refs = refs[:n_a]
    b_refs = refs[n_a:n_a + n_b]
    e_refs = refs[n_a + n_b:n_a + n_b + n_e]
    o_refs = refs[n_a + n_b + n_e:n_a + n_b + n_e + n_o]
    w_refs = refs[n_a + n_b + n_e + n_o:]

    @pl.when(pl.program_id(inner_axis) == 0)
    def _():
        for b_ref, w_ref in zip(b_refs, w_refs):
            w_ref[...] = b_ref[...].astype(BF16)

    accs = [_dot(a_refs[ia][...], w_refs[ib][...]) for ia, ib in pairs]
    epilogue(accs, e_refs, o_refs)


def mm_full(name, grid, a_ops, b_ops, e_ops, outs, pairs, epilogue):
    ops = a_ops + b_ops + e_ops
    kernel = functools.partial(
        _mm_full_kernel, n_a=len(a_ops), n_b=len(b_ops), n_e=len(e_ops), n_o=len(outs),
        pairs=pairs, inner_axis=len(grid) - 1, epilogue=epilogue)
    scratch = [pltpu.VMEM(tuple(s for s in blk if s is not None), BF16) for _, blk, _ in b_ops]
    res = pl.pallas_call(
        kernel,
        grid=grid,
        in_specs=[pl.BlockSpec(blk, imap) for _, blk, imap in ops],
        out_specs=[pl.BlockSpec(blk, imap) for _, blk, imap in outs],
        out_shape=[sds for sds, _, _ in outs],
        scratch_shapes=scratch,
        compiler_params=_params(*(("arbitrary",) * len(grid))),
        name=name,
    )(*[arr for arr, _, _ in ops])
    return res


COL_CHUNK = 256


def _mm_acc_kernel(a_ref, b_ref, r_ref, o_ref):
    cols = range(0, o_ref.shape[1], COL_CHUNK)

    @pl.when(pl.program_id(2) == 0)
    def _():
        a = a_ref[...]
        for c in cols:
            sl = slice(c, c + COL_CHUNK)
            o_ref[:, sl] = r_ref[:, sl] + _dot(a, b_ref[:, sl].astype(BF16))

    @pl.when(pl.program_id(2) > 0)
    def _():
        a = a_ref[...]
        for c in cols:
            sl = slice(c, c + COL_CHUNK)
            o_ref[:, sl] += _dot(a, b_ref[:, sl].astype(BF16))


def mm_acc(name, a, b, res, tm, tn, tk):
    m, k = a.shape
    n = b.shape[1]
    assert m % tm == 0 and n % tn == 0 and k % tk == 0 and tn % COL_CHUNK == 0 and tk % LANES == 0
    return pl.pallas_call(
        _mm_acc_kernel,
        grid=(m // tm, n // tn, k // tk),
        in_specs=[pl.BlockSpec((tm, tk), lambda i, j, kk: (i, kk)),
                  pl.BlockSpec((tk, tn), lambda i, j, kk: (kk, j)),
                  pl.BlockSpec((tm, tn), lambda i, j, kk: (i, j))],
        out_specs=pl.BlockSpec((tm, tn), lambda i, j, kk: (i, j)),
        out_shape=jax.ShapeDtypeStruct((m, n), F32),
        compiler_params=_params("parallel", "parallel", "arbitrary"),
        name=name,
    )(a, b, res)


def in_projection(h, w_in_all, layer):
    m, d = h.shape
    tm = 1024
    a_op = [(h, (tm, d), lambda j, i: (i, 0))]

    tq = 512
    q_heads = tq // HEAD_DIM

    def q_epi(accs, e_refs, o_refs):
        for c in range(q_heads):
            o_refs[0][c] = (accs[0][:, c * HEAD_DIM:(c + 1) * HEAD_DIM] * (HEAD_DIM ** -0.5)).T.astype(BF16)

    (qt,) = mm_full(
        "in_proj_q", (NSA_WIDTH // tq, m // tm), a_op,
        [(w_in_all, (None, d, tq), lambda j, i: (layer, 0, j))], [],
        [(jax.ShapeDtypeStruct((N_Q_HEADS, HEAD_DIM, m), BF16), (q_heads, HEAD_DIM, tm), lambda j, i: (j, 0, i))],
        [(0, 0)], q_epi)

    sec0 = KV_OFF // KV_WIDTH

    def rows_epi(accs, e_refs, o_refs):
        for c in range(N_KV_HEADS):
            o_refs[0][c] = accs[0][:, c * HEAD_DIM:(c + 1) * HEAD_DIM].astype(BF16)

    (kvr,) = mm_full(
        "in_proj_kv_rows", (4, m // tm), a_op,
        [(w_in_all, (None, d, KV_WIDTH), lambda j, i: (layer, 0, sec0 + j + j // 3))], [],
        [(jax.ShapeDtypeStruct((4 * N_KV_HEADS, m, HEAD_DIM), BF16),
          (N_KV_HEADS, tm, HEAD_DIM), lambda j, i: (j, i, 0))],
        [(0, 0)], rows_epi)

    def cols_epi(accs, e_refs, o_refs):
        for c in range(N_KV_HEADS):
            o_refs[0][c] = accs[0][:, c * HEAD_DIM:(c + 1) * HEAD_DIM].T.astype(BF16)

    (vt,) = mm_full(
        "in_proj_v_cols", (2, m // tm), a_op,
        [(w_in_all, (None, d, KV_WIDTH), lambda j, i: (layer, 0, sec0 + 3 + 2 * j))], [],
        [(jax.ShapeDtypeStruct((2 * N_KV_HEADS, HEAD_DIM, m), BF16),
          (N_KV_HEADS, HEAD_DIM, tm), lambda j, i: (j, 0, i))],
        [(0, 0)], cols_epi)

    def f32_epi(accs, e_refs, o_refs):
        o_refs[0][...] = accs[0]

    def f32_t_epi(accs, e_refs, o_refs):
        o_refs[0][...] = accs[0].T

    w_ng = jnp.pad(w_in_all[layer, :, NG_OFF:REST_OFF], ((0, 0), (0, LANES - NG_WIDTH)))
    (ngt,) = mm_full(
        "in_proj_ng", (1, m // tm), a_op,
        [(w_ng, (d, LANES), lambda j, i: (0, 0))], [],
        [(jax.ShapeDtypeStruct((LANES, m), F32), (LANES, tm), lambda j, i: (0, i))],
        [(0, 0)], f32_t_epi)

    w_rest = w_in_all[layer, :, REST_OFF:]
    n_rest = w_rest.shape[1]
    tr = 512
    (rest,) = mm_full(
        "in_proj_rest", (n_rest // tr, m // tm), a_op,
        [(w_rest, (d, tr), lambda j, i: (0, j))], [],
        [(jax.ShapeDtypeStruct((m, n_rest), F32), (tm, tr), lambda j, i: (i, j))],
        [(0, 0)], f32_epi)
    return qt, kvr, vt, ngt, rest


def _cmp_kernel(a_ref, w1_ref, w2_ref, pe_ref, o_ref):
    a = a_ref[...]
    rows = a.shape[0]
    half = a.shape[1]
    w1 = w1_ref[...].astype(BF16)
    p0 = _dot(a, w1[:half])
    p1 = _dot(a, w1[half:])
    bias = _dot(pe_ref[...].astype(BF16), w1)[0:1]
    hid = p0 + pltpu.roll(p1, rows - 1, 0) + bias
    o_ref[...] = _dot(_gelu(hid).astype(BF16), w2_ref[...].astype(BF16))


def compress(kv, batch, pe_k, pe_v, kw1, kw2, vw1, vw2):
    m = kv.shape[1]
    n_chunks = m // batch // CMP_STRIDE
    rows = batch * n_chunks
    feat = CMP_STRIDE * HEAD_DIM
    a = kv[:2 * N_KV_HEADS].reshape(2, N_KV_HEADS, rows, feat)
    w1 = jnp.stack([kw1, vw1])
    w2 = jnp.stack([kw2, vw2])
    pe = jnp.stack([pe_k, pe_v]).reshape(2, 1, CMP_BLOCK * HEAD_DIM)
    pe = jnp.broadcast_to(pe, (2, 8, CMP_BLOCK * HEAD_DIM))
    return pl.pallas_call(
        _cmp_kernel,
        grid=(2, N_KV_HEADS),
        in_specs=[pl.BlockSpec((None, None, rows, feat), lambda s, g: (s, g, 0, 0)),
                  pl.BlockSpec((None, 2 * feat, CMP_HIDDEN), lambda s, g: (s, 0, 0)),
                  pl.BlockSpec((None, CMP_HIDDEN, HEAD_DIM), lambda s, g: (s, 0, 0)),
                  pl.BlockSpec((None, 8, 2 * feat), lambda s, g: (s, 0, 0))],
        out_specs=pl.BlockSpec((None, None, rows, HEAD_DIM), lambda s, g: (s, g, 0, 0)),
        out_shape=jax.ShapeDtypeStruct((2, N_KV_HEADS, rows, HEAD_DIM), F32),
        compiler_params=_params("parallel", "parallel"),
        name="nsa_compress",
    )(a, w1, w2, pe)


def _alibi_slopes():
    return np.array([2.0 ** (-8.0 * (h + 1) / N_Q_HEADS) for h in range(N_Q_HEADS)], dtype=np.float32)


def _overlap_t(n_cmp_pad, n_slc):
    cs = np.arange(n_cmp_pad)[None, :] * CMP_STRIDE
    ss = np.arange(n_slc)[:, None] * SLC_BLOCK
    ov = np.clip(np.minimum(cs + CMP_BLOCK, ss + SLC_BLOCK) - np.maximum(cs, ss), 0, None)
    return (ov / CMP_STRIDE).astype(np.float32)


def _sel_kernel(slopes_ref, q_ref, kc_ref, vc_ref, ovt_ref, oc_ref, sel_ref, *, n_cmp):
    g = pl.program_id(1)
    i = pl.program_id(2)
    tq = q_ref.shape[2]
    ncp = kc_ref.shape[0]
    n_slc = ovt_ref.shape[0]
    kc = kc_ref[...].astype(BF16)
    vct = vc_ref[...].T.astype(BF16)

    t = i * tq + lax.broadcasted_iota(jnp.int32, (ncp, tq), 1)
    c = lax.broadcasted_iota(jnp.int32, (ncp, tq), 0)
    dist = t - (c * CMP_STRIDE + (CMP_BLOCK - 1))
    mask = (dist >= 0) & (c < n_cmp)
    distf = dist.astype(F32)

    p_sum = jnp.zeros((ncp, tq), F32)
    for j in range(Q_PER_KV):
        s = _dot(kc, q_ref[j])
        s = s - slopes_ref[g * Q_PER_KV + j] * distf
        s = jnp.where(mask, s, NEG)
        mx = jnp.max(s, axis=0, keepdims=True)
        e = jnp.where(mask, jnp.exp(s - mx), 0.0)
        p = e / jnp.maximum(jnp.sum(e, axis=0, keepdims=True), 1e-30)
        p_sum = p_sum + p
        oc_ref[j] = _dot(vct, p.astype(BF16))

    ovt = ovt_ref[...]
    hi = p_sum.astype(BF16)
    lo = (p_sum - hi.astype(F32)).astype(BF16)
    imp = _dot(ovt, hi) + _dot(ovt, lo)
    tt = i * tq + lax.broadcasted_iota(jnp.int32, (n_slc, tq), 1)
    blk = lax.broadcasted_iota(jnp.int32, (n_slc, tq), 0)
    cur = tt // SLC_BLOCK
    valid = blk * SLC_BLOCK <= tt
    forced = (blk == 0) | (blk == cur) | (blk == cur - 1)
    score = jnp.where(valid, imp, -1.0)
    score = jnp.where(forced, FORCED_SCORE, score)
    rank = jnp.zeros((n_slc, tq), F32)
    for mrow in range(n_slc):
        other = jnp.broadcast_to(score[mrow:mrow + 1, :], (n_slc, tq))
        beats = (other > score) | ((other == score) & (blk > mrow))
        rank = rank + jnp.where(beats, 1.0, 0.0)
    sel_ref[...] = jnp.where(rank < float(min(N_SELECT, n_slc)), 1.0, 0.0)


def compressed_and_select(qt, cmp_kv, batch, tq=256):
    m = qt.shape[2]
    seq = m // batch
    nq = seq // tq
    ncp = cmp_kv.shape[2] // batch
    n_slc = seq // SLC_BLOCK
    ovt = jnp.asarray(_overlap_t(ncp, n_slc), BF16)
    slopes = jnp.asarray(_alibi_slopes())
    kernel = functools.partial(_sel_kernel, n_cmp=ncp - 1)
    return pl.pallas_call(
        kernel,
        grid=(batch, N_KV_HEADS, nq),
        in_specs=[pl.BlockSpec(memory_space=pltpu.SMEM),
                  pl.BlockSpec((Q_PER_KV, HEAD_DIM, tq), lambda b, g, i: (g, 0, b * nq + i)),
                  pl.BlockSpec((None, None, ncp, HEAD_DIM), lambda b, g, i: (0, g, b, 0)),
                  pl.BlockSpec((None, None, ncp, HEAD_DIM), lambda b, g, i: (1, g, b, 0)),
                  pl.BlockSpec((n_slc, ncp), lambda b, g, i: (0, 0))],
        out_specs=[pl.BlockSpec((Q_PER_KV, HEAD_DIM, tq), lambda b, g, i: (g, 0, b * nq + i)),
                   pl.BlockSpec((None, None, n_slc, tq), lambda b, g, i: (b, g, 0, i))],
        out_shape=[jax.ShapeDtypeStruct((N_Q_HEADS, HEAD_DIM, m), F32),
                   jax.ShapeDtypeStruct((batch, N_KV_HEADS, n_slc, seq), F32)],
        compiler_params=_params("parallel", "parallel", "parallel"),
        name="nsa_compressed_select",
    )(slopes, qt, cmp_kv, cmp_kv, ovt)


AUX_SLOPE = HEAD_DIM
AUX_SEL = HEAD_DIM + 16
ATTN_K = 2 * HEAD_DIM
STRIP = 32


def _attn_kernel(slopes_ref, q_ref, ks_ref, vs_ref, kw_ref, vw_ref, auxs_ref, auxw_ref, sel_ref, oc_ref,
                 ng_ref, o_ref, ka_s_ref, ka_w_ref, qa_ref, s_ref, p_ref, mask_ref, m_ref, l_ref, acc_ref,
                 os_ref, sig_ref):
    g = pl.program_id(1)
    i = pl.program_id(2)
    tq = q_ref.shape[2]
    tk = tq
    q0 = i * tq
    n_slc = sel_ref.shape[0]

    @pl.when(i == 0)
    def _():
        ka_s_ref[:, 0:HEAD_DIM] = ks_ref[...]
        ka_s_ref[:, HEAD_DIM:ATTN_K] = auxs_ref[...]
        ka_w_ref[:, 0:HEAD_DIM] = kw_ref[...]
        ka_w_ref[:, HEAD_DIM:ATTN_K] = auxw_ref[...]
        kr = lax.broadcasted_iota(jnp.int32, (tk, tq), 0)
        qc = lax.broadcasted_iota(jnp.int32, (tk, tq), 1)
        mask_ref[0] = jnp.where(kr <= qc, 0.0, NEG)
        mask_ref[1] = jnp.where(kr > qc, 0.0, NEG)

    sig_ref[...] = _sigmoid(ng_ref[...])
    sel_neg = (sel_ref[...] - 1.0) * (-NEG)
    row16 = lax.broadcasted_iota(jnp.int32, (16, tq), 0)
    pad = jnp.zeros((ATTN_K - AUX_SEL - n_slc, tq), F32)
    for j in range(Q_PER_KV):
        sl = jnp.full((16, tq), slopes_ref[g * Q_PER_KV + j], F32)
        hi = sl.astype(BF16).astype(F32)
        mid = (sl - hi).astype(BF16).astype(F32)
        lo = (sl - hi - mid).astype(BF16).astype(F32)
        pieces = jnp.where(row16 == 0, hi, jnp.where(row16 == 1, mid, jnp.where(row16 == 2, lo, 0.0)))
        qa = jnp.concatenate([q_ref[j].astype(F32), pieces, sel_neg, pad], axis=0)
        qa_ref[j] = qa.astype(BF16)

    def chunk(ka_ref, v_ref, k0, mask_idx, first):
        ka = ka_ref[pl.ds(k0, tk), :]
        vc = v_ref[:, pl.ds(k0, tk)]
        for j in range(Q_PER_KV):
            s_ref[j] = _dot(ka, qa_ref[j])
        for j in range(Q_PER_KV):
            shift = slopes_ref[g * Q_PER_KV + j] * k0.astype(F32)
            mx = None
            for r in range(0, tk, STRIP):
                x = s_ref[j, r:r + STRIP, :]
                if mask_idx is not None:
                    x = x + mask_ref[mask_idx, r:r + STRIP, :]
                    s_ref[j, r:r + STRIP, :] = x
                mx = x if mx is None else jnp.maximum(mx, x)
            m_cur = jnp.max(mx, axis=0, keepdims=True) + shift
            if first:
                m_new = m_cur
            else:
                m_old = m_ref[j]
                m_new = jnp.maximum(m_old, m_cur)
                alpha = jnp.exp(m_old - m_new)
            m_ref[j] = m_new
            m_loc = m_new - shift
            ls = None
            for r in range(0, tk, STRIP):
                p = jnp.exp(s_ref[j, r:r + STRIP, :] - m_loc)
                ls = p if ls is None else ls + p
                p_ref[j, r:r + STRIP, :] = p.astype(BF16)
            l_cur = jnp.sum(ls, axis=0, keepdims=True)
            pv = _dot(vc, p_ref[j])
            if first:
                l_ref[j] = l_cur
                acc_ref[j] = pv
            else:
                l_ref[j] = alpha * l_ref[j] + l_cur
                acc_ref[j] = alpha * acc_ref[j] + pv

    k_diag = pl.multiple_of(q0, tk)
    chunk(ka_s_ref, vs_ref, k_diag, 0, True)

    def slc_body(c, carry):
        chunk(ka_s_ref, vs_ref, pl.multiple_of(c * tk, tk), None, False)
        return carry

    lax.fori_loop(0, i, slc_body, 0)
    for j in range(Q_PER_KV):
        os_ref[j] = acc_ref[j] / l_ref[j]

    chunk(ka_w_ref, vw_ref, k_diag, 0, True)
    n_back = WINDOW // tk
    for back in range(1, n_back + 1):
        @pl.when(i >= back)
        def _(back=back):
            k0 = pl.multiple_of(q0 - back * tk, tk)
            chunk(ka_w_ref, vw_ref, k0, 1 if back == n_back else None, False)

    for j in range(Q_PER_KV):
        base = (g * Q_PER_KV + j) * 3
        out_t = (sig_ref[pl.ds(base, 1), :] * oc_ref[j]
                 + sig_ref[pl.ds(base + 1, 1), :] * os_ref[j]
                 + sig_ref[pl.ds(base + 2, 1), :] * (acc_ref[j] / l_ref[j]))
        o_ref[:, j * HEAD_DIM:(j + 1) * HEAD_DIM] = out_t.T.astype(o_ref.dtype)


def _key_aux(seq, tk, with_blocks):
    aux = np.zeros((seq, ATTN_K - HEAD_DIM), np.float32)
    pos = np.arange(seq)
    aux[:, AUX_SLOPE - HEAD_DIM:AUX_SLOPE - HEAD_DIM + 3] = (pos % tk)[:, None]
    if with_blocks:
        aux[pos, AUX_SEL - HEAD_DIM + pos // SLC_BLOCK] = 1.0
    return aux


def selected_window_attention(qt, kvr, vt, sel, oct, ngt, batch, tq=256):
    m = qt.shape[2]
    seq = m // batch
    nq = seq // tq
    n_slc = seq // SLC_BLOCK
    assert tq <= 256 and AUX_SEL + n_slc <= ATTN_K
    aux_s = jnp.asarray(_key_aux(seq, tq, True), BF16)
    aux_w = jnp.asarray(_key_aux(seq, tq, False), BF16)
    slopes = jnp.asarray(_alibi_slopes())
    n_g = N_KV_HEADS

    def k_spec(section):
        return pl.BlockSpec((None, seq, HEAD_DIM), lambda b, g, i: (section * n_g + g, b, 0))

    def v_spec(section):
        return pl.BlockSpec((None, HEAD_DIM, seq), lambda b, g, i: (section * n_g + g, 0, b))

    head_blk = pl.BlockSpec((Q_PER_KV, HEAD_DIM, tq), lambda b, g, i: (g, 0, b * nq + i))
    aux_blk = pl.BlockSpec((seq, ATTN_K - HEAD_DIM), lambda b, g, i: (0, 0))
    return pl.pallas_call(
        _attn_kernel,
        grid=(batch, N_KV_HEADS, nq),
        in_specs=[pl.BlockSpec(memory_space=pltpu.SMEM),
                  head_blk,
                  k_spec(2), v_spec(0), k_spec(3), v_spec(1),
                  aux_blk, aux_blk,
                  pl.BlockSpec((None, None, n_slc, tq), lambda b, g, i: (b, g, 0, i)),
                  head_blk,
                  pl.BlockSpec((LANES, tq), lambda b, g, i: (0, b * nq + i))],
        out_specs=pl.BlockSpec((tq, GROUP_WIDTH), lambda b, g, i: (b * nq + i, g)),
        out_shape=jax.ShapeDtypeStruct((m, NSA_WIDTH), BF16),
        scratch_shapes=[pltpu.VMEM((seq, ATTN_K), BF16),
                        pltpu.VMEM((seq, ATTN_K), BF16),
                        pltpu.VMEM((Q_PER_KV, ATTN_K, tq), BF16),
                        pltpu.VMEM((Q_PER_KV, tq, tq), F32),
                        pltpu.VMEM((Q_PER_KV, tq, tq), BF16),
                        pltpu.VMEM((2, tq, tq), F32),
                        pltpu.VMEM((Q_PER_KV, 1, tq), F32),
                        pltpu.VMEM((Q_PER_KV, 1, tq), F32),
                        pltpu.VMEM((Q_PER_KV, HEAD_DIM, tq), F32),
                        pltpu.VMEM((Q_PER_KV, HEAD_DIM, tq), F32),
                        pltpu.VMEM((LANES, tq), F32)],
        compiler_params=_params("arbitrary", "arbitrary", "arbitrary"),
        name="nsa_selected_window",
    )(slopes, qt, kvr, vt, kvr, vt, aux_s, aux_w, sel, oct, ngt)


def _sgu_kernel(u_ref, v_ref, lg_ref, lb_ref, ws_ref, bs_ref, o_ref):
    rows = u_ref.shape[0]
    v = _gelu(v_ref[...])
    mu = jnp.mean(v, axis=-1, keepdims=True)
    var = jnp.mean(jnp.square(v - mu), axis=-1, keepdims=True)
    vl = ((v - mu) * lax.rsqrt(var + NORM_EPS) * lg_ref[...] + lb_ref[...]).astype(BF16)
    r = lax.broadcasted_iota(jnp.int32, (SGU_CHUNK, SGU_CHUNK), 0)
    c = lax.broadcasted_iota(jnp.int32, (SGU_CHUNK, SGU_CHUNK), 1)
    gd = SGU_WIDTH // SGU_GROUPS
    for grp in range(SGU_GROUPS):
        w = jnp.where(c <= r, ws_ref[grp], 0.0).astype(BF16)
        lanes = slice(grp * gd, (grp + 1) * gd)
        for n in range(rows // SGU_CHUNK):
            rs = slice(n * SGU_CHUNK, (n + 1) * SGU_CHUNK)
            vm = _dot(w, vl[rs, lanes]) + bs_ref[:, lanes]
            o_ref[rs, lanes] = (_gelu(u_ref[rs, lanes]) * vm).astype(o_ref.dtype)


def spatial_gating(rest, ln_g, ln_b, w_s, b_s, tm=512):
    m = rest.shape[0]
    gd = SGU_WIDTH // SGU_GROUPS
    bias = jnp.repeat(b_s.T, gd, axis=1)
    return pl.pallas_call(
        _sgu_kernel,
        grid=(m // tm,),
        in_specs=[pl.BlockSpec((tm, SGU_WIDTH), lambda i: (i, 0)),
                  pl.BlockSpec((tm, SGU_WIDTH), lambda i: (i, 1)),
                  pl.BlockSpec((1, SGU_WIDTH), lambda i: (0, 0)),
                  pl.BlockSpec((1, SGU_WIDTH), lambda i: (0, 0)),
                  pl.BlockSpec((SGU_GROUPS, SGU_CHUNK, SGU_CHUNK), lambda i: (0, 0, 0)),
                  pl.BlockSpec((SGU_CHUNK, SGU_WIDTH), lambda i: (0, 0))],
        out_specs=pl.BlockSpec((tm, SGU_WIDTH), lambda i: (i, 0)),
        out_shape=jax.ShapeDtypeStruct((m, SGU_WIDTH), BF16),
        compiler_params=_params("parallel"),
        name="spatial_gating",
    )(rest, rest, ln_g.reshape(1, -1), ln_b.reshape(1, -1), w_s, bias)


def merge_branches(oa, ob, rest, w_a, w_b, layer, tm=1024, tn=512):
    m = oa.shape[0]
    d = w_a.shape[2]
    ga_blk = 2 * SGU_WIDTH // tn
    gb_blk = (2 * SGU_WIDTH + d) // tn

    def epi(accs, e_refs, o_refs):
        y = _sigmoid(e_refs[0][...]) * accs[0] + _sigmoid(e_refs[1][...]) * accs[1]
        o_refs[0][...] = y.astype(BF16)

    (y,) = mm_full(
        "merge_branches", (d // tn, m // tm),
        [(oa, (tm, oa.shape[1]), lambda j, i: (i, 0)), (ob, (tm, ob.shape[1]), lambda j, i: (i, 0))],
        [(w_a, (None, w_a.shape[1], tn), lambda j, i: (layer, 0, j)),
         (w_b, (None, w_b.shape[1], tn), lambda j, i: (layer, 0, j))],
        [(rest, (tm, tn), lambda j, i: (i, ga_blk + j)), (rest, (tm, tn), lambda j, i: (i, gb_blk + j))],
        [(jax.ShapeDtypeStruct((m, d), BF16), (tm, tn), lambda j, i: (i, j))],
        [(0, 0), (1, 1)], epi)
    return y


def residual_matmul(name, a, w, layer, x, tm=1024, tn=512):
    m, k = a.shape
    n = w.shape[2]

    def epi(accs, e_refs, o_refs):
        o_refs[0][...] = e_refs[0][...] + accs[0]

    (out,) = mm_full(
        name, (n // tn, m // tm),
        [(a, (tm, k), lambda j, i: (i, 0))],
        [(w, (None, k, tn), lambda j, i: (layer, 0, j))],
        [(x, (tm, tn), lambda j, i: (i, j))],
        [(jax.ShapeDtypeStruct((m, n), F32), (tm, tn), lambda j, i: (i, j))],
        [(0, 0)], epi)
    return out


def swiglu_up(h, w1, w3, tm=1024, tn=512):
    m, d = h.shape
    f = w1.shape[1]

    def epi(accs, e_refs, o_refs):
        o_refs[0][...] = (accs[0] * _sigmoid(accs[0]) * accs[1]).astype(BF16)

    (out,) = mm_full(
        "swiglu_up", (f // tn, m // tm),
        [(h, (tm, d), lambda j, i: (i, 0))],
        [(w1, (d, tn), lambda j, i: (0, j)), (w3, (d, tn), lambda j, i: (0, j))],
        [],
        [(jax.ShapeDtypeStruct((m, f), BF16), (tm, tn), lambda j, i: (i, j))],
        [(0, 0), (0, 1)], epi)
    return out


MOE_TILE = 512


def _route_kernel(sel_ref, w_ref, tri_ref, pos_ref, wt_ref, meta_ref, cum_ref, *, tile):
    n_e, m = sel_ref.shape
    ck = tri_ref.shape[0]
    carry = jnp.zeros((n_e, 1), F32)
    for c in range(m // ck):
        sl = slice(c * ck, (c + 1) * ck)
        cs = _dot(sel_ref[:, sl].astype(BF16), tri_ref[...]) + carry
        cum_ref[:, sl] = cs
        carry = cs[:, ck - 1:ck]
    padded = jnp.ceil(carry / tile) * tile
    sub = lax.broadcasted_iota(jnp.int32, (n_e, 1), 0)
    start = jnp.zeros((n_e, 1), F32)
    run = jnp.zeros((1, 1), F32)
    for ex in range(n_e):
        start = jnp.where(sub == ex, run, start)
        run = run + padded[ex:ex + 1, :]
    sel = sel_ref[...] > 0.5
    pos = start + cum_ref[...] - 1.0
    eidx = lax.broadcasted_iota(jnp.int32, (n_e, m), 0)
    e_lo = jnp.min(jnp.where(sel, eidx, n_e), axis=0, keepdims=True)
    e_hi = jnp.max(jnp.where(sel, eidx, -1), axis=0, keepdims=True)
    is_lo = eidx == e_lo
    is_hi = eidx == e_hi
    w = w_ref[...]
    pos_a = jnp.sum(jnp.where(is_lo, pos, 0.0), axis=0, keepdims=True)
    pos_b = jnp.sum(jnp.where(is_hi, pos, 0.0), axis=0, keepdims=True)
    w_a = jnp.sum(jnp.where(is_lo, w, 0.0), axis=0, keepdims=True)
    w_b = jnp.sum(jnp.where(is_hi, w, 0.0), axis=0, keepdims=True)
    pos_ref[...] = jnp.where(eidx == 0, pos_a, jnp.where(eidx == 1, pos_b, 0.0)).astype(jnp.int32)
    wt_ref[...] = jnp.where(eidx == 0, w_a, jnp.where(eidx == 1, w_b, 0.0)).T
    tile_lo = lax.broadcasted_iota(jnp.int32, (n_e, LANES), 1).astype(F32) * tile
    t_exp = jnp.sum(jnp.where(start + padded <= tile_lo, 1.0, 0.0), axis=0, keepdims=True)
    t_exp = jnp.minimum(t_exp, n_e - 1.0)
    r8 = lax.broadcasted_iota(jnp.int32, (n_e, LANES), 0)
    meta_ref[...] = jnp.where(r8 == 0, t_exp, jnp.where(r8 == 1, run / tile, 0.0)).astype(jnp.int32)


def moe_route(sel, w, tile):
    n_e, m = sel.shape
    ck = 256
    tri = jnp.asarray(np.triu(np.ones((ck, ck), np.float32)), BF16)
    full = lambda shape: pl.BlockSpec(shape, lambda: tuple(0 for _ in shape))
    return pl.pallas_call(
        functools.partial(_route_kernel, tile=tile),
        in_specs=[full((n_e, m)), full((n_e, m)), full((ck, ck))],
        out_specs=[full((n_e, m)), full((m, n_e)), full((n_e, LANES))],
        out_shape=[jax.ShapeDtypeStruct((n_e, m), jnp.int32),
                   jax.ShapeDtypeStruct((m, n_e), F32),
                   jax.ShapeDtypeStruct((n_e, LANES), jnp.int32)],
        scratch_shapes=[pltpu.VMEM((n_e, m), F32)],
        compiler_params=pltpu.CompilerParams(vmem_limit_bytes=VMEM_LIMIT),
        name="moe_route",
    )(sel, w, tri)


def _row_gather_start(src_hbm, dst, rows_ref, base, n_rows, sem, unroll=8):
    def body(r, carry):
        src_row = rows_ref[base + r]
        pltpu.make_async_copy(src_hbm.at[pl.ds(src_row, 1)], dst.at[pl.ds(r, 1)], sem).start()
        return carry
    lax.fori_loop(0, n_rows, body, 0, unroll=unroll)


def _row_gather_wait(src_hbm, dst, sem):
    pltpu.make_async_copy(src_hbm.at[pl.ds(0, dst.shape[0])], dst, sem).wait()


def _moe_gather_kernel(pos_ref, meta_ref, h_hbm, zeros_hbm, o_ref, tok_ref, buf_ref, sem_ref):
    i = pl.program_id(0)
    n = meta_ref[1, 0]
    tile = o_ref.shape[0]
    m = pos_ref.shape[0] // 2

    @pl.when(i == 0)
    def _():
        clear = pltpu.make_async_copy(zeros_hbm, tok_ref, sem_ref.at[2])
        clear.start()
        clear.wait()

        def fill(t, carry):
            tok_ref[pos_ref[t]] = t
            tok_ref[pos_ref[m + t]] = t
            return carry
        lax.fori_loop(0, m, fill, 0, unroll=8)
        _row_gather_start(h_hbm, buf_ref.at[0], tok_ref, 0, tile, sem_ref.at[0])

    @pl.when(i + 1 < n)
    def _():
        nxt = (i + 1) % 2
        _row_gather_start(h_hbm, buf_ref.at[nxt], tok_ref, (i + 1) * tile, tile, sem_ref.at[nxt])

    @pl.when(i < n)
    def _():
        cur = i % 2
        _row_gather_wait(h_hbm, buf_ref.at[cur], sem_ref.at[cur])
        o_ref[...] = buf_ref[cur].astype(o_ref.dtype)

    @pl.when(i >= n)
    def _():
        o_ref[...] = jnp.zeros_like(o_ref)


def moe_gather(pos2, meta, h, n_tiles, tile):
    m, d = h.shape
    rows = n_tiles * tile
    return pl.pallas_call(
        _moe_gather_kernel,
        grid_spec=pltpu.PrefetchScalarGridSpec(
            num_scalar_prefetch=2,
            grid=(n_tiles,),
            in_specs=[pl.BlockSpec(memory_space=pl.ANY), pl.BlockSpec(memory_space=pl.ANY)],
            out_specs=pl.BlockSpec((tile, d), lambda i, pos, meta: (i, 0)),
            scratch_shapes=[pltpu.SMEM((rows,), jnp.int32),
                            pltpu.VMEM((2, tile, d), F32),
                            pltpu.SemaphoreType.DMA((3,))]),
        out_shape=jax.ShapeDtypeStruct((rows, d), BF16),
        compiler_params=_params("arbitrary"),
        name="moe_gather",
    )(pos2, meta, h, jnp.zeros((rows,), jnp.int32))


def _moe_up_kernel(meta_ref, x_ref, w1_ref, w3_ref, o_ref):
    i = pl.program_id(1)

    @pl.when(i < meta_ref[1, 0])
    def _():
        x = x_ref[...]
        for c in range(0, o_ref.shape[1], COL_CHUNK):
            sl = slice(c, c + COL_CHUNK)
            a1 = _dot(x, w1_ref[:, sl].astype(BF16))
            a3 = _dot(x, w3_ref[:, sl].astype(BF16))
            o_ref[:, sl] = (a1 * _sigmoid(a1) * a3).astype(o_ref.dtype)

    @pl.when(i >= meta_ref[1, 0])
    def _():
        o_ref[...] = jnp.zeros_like(o_ref)


def moe_up(meta, xs, w1, w3, tile, tn=1024):
    rows, d = xs.shape
    f = w1.shape[2]
    assert f % tn == 0 and rows % tile == 0
    n_tiles = rows // tile
    last = lambda i, meta: jnp.maximum(jnp.minimum(i, meta[1, 0] - 1), 0)
    return pl.pallas_call(
        _moe_up_kernel,
        grid_spec=pltpu.PrefetchScalarGridSpec(
            num_scalar_prefetch=1,
            grid=(f // tn, n_tiles),
            in_specs=[pl.BlockSpec((tile, d), lambda j, i, meta: (last(i, meta), 0)),
                      pl.BlockSpec((None, d, tn), lambda j, i, meta: (meta[0, i], 0, j)),
                      pl.BlockSpec((None, d, tn), lambda j, i, meta: (meta[0, i], 0, j))],
            out_specs=pl.BlockSpec((tile, tn), lambda j, i, meta: (i, j))),
        out_shape=jax.ShapeDtypeStruct((rows, f), BF16),
        compiler_params=_params("arbitrary", "arbitrary"),
        name="moe_up",
    )(meta, xs, w1, w3)


def _moe_down_kernel(meta_ref, a_ref, w_ref, o_ref):
    i = pl.program_id(0)
    k = pl.program_id(1)

    active = i < meta_ref[1, 0]
    cn = COL_CHUNK

    @pl.when(active & (k == 0))
    def _():
        a = a_ref[...]
        for c in range(0, o_ref.shape[1], cn):
            o_ref[:, c:c + cn] = _dot(a, w_ref[:, c:c + cn].astype(BF16))

    @pl.when(active & (k > 0))
    def _():
        a = a_ref[...]
        for c in range(0, o_ref.shape[1], cn):
            o_ref[:, c:c + cn] += _dot(a, w_ref[:, c:c + cn].astype(BF16))

    @pl.when(jnp.logical_not(active) & (k == 0))
    def _():
        o_ref[...] = jnp.zeros_like(o_ref)


def moe_down(meta, up, w2, tile, tk=1792):
    rows, f = up.shape
    d = w2.shape[2]
    assert f % tk == 0 and rows % tile == 0
    n_tiles = rows // tile
    last = lambda i, meta: jnp.maximum(jnp.minimum(i, meta[1, 0] - 1), 0)
    return pl.pallas_call(
        _moe_down_kernel,
        grid_spec=pltpu.PrefetchScalarGridSpec(
            num_scalar_prefetch=1,
            grid=(n_tiles, f // tk),
            in_specs=[pl.BlockSpec((tile, tk), lambda i, k, meta: (last(i, meta), k)),
                      pl.BlockSpec((None, tk, d), lambda i, k, meta: (meta[0, i], k, 0))],
            out_specs=pl.BlockSpec((tile, d), lambda i, k, meta: (i, 0))),
        out_shape=jax.ShapeDtypeStruct((rows, d), F32),
        compiler_params=_params("arbitrary", "arbitrary"),
        name="moe_down",
    )(meta, up, w2)


def _moe_combine_kernel(pos_ref, x_ref, wt_ref, g_ref, y_hbm, o_ref, buf_ref, sem_ref, *, final_norm):
    i = pl.program_id(0)
    n = pl.num_programs(0)
    tc = x_ref.shape[0]
    m = pos_ref.shape[0] // 2

    def start(step, slot):
        for s in range(2):
            _row_gather_start(y_hbm, buf_ref.at[slot, s], pos_ref, s * m + step * tc, tc, sem_ref.at[slot, s])

    @pl.when(i == 0)
    def _():
        start(0, 0)

    @pl.when(i + 1 < n)
    def _():
        start(i + 1, (i + 1) % 2)

    cur = i % 2
    for s in range(2):
        _row_gather_wait(y_hbm, buf_ref.at[cur, s], sem_ref.at[cur, s])
    wt = wt_ref[...]
    out = x_ref[...] + wt[:, 0:1] * buf_ref[cur, 0] + wt[:, 1:2] * buf_ref[cur, 1]
    o_ref[...] = _rms(out, g_ref[...]) if final_norm else out


def moe_combine(pos2, x, wt, y, norm_g=None, tc=256):
    m, d = x.shape
    final_norm = norm_g is not None
    g = (norm_g if final_norm else jnp.ones((d,), F32)).reshape(1, d)
    return pl.pallas_call(
        functools.partial(_moe_combine_kernel, final_norm=final_norm),
        grid_spec=pltpu.PrefetchScalarGridSpec(
            num_scalar_prefetch=1,
            grid=(m // tc,),
            in_specs=[pl.BlockSpec((tc, d), lambda i, pos: (i, 0)),
                      pl.BlockSpec((tc, wt.shape[1]), lambda i, pos: (i, 0)),
                      pl.BlockSpec((1, d), lambda i, pos: (0, 0)),
                      pl.BlockSpec(memory_space=pl.ANY)],
            out_specs=pl.BlockSpec((tc, d), lambda i, pos: (i, 0)),
            scratch_shapes=[pltpu.VMEM((2, 2, tc, d), F32),
                            pltpu.SemaphoreType.DMA((2, 2))]),
        out_shape=jax.ShapeDtypeStruct((m, d), F32),
        compiler_params=_params("arbitrary"),
        name="moe_combine",
    )(pos2, x, wt, g, y)


def moe_layer(x, norm_g, router_w, router_b, w1, w3, w2, out_norm_g=None):
    m, d = x.shape
    n_e = router_w.shape[1]
    tile = MOE_TILE
    n_tiles = 2 * m // tile + n_e
    h, sel, w = norm_router(x, norm_g, router_w, router_b)
    pos, wt, meta = moe_route(sel, w, tile)
    pos2 = pos[:2].reshape(-1)
    xs = moe_gather(pos2, meta, h, n_tiles, tile)
    up = moe_up(meta, xs, w1, w3, tile)
    y = moe_down(meta, up, w2, tile)
    return moe_combine(pos2, x, wt, y, out_norm_g)


def _mixer(x, batch, layer, norm_g, w_in, pe_k, pe_v, kw1, kw2, vw1, vw2, ln_g, ln_b, w_s, b_s, w_a, w_b, w_out):
    h = rmsnorm(x, norm_g, BF16)
    qt, kvr, vt, ngt, rest = in_projection(h, w_in, layer)
    cmp_kv = compress(kvr, batch, pe_k, pe_v, kw1, kw2, vw1, vw2)
    oct, sel = compressed_and_select(qt, cmp_kv, batch)
    oa = selected_window_attention(qt, kvr, vt, sel, oct, ngt, batch)
    ob = spatial_gating(rest, ln_g, ln_b, w_s, b_s)
    y = merge_branches(oa, ob, rest, w_a, w_b, layer)
    return residual_matmul("out_proj", y, w_out, layer, x)


def kernel(x, norm_mix, w_in, cmp_pe_k, cmp_pe_v, cmp_k_w1, cmp_k_w2, cmp_v_w1, cmp_v_w2, sgu_ln_g, sgu_ln_b, sgu_w, sgu_b, w_branch_a, w_branch_b, w_out, norm_ffn, ffn_w1, ffn_w3, ffn_w2, router_w, router_b, moe_w1, moe_w3, moe_w2, norm_f):
    batch, seq, d = x.shape
    depth = norm_mix.shape[0]
    xf = x.reshape(batch * seq, d)
    normed = False
    for layer in range(depth):
        xf = _mixer(xf, batch, layer, norm_mix[layer], w_in, cmp_pe_k[layer], cmp_pe_v[layer],
                    cmp_k_w1[layer], cmp_k_w2[layer], cmp_v_w1[layer], cmp_v_w2[layer],
                    sgu_ln_g[layer], sgu_ln_b[layer], sgu_w[layer], sgu_b[layer],
                    w_branch_a, w_branch_b, w_out)
        j = layer // 2
        if layer % 2 == 0:
            h = rmsnorm(xf, norm_ffn[layer], BF16)
            up = swiglu_up(h, ffn_w1[j], ffn_w3[j])
            xf = mm_acc("ffn_down", up, ffn_w2[j], xf, 1024, 1024, up.shape[1] // 4)
        else:
            normed = layer == depth - 1
            xf = moe_layer(xf, norm_ffn[layer], router_w[j], router_b[j], moe_w1[j], moe_w3[j], moe_w2[j],
                           norm_f if normed else None)
    if not normed:
        xf = rmsnorm(xf, norm_f, F32)
    return xf.reshape(batch, seq, d)
```

```python
import functools

import numpy as np
import jax
import jax.numpy as jnp
from jax import lax
from jax.experimental import pallas as pl
from jax.experimental.pallas import tpu as pltpu

F32 = jnp.float32
BF16 = jnp.bfloat16

D_MODEL = 2048
N_Q_HEADS = 16
N_KV_HEADS = 4
HEAD_DIM = 64
Q_PER_KV = N_Q_HEADS // N_KV_HEADS
NSA_WIDTH = N_Q_HEADS * HEAD_DIM
KV_WIDTH = N_KV_HEADS * HEAD_DIM
GROUP_WIDTH = Q_PER_KV * HEAD_DIM
CMP_BLOCK = 32
CMP_STRIDE = 16
CMP_HIDDEN = 256
SLC_BLOCK = 64
N_SELECT = 16
WINDOW = 512
FORCED_SCORE = 1e4
SGU_WIDTH = D_MODEL // 2
SGU_GROUPS = 8
SGU_CHUNK = 128
N_EXPERTS = 8
NORM_EPS = 1e-5
NEG = -1e30
LANES = 128
VMEM_LIMIT = 56 * 1024 * 1024

Q_OFF = 0
KV_OFF = NSA_WIDTH
NG_OFF = KV_OFF + 6 * KV_WIDTH
NG_WIDTH = 3 * N_Q_HEADS
REST_OFF = NG_OFF + NG_WIDTH


def _params(*sem):
    return pltpu.CompilerParams(dimension_semantics=sem, vmem_limit_bytes=VMEM_LIMIT)


def _dot(a, b):
    return jnp.dot(a, b, preferred_element_type=F32)


def _dot_nt(a, b):
    return lax.dot_general(a, b, (((1,), (1,)), ((), ())), preferred_element_type=F32)


def _dot_split(a_f32, b_bf16):
    hi = a_f32.astype(BF16)
    lo = (a_f32 - hi.astype(F32)).astype(BF16)
    return _dot(hi, b_bf16) + _dot(lo, b_bf16)


def _sigmoid(x):
    return 1.0 / (1.0 + jnp.exp(-x))


def _gelu(x):
    return x * (0.5 * (1.0 + jnp.tanh(0.7978845608028654 * (x + 0.044715 * (x * x * x)))))


def _rms(x, g):
    return x * lax.rsqrt(jnp.mean(x * x, axis=-1, keepdims=True) + NORM_EPS) * g


def _rmsnorm_kernel(x_ref, g_ref, o_ref):
    o_ref[...] = _rms(x_ref[...], g_ref[...]).astype(o_ref.dtype)


def rmsnorm(x, g, out_dtype, tm=512):
    m, d = x.shape
    return pl.pallas_call(
        _rmsnorm_kernel,
        grid=(m // tm,),
        in_specs=[pl.BlockSpec((tm, d), lambda i: (i, 0)),
                  pl.BlockSpec((1, d), lambda i: (0, 0))],
        out_specs=pl.BlockSpec((tm, d), lambda i: (i, 0)),
        out_shape=jax.ShapeDtypeStruct((m, d), out_dtype),
        compiler_params=_params("parallel"),
        name="rmsnorm",
    )(x, g.reshape(1, d))


def _norm_router_kernel(x_ref, g_ref, rwt_ref, rb_ref, h_ref, sel_ref, w_ref):
    h = _rms(x_ref[...], g_ref[...])
    h_ref[...] = h
    logits = lax.dot_general(rwt_ref[...], h, (((1,), (1,)), ((), ())), preferred_element_type=F32,
                             precision=lax.Precision.HIGHEST) + rb_ref[...]
    n_e = logits.shape[0]
    e = lax.broadcasted_iota(jnp.int32, logits.shape, 0)
    m1 = jnp.max(logits, axis=0, keepdims=True)
    i1 = jnp.min(jnp.where(logits == m1, e, n_e), axis=0, keepdims=True)
    rest = jnp.where(e == i1, -jnp.inf, logits)
    m2 = jnp.max(rest, axis=0, keepdims=True)
    i2 = jnp.min(jnp.where(rest == m2, e, n_e), axis=0, keepdims=True)
    e2 = jnp.exp(m2 - m1)
    w1 = 1.0 / (1.0 + e2)
    w2 = e2 / (1.0 + e2)
    sel_ref[...] = jnp.where((e == i1) | (e == i2), 1.0, 0.0)
    w_ref[...] = jnp.where(e == i1, w1, 0.0) + jnp.where(e == i2, w2, 0.0)


def norm_router(x, g, router_w, router_b, tm=512):
    m, d = x.shape
    n_e = router_w.shape[1]
    return pl.pallas_call(
        _norm_router_kernel,
        grid=(m // tm,),
        in_specs=[pl.BlockSpec((tm, d), lambda i: (i, 0)),
                  pl.BlockSpec((1, d), lambda i: (0, 0)),
                  pl.BlockSpec((n_e, d), lambda i: (0, 0)),
                  pl.BlockSpec((n_e, 1), lambda i: (0, 0))],
        out_specs=[pl.BlockSpec((tm, d), lambda i: (i, 0)),
                   pl.BlockSpec((n_e, tm), lambda i: (0, i)),
                   pl.BlockSpec((n_e, tm), lambda i: (0, i))],
        out_shape=[jax.ShapeDtypeStruct((m, d), F32),
                   jax.ShapeDtypeStruct((n_e, m), F32),
                   jax.ShapeDtypeStruct((n_e, m), F32)],
        compiler_params=_params("parallel"),
        name="norm_router",
    )(x, g.reshape(1, d), router_w.T, router_b.reshape(n_e, 1))


def _mm_full_kernel(*refs, n_a, n_b, n_e, n_o, pairs, inner_axis, epilogue, b_transposed):
    a_refs = refs[:n_a]
    b_refs = refs[n_a:n_a + n_b]
    e_refs = refs[n_a + n_b:n_a + n_b + n_e]
    o_refs = refs[n_a + n_b + n_e:n_a + n_b + n_e + n_o]
    w_refs = refs[n_a + n_b + n_e + n_o:]

    @pl.when(pl.program_id(inner_axis) == 0)
    def _():
        for b_ref, w_ref in zip(b_refs, w_refs):
            w_ref[...] = b_ref[...].astype(BF16)

    dot = _dot_nt if b_transposed else _dot
    ws = [w_ref[0] if len(w_ref.shape) == 3 else w_ref[...] for w_ref in w_refs]
    accs = [dot(a_refs[ia][...], ws[ib]) for ia, ib in pairs]
    epilogue(accs, e_refs, o_refs)


def _block_dim_size(s):
    return s.block_size if isinstance(s, pl.Element) else s


def mm_full(name, grid, a_ops, b_ops, e_ops, outs, pairs, epilogue, b_transposed=False):
    ops = a_ops + b_ops + e_ops
    kernel = functools.partial(
        _mm_full_kernel, n_a=len(a_ops), n_b=len(b_ops), n_e=len(e_ops), n_o=len(outs),
        pairs=pairs, inner_axis=len(grid) - 1, epilogue=epilogue, b_transposed=b_transposed)
    scratch = [pltpu.VMEM(tuple(_block_dim_size(s) for s in blk if s is not None), BF16) for _, blk, _ in b_ops]
    res = pl.pallas_call(
        kernel,
        grid=grid,
        in_specs=[pl.BlockSpec(blk, imap) for _, blk, imap in ops],
        out_specs=[pl.BlockSpec(blk, imap) for _, blk, imap in outs],
        out_shape=[sds for sds, _, _ in outs],
        scratch_shapes=scratch,
        compiler_params=_params(*(("arbitrary",) * len(grid))),
        name=name,
    )(*[arr for arr, _, _ in ops])
    return res


COL_CHUNK = 256


def _mm_acc_kernel(a_ref, b_ref, r_ref, o_ref):
    cols = range(0, o_ref.shape[1], COL_CHUNK)

    @pl.when(pl.program_id(2) == 0)
    def _():
        a = a_ref[...]
        for c in cols:
            sl = slice(c, c + COL_CHUNK)
            o_ref[:, sl] = r_ref[:, sl] + _dot(a, b_ref[:, sl].astype(BF16))

    @pl.when(pl.program_id(2) > 0)
    def _():
        a = a_ref[...]
        for c in cols:
            sl = slice(c, c + COL_CHUNK)
            o_ref[:, sl] += _dot(a, b_ref[:, sl].astype(BF16))


def mm_acc(name, a, b, res, tm, tn, tk):
    m, k = a.shape
    n = b.shape[1]
    assert m % tm == 0 and n % tn == 0 and k % tk == 0 and tn % COL_CHUNK == 0 and tk % LANES == 0
    return pl.pallas_call(
        _mm_acc_kernel,
        grid=(m // tm, n // tn, k // tk),
        in_specs=[pl.BlockSpec((tm, tk), lambda i, j, kk: (i, kk)),
                  pl.BlockSpec((tk, tn), lambda i, j, kk: (kk, j)),
                  pl.BlockSpec((tm, tn), lambda i, j, kk: (i, j))],
        out_specs=pl.BlockSpec((tm, tn), lambda i, j, kk: (i, j)),
        out_shape=jax.ShapeDtypeStruct((m, n), F32),
        compiler_params=_params("parallel", "parallel", "arbitrary"),
        name=name,
    )(a, b, res)


def in_projection(h, w_in_t, layer):
    m, d = h.shape
    tm = 1024
    a_op = [(h, (tm, d), lambda j, i: (i, 0))]

    tq = 512
    q_heads = tq // HEAD_DIM

    def q_epi(accs, e_refs, o_refs):
        for c in range(q_heads):
            o_refs[0][c] = (accs[0][:, c * HEAD_DIM:(c + 1) * HEAD_DIM] * (HEAD_DIM ** -0.5)).T.astype(BF16)

    (qt,) = mm_full(
        "in_proj_q", (NSA_WIDTH // tq, m // tm), a_op,
        [(w_in_t, (None, tq, d), lambda j, i: (layer, j, 0))], [],
        [(jax.ShapeDtypeStruct((N_Q_HEADS, HEAD_DIM, m), BF16), (q_heads, HEAD_DIM, tm), lambda j, i: (j, 0, i))],
        [(0, 0)], q_epi, b_transposed=True)

    sec0 = KV_OFF // KV_WIDTH

    def rows_epi(accs, e_refs, o_refs):
        for c in range(N_KV_HEADS):
            o_refs[0][c] = accs[0][:, c * HEAD_DIM:(c + 1) * HEAD_DIM].astype(BF16)

    (kvr,) = mm_full(
        "in_proj_kv_rows", (4, m // tm), a_op,
        [(w_in_t, (None, KV_WIDTH, d), lambda j, i: (layer, sec0 + j + j // 3, 0))], [],
        [(jax.ShapeDtypeStruct((4 * N_KV_HEADS, m, HEAD_DIM), BF16),
          (N_KV_HEADS, tm, HEAD_DIM), lambda j, i: (j, i, 0))],
        [(0, 0)], rows_epi, b_transposed=True)

    def cols_epi(accs, e_refs, o_refs):
        for c in range(N_KV_HEADS):
            o_refs[0][c] = accs[0][:, c * HEAD_DIM:(c + 1) * HEAD_DIM].T.astype(BF16)

    (vt,) = mm_full(
        "in_proj_v_cols", (2, m // tm), a_op,
        [(w_in_t, (None, KV_WIDTH, d), lambda j, i: (layer, sec0 + 3 + 2 * j, 0))], [],
        [(jax.ShapeDtypeStruct((2 * N_KV_HEADS, HEAD_DIM, m), BF16),
          (N_KV_HEADS, HEAD_DIM, tm), lambda j, i: (j, 0, i))],
        [(0, 0)], cols_epi, b_transposed=True)

    def f32_epi(accs, e_refs, o_refs):
        o_refs[0][...] = accs[0]

    def f32_t_epi(accs, e_refs, o_refs):
        o_refs[0][...] = accs[0].T

    assert NG_OFF % LANES == 0
    (ngt,) = mm_full(
        "in_proj_ng", (1, m // tm), a_op,
        [(w_in_t, (None, LANES, d), lambda j, i: (layer, NG_OFF // LANES, 0))], [],
        [(jax.ShapeDtypeStruct((LANES, m), F32), (LANES, tm), lambda j, i: (0, i))],
        [(0, 0)], f32_t_epi, b_transposed=True)

    n_rest = w_in_t.shape[1] - REST_OFF
    tr = 512
    assert n_rest % tr == 0 and REST_OFF % 8 == 0
    (rest,) = mm_full(
        "in_proj_rest", (n_rest // tr, m // tm), a_op,
        [(w_in_t, (pl.Element(1), pl.Element(tr), pl.Element(d)), lambda j, i: (layer, pl.multiple_of(REST_OFF + j * tr, 8), 0))], [],
        [(jax.ShapeDtypeStruct((m, n_rest), F32), (tm, tr), lambda j, i: (i, j))],
        [(0, 0)], f32_epi, b_transposed=True)
    return qt, kvr, vt, ngt, rest


def _cmp_kernel(a_ref, w1_ref, w2_ref, pe_ref, o_ref):
    a = a_ref[...]
    rows = a.shape[0]
    half = a.shape[1]
    w1 = w1_ref[...].astype(BF16)
    p0 = _dot(a, w1[:half])
    p1 = _dot(a, w1[half:])
    bias = _dot(pe_ref[...].astype(BF16), w1)[0:1]
    hid = p0 + pltpu.roll(p1, rows - 1, 0) + bias
    o_ref[...] = _dot(_gelu(hid).astype(BF16), w2_ref[...].astype(BF16))


def compress(kv, batch, pe_k, pe_v, kw1, kw2, vw1, vw2):
    m = kv.shape[1]
    n_chunks = m // batch // CMP_STRIDE
    rows = batch * n_chunks
    feat = CMP_STRIDE * HEAD_DIM
    a = kv[:2 * N_KV_HEADS].reshape(2, N_KV_HEADS, rows, feat)
    w1 = jnp.stack([kw1, vw1])
    w2 = jnp.stack([kw2, vw2])
    pe = jnp.stack([pe_k, pe_v]).reshape(2, 1, CMP_BLOCK * HEAD_DIM)
    pe = jnp.broadcast_to(pe, (2, 8, CMP_BLOCK * HEAD_DIM))
    return pl.pallas_call(
        _cmp_kernel,
        grid=(2, N_KV_HEADS),
        in_specs=[pl.BlockSpec((None, None, rows, feat), lambda s, g: (s, g, 0, 0)),
                  pl.BlockSpec((None, 2 * feat, CMP_HIDDEN), lambda s, g: (s, 0, 0)),
                  pl.BlockSpec((None, CMP_HIDDEN, HEAD_DIM), lambda s, g: (s, 0, 0)),
                  pl.BlockSpec((None, 8, 2 * feat), lambda s, g: (s, 0, 0))],
        out_specs=pl.BlockSpec((None, None, rows, HEAD_DIM), lambda s, g: (s, g, 0, 0)),
        out_shape=jax.ShapeDtypeStruct((2, N_KV_HEADS, rows, HEAD_DIM), F32),
        compiler_params=_params("parallel", "parallel"),
        name="nsa_compress",
    )(a, w1, w2, pe)


def _alibi_slopes():
    return np.array([2.0 ** (-8.0 * (h + 1) / N_Q_HEADS) for h in range(N_Q_HEADS)], dtype=np.float32)


def _overlap_t(n_cmp_pad, n_slc):
    cs = np.arange(n_cmp_pad)[None, :] * CMP_STRIDE
    ss = np.arange(n_slc)[:, None] * SLC_BLOCK
    ov = np.clip(np.minimum(cs + CMP_BLOCK, ss + SLC_BLOCK) - np.maximum(cs, ss), 0, None)
    return (ov / CMP_STRIDE).astype(np.float32)


def _sel_kernel(slopes_ref, q_ref, kc_ref, vc_ref, ovt_ref, oc_ref, sel_ref, *, n_cmp):
    g = pl.program_id(1)
    i = pl.program_id(2)
    tq = q_ref.shape[2]
    ncp = kc_ref.shape[0]
    n_slc = ovt_ref.shape[0]
    kc = kc_ref[...].astype(BF16)
    vct = vc_ref[...].T.astype(BF16)

    t = i * tq + lax.broadcasted_iota(jnp.int32, (ncp, tq), 1)
    c = lax.broadcasted_iota(jnp.int32, (ncp, tq), 0)
    dist = t - (c * CMP_STRIDE + (CMP_BLOCK - 1))
    mask = (dist >= 0) & (c < n_cmp)
    distf = dist.astype(F32)

    p_sum = jnp.zeros((ncp, tq), F32)
    for j in range(Q_PER_KV):
        s = _dot(kc, q_ref[j])
        s = s - slopes_ref[g * Q_PER_KV + j] * distf
        s = jnp.where(mask, s, NEG)
        mx = jnp.max(s, axis=0, keepdims=True)
        e = jnp.where(mask, jnp.exp(s - mx), 0.0)
        p = e / jnp.maximum(jnp.sum(e, axis=0, keepdims=True), 1e-30)
        p_sum = p_sum + p
        oc_ref[j] = _dot(vct, p.astype(BF16))

    ovt = ovt_ref[...]
    hi = p_sum.astype(BF16)
    lo = (p_sum - hi.astype(F32)).astype(BF16)
    imp = _dot(ovt, hi) + _dot(ovt, lo)
    tt = i * tq + lax.broadcasted_iota(jnp.int32, (n_slc, tq), 1)
    blk = lax.broadcasted_iota(jnp.int32, (n_slc, tq), 0)
    cur = tt // SLC_BLOCK
    valid = blk * SLC_BLOCK <= tt
    forced = (blk == 0) | (blk == cur) | (blk == cur - 1)
    score = jnp.where(valid, imp, -1.0)
    score = jnp.where(forced, FORCED_SCORE, score)
    rank = jnp.zeros((n_slc, tq), F32)
    for mrow in range(n_slc):
        other = jnp.broadcast_to(score[mrow:mrow + 1, :], (n_slc, tq))
        beats = (other > score) | ((other == score) & (blk > mrow))
        rank = rank + jnp.where(beats, 1.0, 0.0)
    sel_ref[...] = jnp.where(rank < float(min(N_SELECT, n_slc)), 1.0, 0.0)


def compressed_and_select(qt, cmp_kv, batch, tq=256):
    m = qt.shape[2]
    seq = m // batch
    nq = seq // tq
    ncp = cmp_kv.shape[2] // batch
    n_slc = seq // SLC_BLOCK
    ovt = jnp.asarray(_overlap_t(ncp, n_slc), BF16)
    slopes = jnp.asarray(_alibi_slopes())
    kernel = functools.partial(_sel_kernel, n_cmp=ncp - 1)
    return pl.pallas_call(
        kernel,
        grid=(batch, N_KV_HEADS, nq),
        in_specs=[pl.BlockSpec(memory_space=pltpu.SMEM),
                  pl.BlockSpec((Q_PER_KV, HEAD_DIM, tq), lambda b, g, i: (g, 0, b * nq + i)),
                  pl.BlockSpec((None, None, ncp, HEAD_DIM), lambda b, g, i: (0, g, b, 0)),
                  pl.BlockSpec((None, None, ncp, HEAD_DIM), lambda b, g, i: (1, g, b, 0)),
                  pl.BlockSpec((n_slc, ncp), lambda b, g, i: (0, 0))],
        out_specs=[pl.BlockSpec((Q_PER_KV, HEAD_DIM, tq), lambda b, g, i: (g, 0, b * nq + i)),
                   pl.BlockSpec((None, None, n_slc, tq), lambda b, g, i: (b, g, 0, i))],
        out_shape=[jax.ShapeDtypeStruct((N_Q_HEADS, HEAD_DIM, m), F32),
                   jax.ShapeDtypeStruct((batch, N_KV_HEADS, n_slc, seq), F32)],
        compiler_params=_params("parallel", "parallel", "parallel"),
        name="nsa_compressed_select",
    )(slopes, qt, cmp_kv, cmp_kv, ovt)


AUX_SLOPE = HEAD_DIM
AUX_SEL = HEAD_DIM + 16
ATTN_K = 2 * HEAD_DIM
STRIP = 32


def _attn_kernel(slopes_ref, q_ref, ks_ref, vs_ref, kw_ref, vw_ref, auxs_ref, auxw_ref, sel_ref, oc_ref,
                 ng_ref, o_ref, ka_s_ref, ka_w_ref, qa_ref, s_ref, p_ref, mask_ref, m_ref, l_ref, acc_ref,
                 os_ref, sig_ref):
    g = pl.program_id(1)
    i = pl.program_id(2)
    tq = q_ref.shape[2]
    tk = tq
    q0 = i * tq
    n_slc = sel_ref.shape[0]

    @pl.when(i == 0)
    def _():
        ka_s_ref[:, 0:HEAD_DIM] = ks_ref[...]
        ka_s_ref[:, HEAD_DIM:ATTN_K] = auxs_ref[...]
        ka_w_ref[:, 0:HEAD_DIM] = kw_ref[...]
        ka_w_ref[:, HEAD_DIM:ATTN_K] = auxw_ref[...]
        kr = lax.broadcasted_iota(jnp.int32, (tk, tq), 0)
        qc = lax.broadcasted_iota(jnp.int32, (tk, tq), 1)
        mask_ref[0] = jnp.where(kr <= qc, 0.0, NEG)
        mask_ref[1] = jnp.where(kr > qc, 0.0, NEG)

    sig_ref[...] = _sigmoid(ng_ref[...])
    sel_neg = (sel_ref[...] - 1.0) * (-NEG)
    row16 = lax.broadcasted_iota(jnp.int32, (16, tq), 0)
    pad = jnp.zeros((ATTN_K - AUX_SEL - n_slc, tq), F32)
    for j in range(Q_PER_KV):
        sl = jnp.full((16, tq), slopes_ref[g * Q_PER_KV + j], F32)
        hi = sl.astype(BF16).astype(F32)
        mid = (sl - hi).astype(BF16).astype(F32)
        lo = (sl - hi - mid).astype(BF16).astype(F32)
        pieces = jnp.where(row16 == 0, hi, jnp.where(row16 == 1, mid, jnp.where(row16 == 2, lo, 0.0)))
        qa = jnp.concatenate([q_ref[j].astype(F32), pieces, sel_neg, pad], axis=0)
        qa_ref[j] = qa.astype(BF16)

    def chunk(ka_ref, v_ref, k0, mask_idx, first):
        ka = ka_ref[pl.ds(k0, tk), :]
        vc = v_ref[:, pl.ds(k0, tk)]
        for j in range(Q_PER_KV):
            s_ref[j] = _dot(ka, qa_ref[j])
        for j in range(Q_PER_KV):
            shift = slopes_ref[g * Q_PER_KV + j] * k0.astype(F32)
            mx = None
            for r in range(0, tk, STRIP):
                x = s_ref[j, r:r + STRIP, :]
                if mask_idx is not None:
                    x = x + mask_ref[mask_idx, r:r + STRIP, :]
                    s_ref[j, r:r + STRIP, :] = x
                mx = x if mx is None else jnp.maximum(mx, x)
            m_cur = jnp.max(mx, axis=0, keepdims=True) + shift
            if first:
                m_new = m_cur
            else:
                m_old = m_ref[j]
                m_new = jnp.maximum(m_old, m_cur)
                alpha = jnp.exp(m_old - m_new)
            m_ref[j] = m_new
            m_loc = m_new - shift
            ls = None
            for r in range(0, tk, STRIP):
                p = jnp.exp(s_ref[j, r:r + STRIP, :] - m_loc)
                ls = p if ls is None else ls + p
                p_ref[j, r:r + STRIP, :] = p.astype(BF16)
            l_cur = jnp.sum(ls, axis=0, keepdims=True)
            pv = _dot(vc, p_ref[j])
            if first:
                l_ref[j] = l_cur
                acc_ref[j] = pv
            else:
                l_ref[j] = alpha * l_ref[j] + l_cur
                acc_ref[j] = alpha * acc_ref[j] + pv

    k_diag = pl.multiple_of(q0, tk)
    chunk(ka_s_ref, vs_ref, k_diag, 0, True)

    def slc_body(c, carry):
        chunk(ka_s_ref, vs_ref, pl.multiple_of(c * tk, tk), None, False)
        return carry

    lax.fori_loop(0, i, slc_body, 0)
    for j in range(Q_PER_KV):
        os_ref[j] = acc_ref[j] / l_ref[j]

    chunk(ka_w_ref, vw_ref, k_diag, 0, True)
    n_back = WINDOW // tk
    for back in range(1, n_back + 1):
        @pl.when(i >= back)
        def _(back=back):
            k0 = pl.multiple_of(q0 - back * tk, tk)
            chunk(ka_w_ref, vw_ref, k0, 1 if back == n_back else None, False)

    for j in range(Q_PER_KV):
        base = (g * Q_PER_KV + j) * 3
        out_t = (sig_ref[pl.ds(base, 1), :] * oc_ref[j]
                 + sig_ref[pl.ds(base + 1, 1), :] * os_ref[j]
                 + sig_ref[pl.ds(base + 2, 1), :] * (acc_ref[j] / l_ref[j]))
        o_ref[:, j * HEAD_DIM:(j + 1) * HEAD_DIM] = out_t.T.astype(o_ref.dtype)


def _key_aux(seq, tk, with_blocks):
    aux = np.zeros((seq, ATTN_K - HEAD_DIM), np.float32)
    pos = np.arange(seq)
    aux[:, AUX_SLOPE - HEAD_DIM:AUX_SLOPE - HEAD_DIM + 3] = (pos % tk)[:, None]
    if with_blocks:
        aux[pos, AUX_SEL - HEAD_DIM + pos // SLC_BLOCK] = 1.0
    return aux


def selected_window_attention(qt, kvr, vt, sel, oct, ngt, batch, tq=256):
    m = qt.shape[2]
    seq = m // batch
    nq = seq // tq
    n_slc = seq // SLC_BLOCK
    assert tq <= 256 and AUX_SEL + n_slc <= ATTN_K
    aux_s = jnp.asarray(_key_aux(seq, tq, True), BF16)
    aux_w = jnp.asarray(_key_aux(seq, tq, False), BF16)
    slopes = jnp.asarray(_alibi_slopes())
    n_g = N_KV_HEADS

    def k_spec(section):
        return pl.BlockSpec((None, seq, HEAD_DIM), lambda b, g, i: (section * n_g + g, b, 0))

    def v_spec(section):
        return pl.BlockSpec((None, HEAD_DIM, seq), lambda b, g, i: (section * n_g + g, 0, b))

    head_blk = pl.BlockSpec((Q_PER_KV, HEAD_DIM, tq), lambda b, g, i: (g, 0, b * nq + i))
    aux_blk = pl.BlockSpec((seq, ATTN_K - HEAD_DIM), lambda b, g, i: (0, 0))
    return pl.pallas_call(
        _attn_kernel,
        grid=(batch, N_KV_HEADS, nq),
        in_specs=[pl.BlockSpec(memory_space=pltpu.SMEM),
                  head_blk,
                  k_spec(2), v_spec(0), k_spec(3), v_spec(1),
                  aux_blk, aux_blk,
                  pl.BlockSpec((None, None, n_slc, tq), lambda b, g, i: (b, g, 0, i)),
                  head_blk,
                  pl.BlockSpec((LANES, tq), lambda b, g, i: (0, b * nq + i))],
        out_specs=pl.BlockSpec((tq, GROUP_WIDTH), lambda b, g, i: (b * nq + i, g)),
        out_shape=jax.ShapeDtypeStruct((m, NSA_WIDTH), BF16),
        scratch_shapes=[pltpu.VMEM((seq, ATTN_K), BF16),
                        pltpu.VMEM((seq, ATTN_K), BF16),
                        pltpu.VMEM((Q_PER_KV, ATTN_K, tq), BF16),
                        pltpu.VMEM((Q_PER_KV, tq, tq), F32),
                        pltpu.VMEM((Q_PER_KV, tq, tq), BF16),
                        pltpu.VMEM((2, tq, tq), F32),
                        pltpu.VMEM((Q_PER_KV, 1, tq), F32),
                        pltpu.VMEM((Q_PER_KV, 1, tq), F32),
                        pltpu.VMEM((Q_PER_KV, HEAD_DIM, tq), F32),
                        pltpu.VMEM((Q_PER_KV, HEAD_DIM, tq), F32),
                        pltpu.VMEM((LANES, tq), F32)],
        compiler_params=_params("arbitrary", "arbitrary", "arbitrary"),
        name="nsa_selected_window",
    )(slopes, qt, kvr, vt, kvr, vt, aux_s, aux_w, sel, oct, ngt)


def _sgu_kernel(u_ref, v_ref, lg_ref, lb_ref, ws_ref, bs_ref, o_ref):
    rows = u_ref.shape[0]
    v = _gelu(v_ref[...])
    mu = jnp.mean(v, axis=-1, keepdims=True)
    var = jnp.mean(jnp.square(v - mu), axis=-1, keepdims=True)
    vl = ((v - mu) * lax.rsqrt(var + NORM_EPS) * lg_ref[...] + lb_ref[...]).astype(BF16)
    r = lax.broadcasted_iota(jnp.int32, (SGU_CHUNK, SGU_CHUNK), 0)
    c = lax.broadcasted_iota(jnp.int32, (SGU_CHUNK, SGU_CHUNK), 1)
    gd = SGU_WIDTH // SGU_GROUPS
    for grp in range(SGU_GROUPS):
        w = jnp.where(c <= r, ws_ref[grp], 0.0).astype(BF16)
        lanes = slice(grp * gd, (grp + 1) * gd)
        for n in range(rows // SGU_CHUNK):
            rs = slice(n * SGU_CHUNK, (n + 1) * SGU_CHUNK)
            vm = _dot(w, vl[rs, lanes]) + bs_ref[:, lanes]
            o_ref[rs, lanes] = (_gelu(u_ref[rs, lanes]) * vm).astype(o_ref.dtype)


def spatial_gating(rest, ln_g, ln_b, w_s, b_s, tm=512):
    m = rest.shape[0]
    gd = SGU_WIDTH // SGU_GROUPS
    bias = jnp.repeat(b_s.T, gd, axis=1)
    return pl.pallas_call(
        _sgu_kernel,
        grid=(m // tm,),
        in_specs=[pl.BlockSpec((tm, SGU_WIDTH), lambda i: (i, 0)),
                  pl.BlockSpec((tm, SGU_WIDTH), lambda i: (i, 1)),
                  pl.BlockSpec((1, SGU_WIDTH), lambda i: (0, 0)),
                  pl.BlockSpec((1, SGU_WIDTH), lambda i: (0, 0)),
                  pl.BlockSpec((SGU_GROUPS, SGU_CHUNK, SGU_CHUNK), lambda i: (0, 0, 0)),
                  pl.BlockSpec((SGU_CHUNK, SGU_WIDTH), lambda i: (0, 0))],
        out_specs=pl.BlockSpec((tm, SGU_WIDTH), lambda i: (i, 0)),
        out_shape=jax.ShapeDtypeStruct((m, SGU_WIDTH), BF16),
        compiler_params=_params("parallel"),
        name="spatial_gating",
    )(rest, rest, ln_g.reshape(1, -1), ln_b.reshape(1, -1), w_s, bias)


def merge_branches(oa, ob, rest, w_a, w_b, layer, tm=1024, tn=512):
    m = oa.shape[0]
    d = w_a.shape[2]
    ga_blk = 2 * SGU_WIDTH // tn
    gb_blk = (2 * SGU_WIDTH + d) // tn

    def epi(accs, e_refs, o_refs):
        y = _sigmoid(e_refs[0][...]) * accs[0] + _sigmoid(e_refs[1][...]) * accs[1]
        o_refs[0][...] = y.astype(BF16)

    (y,) = mm_full(
        "merge_branches", (d // tn, m // tm),
        [(oa, (tm, oa.shape[1]), lambda j, i: (i, 0)), (ob, (tm, ob.shape[1]), lambda j, i: (i, 0))],
        [(w_a, (None, w_a.shape[1], tn), lambda j, i: (layer, 0, j)),
         (w_b, (None, w_b.shape[1], tn), lambda j, i: (layer, 0, j))],
        [(rest, (tm, tn), lambda j, i: (i, ga_blk + j)), (rest, (tm, tn), lambda j, i: (i, gb_blk + j))],
        [(jax.ShapeDtypeStruct((m, d), BF16), (tm, tn), lambda j, i: (i, j))],
        [(0, 0), (1, 1)], epi)
    return y


def residual_matmul(name, a, w, layer, x, tm=1024, tn=512):
    m, k = a.shape
    n = w.shape[2]

    def epi(accs, e_refs, o_refs):
        o_refs[0][...] = e_refs[0][...] + accs[0]

    (out,) = mm_full(
        name, (n // tn, m // tm),
        [(a, (tm, k), lambda j, i: (i, 0))],
        [(w, (None, k, tn), lambda j, i: (layer, 0, j))],
        [(x, (tm, tn), lambda j, i: (i, j))],
        [(jax.ShapeDtypeStruct((m, n), F32), (tm, tn), lambda j, i: (i, j))],
        [(0, 0)], epi)
    return out


def swiglu_up(h, w1, w3, tm=1024, tn=512):
    m, d = h.shape
    f = w1.shape[1]

    def epi(accs, e_refs, o_refs):
        o_refs[0][...] = (accs[0] * _sigmoid(accs[0]) * accs[1]).astype(BF16)

    (out,) = mm_full(
        "swiglu_up", (f // tn, m // tm),
        [(h, (tm, d), lambda j, i: (i, 0))],
        [(w1, (d, tn), lambda j, i: (0, j)), (w3, (d, tn), lambda j, i: (0, j))],
        [],
        [(jax.ShapeDtypeStruct((m, f), BF16), (tm, tn), lambda j, i: (i, j))],
        [(0, 0), (0, 1)], epi)
    return out


MOE_TILE = 512


def _route_kernel(sel_ref, w_ref, tri_ref, pos_ref, wt_ref, meta_ref, cum_ref, *, tile):
    n_e, m = sel_ref.shape
    ck = tri_ref.shape[0]
    carry = jnp.zeros((n_e, 1), F32)
    for c in range(m // ck):
        sl = slice(c * ck, (c + 1) * ck)
        cs = _dot(sel_ref[:, sl].astype(BF16), tri_ref[...]) + carry
        cum_ref[:, sl] = cs
        carry = cs[:, ck - 1:ck]
    padded = jnp.ceil(carry / tile) * tile
    sub = lax.broadcasted_iota(jnp.int32, (n_e, 1), 0)
    start = jnp.zeros((n_e, 1), F32)
    run = jnp.zeros((1, 1), F32)
    for ex in range(n_e):
        start = jnp.where(sub == ex, run, start)
        run = run + padded[ex:ex + 1, :]
    sel = sel_ref[...] > 0.5
    pos = start + cum_ref[...] - 1.0
    eidx = lax.broadcasted_iota(jnp.int32, (n_e, m), 0)
    e_lo = jnp.min(jnp.where(sel, eidx, n_e), axis=0, keepdims=True)
    e_hi = jnp.max(jnp.where(sel, eidx, -1), axis=0, keepdims=True)
    is_lo = eidx == e_lo
    is_hi = eidx == e_hi
    w = w_ref[...]
    pos_a = jnp.sum(jnp.where(is_lo, pos, 0.0), axis=0, keepdims=True)
    pos_b = jnp.sum(jnp.where(is_hi, pos, 0.0), axis=0, keepdims=True)
    w_a = jnp.sum(jnp.where(is_lo, w, 0.0), axis=0, keepdims=True)
    w_b = jnp.sum(jnp.where(is_hi, w, 0.0), axis=0, keepdims=True)
    pos_ref[...] = jnp.where(eidx == 0, pos_a, jnp.where(eidx == 1, pos_b, 0.0)).astype(jnp.int32)
    wt_ref[...] = jnp.where(eidx == 0, w_a, jnp.where(eidx == 1, w_b, 0.0)).T
    tile_lo = lax.broadcasted_iota(jnp.int32, (n_e, LANES), 1).astype(F32) * tile
    t_exp = jnp.sum(jnp.where(start + padded <= tile_lo, 1.0, 0.0), axis=0, keepdims=True)
    t_exp = jnp.minimum(t_exp, n_e - 1.0)
    r8 = lax.broadcasted_iota(jnp.int32, (n_e, LANES), 0)
    meta_ref[...] = jnp.where(r8 == 0, t_exp, jnp.where(r8 == 1, run / tile, 0.0)).astype(jnp.int32)


def moe_route(sel, w, tile):
    n_e, m = sel.shape
    ck = 256
    tri = jnp.asarray(np.triu(np.ones((ck, ck), np.float32)), BF16)
    full = lambda shape: pl.BlockSpec(shape, lambda: tuple(0 for _ in shape))
    return pl.pallas_call(
        functools.partial(_route_kernel, tile=tile),
        in_specs=[full((n_e, m)), full((n_e, m)), full((ck, ck))],
        out_specs=[full((n_e, m)), full((m, n_e)), full((n_e, LANES))],
        out_shape=[jax.ShapeDtypeStruct((n_e, m), jnp.int32),
                   jax.ShapeDtypeStruct((m, n_e), F32),
                   jax.ShapeDtypeStruct((n_e, LANES), jnp.int32)],
        scratch_shapes=[pltpu.VMEM((n_e, m), F32)],
        compiler_params=pltpu.CompilerParams(vmem_limit_bytes=VMEM_LIMIT),
        name="moe_route",
    )(sel, w, tri)


def _row_gather_start(src_hbm, dst, rows_ref, base, n_rows, sem, unroll=8):
    def body(r, carry):
        src_row = rows_ref[base + r]
        pltpu.make_async_copy(src_hbm.at[pl.ds(src_row, 1)], dst.at[pl.ds(r, 1)], sem).start()
        return carry
    lax.fori_loop(0, n_rows, body, 0, unroll=unroll)


def _row_gather_wait(src_hbm, dst, sem):
    pltpu.make_async_copy(src_hbm.at[pl.ds(0, dst.shape[0])], dst, sem).wait()


def _moe_gather_kernel(pos_ref, meta_ref, h_hbm, zeros_hbm, o_ref, tok_ref, buf_ref, sem_ref):
    i = pl.program_id(0)
    n = meta_ref[1, 0]
    tile = o_ref.shape[0]
    m = pos_ref.shape[0] // 2

    @pl.when(i == 0)
    def _():
        clear = pltpu.make_async_copy(zeros_hbm, tok_ref, sem_ref.at[2])
        clear.start()
        clear.wait()

        def fill(t, carry):
            tok_ref[pos_ref[t]] = t
            tok_ref[pos_ref[m + t]] = t
            return carry
        lax.fori_loop(0, m, fill, 0, unroll=8)
        _row_gather_start(h_hbm, buf_ref.at[0], tok_ref, 0, tile, sem_ref.at[0])

    @pl.when(i + 1 < n)
    def _():
        nxt = (i + 1) % 2
        _row_gather_start(h_hbm, buf_ref.at[nxt], tok_ref, (i + 1) * tile, tile, sem_ref.at[nxt])

    @pl.when(i < n)
    def _():
        cur = i % 2
        _row_gather_wait(h_hbm, buf_ref.at[cur], sem_ref.at[cur])
        o_ref[...] = buf_ref[cur].astype(o_ref.dtype)

    @pl.when(i >= n)
    def _():
        o_ref[...] = jnp.zeros_like(o_ref)


def moe_gather(pos2, meta, h, n_tiles, tile):
    m, d = h.shape
    rows = n_tiles * tile
    return pl.pallas_call(
        _moe_gather_kernel,
        grid_spec=pltpu.PrefetchScalarGridSpec(
            num_scalar_prefetch=2,
            grid=(n_tiles,),
            in_specs=[pl.BlockSpec(memory_space=pl.ANY), pl.BlockSpec(memory_space=pl.ANY)],
            out_specs=pl.BlockSpec((tile, d), lambda i, pos, meta: (i, 0)),
            scratch_shapes=[pltpu.SMEM((rows,), jnp.int32),
                            pltpu.VMEM((2, tile, d), F32),
                            pltpu.SemaphoreType.DMA((3,))]),
        out_shape=jax.ShapeDtypeStruct((rows, d), BF16),
        compiler_params=_params("arbitrary"),
        name="moe_gather",
    )(pos2, meta, h, jnp.zeros((rows,), jnp.int32))


def _moe_up_kernel(meta_ref, x_ref, w1_ref, w3_ref, w2_ref, o_ref, w2b_ref):
    i = pl.program_id(1)

    def round_w2_block():
        w2b_ref[...] = w2_ref[...].astype(BF16)

    @pl.when(i < meta_ref[1, 0])
    def _():
        x = x_ref[...]
        for c in range(0, o_ref.shape[1], COL_CHUNK):
            sl = slice(c, c + COL_CHUNK)
            a1 = _dot(x, w1_ref[:, sl].astype(BF16))
            a3 = _dot(x, w3_ref[:, sl].astype(BF16))
            o_ref[:, sl] = (a1 * _sigmoid(a1) * a3).astype(o_ref.dtype)
        round_w2_block()

    @pl.when(i >= meta_ref[1, 0])
    def _():
        o_ref[...] = jnp.zeros_like(o_ref)
        round_w2_block()


BF16_SUBLANES = 16


def _cast_rows_per_step(total_rows, steps):
    for rows in range(BF16_SUBLANES, total_rows + 1, BF16_SUBLANES):
        if total_rows % rows == 0 and total_rows // rows <= steps:
            return rows
    raise ValueError("weights cannot be split over the grid steps")


def moe_up(meta, xs, w1, w3, w2, tile, tn=1024):
    rows, d = xs.shape
    n_e, _, f = w1.shape
    assert f % tn == 0 and rows % tile == 0
    n_tiles = rows // tile
    w2_flat = w2.reshape(n_e * f, w2.shape[2])
    cast_rows = _cast_rows_per_step(w2_flat.shape[0], (f // tn) * n_tiles)
    n_cast = w2_flat.shape[0] // cast_rows
    last = lambda i, meta: jnp.maximum(jnp.minimum(i, meta[1, 0] - 1), 0)
    cast_blk = lambda j, i, meta: (jnp.minimum(j * n_tiles + i, n_cast - 1), 0)
    up, w2b = pl.pallas_call(
        _moe_up_kernel,
        grid_spec=pltpu.PrefetchScalarGridSpec(
            num_scalar_prefetch=1,
            grid=(f // tn, n_tiles),
            in_specs=[pl.BlockSpec((tile, d), lambda j, i, meta: (last(i, meta), 0)),
                      pl.BlockSpec((None, d, tn), lambda j, i, meta: (meta[0, i], 0, j)),
                      pl.BlockSpec((None, d, tn), lambda j, i, meta: (meta[0, i], 0, j)),
                      pl.BlockSpec((cast_rows, w2_flat.shape[1]), cast_blk)],
            out_specs=[pl.BlockSpec((tile, tn), lambda j, i, meta: (i, j)),
                       pl.BlockSpec((cast_rows, w2_flat.shape[1]), cast_blk)]),
        out_shape=[jax.ShapeDtypeStruct((rows, f), BF16),
                   jax.ShapeDtypeStruct(w2_flat.shape, BF16)],
        compiler_params=_params("arbitrary", "arbitrary"),
        name="moe_up",
    )(meta, xs, w1, w3, w2_flat)
    return up, w2b.reshape(w2.shape)


def _moe_down_kernel(meta_ref, a_ref, w_ref, o_ref):
    i = pl.program_id(0)
    k = pl.program_id(1)

    active = i < meta_ref[1, 0]
    cn = COL_CHUNK

    @pl.when(active & (k == 0))
    def _():
        a = a_ref[...]
        for c in range(0, o_ref.shape[1], cn):
            o_ref[:, c:c + cn] = _dot(a, w_ref[:, c:c + cn])

    @pl.when(active & (k > 0))
    def _():
        a = a_ref[...]
        for c in range(0, o_ref.shape[1], cn):
            o_ref[:, c:c + cn] += _dot(a, w_ref[:, c:c + cn])

    @pl.when(jnp.logical_not(active) & (k == 0))
    def _():
        o_ref[...] = jnp.zeros_like(o_ref)


def moe_down(meta, up, w2, tile, tk=1792):
    rows, f = up.shape
    d = w2.shape[2]
    assert f % tk == 0 and rows % tile == 0
    n_tiles = rows // tile
    last = lambda i, meta: jnp.maximum(jnp.minimum(i, meta[1, 0] - 1), 0)
    return pl.pallas_call(
        _moe_down_kernel,
        grid_spec=pltpu.PrefetchScalarGridSpec(
            num_scalar_prefetch=1,
            grid=(n_tiles, f // tk),
            in_specs=[pl.BlockSpec((tile, tk), lambda i, k, meta: (last(i, meta), k)),
                      pl.BlockSpec((None, tk, d), lambda i, k, meta: (meta[0, i], k, 0))],
            out_specs=pl.BlockSpec((tile, d), lambda i, k, meta: (i, 0))),
        out_shape=jax.ShapeDtypeStruct((rows, d), F32),
        compiler_params=_params("arbitrary", "arbitrary"),
        name="moe_down",
    )(meta, up, w2)


def _moe_combine_kernel(pos_ref, x_ref, wt_ref, g_ref, y_hbm, o_ref, buf_ref, sem_ref, *, final_norm):
    i = pl.program_id(0)
    n = pl.num_programs(0)
    tc = x_ref.shape[0]
    m = pos_ref.shape[0] // 2

    def start(step, slot):
        for s in range(2):
            _row_gather_start(y_hbm, buf_ref.at[slot, s], pos_ref, s * m + step * tc, tc, sem_ref.at[slot, s])

    @pl.when(i == 0)
    def _():
        start(0, 0)

    @pl.when(i + 1 < n)
    def _():
        start(i + 1, (i + 1) % 2)

    cur = i % 2
    for s in range(2):
        _row_gather_wait(y_hbm, buf_ref.at[cur, s], sem_ref.at[cur, s])
    wt = wt_ref[...]
    out = x_ref[...] + wt[:, 0:1] * buf_ref[cur, 0] + wt[:, 1:2] * buf_ref[cur, 1]
    o_ref[...] = _rms(out, g_ref[...]) if final_norm else out


def moe_combine(pos2, x, wt, y, norm_g=None, tc=256):
    m, d = x.shape
    final_norm = norm_g is not None
    g = (norm_g if final_norm else jnp.ones((d,), F32)).reshape(1, d)
    return pl.pallas_call(
        functools.partial(_moe_combine_kernel, final_norm=final_norm),
        grid_spec=pltpu.PrefetchScalarGridSpec(
            num_scalar_prefetch=1,
            grid=(m // tc,),
            in_specs=[pl.BlockSpec((tc, d), lambda i, pos: (i, 0)),
                      pl.BlockSpec((tc, wt.shape[1]), lambda i, pos: (i, 0)),
                      pl.BlockSpec((1, d), lambda i, pos: (0, 0)),
                      pl.BlockSpec(memory_space=pl.ANY)],
            out_specs=pl.BlockSpec((tc, d), lambda i, pos: (i, 0)),
            scratch_shapes=[pltpu.VMEM((2, 2, tc, d), F32),
                            pltpu.SemaphoreType.DMA((2, 2))]),
        out_shape=jax.ShapeDtypeStruct((m, d), F32),
        compiler_params=_params("arbitrary"),
        name="moe_combine",
    )(pos2, x, wt, g, y)


def moe_layer(x, norm_g, router_w, router_b, w1, w3, w2, out_norm_g=None):
    m, d = x.shape
    n_e = router_w.shape[1]
    tile = MOE_TILE
    n_tiles = 2 * m // tile + n_e
    h, sel, w = norm_router(x, norm_g, router_w, router_b)
    pos, wt, meta = moe_route(sel, w, tile)
    pos2 = pos[:2].reshape(-1)
    xs = moe_gather(pos2, meta, h, n_tiles, tile)
    up, w2_bf16 = moe_up(meta, xs, w1, w3, w2, tile)
    y = moe_down(meta, up, w2_bf16, tile)
    return moe_combine(pos2, x, wt, y, out_norm_g)


def _mixer(x, batch, layer, norm_g, w_in_t, pe_k, pe_v, kw1, kw2, vw1, vw2, ln_g, ln_b, w_s, b_s, w_a, w_b, w_out):
    h = rmsnorm(x, norm_g, BF16)
    qt, kvr, vt, ngt, rest = in_projection(h, w_in_t, layer)
    cmp_kv = compress(kvr, batch, pe_k, pe_v, kw1, kw2, vw1, vw2)
    oct, sel = compressed_and_select(qt, cmp_kv, batch)
    oa = selected_window_attention(qt, kvr, vt, sel, oct, ngt, batch)
    ob = spatial_gating(rest, ln_g, ln_b, w_s, b_s)
    y = merge_branches(oa, ob, rest, w_a, w_b, layer)
    return residual_matmul("out_proj", y, w_out, layer, x)


def kernel(x, norm_mix, w_in, cmp_pe_k, cmp_pe_v, cmp_k_w1, cmp_k_w2, cmp_v_w1, cmp_v_w2, sgu_ln_g, sgu_ln_b, sgu_w, sgu_b, w_branch_a, w_branch_b, w_out, norm_ffn, ffn_w1, ffn_w3, ffn_w2, router_w, router_b, moe_w1, moe_w3, moe_w2, norm_f):
    batch, seq, d = x.shape
    depth = norm_mix.shape[0]
    xf = x.reshape(batch * seq, d)
    w_in_t = jnp.swapaxes(w_in, 1, 2)
    normed = False
    for layer in range(depth):
        xf = _mixer(xf, batch, layer, norm_mix[layer], w_in_t, cmp_pe_k[layer], cmp_pe_v[layer],
                    cmp_k_w1[layer], cmp_k_w2[layer], cmp_v_w1[layer], cmp_v_w2[layer],
                    sgu_ln_g[layer], sgu_ln_b[layer], sgu_w[layer], sgu_b[layer],
                    w_branch_a, w_branch_b, w_out)
        j = layer // 2
        if layer % 2 == 0:
            h = rmsnorm(xf, norm_ffn[layer], BF16)
            up = swiglu_up(h, ffn_w1[j], ffn_w3[j])
            xf = mm_acc("ffn_down", up, ffn_w2[j], xf, 1024, 1024, up.shape[1] // 4)
        else:
            normed = layer == depth - 1
            xf = moe_layer(xf, norm_ffn[layer], router_w[j], router_b[j], moe_w1[j], moe_w3[j], moe_w2[j],
                           norm_f if normed else None)
    if not normed:
        xf = rmsnorm(xf, norm_f, F32)
    return xf.reshape(batch, seq, d)
```

```python
import functools

import numpy as np
import jax
import jax.numpy as jnp
from jax import lax
from jax.experimental import pallas as pl
from jax.experimental.pallas import tpu as pltpu

F32 = jnp.float32
BF16 = jnp.bfloat16

D_MODEL = 2048
N_Q_HEADS = 16
N_KV_HEADS = 4
HEAD_DIM = 64
Q_PER_KV = N_Q_HEADS // N_KV_HEADS
NSA_WIDTH = N_Q_HEADS * HEAD_DIM
KV_WIDTH = N_KV_HEADS * HEAD_DIM
GROUP_WIDTH = Q_PER_KV * HEAD_DIM
CMP_BLOCK = 32
CMP_STRIDE = 16
CMP_HIDDEN = 256
SLC_BLOCK = 64
N_SELECT = 16
WINDOW = 512
FORCED_SCORE = 1e4
SGU_WIDTH = D_MODEL // 2
SGU_GROUPS = 8
SGU_CHUNK = 128
N_EXPERTS = 8
NORM_EPS = 1e-5
NEG = -1e30
LANES = 128
VMEM_LIMIT = 56 * 1024 * 1024

Q_OFF = 0
KV_OFF = NSA_WIDTH
NG_OFF = KV_OFF + 6 * KV_WIDTH
NG_WIDTH = 3 * N_Q_HEADS
REST_OFF = NG_OFF + NG_WIDTH


def _params(*sem):
    return pltpu.CompilerParams(dimension_semantics=sem, vmem_limit_bytes=VMEM_LIMIT)


def _dot(a, b):
    return jnp.dot(a, b, preferred_element_type=F32)


def _dot_nt(a, b):
    return lax.dot_general(a, b, (((1,), (1,)), ((), ())), preferred_element_type=F32)


def _dot_split(a_f32, b_bf16):
    hi = a_f32.astype(BF16)
    lo = (a_f32 - hi.astype(F32)).astype(BF16)
    return _dot(hi, b_bf16) + _dot(lo, b_bf16)


def _sigmoid(x):
    return 1.0 / (1.0 + jnp.exp(-x))


def _gelu(x):
    return x * (0.5 * (1.0 + jnp.tanh(0.7978845608028654 * (x + 0.044715 * (x * x * x)))))


def _rms(x, g):
    return x * lax.rsqrt(jnp.mean(x * x, axis=-1, keepdims=True) + NORM_EPS) * g


def _rmsnorm_kernel(x_ref, g_ref, o_ref):
    o_ref[...] = _rms(x_ref[...], g_ref[...]).astype(o_ref.dtype)


def rmsnorm(x, g, out_dtype, tm=512):
    m, d = x.shape
    return pl.pallas_call(
        _rmsnorm_kernel,
        grid=(m // tm,),
        in_specs=[pl.BlockSpec((tm, d), lambda i: (i, 0)),
                  pl.BlockSpec((1, d), lambda i: (0, 0))],
        out_specs=pl.BlockSpec((tm, d), lambda i: (i, 0)),
        out_shape=jax.ShapeDtypeStruct((m, d), out_dtype),
        compiler_params=_params("parallel"),
        name="rmsnorm",
    )(x, g.reshape(1, d))


def _norm_router_kernel(x_ref, g_ref, rwt_ref, rb_ref, h_ref, sel_ref, w_ref):
    h = _rms(x_ref[...], g_ref[...])
    h_ref[...] = h
    logits = lax.dot_general(rwt_ref[...], h, (((1,), (1,)), ((), ())), preferred_element_type=F32,
                             precision=lax.Precision.HIGHEST) + rb_ref[...]
    n_e = logits.shape[0]
    e = lax.broadcasted_iota(jnp.int32, logits.shape, 0)
    m1 = jnp.max(logits, axis=0, keepdims=True)
    i1 = jnp.min(jnp.where(logits == m1, e, n_e), axis=0, keepdims=True)
    rest = jnp.where(e == i1, -jnp.inf, logits)
    m2 = jnp.max(rest, axis=0, keepdims=True)
    i2 = jnp.min(jnp.where(rest == m2, e, n_e), axis=0, keepdims=True)
    e2 = jnp.exp(m2 - m1)
    w1 = 1.0 / (1.0 + e2)
    w2 = e2 / (1.0 + e2)
    sel_ref[...] = jnp.where((e == i1) | (e == i2), 1.0, 0.0)
    w_ref[...] = jnp.where(e == i1, w1, 0.0) + jnp.where(e == i2, w2, 0.0)


def norm_router(x, g, router_w, router_b, tm=512):
    m, d = x.shape
    n_e = router_w.shape[1]
    return pl.pallas_call(
        _norm_router_kernel,
        grid=(m // tm,),
        in_specs=[pl.BlockSpec((tm, d), lambda i: (i, 0)),
                  pl.BlockSpec((1, d), lambda i: (0, 0)),
                  pl.BlockSpec((n_e, d), lambda i: (0, 0)),
                  pl.BlockSpec((n_e, 1), lambda i: (0, 0))],
        out_specs=[pl.BlockSpec((tm, d), lambda i: (i, 0)),
                   pl.BlockSpec((n_e, tm), lambda i: (0, i)),
                   pl.BlockSpec((n_e, tm), lambda i: (0, i))],
        out_shape=[jax.ShapeDtypeStruct((m, d), F32),
                   jax.ShapeDtypeStruct((n_e, m), F32),
                   jax.ShapeDtypeStruct((n_e, m), F32)],
        compiler_params=_params("parallel"),
        name="norm_router",
    )(x, g.reshape(1, d), router_w.T, router_b.reshape(n_e, 1))


def _mm_full_kernel(*refs, n_a, n_b, n_e, n_o, pairs, inner_axis, epilogue, b_transposed):
    a_refs = refs[:n_a]
    b_refs = refs[n_a:n_a + n_b]
    e_refs = refs[n_a + n_b:n_a + n_b + n_e]
    o_refs = refs[n_a + n_b + n_e:n_a + n_b + n_e + n_o]
    w_refs = refs[n_a + n_b + n_e + n_o:]

    @pl.when(pl.program_id(inner_axis) == 0)
    def _():
        for b_ref, w_ref in zip(b_refs, w_refs):
            w_ref[...] = b_ref[...].astype(BF16)

    dot = _dot_nt if b_transposed else _dot
    ws = [w_ref[0] if len(w_ref.shape) == 3 else w_ref[...] for w_ref in w_refs]
    accs = [dot(a_refs[ia][...], ws[ib]) for ia, ib in pairs]
    epilogue(accs, e_refs, o_refs)


def _block_dim_size(s):
    return s.block_size if isinstance(s, pl.Element) else s


def mm_full(name, grid, a_ops, b_ops, e_ops, outs, pairs, epilogue, b_transposed=False):
    ops = a_ops + b_ops + e_ops
    kernel = functools.partial(
        _mm_full_kernel, n_a=len(a_ops), n_b=len(b_ops), n_e=len(e_ops), n_o=len(outs),
        pairs=pairs, inner_axis=len(grid) - 1, epilogue=epilogue, b_transposed=b_transposed)
    scratch = [pltpu.VMEM(tuple(_block_dim_size(s) for s in blk if s is not None), BF16) for _, blk, _ in b_ops]
    res = pl.pallas_call(
        kernel,
        grid=grid,
        in_specs=[pl.BlockSpec(blk, imap) for _, blk, imap in ops],
        out_specs=[pl.BlockSpec(blk, imap) for _, blk, imap in outs],
        out_shape=[sds for sds, _, _ in outs],
        scratch_shapes=scratch,
        compiler_params=_params(*(("arbitrary",) * len(grid))),
        name=name,
    )(*[arr for arr, _, _ in ops])
    return res


COL_CHUNK = 256


def _mm_acc_kernel(a_ref, b_ref, r_ref, o_ref):
    cols = range(0, o_ref.shape[1], COL_CHUNK)

    @pl.when(pl.program_id(2) == 0)
    def _():
        a = a_ref[...]
        for c in cols:
            sl = slice(c, c + COL_CHUNK)
            o_ref[:, sl] = r_ref[:, sl] + _dot(a, b_ref[:, sl].astype(BF16))

    @pl.when(pl.program_id(2) > 0)
    def _():
        a = a_ref[...]
        for c in cols:
            sl = slice(c, c + COL_CHUNK)
            o_ref[:, sl] += _dot(a, b_ref[:, sl].astype(BF16))


def mm_acc(name, a, b, res, tm, tn, tk):
    m, k = a.shape
    n = b.shape[1]
    assert m % tm == 0 and n % tn == 0 and k % tk == 0 and tn % COL_CHUNK == 0 and tk % LANES == 0
    return pl.pallas_call(
        _mm_acc_kernel,
        grid=(m // tm, n // tn, k // tk),
        in_specs=[pl.BlockSpec((tm, tk), lambda i, j, kk: (i, kk)),
                  pl.BlockSpec((tk, tn), lambda i, j, kk: (kk, j)),
                  pl.BlockSpec((tm, tn), lambda i, j, kk: (i, j))],
        out_specs=pl.BlockSpec((tm, tn), lambda i, j, kk: (i, j)),
        out_shape=jax.ShapeDtypeStruct((m, n), F32),
        compiler_params=_params("parallel", "parallel", "arbitrary"),
        name=name,
    )(a, b, res)


def in_projection(h, w_in_t, layer):
    m, d = h.shape
    tm = 1024
    a_op = [(h, (tm, d), lambda j, i: (i, 0))]

    tq = 512
    q_heads = tq // HEAD_DIM

    def q_epi(accs, e_refs, o_refs):
        for c in range(q_heads):
            o_refs[0][c] = (accs[0][:, c * HEAD_DIM:(c + 1) * HEAD_DIM] * (HEAD_DIM ** -0.5)).T.astype(BF16)

    (qt,) = mm_full(
        "in_proj_q", (NSA_WIDTH // tq, m // tm), a_op,
        [(w_in_t, (None, tq, d), lambda j, i: (layer, j, 0))], [],
        [(jax.ShapeDtypeStruct((N_Q_HEADS, HEAD_DIM, m), BF16), (q_heads, HEAD_DIM, tm), lambda j, i: (j, 0, i))],
        [(0, 0)], q_epi, b_transposed=True)

    sec0 = KV_OFF // KV_WIDTH

    def rows_epi(accs, e_refs, o_refs):
        for c in range(N_KV_HEADS):
            o_refs[0][c] = accs[0][:, c * HEAD_DIM:(c + 1) * HEAD_DIM].astype(BF16)

    (kvr,) = mm_full(
        "in_proj_kv_rows", (4, m // tm), a_op,
        [(w_in_t, (None, KV_WIDTH, d), lambda j, i: (layer, sec0 + j + j // 3, 0))], [],
        [(jax.ShapeDtypeStruct((4 * N_KV_HEADS, m, HEAD_DIM), BF16),
          (N_KV_HEADS, tm, HEAD_DIM), lambda j, i: (j, i, 0))],
        [(0, 0)], rows_epi, b_transposed=True)

    def cols_epi(accs, e_refs, o_refs):
        for c in range(N_KV_HEADS):
            o_refs[0][c] = accs[0][:, c * HEAD_DIM:(c + 1) * HEAD_DIM].T.astype(BF16)

    (vt,) = mm_full(
        "in_proj_v_cols", (2, m // tm), a_op,
        [(w_in_t, (None, KV_WIDTH, d), lambda j, i: (layer, sec0 + 3 + 2 * j, 0))], [],
        [(jax.ShapeDtypeStruct((2 * N_KV_HEADS, HEAD_DIM, m), BF16),
          (N_KV_HEADS, HEAD_DIM, tm), lambda j, i: (j, 0, i))],
        [(0, 0)], cols_epi, b_transposed=True)

    def f32_epi(accs, e_refs, o_refs):
        o_refs[0][...] = accs[0]

    def f32_t_epi(accs, e_refs, o_refs):
        o_refs[0][...] = accs[0].T

    assert NG_OFF % LANES == 0
    (ngt,) = mm_full(
        "in_proj_ng", (1, m // tm), a_op,
        [(w_in_t, (None, LANES, d), lambda j, i: (layer, NG_OFF // LANES, 0))], [],
        [(jax.ShapeDtypeStruct((LANES, m), F32), (LANES, tm), lambda j, i: (0, i))],
        [(0, 0)], f32_t_epi, b_transposed=True)

    n_rest = w_in_t.shape[1] - REST_OFF
    tr = 512
    assert n_rest % tr == 0 and REST_OFF % 8 == 0
    (rest,) = mm_full(
        "in_proj_rest", (n_rest // tr, m // tm), a_op,
        [(w_in_t, (pl.Element(1), pl.Element(tr), pl.Element(d)), lambda j, i: (layer, pl.multiple_of(REST_OFF + j * tr, 8), 0))], [],
        [(jax.ShapeDtypeStruct((m, n_rest), F32), (tm, tr), lambda j, i: (i, j))],
        [(0, 0)], f32_epi, b_transposed=True)
    return qt, kvr, vt, ngt, rest


def _cmp_kernel(a_ref, w1_ref, w2_ref, pe_ref, o_ref):
    a = a_ref[...]
    rows = a.shape[0]
    half = a.shape[1]
    w1 = w1_ref[...].astype(BF16)
    p0 = _dot(a, w1[:half])
    p1 = _dot(a, w1[half:])
    bias = _dot(pe_ref[...].astype(BF16), w1)[0:1]
    hid = p0 + pltpu.roll(p1, rows - 1, 0) + bias
    o_ref[...] = _dot(_gelu(hid).astype(BF16), w2_ref[...].astype(BF16))


def compress(kv, batch, pe_k, pe_v, kw1, kw2, vw1, vw2):
    m = kv.shape[1]
    n_chunks = m // batch // CMP_STRIDE
    rows = batch * n_chunks
    feat = CMP_STRIDE * HEAD_DIM
    a = kv[:2 * N_KV_HEADS].reshape(2, N_KV_HEADS, rows, feat)
    w1 = jnp.stack([kw1, vw1])
    w2 = jnp.stack([kw2, vw2])
    pe = jnp.stack([pe_k, pe_v]).reshape(2, 1, CMP_BLOCK * HEAD_DIM)
    pe = jnp.broadcast_to(pe, (2, 8, CMP_BLOCK * HEAD_DIM))
    return pl.pallas_call(
        _cmp_kernel,
        grid=(2, N_KV_HEADS),
        in_specs=[pl.BlockSpec((None, None, rows, feat), lambda s, g: (s, g, 0, 0)),
                  pl.BlockSpec((None, 2 * feat, CMP_HIDDEN), lambda s, g: (s, 0, 0)),
                  pl.BlockSpec((None, CMP_HIDDEN, HEAD_DIM), lambda s, g: (s, 0, 0)),
                  pl.BlockSpec((None, 8, 2 * feat), lambda s, g: (s, 0, 0))],
        out_specs=pl.BlockSpec((None, None, rows, HEAD_DIM), lambda s, g: (s, g, 0, 0)),
        out_shape=jax.ShapeDtypeStruct((2, N_KV_HEADS, rows, HEAD_DIM), F32),
        compiler_params=_params("parallel", "parallel"),
        name="nsa_compress",
    )(a, w1, w2, pe)


def _alibi_slopes():
    return np.array([2.0 ** (-8.0 * (h + 1) / N_Q_HEADS) for h in range(N_Q_HEADS)], dtype=np.float32)


def _overlap_t(n_cmp_pad, n_slc):
    cs = np.arange(n_cmp_pad)[None, :] * CMP_STRIDE
    ss = np.arange(n_slc)[:, None] * SLC_BLOCK
    ov = np.clip(np.minimum(cs + CMP_BLOCK, ss + SLC_BLOCK) - np.maximum(cs, ss), 0, None)
    return (ov / CMP_STRIDE).astype(np.float32)


def _sel_kernel(slopes_ref, q_ref, kc_ref, vc_ref, ovt_ref, oc_ref, sel_ref, *, n_cmp):
    g = pl.program_id(1)
    i = pl.program_id(2)
    tq = q_ref.shape[2]
    ncp = kc_ref.shape[0]
    n_slc = ovt_ref.shape[0]
    kc = kc_ref[...].astype(BF16)
    vct = vc_ref[...].T.astype(BF16)

    t = i * tq + lax.broadcasted_iota(jnp.int32, (ncp, tq), 1)
    c = lax.broadcasted_iota(jnp.int32, (ncp, tq), 0)
    dist = t - (c * CMP_STRIDE + (CMP_BLOCK - 1))
    mask = (dist >= 0) & (c < n_cmp)
    distf = dist.astype(F32)

    scores = [_dot(kc, q_ref[j]) for j in range(Q_PER_KV)]
    probs = []
    for j in range(Q_PER_KV):
        s = scores[j] - slopes_ref[g * Q_PER_KV + j] * distf
        s = jnp.where(mask, s, NEG)
        mx = jnp.max(s, axis=0, keepdims=True)
        e = jnp.where(mask, jnp.exp(s - mx), 0.0)
        probs.append(e / jnp.maximum(jnp.sum(e, axis=0, keepdims=True), 1e-30))
    for j in range(Q_PER_KV):
        oc_ref[j] = _dot(vct, probs[j].astype(BF16))
    p_sum = functools.reduce(lambda a, b: a + b, probs)

    ovt = ovt_ref[...]
    hi = p_sum.astype(BF16)
    lo = (p_sum - hi.astype(F32)).astype(BF16)
    imp = _dot(ovt, hi) + _dot(ovt, lo)
    tt = i * tq + lax.broadcasted_iota(jnp.int32, (n_slc, tq), 1)
    blk = lax.broadcasted_iota(jnp.int32, (n_slc, tq), 0)
    cur = tt // SLC_BLOCK
    valid = blk * SLC_BLOCK <= tt
    forced = (blk == 0) | (blk == cur) | (blk == cur - 1)
    score = jnp.where(valid, imp, -1.0)
    score = jnp.where(forced, FORCED_SCORE, score)
    rank = jnp.zeros((n_slc, tq), F32)
    for mrow in range(n_slc):
        other = jnp.broadcast_to(score[mrow:mrow + 1, :], (n_slc, tq))
        beats = (other > score) | ((other == score) & (blk > mrow))
        rank = rank + jnp.where(beats, 1.0, 0.0)
    sel_ref[...] = jnp.where(rank < float(min(N_SELECT, n_slc)), 1.0, 0.0)


def compressed_and_select(qt, cmp_kv, batch, tq=256):
    m = qt.shape[2]
    seq = m // batch
    nq = seq // tq
    ncp = cmp_kv.shape[2] // batch
    n_slc = seq // SLC_BLOCK
    ovt = jnp.asarray(_overlap_t(ncp, n_slc), BF16)
    slopes = jnp.asarray(_alibi_slopes())
    kernel = functools.partial(_sel_kernel, n_cmp=ncp - 1)
    return pl.pallas_call(
        kernel,
        grid=(batch, N_KV_HEADS, nq),
        in_specs=[pl.BlockSpec(memory_space=pltpu.SMEM),
                  pl.BlockSpec((Q_PER_KV, HEAD_DIM, tq), lambda b, g, i: (g, 0, b * nq + i)),
                  pl.BlockSpec((None, None, ncp, HEAD_DIM), lambda b, g, i: (0, g, b, 0)),
                  pl.BlockSpec((None, None, ncp, HEAD_DIM), lambda b, g, i: (1, g, b, 0)),
                  pl.BlockSpec((n_slc, ncp), lambda b, g, i: (0, 0))],
        out_specs=[pl.BlockSpec((Q_PER_KV, HEAD_DIM, tq), lambda b, g, i: (g, 0, b * nq + i)),
                   pl.BlockSpec((None, None, n_slc, tq), lambda b, g, i: (b, g, 0, i))],
        out_shape=[jax.ShapeDtypeStruct((N_Q_HEADS, HEAD_DIM, m), F32),
                   jax.ShapeDtypeStruct((batch, N_KV_HEADS, n_slc, seq), F32)],
        compiler_params=_params("parallel", "parallel", "parallel"),
        name="nsa_compressed_select",
    )(slopes, qt, cmp_kv, cmp_kv, ovt)


AUX_SLOPE = HEAD_DIM
AUX_SEL = HEAD_DIM + 16
ATTN_K = 2 * HEAD_DIM
STRIP = 32


def _attn_kernel(slopes_ref, q_ref, ks_ref, vs_ref, kw_ref, vw_ref, auxs_ref, auxw_ref, sel_ref, oc_ref,
                 ng_ref, o_ref, ka_s_ref, ka_w_ref, qa_ref, s_ref, p_ref, mask_ref, m_ref, l_ref, acc_ref,
                 os_ref, sig_ref):
    g = pl.program_id(1)
    i = pl.program_id(2)
    tq = q_ref.shape[2]
    tk = tq
    q0 = i * tq
    n_slc = sel_ref.shape[0]

    @pl.when(i == 0)
    def _():
        ka_s_ref[:, 0:HEAD_DIM] = ks_ref[...]
        ka_s_ref[:, HEAD_DIM:ATTN_K] = auxs_ref[...]
        ka_w_ref[:, 0:HEAD_DIM] = kw_ref[...]
        ka_w_ref[:, HEAD_DIM:ATTN_K] = auxw_ref[...]
        kr = lax.broadcasted_iota(jnp.int32, (tk, tq), 0)
        qc = lax.broadcasted_iota(jnp.int32, (tk, tq), 1)
        mask_ref[0] = jnp.where(kr <= qc, 0.0, NEG)
        mask_ref[1] = jnp.where(kr > qc, 0.0, NEG)

    sig_ref[...] = _sigmoid(ng_ref[...])
    sel_neg = (sel_ref[...] - 1.0) * (-NEG)
    row16 = lax.broadcasted_iota(jnp.int32, (16, tq), 0)
    pad = jnp.zeros((ATTN_K - AUX_SEL - n_slc, tq), F32)
    for j in range(Q_PER_KV):
        sl = jnp.full((16, tq), slopes_ref[g * Q_PER_KV + j], F32)
        hi = sl.astype(BF16).astype(F32)
        mid = (sl - hi).astype(BF16).astype(F32)
        lo = (sl - hi - mid).astype(BF16).astype(F32)
        pieces = jnp.where(row16 == 0, hi, jnp.where(row16 == 1, mid, jnp.where(row16 == 2, lo, 0.0)))
        qa = jnp.concatenate([q_ref[j].astype(F32), pieces, sel_neg, pad], axis=0)
        qa_ref[j] = qa.astype(BF16)

    def scores(ka_ref, k0, buf):
        ka = ka_ref[pl.ds(k0, tk), :]
        for j in range(Q_PER_KV):
            s_ref[buf, j] = _dot(ka, qa_ref[j])

    def softmax_pv(v_ref, k0, buf, mask_idx, first):
        vc = v_ref[:, pl.ds(k0, tk)]
        for j in range(Q_PER_KV):
            shift = slopes_ref[g * Q_PER_KV + j] * k0.astype(F32)
            mx = None
            for r in range(0, tk, STRIP):
                x = s_ref[buf, j, r:r + STRIP, :]
                if mask_idx is not None:
                    x = x + mask_ref[mask_idx, r:r + STRIP, :]
                    s_ref[buf, j, r:r + STRIP, :] = x
                mx = x if mx is None else jnp.maximum(mx, x)
            m_cur = jnp.max(mx, axis=0, keepdims=True) + shift
            if first:
                m_new = m_cur
            else:
                m_old = m_ref[j]
                m_new = jnp.maximum(m_old, m_cur)
                alpha = jnp.exp(m_old - m_new)
            m_ref[j] = m_new
            m_loc = m_new - shift
            ls = None
            for r in range(0, tk, STRIP):
                p = jnp.exp(s_ref[buf, j, r:r + STRIP, :] - m_loc)
                ls = p if ls is None else ls + p
                p_ref[j, r:r + STRIP, :] = p.astype(BF16)
            l_cur = jnp.sum(ls, axis=0, keepdims=True)
            pv = _dot(vc, p_ref[j])
            if first:
                l_ref[j] = l_cur
                acc_ref[j] = pv
            else:
                l_ref[j] = alpha * l_ref[j] + l_cur
                acc_ref[j] = alpha * acc_ref[j] + pv

    k_diag = pl.multiple_of(q0, tk)
    last = jnp.maximum(i - 1, 0)
    k_of = lambda c: pl.multiple_of(jnp.minimum(c, last) * tk, tk)
    scores(ka_s_ref, k_diag, 1)
    scores(ka_s_ref, k_of(0), 0)
    softmax_pv(vs_ref, k_diag, 1, 0, True)

    def slc_pair(cc, carry):
        c0 = 2 * cc
        scores(ka_s_ref, k_of(c0 + 1), 1)
        softmax_pv(vs_ref, k_of(c0), 0, None, False)

        @pl.when(c0 + 1 < i)
        def _():
            scores(ka_s_ref, k_of(c0 + 2), 0)
            softmax_pv(vs_ref, k_of(c0 + 1), 1, None, False)
        return carry

    lax.fori_loop(0, (i + 1) // 2, slc_pair, 0)
    for j in range(Q_PER_KV):
        os_ref[j] = acc_ref[j] / l_ref[j]

    n_back = WINDOW // tk
    back_k0 = lambda back: pl.multiple_of(jnp.maximum(q0 - back * tk, 0), tk)
    scores(ka_w_ref, k_diag, 0)
    scores(ka_w_ref, back_k0(1), 1)
    softmax_pv(vw_ref, k_diag, 0, 0, True)
    for back in range(1, n_back + 1):
        @pl.when(i >= back)
        def _(back=back):
            if back < n_back:
                scores(ka_w_ref, back_k0(back + 1), (back + 1) % 2)
            softmax_pv(vw_ref, back_k0(back), back % 2, 1 if back == n_back else None, False)

    for j in range(Q_PER_KV):
        base = (g * Q_PER_KV + j) * 3
        out_t = (sig_ref[pl.ds(base, 1), :] * oc_ref[j]
                 + sig_ref[pl.ds(base + 1, 1), :] * os_ref[j]
                 + sig_ref[pl.ds(base + 2, 1), :] * (acc_ref[j] / l_ref[j]))
        o_ref[:, j * HEAD_DIM:(j + 1) * HEAD_DIM] = out_t.T.astype(o_ref.dtype)


def _key_aux(seq, tk, with_blocks):
    aux = np.zeros((seq, ATTN_K - HEAD_DIM), np.float32)
    pos = np.arange(seq)
    aux[:, AUX_SLOPE - HEAD_DIM:AUX_SLOPE - HEAD_DIM + 3] = (pos % tk)[:, None]
    if with_blocks:
        aux[pos, AUX_SEL - HEAD_DIM + pos // SLC_BLOCK] = 1.0
    return aux


def selected_window_attention(qt, kvr, vt, sel, oct, ngt, batch, tq=256):
    m = qt.shape[2]
    seq = m // batch
    nq = seq // tq
    n_slc = seq // SLC_BLOCK
    assert tq <= 256 and AUX_SEL + n_slc <= ATTN_K
    aux_s = jnp.asarray(_key_aux(seq, tq, True), BF16)
    aux_w = jnp.asarray(_key_aux(seq, tq, False), BF16)
    slopes = jnp.asarray(_alibi_slopes())
    n_g = N_KV_HEADS

    def k_spec(section):
        return pl.BlockSpec((None, seq, HEAD_DIM), lambda b, g, i: (section * n_g + g, b, 0))

    def v_spec(section):
        return pl.BlockSpec((None, HEAD_DIM, seq), lambda b, g, i: (section * n_g + g, 0, b))

    head_blk = pl.BlockSpec((Q_PER_KV, HEAD_DIM, tq), lambda b, g, i: (g, 0, b * nq + i))
    aux_blk = pl.BlockSpec((seq, ATTN_K - HEAD_DIM), lambda b, g, i: (0, 0))
    return pl.pallas_call(
        _attn_kernel,
        grid=(batch, N_KV_HEADS, nq),
        in_specs=[pl.BlockSpec(memory_space=pltpu.SMEM),
                  head_blk,
                  k_spec(2), v_spec(0), k_spec(3), v_spec(1),
                  aux_blk, aux_blk,
                  pl.BlockSpec((None, None, n_slc, tq), lambda b, g, i: (b, g, 0, i)),
                  head_blk,
                  pl.BlockSpec((LANES, tq), lambda b, g, i: (0, b * nq + i))],
        out_specs=pl.BlockSpec((tq, GROUP_WIDTH), lambda b, g, i: (b * nq + i, g)),
        out_shape=jax.ShapeDtypeStruct((m, NSA_WIDTH), BF16),
        scratch_shapes=[pltpu.VMEM((seq, ATTN_K), BF16),
                        pltpu.VMEM((seq, ATTN_K), BF16),
                        pltpu.VMEM((Q_PER_KV, ATTN_K, tq), BF16),
                        pltpu.VMEM((2, Q_PER_KV, tq, tq), F32),
                        pltpu.VMEM((Q_PER_KV, tq, tq), BF16),
                        pltpu.VMEM((2, tq, tq), F32),
                        pltpu.VMEM((Q_PER_KV, 1, tq), F32),
                        pltpu.VMEM((Q_PER_KV, 1, tq), F32),
                        pltpu.VMEM((Q_PER_KV, HEAD_DIM, tq), F32),
                        pltpu.VMEM((Q_PER_KV, HEAD_DIM, tq), F32),
                        pltpu.VMEM((LANES, tq), F32)],
        compiler_params=_params("arbitrary", "arbitrary", "arbitrary"),
        name="nsa_selected_window",
    )(slopes, qt, kvr, vt, kvr, vt, aux_s, aux_w, sel, oct, ngt)


def _sgu_kernel(u_ref, v_ref, lg_ref, lb_ref, ws_ref, bs_ref, o_ref):
    rows = u_ref.shape[0]
    v = _gelu(v_ref[...])
    mu = jnp.mean(v, axis=-1, keepdims=True)
    var = jnp.mean(jnp.square(v - mu), axis=-1, keepdims=True)
    vl = ((v - mu) * lax.rsqrt(var + NORM_EPS) * lg_ref[...] + lb_ref[...]).astype(BF16)
    r = lax.broadcasted_iota(jnp.int32, (SGU_CHUNK, SGU_CHUNK), 0)
    c = lax.broadcasted_iota(jnp.int32, (SGU_CHUNK, SGU_CHUNK), 1)
    gd = SGU_WIDTH // SGU_GROUPS
    for grp in range(SGU_GROUPS):
        w = jnp.where(c <= r, ws_ref[grp], 0.0).astype(BF16)
        lanes = slice(grp * gd, (grp + 1) * gd)
        for n in range(rows // SGU_CHUNK):
            rs = slice(n * SGU_CHUNK, (n + 1) * SGU_CHUNK)
            vm = _dot(w, vl[rs, lanes]) + bs_ref[:, lanes]
            o_ref[rs, lanes] = (_gelu(u_ref[rs, lanes]) * vm).astype(o_ref.dtype)


def spatial_gating(rest, ln_g, ln_b, w_s, b_s, tm=512):
    m = rest.shape[0]
    gd = SGU_WIDTH // SGU_GROUPS
    bias = jnp.repeat(b_s.T, gd, axis=1)
    return pl.pallas_call(
        _sgu_kernel,
        grid=(m // tm,),
        in_specs=[pl.BlockSpec((tm, SGU_WIDTH), lambda i: (i, 0)),
                  pl.BlockSpec((tm, SGU_WIDTH), lambda i: (i, 1)),
                  pl.BlockSpec((1, SGU_WIDTH), lambda i: (0, 0)),
                  pl.BlockSpec((1, SGU_WIDTH), lambda i: (0, 0)),
                  pl.BlockSpec((SGU_GROUPS, SGU_CHUNK, SGU_CHUNK), lambda i: (0, 0, 0)),
                  pl.BlockSpec((SGU_CHUNK, SGU_WIDTH), lambda i: (0, 0))],
        out_specs=pl.BlockSpec((tm, SGU_WIDTH), lambda i: (i, 0)),
        out_shape=jax.ShapeDtypeStruct((m, SGU_WIDTH), BF16),
        compiler_params=_params("parallel"),
        name="spatial_gating",
    )(rest, rest, ln_g.reshape(1, -1), ln_b.reshape(1, -1), w_s, bias)


def merge_branches(oa, ob, rest, w_a, w_b, layer, tm=1024, tn=512):
    m = oa.shape[0]
    d = w_a.shape[2]
    ga_blk = 2 * SGU_WIDTH // tn
    gb_blk = (2 * SGU_WIDTH + d) // tn

    def epi(accs, e_refs, o_refs):
        y = _sigmoid(e_refs[0][...]) * accs[0] + _sigmoid(e_refs[1][...]) * accs[1]
        o_refs[0][...] = y.astype(BF16)

    (y,) = mm_full(
        "merge_branches", (d // tn, m // tm),
        [(oa, (tm, oa.shape[1]), lambda j, i: (i, 0)), (ob, (tm, ob.shape[1]), lambda j, i: (i, 0))],
        [(w_a, (None, w_a.shape[1], tn), lambda j, i: (layer, 0, j)),
         (w_b, (None, w_b.shape[1], tn), lambda j, i: (layer, 0, j))],
        [(rest, (tm, tn), lambda j, i: (i, ga_blk + j)), (rest, (tm, tn), lambda j, i: (i, gb_blk + j))],
        [(jax.ShapeDtypeStruct((m, d), BF16), (tm, tn), lambda j, i: (i, j))],
        [(0, 0), (1, 1)], epi)
    return y


def residual_matmul(name, a, w, layer, x, tm=1024, tn=512):
    m, k = a.shape
    n = w.shape[2]

    def epi(accs, e_refs, o_refs):
        o_refs[0][...] = e_refs[0][...] + accs[0]

    (out,) = mm_full(
        name, (n // tn, m // tm),
        [(a, (tm, k), lambda j, i: (i, 0))],
        [(w, (None, k, tn), lambda j, i: (layer, 0, j))],
        [(x, (tm, tn), lambda j, i: (i, j))],
        [(jax.ShapeDtypeStruct((m, n), F32), (tm, tn), lambda j, i: (i, j))],
        [(0, 0)], epi)
    return out


def swiglu_up(h, w1, w3, tm=1024, tn=512):
    m, d = h.shape
    f = w1.shape[1]

    def epi(accs, e_refs, o_refs):
        o_refs[0][...] = (accs[0] * _sigmoid(accs[0]) * accs[1]).astype(BF16)

    (out,) = mm_full(
        "swiglu_up", (f // tn, m // tm),
        [(h, (tm, d), lambda j, i: (i, 0))],
        [(w1, (d, tn), lambda j, i: (0, j)), (w3, (d, tn), lambda j, i: (0, j))],
        [],
        [(jax.ShapeDtypeStruct((m, f), BF16), (tm, tn), lambda j, i: (i, j))],
        [(0, 0), (0, 1)], epi)
    return out


MOE_TILE = 512


def _route_kernel(sel_ref, w_ref, tri_ref, pos_ref, wt_ref, meta_ref, cum_ref, *, tile):
    n_e, m = sel_ref.shape
    ck = tri_ref.shape[0]
    carry = jnp.zeros((n_e, 1), F32)
    for c in range(m // ck):
        sl = slice(c * ck, (c + 1) * ck)
        cs = _dot(sel_ref[:, sl].astype(BF16), tri_ref[...]) + carry
        cum_ref[:, sl] = cs
        carry = cs[:, ck - 1:ck]
    padded = jnp.ceil(carry / tile) * tile
    sub = lax.broadcasted_iota(jnp.int32, (n_e, 1), 0)
    start = jnp.zeros((n_e, 1), F32)
    run = jnp.zeros((1, 1), F32)
    for ex in range(n_e):
        start = jnp.where(sub == ex, run, start)
        run = run + padded[ex:ex + 1, :]
    sel = sel_ref[...] > 0.5
    pos = start + cum_ref[...] - 1.0
    eidx = lax.broadcasted_iota(jnp.int32, (n_e, m), 0)
    e_lo = jnp.min(jnp.where(sel, eidx, n_e), axis=0, keepdims=True)
    e_hi = jnp.max(jnp.where(sel, eidx, -1), axis=0, keepdims=True)
    is_lo = eidx == e_lo
    is_hi = eidx == e_hi
    w = w_ref[...]
    pos_a = jnp.sum(jnp.where(is_lo, pos, 0.0), axis=0, keepdims=True)
    pos_b = jnp.sum(jnp.where(is_hi, pos, 0.0), axis=0, keepdims=True)
    w_a = jnp.sum(jnp.where(is_lo, w, 0.0), axis=0, keepdims=True)
    w_b = jnp.sum(jnp.where(is_hi, w, 0.0), axis=0, keepdims=True)
    pos_ref[...] = jnp.where(eidx == 0, pos_a, jnp.where(eidx == 1, pos_b, 0.0)).astype(jnp.int32)
    wt_ref[...] = jnp.where(eidx == 0, w_a, jnp.where(eidx == 1, w_b, 0.0)).T
    tile_lo = lax.broadcasted_iota(jnp.int32, (n_e, LANES), 1).astype(F32) * tile
    t_exp = jnp.sum(jnp.where(start + padded <= tile_lo, 1.0, 0.0), axis=0, keepdims=True)
    t_exp = jnp.minimum(t_exp, n_e - 1.0)
    r8 = lax.broadcasted_iota(jnp.int32, (n_e, LANES), 0)
    meta_ref[...] = jnp.where(r8 == 0, t_exp, jnp.where(r8 == 1, run / tile, 0.0)).astype(jnp.int32)


def moe_route(sel, w, tile):
    n_e, m = sel.shape
    ck = 256
    tri = jnp.asarray(np.triu(np.ones((ck, ck), np.float32)), BF16)
    full = lambda shape: pl.BlockSpec(shape, lambda: tuple(0 for _ in shape))
    return pl.pallas_call(
        functools.partial(_route_kernel, tile=tile),
        in_specs=[full((n_e, m)), full((n_e, m)), full((ck, ck))],
        out_specs=[full((n_e, m)), full((m, n_e)), full((n_e, LANES))],
        out_shape=[jax.ShapeDtypeStruct((n_e, m), jnp.int32),
                   jax.ShapeDtypeStruct((m, n_e), F32),
                   jax.ShapeDtypeStruct((n_e, LANES), jnp.int32)],
        scratch_shapes=[pltpu.VMEM((n_e, m), F32)],
        compiler_params=pltpu.CompilerParams(vmem_limit_bytes=VMEM_LIMIT),
        name="moe_route",
    )(sel, w, tri)


def _row_gather_start(src_hbm, dst, rows_ref, base, n_rows, sem, unroll=8):
    def body(r, carry):
        src_row = rows_ref[base + r]
        pltpu.make_async_copy(src_hbm.at[pl.ds(src_row, 1)], dst.at[pl.ds(r, 1)], sem).start()
        return carry
    lax.fori_loop(0, n_rows, body, 0, unroll=unroll)


def _row_gather_wait(src_hbm, dst, sem):
    pltpu.make_async_copy(src_hbm.at[pl.ds(0, dst.shape[0])], dst, sem).wait()


def _moe_gather_kernel(pos_ref, meta_ref, h_hbm, zeros_hbm, o_ref, tok_ref, buf_ref, sem_ref):
    i = pl.program_id(0)
    n = meta_ref[1, 0]
    tile = o_ref.shape[0]
    m = pos_ref.shape[0] // 2

    @pl.when(i == 0)
    def _():
        clear = pltpu.make_async_copy(zeros_hbm, tok_ref, sem_ref.at[2])
        clear.start()
        clear.wait()

        def fill(t, carry):
            tok_ref[pos_ref[t]] = t
            tok_ref[pos_ref[m + t]] = t
            return carry
        lax.fori_loop(0, m, fill, 0, unroll=8)
        _row_gather_start(h_hbm, buf_ref.at[0], tok_ref, 0, tile, sem_ref.at[0])

    @pl.when(i + 1 < n)
    def _():
        nxt = (i + 1) % 2
        _row_gather_start(h_hbm, buf_ref.at[nxt], tok_ref, (i + 1) * tile, tile, sem_ref.at[nxt])

    @pl.when(i < n)
    def _():
        cur = i % 2
        _row_gather_wait(h_hbm, buf_ref.at[cur], sem_ref.at[cur])
        o_ref[...] = buf_ref[cur].astype(o_ref.dtype)

    @pl.when(i >= n)
    def _():
        o_ref[...] = jnp.zeros_like(o_ref)


def moe_gather(pos2, meta, h, n_tiles, tile):
    m, d = h.shape
    rows = n_tiles * tile
    return pl.pallas_call(
        _moe_gather_kernel,
        grid_spec=pltpu.PrefetchScalarGridSpec(
            num_scalar_prefetch=2,
            grid=(n_tiles,),
            in_specs=[pl.BlockSpec(memory_space=pl.ANY), pl.BlockSpec(memory_space=pl.ANY)],
            out_specs=pl.BlockSpec((tile, d), lambda i, pos, meta: (i, 0)),
            scratch_shapes=[pltpu.SMEM((rows,), jnp.int32),
                            pltpu.VMEM((2, tile, d), F32),
                            pltpu.SemaphoreType.DMA((3,))]),
        out_shape=jax.ShapeDtypeStruct((rows, d), BF16),
        compiler_params=_params("arbitrary"),
        name="moe_gather",
    )(pos2, meta, h, jnp.zeros((rows,), jnp.int32))


def _moe_up_kernel(meta_ref, x_ref, w1_ref, w3_ref, w2_ref, o_ref, w2b_ref):
    i = pl.program_id(1)

    def round_w2_block():
        w2b_ref[...] = w2_ref[...].astype(BF16)

    @pl.when(i < meta_ref[1, 0])
    def _():
        x = x_ref[...]
        for c in range(0, o_ref.shape[1], COL_CHUNK):
            sl = slice(c, c + COL_CHUNK)
            a1 = _dot(x, w1_ref[:, sl].astype(BF16))
            a3 = _dot(x, w3_ref[:, sl].astype(BF16))
            o_ref[:, sl] = (a1 * _sigmoid(a1) * a3).astype(o_ref.dtype)
        round_w2_block()

    @pl.when(i >= meta_ref[1, 0])
    def _():
        o_ref[...] = jnp.zeros_like(o_ref)
        round_w2_block()


BF16_SUBLANES = 16


def _cast_rows_per_step(total_rows, steps):
    for rows in range(BF16_SUBLANES, total_rows + 1, BF16_SUBLANES):
        if total_rows % rows == 0 and total_rows // rows <= steps:
            return rows
    raise ValueError("weights cannot be split over the grid steps")


def moe_up(meta, xs, w1, w3, w2, tile, tn=1024):
    rows, d = xs.shape
    n_e, _, f = w1.shape
    assert f % tn == 0 and rows % tile == 0
    n_tiles = rows // tile
    w2_flat = w2.reshape(n_e * f, w2.shape[2])
    cast_rows = _cast_rows_per_step(w2_flat.shape[0], (f // tn) * n_tiles)
    n_cast = w2_flat.shape[0] // cast_rows
    last = lambda i, meta: jnp.maximum(jnp.minimum(i, meta[1, 0] - 1), 0)
    cast_blk = lambda j, i, meta: (jnp.minimum(j * n_tiles + i, n_cast - 1), 0)
    up, w2b = pl.pallas_call(
        _moe_up_kernel,
        grid_spec=pltpu.PrefetchScalarGridSpec(
            num_scalar_prefetch=1,
            grid=(f // tn, n_tiles),
            in_specs=[pl.BlockSpec((tile, d), lambda j, i, meta: (last(i, meta), 0)),
                      pl.BlockSpec((None, d, tn), lambda j, i, meta: (meta[0, i], 0, j)),
                      pl.BlockSpec((None, d, tn), lambda j, i, meta: (meta[0, i], 0, j)),
                      pl.BlockSpec((cast_rows, w2_flat.shape[1]), cast_blk)],
            out_specs=[pl.BlockSpec((tile, tn), lambda j, i, meta: (i, j)),
                       pl.BlockSpec((cast_rows, w2_flat.shape[1]), cast_blk)]),
        out_shape=[jax.ShapeDtypeStruct((rows, f), BF16),
                   jax.ShapeDtypeStruct(w2_flat.shape, BF16)],
        compiler_params=_params("arbitrary", "arbitrary"),
        name="moe_up",
    )(meta, xs, w1, w3, w2_flat)
    return up, w2b.reshape(w2.shape)


def _moe_down_kernel(meta_ref, a_ref, w_ref, o_ref):
    i = pl.program_id(0)
    k = pl.program_id(1)

    active = i < meta_ref[1, 0]
    cn = COL_CHUNK

    @pl.when(active & (k == 0))
    def _():
        a = a_ref[...]
        for c in range(0, o_ref.shape[1], cn):
            o_ref[:, c:c + cn] = _dot(a, w_ref[:, c:c + cn])

    @pl.when(active & (k > 0))
    def _():
        a = a_ref[...]
        for c in range(0, o_ref.shape[1], cn):
            o_ref[:, c:c + cn] += _dot(a, w_ref[:, c:c + cn])

    @pl.when(jnp.logical_not(active) & (k == 0))
    def _():
        o_ref[...] = jnp.zeros_like(o_ref)


def moe_down(meta, up, w2, tile, tk=1792):
    rows, f = up.shape
    d = w2.shape[2]
    assert f % tk == 0 and rows % tile == 0
    n_tiles = rows // tile
    last = lambda i, meta: jnp.maximum(jnp.minimum(i, meta[1, 0] - 1), 0)
    return pl.pallas_call(
        _moe_down_kernel,
        grid_spec=pltpu.PrefetchScalarGridSpec(
            num_scalar_prefetch=1,
            grid=(n_tiles, f // tk),
            in_specs=[pl.BlockSpec((tile, tk), lambda i, k, meta: (last(i, meta), k)),
                      pl.BlockSpec((None, tk, d), lambda i, k, meta: (meta[0, i], k, 0))],
            out_specs=pl.BlockSpec((tile, d), lambda i, k, meta: (i, 0))),
        out_shape=jax.ShapeDtypeStruct((rows, d), F32),
        compiler_params=_params("arbitrary", "arbitrary"),
        name="moe_down",
    )(meta, up, w2)


def _moe_combine_kernel(pos_ref, x_ref, wt_ref, g_ref, y_hbm, o_ref, buf_ref, sem_ref, *, final_norm):
    i = pl.program_id(0)
    n = pl.num_programs(0)
    tc = x_ref.shape[0]
    m = pos_ref.shape[0] // 2

    def start(step, slot):
        for s in range(2):
            _row_gather_start(y_hbm, buf_ref.at[slot, s], pos_ref, s * m + step * tc, tc, sem_ref.at[slot, s])

    @pl.when(i == 0)
    def _():
        start(0, 0)

    @pl.when(i + 1 < n)
    def _():
        start(i + 1, (i + 1) % 2)

    cur = i % 2
    for s in range(2):
        _row_gather_wait(y_hbm, buf_ref.at[cur, s], sem_ref.at[cur, s])
    wt = wt_ref[...]
    out = x_ref[...] + wt[:, 0:1] * buf_ref[cur, 0] + wt[:, 1:2] * buf_ref[cur, 1]
    o_ref[...] = _rms(out, g_ref[...]) if final_norm else out


def moe_combine(pos2, x, wt, y, norm_g=None, tc=256):
    m, d = x.shape
    final_norm = norm_g is not None
    g = (norm_g if final_norm else jnp.ones((d,), F32)).reshape(1, d)
    return pl.pallas_call(
        functools.partial(_moe_combine_kernel, final_norm=final_norm),
        grid_spec=pltpu.PrefetchScalarGridSpec(
            num_scalar_prefetch=1,
            grid=(m // tc,),
            in_specs=[pl.BlockSpec((tc, d), lambda i, pos: (i, 0)),
                      pl.BlockSpec((tc, wt.shape[1]), lambda i, pos: (i, 0)),
                      pl.BlockSpec((1, d), lambda i, pos: (0, 0)),
                      pl.BlockSpec(memory_space=pl.ANY)],
            out_specs=pl.BlockSpec((tc, d), lambda i, pos: (i, 0)),
            scratch_shapes=[pltpu.VMEM((2, 2, tc, d), F32),
                            pltpu.SemaphoreType.DMA((2, 2))]),
        out_shape=jax.ShapeDtypeStruct((m, d), F32),
        compiler_params=_params("arbitrary"),
        name="moe_combine",
    )(pos2, x, wt, g, y)


def moe_layer(x, norm_g, router_w, router_b, w1, w3, w2, out_norm_g=None):
    m, d = x.shape
    n_e = router_w.shape[1]
    tile = MOE_TILE
    n_tiles = 2 * m // tile + n_e
    h, sel, w = norm_router(x, norm_g, router_w, router_b)
    pos, wt, meta = moe_route(sel, w, tile)
    pos2 = pos[:2].reshape(-1)
    xs = moe_gather(pos2, meta, h, n_tiles, tile)
    up, w2_bf16 = moe_up(meta, xs, w1, w3, w2, tile)
    y = moe_down(meta, up, w2_bf16, tile)
    return moe_combine(pos2, x, wt, y, out_norm_g)


def _mixer(x, batch, layer, norm_g, w_in_t, pe_k, pe_v, kw1, kw2, vw1, vw2, ln_g, ln_b, w_s, b_s, w_a, w_b, w_out):
    h = rmsnorm(x, norm_g, BF16)
    qt, kvr, vt, ngt, rest = in_projection(h, w_in_t, layer)
    cmp_kv = compress(kvr, batch, pe_k, pe_v, kw1, kw2, vw1, vw2)
    oct, sel = compressed_and_select(qt, cmp_kv, batch)
    oa = selected_window_attention(qt, kvr, vt, sel, oct, ngt, batch)
    ob = spatial_gating(rest, ln_g, ln_b, w_s, b_s)
    y = merge_branches(oa, ob, rest, w_a, w_b, layer)
    return residual_matmul("out_proj", y, w_out, layer, x)


def kernel(x, norm_mix, w_in, cmp_pe_k, cmp_pe_v, cmp_k_w1, cmp_k_w2, cmp_v_w1, cmp_v_w2, sgu_ln_g, sgu_ln_b, sgu_w, sgu_b, w_branch_a, w_branch_b, w_out, norm_ffn, ffn_w1, ffn_w3, ffn_w2, router_w, router_b, moe_w1, moe_w3, moe_w2, norm_f):
    batch, seq, d = x.shape
    depth = norm_mix.shape[0]
    xf = x.reshape(batch * seq, d)
    w_in_t = jnp.swapaxes(w_in, 1, 2)
    normed = False
    for layer in range(depth):
        xf = _mixer(xf, batch, layer, norm_mix[layer], w_in_t, cmp_pe_k[layer], cmp_pe_v[layer],
                    cmp_k_w1[layer], cmp_k_w2[layer], cmp_v_w1[layer], cmp_v_w2[layer],
                    sgu_ln_g[layer], sgu_ln_b[layer], sgu_w[layer], sgu_b[layer],
                    w_branch_a, w_branch_b, w_out)
        j = layer // 2
        if layer % 2 == 0:
            h = rmsnorm(xf, norm_ffn[layer], BF16)
            up = swiglu_up(h, ffn_w1[j], ffn_w3[j])
            xf = mm_acc("ffn_down", up, ffn_w2[j], xf, 1024, 1024, up.shape[1] // 4)
        else:
            normed = layer == depth - 1
            xf = moe_layer(xf, norm_ffn[layer], router_w[j], router_b[j], moe_w1[j], moe_w3[j], moe_w2[j],
                           norm_f if normed else None)
    if not normed:
        xf = rmsnorm(xf, norm_f, F32)
    return xf.reshape(batch, seq, d)
```

```python
import functools

import numpy as np
import jax
import jax.numpy as jnp
from jax import lax
from jax.experimental import pallas as pl
from jax.experimental.pallas import tpu as pltpu

F32 = jnp.float32
BF16 = jnp.bfloat16

D_MODEL = 2048
N_Q_HEADS = 16
N_KV_HEADS = 4
HEAD_DIM = 64
Q_PER_KV = N_Q_HEADS // N_KV_HEADS
NSA_WIDTH = N_Q_HEADS * HEAD_DIM
KV_WIDTH = N_KV_HEADS * HEAD_DIM
GROUP_WIDTH = Q_PER_KV * HEAD_DIM
CMP_BLOCK = 32
CMP_STRIDE = 16
CMP_HIDDEN = 256
SLC_BLOCK = 64
N_SELECT = 16
WINDOW = 512
FORCED_SCORE = 1e4
SGU_WIDTH = D_MODEL // 2
SGU_GROUPS = 8
SGU_CHUNK = 128
N_EXPERTS = 8
NORM_EPS = 1e-5
NEG = -1e30
LANES = 128
VMEM_LIMIT = 56 * 1024 * 1024

Q_OFF = 0
KV_OFF = NSA_WIDTH
NG_OFF = KV_OFF + 6 * KV_WIDTH
NG_WIDTH = 3 * N_Q_HEADS
REST_OFF = NG_OFF + NG_WIDTH


def _params(*sem):
    return pltpu.CompilerParams(dimension_semantics=sem, vmem_limit_bytes=VMEM_LIMIT)


def _dot(a, b):
    return jnp.dot(a, b, preferred_element_type=F32)


def _dot_nt(a, b):
    return lax.dot_general(a, b, (((1,), (1,)), ((), ())), preferred_element_type=F32)


def _dot_split(a_f32, b_bf16):
    hi = a_f32.astype(BF16)
    lo = (a_f32 - hi.astype(F32)).astype(BF16)
    return _dot(hi, b_bf16) + _dot(lo, b_bf16)


def _sigmoid(x):
    return 1.0 / (1.0 + jnp.exp(-x))


def _gelu(x):
    return x * (0.5 * (1.0 + jnp.tanh(0.7978845608028654 * (x + 0.044715 * (x * x * x)))))


def _rms(x, g):
    return x * lax.rsqrt(jnp.mean(x * x, axis=-1, keepdims=True) + NORM_EPS) * g


def _rmsnorm_kernel(x_ref, g_ref, o_ref):
    o_ref[...] = _rms(x_ref[...], g_ref[...]).astype(o_ref.dtype)


def rmsnorm(x, g, out_dtype, tm=512):
    m, d = x.shape
    return pl.pallas_call(
        _rmsnorm_kernel,
        grid=(m // tm,),
        in_specs=[pl.BlockSpec((tm, d), lambda i: (i, 0)),
                  pl.BlockSpec((1, d), lambda i: (0, 0))],
        out_specs=pl.BlockSpec((tm, d), lambda i: (i, 0)),
        out_shape=jax.ShapeDtypeStruct((m, d), out_dtype),
        compiler_params=_params("parallel"),
        name="rmsnorm",
    )(x, g.reshape(1, d))


def _norm_router_kernel(x_ref, g_ref, rwt_ref, rb_ref, h_ref, sel_ref, w_ref):
    h = _rms(x_ref[...], g_ref[...])
    h_ref[...] = h
    logits = lax.dot_general(rwt_ref[...], h, (((1,), (1,)), ((), ())), preferred_element_type=F32,
                             precision=lax.Precision.HIGHEST) + rb_ref[...]
    n_e = logits.shape[0]
    e = lax.broadcasted_iota(jnp.int32, logits.shape, 0)
    m1 = jnp.max(logits, axis=0, keepdims=True)
    i1 = jnp.min(jnp.where(logits == m1, e, n_e), axis=0, keepdims=True)
    rest = jnp.where(e == i1, -jnp.inf, logits)
    m2 = jnp.max(rest, axis=0, keepdims=True)
    i2 = jnp.min(jnp.where(rest == m2, e, n_e), axis=0, keepdims=True)
    e2 = jnp.exp(m2 - m1)
    w1 = 1.0 / (1.0 + e2)
    w2 = e2 / (1.0 + e2)
    sel_ref[...] = jnp.where((e == i1) | (e == i2), 1.0, 0.0)
    w_ref[...] = jnp.where(e == i1, w1, 0.0) + jnp.where(e == i2, w2, 0.0)


def norm_router(x, g, router_w, router_b, tm=512):
    m, d = x.shape
    n_e = router_w.shape[1]
    return pl.pallas_call(
        _norm_router_kernel,
        grid=(m // tm,),
        in_specs=[pl.BlockSpec((tm, d), lambda i: (i, 0)),
                  pl.BlockSpec((1, d), lambda i: (0, 0)),
                  pl.BlockSpec((n_e, d), lambda i: (0, 0)),
                  pl.BlockSpec((n_e, 1), lambda i: (0, 0))],
        out_specs=[pl.BlockSpec((tm, d), lambda i: (i, 0)),
                   pl.BlockSpec((n_e, tm), lambda i: (0, i)),
                   pl.BlockSpec((n_e, tm), lambda i: (0, i))],
        out_shape=[jax.ShapeDtypeStruct((m, d), F32),
                   jax.ShapeDtypeStruct((n_e, m), F32),
                   jax.ShapeDtypeStruct((n_e, m), F32)],
        compiler_params=_params("parallel"),
        name="norm_router",
    )(x, g.reshape(1, d), router_w.T, router_b.reshape(n_e, 1))


def _mm_full_kernel(*refs, n_a, n_b, n_e, n_o, pairs, inner_axis, epilogue, b_transposed):
    a_refs = refs[:n_a]
    b_refs = refs[n_a:n_a + n_b]
    e_refs = refs[n_a + n_b:n_a + n_b + n_e]
    o_refs = refs[n_a + n_b + n_e:n_a + n_b + n_e + n_o]
    w_refs = refs[n_a + n_b + n_e + n_o:]

    @pl.when(pl.program_id(inner_axis) == 0)
    def _():
        for b_ref, w_ref in zip(b_refs, w_refs):
            w_ref[...] = b_ref[...].astype(BF16)

    dot = _dot_nt if b_transposed else _dot
    ws = [w_ref[0] if len(w_ref.shape) == 3 else w_ref[...] for w_ref in w_refs]
    accs = [dot(a_refs[ia][...], ws[ib]) for ia, ib in pairs]
    epilogue(accs, e_refs, o_refs)


def _block_dim_size(s):
    return s.block_size if isinstance(s, pl.Element) else s


def mm_full(name, grid, a_ops, b_ops, e_ops, outs, pairs, epilogue, b_transposed=False):
    ops = a_ops + b_ops + e_ops
    kernel = functools.partial(
        _mm_full_kernel, n_a=len(a_ops), n_b=len(b_ops), n_e=len(e_ops), n_o=len(outs),
        pairs=pairs, inner_axis=len(grid) - 1, epilogue=epilogue, b_transposed=b_transposed)
    scratch = [pltpu.VMEM(tuple(_block_dim_size(s) for s in blk if s is not None), BF16) for _, blk, _ in b_ops]
    res = pl.pallas_call(
        kernel,
        grid=grid,
        in_specs=[pl.BlockSpec(blk, imap) for _, blk, imap in ops],
        out_specs=[pl.BlockSpec(blk, imap) for _, blk, imap in outs],
        out_shape=[sds for sds, _, _ in outs],
        scratch_shapes=scratch,
        compiler_params=_params(*(("arbitrary",) * len(grid))),
        name=name,
    )(*[arr for arr, _, _ in ops])
    return res


COL_CHUNK = 256


def _mm_acc_kernel(a_ref, b_ref, r_ref, o_ref):
    cols = range(0, o_ref.shape[1], COL_CHUNK)

    @pl.when(pl.program_id(2) == 0)
    def _():
        a = a_ref[...]
        for c in cols:
            sl = slice(c, c + COL_CHUNK)
            o_ref[:, sl] = r_ref[:, sl] + _dot(a, b_ref[:, sl].astype(BF16))

    @pl.when(pl.program_id(2) > 0)
    def _():
        a = a_ref[...]
        for c in cols:
            sl = slice(c, c + COL_CHUNK)
            o_ref[:, sl] += _dot(a, b_ref[:, sl].astype(BF16))


def mm_acc(name, a, b, res, tm, tn, tk):
    m, k = a.shape
    n = b.shape[1]
    assert m % tm == 0 and n % tn == 0 and k % tk == 0 and tn % COL_CHUNK == 0 and tk % LANES == 0
    return pl.pallas_call(
        _mm_acc_kernel,
        grid=(m // tm, n // tn, k // tk),
        in_specs=[pl.BlockSpec((tm, tk), lambda i, j, kk: (i, kk)),
                  pl.BlockSpec((tk, tn), lambda i, j, kk: (kk, j)),
                  pl.BlockSpec((tm, tn), lambda i, j, kk: (i, j))],
        out_specs=pl.BlockSpec((tm, tn), lambda i, j, kk: (i, j)),
        out_shape=jax.ShapeDtypeStruct((m, n), F32),
        compiler_params=_params("parallel", "parallel", "arbitrary"),
        name=name,
    )(a, b, res)


def in_projection(h, w_in_t, layer):
    m, d = h.shape
    tm = 1024
    a_op = [(h, (tm, d), lambda j, i: (i, 0))]

    tq = 512
    q_heads = tq // HEAD_DIM

    def q_epi(accs, e_refs, o_refs):
        for c in range(q_heads):
            o_refs[0][c] = (accs[0][:, c * HEAD_DIM:(c + 1) * HEAD_DIM] * (HEAD_DIM ** -0.5)).T.astype(BF16)

    (qt,) = mm_full(
        "in_proj_q", (NSA_WIDTH // tq, m // tm), a_op,
        [(w_in_t, (None, tq, d), lambda j, i: (layer, j, 0))], [],
        [(jax.ShapeDtypeStruct((N_Q_HEADS, HEAD_DIM, m), BF16), (q_heads, HEAD_DIM, tm), lambda j, i: (j, 0, i))],
        [(0, 0)], q_epi, b_transposed=True)

    sec0 = KV_OFF // KV_WIDTH

    def rows_epi(accs, e_refs, o_refs):
        for c in range(N_KV_HEADS):
            o_refs[0][c] = accs[0][:, c * HEAD_DIM:(c + 1) * HEAD_DIM].astype(BF16)

    (kvr,) = mm_full(
        "in_proj_kv_rows", (4, m // tm), a_op,
        [(w_in_t, (None, KV_WIDTH, d), lambda j, i: (layer, sec0 + j + j // 3, 0))], [],
        [(jax.ShapeDtypeStruct((4 * N_KV_HEADS, m, HEAD_DIM), BF16),
          (N_KV_HEADS, tm, HEAD_DIM), lambda j, i: (j, i, 0))],
        [(0, 0)], rows_epi, b_transposed=True)

    def cols_epi(accs, e_refs, o_refs):
        for c in range(N_KV_HEADS):
            o_refs[0][c] = accs[0][:, c * HEAD_DIM:(c + 1) * HEAD_DIM].T.astype(BF16)

    (vt,) = mm_full(
        "in_proj_v_cols", (2, m // tm), a_op,
        [(w_in_t, (None, KV_WIDTH, d), lambda j, i: (layer, sec0 + 3 + 2 * j, 0))], [],
        [(jax.ShapeDtypeStruct((2 * N_KV_HEADS, HEAD_DIM, m), BF16),
          (N_KV_HEADS, HEAD_DIM, tm), lambda j, i: (j, 0, i))],
        [(0, 0)], cols_epi, b_transposed=True)

    def f32_epi(accs, e_refs, o_refs):
        o_refs[0][...] = accs[0]

    def f32_t_epi(accs, e_refs, o_refs):
        o_refs[0][...] = accs[0].T

    assert NG_OFF % LANES == 0
    (ngt,) = mm_full(
        "in_proj_ng", (1, m // tm), a_op,
        [(w_in_t, (None, LANES, d), lambda j, i: (layer, NG_OFF // LANES, 0))], [],
        [(jax.ShapeDtypeStruct((LANES, m), F32), (LANES, tm), lambda j, i: (0, i))],
        [(0, 0)], f32_t_epi, b_transposed=True)

    n_rest = w_in_t.shape[1] - REST_OFF
    tr = 1024
    assert n_rest % tr == 0 and REST_OFF % 8 == 0
    (rest,) = mm_full(
        "in_proj_rest", (n_rest // tr, m // tm), a_op,
        [(w_in_t, (pl.Element(1), pl.Element(tr), pl.Element(d)), lambda j, i: (layer, pl.multiple_of(REST_OFF + j * tr, 8), 0))], [],
        [(jax.ShapeDtypeStruct((m, n_rest), F32), (tm, tr), lambda j, i: (i, j))],
        [(0, 0)], f32_epi, b_transposed=True)
    return qt, kvr, vt, ngt, rest


def _cmp_kernel(a_ref, w1_ref, w2_ref, pe_ref, o_ref):
    a = a_ref[...]
    rows = a.shape[0]
    half = a.shape[1]
    w1 = w1_ref[...].astype(BF16)
    p0 = _dot(a, w1[:half])
    p1 = _dot(a, w1[half:])
    bias = _dot(pe_ref[...].astype(BF16), w1)[0:1]
    hid = p0 + pltpu.roll(p1, rows - 1, 0) + bias
    o_ref[...] = _dot(_gelu(hid).astype(BF16), w2_ref[...].astype(BF16))


def compress(kv, batch, pe_k, pe_v, kw1, kw2, vw1, vw2):
    m = kv.shape[1]
    n_chunks = m // batch // CMP_STRIDE
    rows = batch * n_chunks
    feat = CMP_STRIDE * HEAD_DIM
    a = kv[:2 * N_KV_HEADS].reshape(2, N_KV_HEADS, rows, feat)
    w1 = jnp.stack([kw1, vw1])
    w2 = jnp.stack([kw2, vw2])
    pe = jnp.stack([pe_k, pe_v]).reshape(2, 1, CMP_BLOCK * HEAD_DIM)
    pe = jnp.broadcast_to(pe, (2, 8, CMP_BLOCK * HEAD_DIM))
    return pl.pallas_call(
        _cmp_kernel,
        grid=(2, N_KV_HEADS),
        in_specs=[pl.BlockSpec((None, None, rows, feat), lambda s, g: (s, g, 0, 0)),
                  pl.BlockSpec((None, 2 * feat, CMP_HIDDEN), lambda s, g: (s, 0, 0)),
                  pl.BlockSpec((None, CMP_HIDDEN, HEAD_DIM), lambda s, g: (s, 0, 0)),
                  pl.BlockSpec((None, 8, 2 * feat), lambda s, g: (s, 0, 0))],
        out_specs=pl.BlockSpec((None, None, rows, HEAD_DIM), lambda s, g: (s, g, 0, 0)),
        out_shape=jax.ShapeDtypeStruct((2, N_KV_HEADS, rows, HEAD_DIM), F32),
        compiler_params=_params("parallel", "parallel"),
        name="nsa_compress",
    )(a, w1, w2, pe)


def _alibi_slopes():
    return np.array([2.0 ** (-8.0 * (h + 1) / N_Q_HEADS) for h in range(N_Q_HEADS)], dtype=np.float32)


def _overlap_t(n_cmp_pad, n_slc):
    cs = np.arange(n_cmp_pad)[None, :] * CMP_STRIDE
    ss = np.arange(n_slc)[:, None] * SLC_BLOCK
    ov = np.clip(np.minimum(cs + CMP_BLOCK, ss + SLC_BLOCK) - np.maximum(cs, ss), 0, None)
    return (ov / CMP_STRIDE).astype(np.float32)


def _sel_kernel(slopes_ref, q_ref, kc_ref, vc_ref, ovt_ref, oc_ref, sel_ref, *, n_cmp):
    g = pl.program_id(1)
    i = pl.program_id(2)
    tq = q_ref.shape[2]
    ncp = kc_ref.shape[0]
    n_slc = ovt_ref.shape[0]
    kc = kc_ref[...].astype(BF16)
    vct = vc_ref[...].T.astype(BF16)

    t = i * tq + lax.broadcasted_iota(jnp.int32, (ncp, tq), 1)
    c = lax.broadcasted_iota(jnp.int32, (ncp, tq), 0)
    dist = t - (c * CMP_STRIDE + (CMP_BLOCK - 1))
    mask = (dist >= 0) & (c < n_cmp)
    distf = dist.astype(F32)

    scores = [_dot(kc, q_ref[j]) for j in range(Q_PER_KV)]
    probs = []
    for j in range(Q_PER_KV):
        s = scores[j] - slopes_ref[g * Q_PER_KV + j] * distf
        s = jnp.where(mask, s, NEG)
        mx = jnp.max(s, axis=0, keepdims=True)
        e = jnp.where(mask, jnp.exp(s - mx), 0.0)
        probs.append(e / jnp.maximum(jnp.sum(e, axis=0, keepdims=True), 1e-30))
    for j in range(Q_PER_KV):
        oc_ref[j] = _dot(vct, probs[j].astype(BF16))
    p_sum = functools.reduce(lambda a, b: a + b, probs)

    ovt = ovt_ref[...]
    hi = p_sum.astype(BF16)
    lo = (p_sum - hi.astype(F32)).astype(BF16)
    imp = _dot(ovt, hi) + _dot(ovt, lo)
    tt = i * tq + lax.broadcasted_iota(jnp.int32, (n_slc, tq), 1)
    blk = lax.broadcasted_iota(jnp.int32, (n_slc, tq), 0)
    cur = tt // SLC_BLOCK
    valid = blk * SLC_BLOCK <= tt
    forced = (blk == 0) | (blk == cur) | (blk == cur - 1)
    score = jnp.where(valid, imp, -1.0)
    score = jnp.where(forced, FORCED_SCORE, score)
    rank = jnp.zeros((n_slc, tq), F32)
    for mrow in range(n_slc):
        other = jnp.broadcast_to(score[mrow:mrow + 1, :], (n_slc, tq))
        beats = (other > score) | ((other == score) & (blk > mrow))
        rank = rank + jnp.where(beats, 1.0, 0.0)
    sel_ref[...] = jnp.where(rank < float(min(N_SELECT, n_slc)), 1.0, 0.0)


def compressed_and_select(qt, cmp_kv, batch, tq=256):
    m = qt.shape[2]
    seq = m // batch
    nq = seq // tq
    ncp = cmp_kv.shape[2] // batch
    n_slc = seq // SLC_BLOCK
    ovt = jnp.asarray(_overlap_t(ncp, n_slc), BF16)
    slopes = jnp.asarray(_alibi_slopes())
    kernel = functools.partial(_sel_kernel, n_cmp=ncp - 1)
    return pl.pallas_call(
        kernel,
        grid=(batch, N_KV_HEADS, nq),
        in_specs=[pl.BlockSpec(memory_space=pltpu.SMEM),
                  pl.BlockSpec((Q_PER_KV, HEAD_DIM, tq), lambda b, g, i: (g, 0, b * nq + i)),
                  pl.BlockSpec((None, None, ncp, HEAD_DIM), lambda b, g, i: (0, g, b, 0)),
                  pl.BlockSpec((None, None, ncp, HEAD_DIM), lambda b, g, i: (1, g, b, 0)),
                  pl.BlockSpec((n_slc, ncp), lambda b, g, i: (0, 0))],
        out_specs=[pl.BlockSpec((Q_PER_KV, HEAD_DIM, tq), lambda b, g, i: (g, 0, b * nq + i)),
                   pl.BlockSpec((None, None, n_slc, tq), lambda b, g, i: (b, g, 0, i))],
        out_shape=[jax.ShapeDtypeStruct((N_Q_HEADS, HEAD_DIM, m), F32),
                   jax.ShapeDtypeStruct((batch, N_KV_HEADS, n_slc, seq), F32)],
        compiler_params=_params("parallel", "parallel", "parallel"),
        name="nsa_compressed_select",
    )(slopes, qt, cmp_kv, cmp_kv, ovt)


AUX_SLOPE = HEAD_DIM
AUX_SEL = HEAD_DIM + 16
ATTN_K = 2 * HEAD_DIM
STRIP = 32


def _attn_kernel(slopes_ref, q_ref, ks_ref, vs_ref, kw_ref, vw_ref, auxs_ref, auxw_ref, sel_ref, oc_ref,
                 ng_ref, o_ref, ka_s_ref, ka_w_ref, qa_ref, s_ref, p_ref, mask_ref, m_ref, l_ref, acc_ref,
                 os_ref, sig_ref):
    g = pl.program_id(1)
    i = pl.program_id(2)
    tq = q_ref.shape[2]
    tk = tq
    q0 = i * tq
    n_slc = sel_ref.shape[0]

    @pl.when(i == 0)
    def _():
        ka_s_ref[:, 0:HEAD_DIM] = ks_ref[...]
        ka_s_ref[:, HEAD_DIM:ATTN_K] = auxs_ref[...]
        ka_w_ref[:, 0:HEAD_DIM] = kw_ref[...]
        ka_w_ref[:, HEAD_DIM:ATTN_K] = auxw_ref[...]
        kr = lax.broadcasted_iota(jnp.int32, (tk, tq), 0)
        qc = lax.broadcasted_iota(jnp.int32, (tk, tq), 1)
        mask_ref[0] = jnp.where(kr <= qc, 0.0, NEG)
        mask_ref[1] = jnp.where(kr > qc, 0.0, NEG)

    sig_ref[...] = _sigmoid(ng_ref[...])
    sel_neg = (sel_ref[...] - 1.0) * (-NEG)
    row16 = lax.broadcasted_iota(jnp.int32, (16, tq), 0)
    pad = jnp.zeros((ATTN_K - AUX_SEL - n_slc, tq), F32)
    for j in range(Q_PER_KV):
        sl = jnp.full((16, tq), slopes_ref[g * Q_PER_KV + j], F32)
        hi = sl.astype(BF16).astype(F32)
        mid = (sl - hi).astype(BF16).astype(F32)
        lo = (sl - hi - mid).astype(BF16).astype(F32)
        pieces = jnp.where(row16 == 0, hi, jnp.where(row16 == 1, mid, jnp.where(row16 == 2, lo, 0.0)))
        qa = jnp.concatenate([q_ref[j].astype(F32), pieces, sel_neg, pad], axis=0)
        qa_ref[j] = qa.astype(BF16)

    def scores(ka_ref, k0, buf):
        ka = ka_ref[pl.ds(k0, tk), :]
        for j in range(Q_PER_KV):
            s_ref[buf, j] = _dot(ka, qa_ref[j])

    def softmax_pv(v_ref, k0, buf, mask_idx, first):
        vc = v_ref[:, pl.ds(k0, tk)]
        for j in range(Q_PER_KV):
            shift = slopes_ref[g * Q_PER_KV + j] * k0.astype(F32)
            mx = None
            for r in range(0, tk, STRIP):
                x = s_ref[buf, j, r:r + STRIP, :]
                if mask_idx is not None:
                    x = x + mask_ref[mask_idx, r:r + STRIP, :]
                    s_ref[buf, j, r:r + STRIP, :] = x
                mx = x if mx is None else jnp.maximum(mx, x)
            m_cur = jnp.max(mx, axis=0, keepdims=True) + shift
            if first:
                m_new = m_cur
            else:
                m_old = m_ref[j]
                m_new = jnp.maximum(m_old, m_cur)
                alpha = jnp.exp(m_old - m_new)
            m_ref[j] = m_new
            m_loc = m_new - shift
            ls = None
            for r in range(0, tk, STRIP):
                p = jnp.exp(s_ref[buf, j, r:r + STRIP, :] - m_loc)
                ls = p if ls is None else ls + p
                p_ref[j, r:r + STRIP, :] = p.astype(BF16)
            l_cur = jnp.sum(ls, axis=0, keepdims=True)
            pv = _dot(vc, p_ref[j])
            if first:
                l_ref[j] = l_cur
                acc_ref[j] = pv
            else:
                l_ref[j] = alpha * l_ref[j] + l_cur
                acc_ref[j] = alpha * acc_ref[j] + pv

    k_diag = pl.multiple_of(q0, tk)
    last = jnp.maximum(i - 1, 0)
    k_of = lambda c: pl.multiple_of(jnp.minimum(c, last) * tk, tk)
    scores(ka_s_ref, k_diag, 1)
    scores(ka_s_ref, k_of(0), 0)
    softmax_pv(vs_ref, k_diag, 1, 0, True)

    def slc_pair(cc, carry):
        c0 = 2 * cc
        scores(ka_s_ref, k_of(c0 + 1), 1)
        softmax_pv(vs_ref, k_of(c0), 0, None, False)

        @pl.when(c0 + 1 < i)
        def _():
            scores(ka_s_ref, k_of(c0 + 2), 0)
            softmax_pv(vs_ref, k_of(c0 + 1), 1, None, False)
        return carry

    lax.fori_loop(0, (i + 1) // 2, slc_pair, 0)
    for j in range(Q_PER_KV):
        os_ref[j] = acc_ref[j] / l_ref[j]

    n_back = WINDOW // tk
    back_k0 = lambda back: pl.multiple_of(jnp.maximum(q0 - back * tk, 0), tk)
    scores(ka_w_ref, k_diag, 0)
    scores(ka_w_ref, back_k0(1), 1)
    softmax_pv(vw_ref, k_diag, 0, 0, True)
    for back in range(1, n_back + 1):
        @pl.when(i >= back)
        def _(back=back):
            if back < n_back:
                scores(ka_w_ref, back_k0(back + 1), (back + 1) % 2)
            softmax_pv(vw_ref, back_k0(back), back % 2, 1 if back == n_back else None, False)

    for j in range(Q_PER_KV):
        base = (g * Q_PER_KV + j) * 3
        out_t = (sig_ref[pl.ds(base, 1), :] * oc_ref[j]
                 + sig_ref[pl.ds(base + 1, 1), :] * os_ref[j]
                 + sig_ref[pl.ds(base + 2, 1), :] * (acc_ref[j] / l_ref[j]))
        o_ref[:, j * HEAD_DIM:(j + 1) * HEAD_DIM] = out_t.T.astype(o_ref.dtype)


def _key_aux(seq, tk, with_blocks):
    aux = np.zeros((seq, ATTN_K - HEAD_DIM), np.float32)
    pos = np.arange(seq)
    aux[:, AUX_SLOPE - HEAD_DIM:AUX_SLOPE - HEAD_DIM + 3] = (pos % tk)[:, None]
    if with_blocks:
        aux[pos, AUX_SEL - HEAD_DIM + pos // SLC_BLOCK] = 1.0
    return aux


def selected_window_attention(qt, kvr, vt, sel, oct, ngt, batch, tq=256):
    m = qt.shape[2]
    seq = m // batch
    nq = seq // tq
    n_slc = seq // SLC_BLOCK
    assert tq <= 256 and AUX_SEL + n_slc <= ATTN_K
    aux_s = jnp.asarray(_key_aux(seq, tq, True), BF16)
    aux_w = jnp.asarray(_key_aux(seq, tq, False), BF16)
    slopes = jnp.asarray(_alibi_slopes())
    n_g = N_KV_HEADS

    def k_spec(section):
        return pl.BlockSpec((None, seq, HEAD_DIM), lambda b, g, i: (section * n_g + g, b, 0))

    def v_spec(section):
        return pl.BlockSpec((None, HEAD_DIM, seq), lambda b, g, i: (section * n_g + g, 0, b))

    head_blk = pl.BlockSpec((Q_PER_KV, HEAD_DIM, tq), lambda b, g, i: (g, 0, b * nq + i))
    aux_blk = pl.BlockSpec((seq, ATTN_K - HEAD_DIM), lambda b, g, i: (0, 0))
    return pl.pallas_call(
        _attn_kernel,
        grid=(batch, N_KV_HEADS, nq),
        in_specs=[pl.BlockSpec(memory_space=pltpu.SMEM),
                  head_blk,
                  k_spec(2), v_spec(0), k_spec(3), v_spec(1),
                  aux_blk, aux_blk,
                  pl.BlockSpec((None, None, n_slc, tq), lambda b, g, i: (b, g, 0, i)),
                  head_blk,
                  pl.BlockSpec((LANES, tq), lambda b, g, i: (0, b * nq + i))],
        out_specs=pl.BlockSpec((tq, GROUP_WIDTH), lambda b, g, i: (b * nq + i, g)),
        out_shape=jax.ShapeDtypeStruct((m, NSA_WIDTH), BF16),
        scratch_shapes=[pltpu.VMEM((seq, ATTN_K), BF16),
                        pltpu.VMEM((seq, ATTN_K), BF16),
                        pltpu.VMEM((Q_PER_KV, ATTN_K, tq), BF16),
                        pltpu.VMEM((2, Q_PER_KV, tq, tq), F32),
                        pltpu.VMEM((Q_PER_KV, tq, tq), BF16),
                        pltpu.VMEM((2, tq, tq), F32),
                        pltpu.VMEM((Q_PER_KV, 1, tq), F32),
                        pltpu.VMEM((Q_PER_KV, 1, tq), F32),
                        pltpu.VMEM((Q_PER_KV, HEAD_DIM, tq), F32),
                        pltpu.VMEM((Q_PER_KV, HEAD_DIM, tq), F32),
                        pltpu.VMEM((LANES, tq), F32)],
        compiler_params=_params("arbitrary", "arbitrary", "arbitrary"),
        name="nsa_selected_window",
    )(slopes, qt, kvr, vt, kvr, vt, aux_s, aux_w, sel, oct, ngt)


def _sgu_kernel(u_ref, v_ref, lg_ref, lb_ref, ws_ref, bs_ref, o_ref):
    rows = u_ref.shape[0]
    v = _gelu(v_ref[...])
    mu = jnp.mean(v, axis=-1, keepdims=True)
    var = jnp.mean(jnp.square(v - mu), axis=-1, keepdims=True)
    vl = ((v - mu) * lax.rsqrt(var + NORM_EPS) * lg_ref[...] + lb_ref[...]).astype(BF16)
    r = lax.broadcasted_iota(jnp.int32, (SGU_CHUNK, SGU_CHUNK), 0)
    c = lax.broadcasted_iota(jnp.int32, (SGU_CHUNK, SGU_CHUNK), 1)
    gd = SGU_WIDTH // SGU_GROUPS
    for grp in range(SGU_GROUPS):
        w = jnp.where(c <= r, ws_ref[grp], 0.0).astype(BF16)
        lanes = slice(grp * gd, (grp + 1) * gd)
        for n in range(rows // SGU_CHUNK):
            rs = slice(n * SGU_CHUNK, (n + 1) * SGU_CHUNK)
            vm = _dot(w, vl[rs, lanes]) + bs_ref[:, lanes]
            o_ref[rs, lanes] = (_gelu(u_ref[rs, lanes]) * vm).astype(o_ref.dtype)


def spatial_gating(rest, ln_g, ln_b, w_s, b_s, tm=512):
    m = rest.shape[0]
    gd = SGU_WIDTH // SGU_GROUPS
    bias = jnp.repeat(b_s.T, gd, axis=1)
    return pl.pallas_call(
        _sgu_kernel,
        grid=(m // tm,),
        in_specs=[pl.BlockSpec((tm, SGU_WIDTH), lambda i: (i, 0)),
                  pl.BlockSpec((tm, SGU_WIDTH), lambda i: (i, 1)),
                  pl.BlockSpec((1, SGU_WIDTH), lambda i: (0, 0)),
                  pl.BlockSpec((1, SGU_WIDTH), lambda i: (0, 0)),
                  pl.BlockSpec((SGU_GROUPS, SGU_CHUNK, SGU_CHUNK), lambda i: (0, 0, 0)),
                  pl.BlockSpec((SGU_CHUNK, SGU_WIDTH), lambda i: (0, 0))],
        out_specs=pl.BlockSpec((tm, SGU_WIDTH), lambda i: (i, 0)),
        out_shape=jax.ShapeDtypeStruct((m, SGU_WIDTH), BF16),
        compiler_params=_params("parallel"),
        name="spatial_gating",
    )(rest, rest, ln_g.reshape(1, -1), ln_b.reshape(1, -1), w_s, bias)


def merge_branches(oa, ob, rest, w_a, w_b, layer, tm=512, tn=1024):
    m = oa.shape[0]
    d = w_a.shape[2]
    ga_blk = 2 * SGU_WIDTH // tn
    gb_blk = (2 * SGU_WIDTH + d) // tn

    def epi(accs, e_refs, o_refs):
        y = _sigmoid(e_refs[0][...]) * accs[0] + _sigmoid(e_refs[1][...]) * accs[1]
        o_refs[0][...] = y.astype(BF16)

    (y,) = mm_full(
        "merge_branches", (d // tn, m // tm),
        [(oa, (tm, oa.shape[1]), lambda j, i: (i, 0)), (ob, (tm, ob.shape[1]), lambda j, i: (i, 0))],
        [(w_a, (None, w_a.shape[1], tn), lambda j, i: (layer, 0, j)),
         (w_b, (None, w_b.shape[1], tn), lambda j, i: (layer, 0, j))],
        [(rest, (tm, tn), lambda j, i: (i, ga_blk + j)), (rest, (tm, tn), lambda j, i: (i, gb_blk + j))],
        [(jax.ShapeDtypeStruct((m, d), BF16), (tm, tn), lambda j, i: (i, j))],
        [(0, 0), (1, 1)], epi)
    return y


def residual_matmul(name, a, w, layer, x, tm=512, tn=1024):
    m, k = a.shape
    n = w.shape[2]

    def epi(accs, e_refs, o_refs):
        o_refs[0][...] = e_refs[0][...] + accs[0]

    (out,) = mm_full(
        name, (n // tn, m // tm),
        [(a, (tm, k), lambda j, i: (i, 0))],
        [(w, (None, k, tn), lambda j, i: (layer, 0, j))],
        [(x, (tm, tn), lambda j, i: (i, j))],
        [(jax.ShapeDtypeStruct((m, n), F32), (tm, tn), lambda j, i: (i, j))],
        [(0, 0)], epi)
    return out


def swiglu_up(h, w1, w3, tm=1024, tn=512):
    m, d = h.shape
    f = w1.shape[1]

    def epi(accs, e_refs, o_refs):
        o_refs[0][...] = (accs[0] * _sigmoid(accs[0]) * accs[1]).astype(BF16)

    (out,) = mm_full(
        "swiglu_up", (f // tn, m // tm),
        [(h, (tm, d), lambda j, i: (i, 0))],
        [(w1, (d, tn), lambda j, i: (0, j)), (w3, (d, tn), lambda j, i: (0, j))],
        [],
        [(jax.ShapeDtypeStruct((m, f), BF16), (tm, tn), lambda j, i: (i, j))],
        [(0, 0), (0, 1)], epi)
    return out


MOE_TILE = 512
ROW_QUARTERS = 4


def _row_options(tile):
    step = tile // ROW_QUARTERS
    return tuple(range(step, tile + 1, step))


def _route_kernel(sel_ref, w_ref, tri_ref, pos_ref, wt_ref, meta_ref, cum_ref, *, tile):
    n_e, m = sel_ref.shape
    ck = tri_ref.shape[0]
    carry = jnp.zeros((n_e, 1), F32)
    for c in range(m // ck):
        sl = slice(c * ck, (c + 1) * ck)
        cs = _dot(sel_ref[:, sl].astype(BF16), tri_ref[...]) + carry
        cum_ref[:, sl] = cs
        carry = cs[:, ck - 1:ck]
    padded = jnp.ceil(carry / tile) * tile
    sub = lax.broadcasted_iota(jnp.int32, (n_e, 1), 0)
    start = jnp.zeros((n_e, 1), F32)
    run = jnp.zeros((1, 1), F32)
    for ex in range(n_e):
        start = jnp.where(sub == ex, run, start)
        run = run + padded[ex:ex + 1, :]
    sel = sel_ref[...] > 0.5
    pos = start + cum_ref[...] - 1.0
    eidx = lax.broadcasted_iota(jnp.int32, (n_e, m), 0)
    e_lo = jnp.min(jnp.where(sel, eidx, n_e), axis=0, keepdims=True)
    e_hi = jnp.max(jnp.where(sel, eidx, -1), axis=0, keepdims=True)
    is_lo = eidx == e_lo
    is_hi = eidx == e_hi
    w = w_ref[...]
    pos_a = jnp.sum(jnp.where(is_lo, pos, 0.0), axis=0, keepdims=True)
    pos_b = jnp.sum(jnp.where(is_hi, pos, 0.0), axis=0, keepdims=True)
    w_a = jnp.sum(jnp.where(is_lo, w, 0.0), axis=0, keepdims=True)
    w_b = jnp.sum(jnp.where(is_hi, w, 0.0), axis=0, keepdims=True)
    pos_ref[...] = jnp.where(eidx == 0, pos_a, jnp.where(eidx == 1, pos_b, 0.0)).astype(jnp.int32)
    wt_ref[...] = jnp.where(eidx == 0, w_a, jnp.where(eidx == 1, w_b, 0.0)).T
    tile_lo = lax.broadcasted_iota(jnp.int32, (n_e, LANES), 1).astype(F32) * tile
    t_exp = jnp.sum(jnp.where(start + padded <= tile_lo, 1.0, 0.0), axis=0, keepdims=True)
    t_exp = jnp.minimum(t_exp, n_e - 1.0)
    r8 = lax.broadcasted_iota(jnp.int32, (n_e, LANES), 0)
    last_row = jnp.sum(jnp.where(r8.astype(F32) == t_exp, start + carry, 0.0), axis=0, keepdims=True)
    filled = jnp.clip(last_row - tile_lo[0:1], 0.0, float(tile))
    meta_ref[...] = jnp.where(r8 == 0, t_exp, jnp.where(r8 == 1, run / tile,
                                                       jnp.where(r8 == 2, filled, 0.0))).astype(jnp.int32)


def moe_route(sel, w, tile):
    n_e, m = sel.shape
    ck = 256
    tri = jnp.asarray(np.triu(np.ones((ck, ck), np.float32)), BF16)
    full = lambda shape: pl.BlockSpec(shape, lambda: tuple(0 for _ in shape))
    return pl.pallas_call(
        functools.partial(_route_kernel, tile=tile),
        in_specs=[full((n_e, m)), full((n_e, m)), full((ck, ck))],
        out_specs=[full((n_e, m)), full((m, n_e)), full((n_e, LANES))],
        out_shape=[jax.ShapeDtypeStruct((n_e, m), jnp.int32),
                   jax.ShapeDtypeStruct((m, n_e), F32),
                   jax.ShapeDtypeStruct((n_e, LANES), jnp.int32)],
        scratch_shapes=[pltpu.VMEM((n_e, m), F32)],
        compiler_params=pltpu.CompilerParams(vmem_limit_bytes=VMEM_LIMIT),
        name="moe_route",
    )(sel, w, tri)


def _row_gather_start(src_hbm, dst, rows_ref, base, n_rows, sem, unroll=8):
    def body(r, carry):
        src_row = rows_ref[base + r]
        pltpu.make_async_copy(src_hbm.at[pl.ds(src_row, 1)], dst.at[pl.ds(r, 1)], sem).start()
        return carry
    lax.fori_loop(0, n_rows, body, 0, unroll=unroll)


def _row_gather_wait(src_hbm, dst, sem):
    pltpu.make_async_copy(src_hbm.at[pl.ds(0, dst.shape[0])], dst, sem).wait()


def _moe_gather_kernel(pos_ref, meta_ref, h_hbm, zeros_hbm, o_ref, tok_ref, buf_ref, sem_ref):
    i = pl.program_id(0)
    n = meta_ref[1, 0]
    tile = o_ref.shape[0]
    m = pos_ref.shape[0] // 2

    @pl.when(i == 0)
    def _():
        clear = pltpu.make_async_copy(zeros_hbm, tok_ref, sem_ref.at[2])
        clear.start()
        clear.wait()

        def fill(t, carry):
            tok_ref[pos_ref[t]] = t
            tok_ref[pos_ref[m + t]] = t
            return carry
        lax.fori_loop(0, m, fill, 0, unroll=8)
        _row_gather_start(h_hbm, buf_ref.at[0], tok_ref, 0, tile, sem_ref.at[0])

    @pl.when(i + 1 < n)
    def _():
        nxt = (i + 1) % 2
        _row_gather_start(h_hbm, buf_ref.at[nxt], tok_ref, (i + 1) * tile, tile, sem_ref.at[nxt])

    @pl.when(i < n)
    def _():
        cur = i % 2
        _row_gather_wait(h_hbm, buf_ref.at[cur], sem_ref.at[cur])
        o_ref[...] = buf_ref[cur].astype(o_ref.dtype)

    @pl.when(i >= n)
    def _():
        o_ref[...] = jnp.zeros_like(o_ref)


def moe_gather(pos2, meta, h, n_tiles, tile):
    m, d = h.shape
    rows = n_tiles * tile
    return pl.pallas_call(
        _moe_gather_kernel,
        grid_spec=pltpu.PrefetchScalarGridSpec(
            num_scalar_prefetch=2,
            grid=(n_tiles,),
            in_specs=[pl.BlockSpec(memory_space=pl.ANY), pl.BlockSpec(memory_space=pl.ANY)],
            out_specs=pl.BlockSpec((tile, d), lambda i, pos, meta: (i, 0)),
            scratch_shapes=[pltpu.SMEM((rows,), jnp.int32),
                            pltpu.VMEM((2, tile, d), F32),
                            pltpu.SemaphoreType.DMA((3,))]),
        out_shape=jax.ShapeDtypeStruct((rows, d), BF16),
        compiler_params=_params("arbitrary"),
        name="moe_gather",
    )(pos2, meta, h, jnp.zeros((rows,), jnp.int32))


def _moe_up_kernel(meta_ref, x_ref, w1_ref, w3_ref, w2_ref, o_ref, w2b_ref):
    i = pl.program_id(1)

    def round_w2_block():
        w2b_ref[...] = w2_ref[...].astype(BF16)

    filled = meta_ref[2, i]
    tile = o_ref.shape[0]
    for rows in _row_options(tile):
        @pl.when((filled > rows - tile // ROW_QUARTERS) & (filled <= rows))
        def _(rows=rows):
            x = x_ref[0:rows, :]
            for c in range(0, o_ref.shape[1], COL_CHUNK):
                sl = slice(c, c + COL_CHUNK)
                a1 = _dot(x, w1_ref[:, sl].astype(BF16))
                a3 = _dot(x, w3_ref[:, sl].astype(BF16))
                o_ref[0:rows, sl] = (a1 * _sigmoid(a1) * a3).astype(o_ref.dtype)
            if rows < tile:
                o_ref[rows:tile, :] = jnp.zeros((tile - rows, o_ref.shape[1]), o_ref.dtype)
            round_w2_block()

    @pl.when(filled == 0)
    def _():
        o_ref[...] = jnp.zeros_like(o_ref)
        round_w2_block()


BF16_SUBLANES = 16


def _cast_rows_per_step(total_rows, steps):
    for rows in range(BF16_SUBLANES, total_rows + 1, BF16_SUBLANES):
        if total_rows % rows == 0 and total_rows // rows <= steps:
            return rows
    raise ValueError("weights cannot be split over the grid steps")


def moe_up(meta, xs, w1, w3, w2, tile, tn=1024):
    rows, d = xs.shape
    n_e, _, f = w1.shape
    assert f % tn == 0 and rows % tile == 0
    n_tiles = rows // tile
    w2_flat = w2.reshape(n_e * f, w2.shape[2])
    cast_rows = _cast_rows_per_step(w2_flat.shape[0], (f // tn) * n_tiles)
    n_cast = w2_flat.shape[0] // cast_rows
    last = lambda i, meta: jnp.maximum(jnp.minimum(i, meta[1, 0] - 1), 0)
    cast_blk = lambda j, i, meta: (jnp.minimum(j * n_tiles + i, n_cast - 1), 0)
    up, w2b = pl.pallas_call(
        _moe_up_kernel,
        grid_spec=pltpu.PrefetchScalarGridSpec(
            num_scalar_prefetch=1,
            grid=(f // tn, n_tiles),
            in_specs=[pl.BlockSpec((tile, d), lambda j, i, meta: (last(i, meta), 0)),
                      pl.BlockSpec((None, d, tn), lambda j, i, meta: (meta[0, i], 0, j)),
                      pl.BlockSpec((None, d, tn), lambda j, i, meta: (meta[0, i], 0, j)),
                      pl.BlockSpec((cast_rows, w2_flat.shape[1]), cast_blk)],
            out_specs=[pl.BlockSpec((tile, tn), lambda j, i, meta: (i, j)),
                       pl.BlockSpec((cast_rows, w2_flat.shape[1]), cast_blk)]),
        out_shape=[jax.ShapeDtypeStruct((rows, f), BF16),
                   jax.ShapeDtypeStruct(w2_flat.shape, BF16)],
        compiler_params=_params("arbitrary", "arbitrary"),
        name="moe_up",
    )(meta, xs, w1, w3, w2_flat)
    return up, w2b.reshape(w2.shape)


def _moe_down_kernel(meta_ref, a_ref, w_ref, o_ref):
    i = pl.program_id(0)
    k = pl.program_id(1)

    filled = meta_ref[2, i]
    tile = o_ref.shape[0]
    cn = COL_CHUNK
    for rows in _row_options(tile):
        in_range = (filled > rows - tile // ROW_QUARTERS) & (filled <= rows)

        @pl.when(in_range & (k == 0))
        def _(rows=rows):
            a = a_ref[0:rows, :]
            for c in range(0, o_ref.shape[1], cn):
                o_ref[0:rows, c:c + cn] = _dot(a, w_ref[:, c:c + cn])
            if rows < tile:
                o_ref[rows:tile, :] = jnp.zeros((tile - rows, o_ref.shape[1]), o_ref.dtype)

        @pl.when(in_range & (k > 0))
        def _(rows=rows):
            a = a_ref[0:rows, :]
            for c in range(0, o_ref.shape[1], cn):
                o_ref[0:rows, c:c + cn] += _dot(a, w_ref[:, c:c + cn])

    @pl.when((filled == 0) & (k == 0))
    def _():
        o_ref[...] = jnp.zeros_like(o_ref)


def moe_down(meta, up, w2, tile, tk=1792):
    rows, f = up.shape
    d = w2.shape[2]
    assert f % tk == 0 and rows % tile == 0
    n_tiles = rows // tile
    last = lambda i, meta: jnp.maximum(jnp.minimum(i, meta[1, 0] - 1), 0)
    return pl.pallas_call(
        _moe_down_kernel,
        grid_spec=pltpu.PrefetchScalarGridSpec(
            num_scalar_prefetch=1,
            grid=(n_tiles, f // tk),
            in_specs=[pl.BlockSpec((tile, tk), lambda i, k, meta: (last(i, meta), k)),
                      pl.BlockSpec((None, tk, d), lambda i, k, meta: (meta[0, i], k, 0))],
            out_specs=pl.BlockSpec((tile, d), lambda i, k, meta: (i, 0))),
        out_shape=jax.ShapeDtypeStruct((rows, d), F32),
        compiler_params=_params("arbitrary", "arbitrary"),
        name="moe_down",
    )(meta, up, w2)


def _moe_combine_kernel(pos_ref, x_ref, wt_ref, g_ref, y_hbm, o_ref, buf_ref, sem_ref, *, final_norm):
    i = pl.program_id(0)
    n = pl.num_programs(0)
    tc = x_ref.shape[0]
    m = pos_ref.shape[0] // 2

    def start(step, slot):
        for s in range(2):
            _row_gather_start(y_hbm, buf_ref.at[slot, s], pos_ref, s * m + step * tc, tc, sem_ref.at[slot, s])

    @pl.when(i == 0)
    def _():
        start(0, 0)

    @pl.when(i + 1 < n)
    def _():
        start(i + 1, (i + 1) % 2)

    cur = i % 2
    for s in range(2):
        _row_gather_wait(y_hbm, buf_ref.at[cur, s], sem_ref.at[cur, s])
    wt = wt_ref[...]
    out = x_ref[...] + wt[:, 0:1] * buf_ref[cur, 0] + wt[:, 1:2] * buf_ref[cur, 1]
    o_ref[...] = _rms(out, g_ref[...]) if final_norm else out


def moe_combine(pos2, x, wt, y, norm_g=None, tc=256):
    m, d = x.shape
    final_norm = norm_g is not None
    g = (norm_g if final_norm else jnp.ones((d,), F32)).reshape(1, d)
    return pl.pallas_call(
        functools.partial(_moe_combine_kernel, final_norm=final_norm),
        grid_spec=pltpu.PrefetchScalarGridSpec(
            num_scalar_prefetch=1,
            grid=(m // tc,),
            in_specs=[pl.BlockSpec((tc, d), lambda i, pos: (i, 0)),
                      pl.BlockSpec((tc, wt.shape[1]), lambda i, pos: (i, 0)),
                      pl.BlockSpec((1, d), lambda i, pos: (0, 0)),
                      pl.BlockSpec(memory_space=pl.ANY)],
            out_specs=pl.BlockSpec((tc, d), lambda i, pos: (i, 0)),
            scratch_shapes=[pltpu.VMEM((2, 2, tc, d), F32),
                            pltpu.SemaphoreType.DMA((2, 2))]),
        out_shape=jax.ShapeDtypeStruct((m, d), F32),
        compiler_params=_params("arbitrary"),
        name="moe_combine",
    )(pos2, x, wt, g, y)


def moe_layer(x, norm_g, router_w, router_b, w1, w3, w2, out_norm_g=None):
    m, d = x.shape
    n_e = router_w.shape[1]
    tile = MOE_TILE
    n_tiles = 2 * m // tile + n_e
    h, sel, w = norm_router(x, norm_g, router_w, router_b)
    pos, wt, meta = moe_route(sel, w, tile)
    pos2 = pos[:2].reshape(-1)
    xs = moe_gather(pos2, meta, h, n_tiles, tile)
    up, w2_bf16 = moe_up(meta, xs, w1, w3, w2, tile)
    y = moe_down(meta, up, w2_bf16, tile)
    return moe_combine(pos2, x, wt, y, out_norm_g)


def _mixer(x, batch, layer, norm_g, w_in_t, pe_k, pe_v, kw1, kw2, vw1, vw2, ln_g, ln_b, w_s, b_s, w_a, w_b, w_out):
    h = rmsnorm(x, norm_g, BF16)
    qt, kvr, vt, ngt, rest = in_projection(h, w_in_t, layer)
    cmp_kv = compress(kvr, batch, pe_k, pe_v, kw1, kw2, vw1, vw2)
    oct, sel = compressed_and_select(qt, cmp_kv, batch)
    oa = selected_window_attention(qt, kvr, vt, sel, oct, ngt, batch)
    ob = spatial_gating(rest, ln_g, ln_b, w_s, b_s)
    y = merge_branches(oa, ob, rest, w_a, w_b, layer)
    return residual_matmul("out_proj", y, w_out, layer, x)


def kernel(x, norm_mix, w_in, cmp_pe_k, cmp_pe_v, cmp_k_w1, cmp_k_w2, cmp_v_w1, cmp_v_w2, sgu_ln_g, sgu_ln_b, sgu_w, sgu_b, w_branch_a, w_branch_b, w_out, norm_ffn, ffn_w1, ffn_w3, ffn_w2, router_w, router_b, moe_w1, moe_w3, moe_w2, norm_f):
    batch, seq, d = x.shape
    depth = norm_mix.shape[0]
    xf = x.reshape(batch * seq, d)
    w_in_t = jnp.swapaxes(w_in, 1, 2)
    normed = False
    for layer in range(depth):
        xf = _mixer(xf, batch, layer, norm_mix[layer], w_in_t, cmp_pe_k[layer], cmp_pe_v[layer],
                    cmp_k_w1[layer], cmp_k_w2[layer], cmp_v_w1[layer], cmp_v_w2[layer],
                    sgu_ln_g[layer], sgu_ln_b[layer], sgu_w[layer], sgu_b[layer],
                    w_branch_a, w_branch_b, w_out)
        j = layer // 2
        if layer % 2 == 0:
            h = rmsnorm(xf, norm_ffn[layer], BF16)
            up = swiglu_up(h, ffn_w1[j], ffn_w3[j])
            xf = mm_acc("ffn_down", up, ffn_w2[j], xf, 1024, 1024, up.shape[1] // 4)
        else:
            normed = layer == depth - 1
            xf = moe_layer(xf, norm_ffn[layer], router_w[j], router_b[j], moe_w1[j], moe_w3[j], moe_w2[j],
                           norm_f if normed else None)
    if not normed:
        xf = rmsnorm(xf, norm_f, F32)
    return xf.reshape(batch, seq, d)
```

```python
import functools

import numpy as np
import jax
import jax.numpy as jnp
from jax import lax
from jax.experimental import pallas as pl
from jax.experimental.pallas import tpu as pltpu

F32 = jnp.float32
BF16 = jnp.bfloat16

D_MODEL = 2048
N_Q_HEADS = 16
N_KV_HEADS = 4
HEAD_DIM = 64
Q_PER_KV = N_Q_HEADS // N_KV_HEADS
NSA_WIDTH = N_Q_HEADS * HEAD_DIM
KV_WIDTH = N_KV_HEADS * HEAD_DIM
GROUP_WIDTH = Q_PER_KV * HEAD_DIM
CMP_BLOCK = 32
CMP_STRIDE = 16
CMP_HIDDEN = 256
SLC_BLOCK = 64
N_SELECT = 16
WINDOW = 512
FORCED_SCORE = 1e4
SGU_WIDTH = D_MODEL // 2
SGU_GROUPS = 8
SGU_CHUNK = 128
N_EXPERTS = 8
NORM_EPS = 1e-5
NEG = -1e30
LANES = 128
VMEM_LIMIT = 56 * 1024 * 1024

Q_OFF = 0
KV_OFF = NSA_WIDTH
NG_OFF = KV_OFF + 6 * KV_WIDTH
NG_WIDTH = 3 * N_Q_HEADS
REST_OFF = NG_OFF + NG_WIDTH


def _params(*sem):
    return pltpu.CompilerParams(dimension_semantics=sem, vmem_limit_bytes=VMEM_LIMIT)


def _dot(a, b):
    return jnp.dot(a, b, preferred_element_type=F32)


def _dot_nt(a, b):
    return lax.dot_general(a, b, (((1,), (1,)), ((), ())), preferred_element_type=F32)


def _dot_split(a_f32, b_bf16):
    hi = a_f32.astype(BF16)
    lo = (a_f32 - hi.astype(F32)).astype(BF16)
    return _dot(hi, b_bf16) + _dot(lo, b_bf16)


def _sigmoid(x):
    return 1.0 / (1.0 + jnp.exp(-x))


def _gelu(x):
    return x * (0.5 * (1.0 + jnp.tanh(0.7978845608028654 * (x + 0.044715 * (x * x * x)))))


def _rms(x, g):
    return x * lax.rsqrt(jnp.mean(x * x, axis=-1, keepdims=True) + NORM_EPS) * g


def _rmsnorm_kernel(x_ref, g_ref, o_ref):
    o_ref[...] = _rms(x_ref[...], g_ref[...]).astype(o_ref.dtype)


def rmsnorm(x, g, out_dtype, tm=512):
    m, d = x.shape
    return pl.pallas_call(
        _rmsnorm_kernel,
        grid=(m // tm,),
        in_specs=[pl.BlockSpec((tm, d), lambda i: (i, 0)),
                  pl.BlockSpec((1, d), lambda i: (0, 0))],
        out_specs=pl.BlockSpec((tm, d), lambda i: (i, 0)),
        out_shape=jax.ShapeDtypeStruct((m, d), out_dtype),
        compiler_params=_params("parallel"),
        name="rmsnorm",
    )(x, g.reshape(1, d))


def _norm_router_kernel(x_ref, g_ref, rwt_ref, rb_ref, h_ref, sel_ref, w_ref):
    h = _rms(x_ref[...], g_ref[...])
    h_ref[...] = h
    logits = lax.dot_general(rwt_ref[...], h, (((1,), (1,)), ((), ())), preferred_element_type=F32,
                             precision=lax.Precision.HIGHEST) + rb_ref[...]
    n_e = logits.shape[0]
    e = lax.broadcasted_iota(jnp.int32, logits.shape, 0)
    m1 = jnp.max(logits, axis=0, keepdims=True)
    i1 = jnp.min(jnp.where(logits == m1, e, n_e), axis=0, keepdims=True)
    rest = jnp.where(e == i1, -jnp.inf, logits)
    m2 = jnp.max(rest, axis=0, keepdims=True)
    i2 = jnp.min(jnp.where(rest == m2, e, n_e), axis=0, keepdims=True)
    e2 = jnp.exp(m2 - m1)
    w1 = 1.0 / (1.0 + e2)
    w2 = e2 / (1.0 + e2)
    sel_ref[...] = jnp.where((e == i1) | (e == i2), 1.0, 0.0)
    w_ref[...] = jnp.where(e == i1, w1, 0.0) + jnp.where(e == i2, w2, 0.0)


def norm_router(x, g, router_w, router_b, tm=512):
    m, d = x.shape
    n_e = router_w.shape[1]
    return pl.pallas_call(
        _norm_router_kernel,
        grid=(m // tm,),
        in_specs=[pl.BlockSpec((tm, d), lambda i: (i, 0)),
                  pl.BlockSpec((1, d), lambda i: (0, 0)),
                  pl.BlockSpec((n_e, d), lambda i: (0, 0)),
                  pl.BlockSpec((n_e, 1), lambda i: (0, 0))],
        out_specs=[pl.BlockSpec((tm, d), lambda i: (i, 0)),
                   pl.BlockSpec((n_e, tm), lambda i: (0, i)),
                   pl.BlockSpec((n_e, tm), lambda i: (0, i))],
        out_shape=[jax.ShapeDtypeStruct((m, d), F32),
                   jax.ShapeDtypeStruct((n_e, m), F32),
                   jax.ShapeDtypeStruct((n_e, m), F32)],
        compiler_params=_params("parallel"),
        name="norm_router",
    )(x, g.reshape(1, d), router_w.T, router_b.reshape(n_e, 1))


def _mm_full_kernel(*refs, n_a, n_b, n_e, n_o, pairs, inner_axis, epilogue, b_transposed):
    a_refs = refs[:n_a]
    b_refs = refs[n_a:n_a + n_b]
    e_refs = refs[n_a + n_b:n_a + n_b + n_e]
    o_refs = refs[n_a + n_b + n_e:n_a + n_b + n_e + n_o]
    w_refs = refs[n_a + n_b + n_e + n_o:]

    @pl.when(pl.program_id(inner_axis) == 0)
    def _():
        for b_ref, w_ref in zip(b_refs, w_refs):
            w_ref[...] = b_ref[...].astype(BF16)

    dot = _dot_nt if b_transposed else _dot
    ws = [w_ref[0] if len(w_ref.shape) == 3 else w_ref[...] for w_ref in w_refs]
    accs = [dot(a_refs[ia][...], ws[ib]) for ia, ib in pairs]
    epilogue(accs, e_refs, o_refs)


def _block_dim_size(s):
    return s.block_size if isinstance(s, pl.Element) else s


def mm_full(name, grid, a_ops, b_ops, e_ops, outs, pairs, epilogue, b_transposed=False):
    ops = a_ops + b_ops + e_ops
    kernel = functools.partial(
        _mm_full_kernel, n_a=len(a_ops), n_b=len(b_ops), n_e=len(e_ops), n_o=len(outs),
        pairs=pairs, inner_axis=len(grid) - 1, epilogue=epilogue, b_transposed=b_transposed)
    scratch = [pltpu.VMEM(tuple(_block_dim_size(s) for s in blk if s is not None), BF16) for _, blk, _ in b_ops]
    res = pl.pallas_call(
        kernel,
        grid=grid,
        in_specs=[pl.BlockSpec(blk, imap) for _, blk, imap in ops],
        out_specs=[pl.BlockSpec(blk, imap) for _, blk, imap in outs],
        out_shape=[sds for sds, _, _ in outs],
        scratch_shapes=scratch,
        compiler_params=_params(*(("arbitrary",) * len(grid))),
        name=name,
    )(*[arr for arr, _, _ in ops])
    return res


COL_CHUNK = 256


def _mm_acc_kernel(a_ref, b_ref, r_ref, o_ref):
    cols = range(0, o_ref.shape[1], COL_CHUNK)

    @pl.when(pl.program_id(2) == 0)
    def _():
        a = a_ref[...]
        for c in cols:
            sl = slice(c, c + COL_CHUNK)
            o_ref[:, sl] = r_ref[:, sl] + _dot(a, b_ref[:, sl].astype(BF16))

    @pl.when(pl.program_id(2) > 0)
    def _():
        a = a_ref[...]
        for c in cols:
            sl = slice(c, c + COL_CHUNK)
            o_ref[:, sl] += _dot(a, b_ref[:, sl].astype(BF16))


def mm_acc(name, a, b, res, tm, tn, tk):
    m, k = a.shape
    n = b.shape[1]
    assert m % tm == 0 and n % tn == 0 and k % tk == 0 and tn % COL_CHUNK == 0 and tk % LANES == 0
    return pl.pallas_call(
        _mm_acc_kernel,
        grid=(m // tm, n // tn, k // tk),
        in_specs=[pl.BlockSpec((tm, tk), lambda i, j, kk: (i, kk)),
                  pl.BlockSpec((tk, tn), lambda i, j, kk: (kk, j)),
                  pl.BlockSpec((tm, tn), lambda i, j, kk: (i, j))],
        out_specs=pl.BlockSpec((tm, tn), lambda i, j, kk: (i, j)),
        out_shape=jax.ShapeDtypeStruct((m, n), F32),
        compiler_params=_params("parallel", "parallel", "arbitrary"),
        name=name,
    )(a, b, res)


def in_projection(h, w_in_t, layer):
    m, d = h.shape
    tm = 1024
    a_op = [(h, (tm, d), lambda j, i: (i, 0))]

    tq = 512
    q_heads = tq // HEAD_DIM

    def q_epi(accs, e_refs, o_refs):
        for c in range(q_heads):
            o_refs[0][c] = (accs[0][:, c * HEAD_DIM:(c + 1) * HEAD_DIM] * (HEAD_DIM ** -0.5)).T.astype(BF16)

    (qt,) = mm_full(
        "in_proj_q", (NSA_WIDTH // tq, m // tm), a_op,
        [(w_in_t, (None, tq, d), lambda j, i: (layer, j, 0))], [],
        [(jax.ShapeDtypeStruct((N_Q_HEADS, HEAD_DIM, m), BF16), (q_heads, HEAD_DIM, tm), lambda j, i: (j, 0, i))],
        [(0, 0)], q_epi, b_transposed=True)

    sec0 = KV_OFF // KV_WIDTH

    def rows_epi(accs, e_refs, o_refs):
        for c in range(N_KV_HEADS):
            o_refs[0][c] = accs[0][:, c * HEAD_DIM:(c + 1) * HEAD_DIM].astype(BF16)

    (kvr,) = mm_full(
        "in_proj_kv_rows", (4, m // tm), a_op,
        [(w_in_t, (None, KV_WIDTH, d), lambda j, i: (layer, sec0 + j + j // 3, 0))], [],
        [(jax.ShapeDtypeStruct((4 * N_KV_HEADS, m, HEAD_DIM), BF16),
          (N_KV_HEADS, tm, HEAD_DIM), lambda j, i: (j, i, 0))],
        [(0, 0)], rows_epi, b_transposed=True)

    def cols_epi(accs, e_refs, o_refs):
        for c in range(N_KV_HEADS):
            o_refs[0][c] = accs[0][:, c * HEAD_DIM:(c + 1) * HEAD_DIM].T.astype(BF16)

    (vt,) = mm_full(
        "in_proj_v_cols", (2, m // tm), a_op,
        [(w_in_t, (None, KV_WIDTH, d), lambda j, i: (layer, sec0 + 3 + 2 * j, 0))], [],
        [(jax.ShapeDtypeStruct((2 * N_KV_HEADS, HEAD_DIM, m), BF16),
          (N_KV_HEADS, HEAD_DIM, tm), lambda j, i: (j, 0, i))],
        [(0, 0)], cols_epi, b_transposed=True)

    def f32_epi(accs, e_refs, o_refs):
        o_refs[0][...] = accs[0]

    def f32_t_epi(accs, e_refs, o_refs):
        o_refs[0][...] = accs[0].T

    assert NG_OFF % LANES == 0
    (ngt,) = mm_full(
        "in_proj_ng", (1, m // tm), a_op,
        [(w_in_t, (None, LANES, d), lambda j, i: (layer, NG_OFF // LANES, 0))], [],
        [(jax.ShapeDtypeStruct((LANES, m), F32), (LANES, tm), lambda j, i: (0, i))],
        [(0, 0)], f32_t_epi, b_transposed=True)

    n_rest = w_in_t.shape[1] - REST_OFF
    tr = 1024
    assert n_rest % tr == 0 and REST_OFF % 8 == 0
    (rest,) = mm_full(
        "in_proj_rest", (n_rest // tr, m // tm), a_op,
        [(w_in_t, (pl.Element(1), pl.Element(tr), pl.Element(d)), lambda j, i: (layer, pl.multiple_of(REST_OFF + j * tr, 8), 0))], [],
        [(jax.ShapeDtypeStruct((m, n_rest), F32), (tm, tr), lambda j, i: (i, j))],
        [(0, 0)], f32_epi, b_transposed=True)
    return qt, kvr, vt, ngt, rest


def _cmp_kernel(a_ref, w1_ref, w2_ref, pe_ref, o_ref):
    a = a_ref[...]
    rows = a.shape[0]
    half = a.shape[1]
    w1 = w1_ref[...].astype(BF16)
    p0 = _dot(a, w1[:half])
    p1 = _dot(a, w1[half:])
    bias = _dot(pe_ref[...].astype(BF16), w1)[0:1]
    hid = p0 + pltpu.roll(p1, rows - 1, 0) + bias
    o_ref[...] = _dot(_gelu(hid).astype(BF16), w2_ref[...].astype(BF16))


def compress(kv, batch, pe_k, pe_v, kw1, kw2, vw1, vw2):
    m = kv.shape[1]
    n_chunks = m // batch // CMP_STRIDE
    rows = batch * n_chunks
    feat = CMP_STRIDE * HEAD_DIM
    a = kv[:2 * N_KV_HEADS].reshape(2, N_KV_HEADS, rows, feat)
    w1 = jnp.stack([kw1, vw1])
    w2 = jnp.stack([kw2, vw2])
    pe = jnp.stack([pe_k, pe_v]).reshape(2, 1, CMP_BLOCK * HEAD_DIM)
    pe = jnp.broadcast_to(pe, (2, 8, CMP_BLOCK * HEAD_DIM))
    return pl.pallas_call(
        _cmp_kernel,
        grid=(2, N_KV_HEADS),
        in_specs=[pl.BlockSpec((None, None, rows, feat), lambda s, g: (s, g, 0, 0)),
                  pl.BlockSpec((None, 2 * feat, CMP_HIDDEN), lambda s, g: (s, 0, 0)),
                  pl.BlockSpec((None, CMP_HIDDEN, HEAD_DIM), lambda s, g: (s, 0, 0)),
                  pl.BlockSpec((None, 8, 2 * feat), lambda s, g: (s, 0, 0))],
        out_specs=pl.BlockSpec((None, None, rows, HEAD_DIM), lambda s, g: (s, g, 0, 0)),
        out_shape=jax.ShapeDtypeStruct((2, N_KV_HEADS, rows, HEAD_DIM), F32),
        compiler_params=_params("parallel", "parallel"),
        name="nsa_compress",
    )(a, w1, w2, pe)


def _alibi_slopes():
    return np.array([2.0 ** (-8.0 * (h + 1) / N_Q_HEADS) for h in range(N_Q_HEADS)], dtype=np.float32)


def _overlap_t(n_cmp_pad, n_slc):
    cs = np.arange(n_cmp_pad)[None, :] * CMP_STRIDE
    ss = np.arange(n_slc)[:, None] * SLC_BLOCK
    ov = np.clip(np.minimum(cs + CMP_BLOCK, ss + SLC_BLOCK) - np.maximum(cs, ss), 0, None)
    return (ov / CMP_STRIDE).astype(np.float32)


def _sel_kernel(slopes_ref, q_ref, kc_ref, vc_ref, ovt_ref, oc_ref, sel_ref, *, n_cmp):
    g = pl.program_id(1)
    i = pl.program_id(2)
    tq = q_ref.shape[2]
    ncp = kc_ref.shape[0]
    n_slc = ovt_ref.shape[0]
    kc = kc_ref[...].astype(BF16)
    vct = vc_ref[...].T.astype(BF16)

    t = i * tq + lax.broadcasted_iota(jnp.int32, (ncp, tq), 1)
    c = lax.broadcasted_iota(jnp.int32, (ncp, tq), 0)
    dist = t - (c * CMP_STRIDE + (CMP_BLOCK - 1))
    mask = (dist >= 0) & (c < n_cmp)
    distf = dist.astype(F32)

    scores = [_dot(kc, q_ref[j]) for j in range(Q_PER_KV)]
    probs = []
    for j in range(Q_PER_KV):
        s = scores[j] - slopes_ref[g * Q_PER_KV + j] * distf
        s = jnp.where(mask, s, NEG)
        mx = jnp.max(s, axis=0, keepdims=True)
        e = jnp.where(mask, jnp.exp(s - mx), 0.0)
        probs.append(e / jnp.maximum(jnp.sum(e, axis=0, keepdims=True), 1e-30))
    for j in range(Q_PER_KV):
        oc_ref[j] = _dot(vct, probs[j].astype(BF16))
    p_sum = functools.reduce(lambda a, b: a + b, probs)

    ovt = ovt_ref[...]
    hi = p_sum.astype(BF16)
    lo = (p_sum - hi.astype(F32)).astype(BF16)
    imp = _dot(ovt, hi) + _dot(ovt, lo)
    tt = i * tq + lax.broadcasted_iota(jnp.int32, (n_slc, tq), 1)
    blk = lax.broadcasted_iota(jnp.int32, (n_slc, tq), 0)
    cur = tt // SLC_BLOCK
    valid = blk * SLC_BLOCK <= tt
    forced = (blk == 0) | (blk == cur) | (blk == cur - 1)
    score = jnp.where(valid, imp, -1.0)
    score = jnp.where(forced, FORCED_SCORE, score)
    rank = jnp.zeros((n_slc, tq), F32)
    for mrow in range(n_slc):
        other = jnp.broadcast_to(score[mrow:mrow + 1, :], (n_slc, tq))
        beats = (other > score) | ((other == score) & (blk > mrow))
        rank = rank + jnp.where(beats, 1.0, 0.0)
    sel_ref[...] = jnp.where(rank < float(min(N_SELECT, n_slc)), 1.0, 0.0)


def compressed_and_select(qt, cmp_kv, batch, tq=256):
    m = qt.shape[2]
    seq = m // batch
    nq = seq // tq
    ncp = cmp_kv.shape[2] // batch
    n_slc = seq // SLC_BLOCK
    ovt = jnp.asarray(_overlap_t(ncp, n_slc), BF16)
    slopes = jnp.asarray(_alibi_slopes())
    kernel = functools.partial(_sel_kernel, n_cmp=ncp - 1)
    return pl.pallas_call(
        kernel,
        grid=(batch, N_KV_HEADS, nq),
        in_specs=[pl.BlockSpec(memory_space=pltpu.SMEM),
                  pl.BlockSpec((Q_PER_KV, HEAD_DIM, tq), lambda b, g, i: (g, 0, b * nq + i)),
                  pl.BlockSpec((None, None, ncp, HEAD_DIM), lambda b, g, i: (0, g, b, 0)),
                  pl.BlockSpec((None, None, ncp, HEAD_DIM), lambda b, g, i: (1, g, b, 0)),
                  pl.BlockSpec((n_slc, ncp), lambda b, g, i: (0, 0))],
        out_specs=[pl.BlockSpec((Q_PER_KV, HEAD_DIM, tq), lambda b, g, i: (g, 0, b * nq + i)),
                   pl.BlockSpec((None, None, n_slc, tq), lambda b, g, i: (b, g, 0, i))],
        out_shape=[jax.ShapeDtypeStruct((N_Q_HEADS, HEAD_DIM, m), F32),
                   jax.ShapeDtypeStruct((batch, N_KV_HEADS, n_slc, seq), F32)],
        compiler_params=_params("parallel", "parallel", "parallel"),
        name="nsa_compressed_select",
    )(slopes, qt, cmp_kv, cmp_kv, ovt)


AUX_SLOPE = HEAD_DIM
AUX_SEL = HEAD_DIM + 16
ATTN_K = 2 * HEAD_DIM
STRIP = 32


def _attn_kernel(slopes_ref, q_ref, ks_ref, vs_ref, kw_ref, vw_ref, auxs_ref, auxw_ref, sel_ref, oc_ref,
                 ng_ref, *rest, round_weights):
    if round_weights:
        wsrc_ref, o_ref, wdst_ref = rest[:3]
        scratch = rest[3:]

        def side_work():
            wdst_ref[...] = wsrc_ref[...].astype(BF16)
    else:
        o_ref = rest[0]
        scratch = rest[1:]
        side_work = lambda: None
    _attn_body(slopes_ref, q_ref, ks_ref, vs_ref, kw_ref, vw_ref, auxs_ref, auxw_ref, sel_ref, oc_ref, ng_ref,
               o_ref, *scratch, side_work=side_work)


def _attn_body(slopes_ref, q_ref, ks_ref, vs_ref, kw_ref, vw_ref, auxs_ref, auxw_ref, sel_ref, oc_ref,
               ng_ref, o_ref, ka_s_ref, ka_w_ref, qa_ref, s_ref, p_ref, mask_ref, m_ref, l_ref, acc_ref,
               os_ref, sig_ref, *, side_work):
    g = pl.program_id(1)
    i = pl.program_id(2)
    tq = q_ref.shape[2]
    tk = tq
    q0 = i * tq
    n_slc = sel_ref.shape[0]

    @pl.when(i == 0)
    def _():
        ka_s_ref[:, 0:HEAD_DIM] = ks_ref[...]
        ka_s_ref[:, HEAD_DIM:ATTN_K] = auxs_ref[...]
        ka_w_ref[:, 0:HEAD_DIM] = kw_ref[...]
        ka_w_ref[:, HEAD_DIM:ATTN_K] = auxw_ref[...]
        kr = lax.broadcasted_iota(jnp.int32, (tk, tq), 0)
        qc = lax.broadcasted_iota(jnp.int32, (tk, tq), 1)
        mask_ref[0] = jnp.where(kr <= qc, 0.0, NEG)
        mask_ref[1] = jnp.where(kr > qc, 0.0, NEG)

    sig_ref[...] = _sigmoid(ng_ref[...])
    side_work()
    sel_neg =(sel_ref[...] - 1.0) * (-NEG)
    row16 = lax.broadcasted_iota(jnp.int32, (16, tq), 0)
    pad = jnp.zeros((ATTN_K - AUX_SEL - n_slc, tq), F32)
    for j in range(Q_PER_KV):
        sl = jnp.full((16, tq), slopes_ref[g * Q_PER_KV + j], F32)
        hi = sl.astype(BF16).astype(F32)
        mid = (sl - hi).astype(BF16).astype(F32)
        lo = (sl - hi - mid).astype(BF16).astype(F32)
        pieces = jnp.where(row16 == 0, hi, jnp.where(row16 == 1, mid, jnp.where(row16 == 2, lo, 0.0)))
        qa = jnp.concatenate([q_ref[j].astype(F32), pieces, sel_neg, pad], axis=0)
        qa_ref[j] = qa.astype(BF16)

    def scores(ka_ref, k0, buf):
        ka = ka_ref[pl.ds(k0, tk), :]
        for j in range(Q_PER_KV):
            s_ref[buf, j] = _dot(ka, qa_ref[j])

    def softmax_pv(v_ref, k0, buf, mask_idx, first):
        vc = v_ref[:, pl.ds(k0, tk)]
        for j in range(Q_PER_KV):
            shift = slopes_ref[g * Q_PER_KV + j] * k0.astype(F32)
            mx = None
            for r in range(0, tk, STRIP):
                x = s_ref[buf, j, r:r + STRIP, :]
                if mask_idx is not None:
                    x = x + mask_ref[mask_idx, r:r + STRIP, :]
                    s_ref[buf, j, r:r + STRIP, :] = x
                mx = x if mx is None else jnp.maximum(mx, x)
            m_cur = jnp.max(mx, axis=0, keepdims=True) + shift
            if first:
                m_new = m_cur
            else:
                m_old = m_ref[j]
                m_new = jnp.maximum(m_old, m_cur)
                alpha = jnp.exp(m_old - m_new)
            m_ref[j] = m_new
            m_loc = m_new - shift
            ls = None
            for r in range(0, tk, STRIP):
                p = jnp.exp(s_ref[buf, j, r:r + STRIP, :] - m_loc)
                ls = p if ls is None else ls + p
                p_ref[j, r:r + STRIP, :] = p.astype(BF16)
            l_cur = jnp.sum(ls, axis=0, keepdims=True)
            pv = _dot(vc, p_ref[j])
            if first:
                l_ref[j] = l_cur
                acc_ref[j] = pv
            else:
                l_ref[j] = alpha * l_ref[j] + l_cur
                acc_ref[j] = alpha * acc_ref[j] + pv

    k_diag = pl.multiple_of(q0, tk)
    last = jnp.maximum(i - 1, 0)
    k_of = lambda c: pl.multiple_of(jnp.minimum(c, last) * tk, tk)
    scores(ka_s_ref, k_diag, 1)
    scores(ka_s_ref, k_of(0), 0)
    softmax_pv(vs_ref, k_diag, 1, 0, True)

    def slc_pair(cc, carry):
        c0 = 2 * cc
        scores(ka_s_ref, k_of(c0 + 1), 1)
        softmax_pv(vs_ref, k_of(c0), 0, None, False)

        @pl.when(c0 + 1 < i)
        def _():
            scores(ka_s_ref, k_of(c0 + 2), 0)
            softmax_pv(vs_ref, k_of(c0 + 1), 1, None, False)
        return carry

    lax.fori_loop(0, (i + 1) // 2, slc_pair, 0)
    for j in range(Q_PER_KV):
        os_ref[j] = acc_ref[j] / l_ref[j]

    n_back = WINDOW // tk
    back_k0 = lambda back: pl.multiple_of(jnp.maximum(q0 - back * tk, 0), tk)
    scores(ka_w_ref, k_diag, 0)
    scores(ka_w_ref, back_k0(1), 1)
    softmax_pv(vw_ref, k_diag, 0, 0, True)
    for back in range(1, n_back + 1):
        @pl.when(i >= back)
        def _(back=back):
            if back < n_back:
                scores(ka_w_ref, back_k0(back + 1), (back + 1) % 2)
            softmax_pv(vw_ref, back_k0(back), back % 2, 1 if back == n_back else None, False)

    for j in range(Q_PER_KV):
        base = (g * Q_PER_KV + j) * 3
        out_t = (sig_ref[pl.ds(base, 1), :] * oc_ref[j]
                 + sig_ref[pl.ds(base + 1, 1), :] * os_ref[j]
                 + sig_ref[pl.ds(base + 2, 1), :] * (acc_ref[j] / l_ref[j]))
        o_ref[:, j * HEAD_DIM:(j + 1) * HEAD_DIM] = out_t.T.astype(o_ref.dtype)


def _key_aux(seq, tk, with_blocks):
    aux = np.zeros((seq, ATTN_K - HEAD_DIM), np.float32)
    pos = np.arange(seq)
    aux[:, AUX_SLOPE - HEAD_DIM:AUX_SLOPE - HEAD_DIM + 3] = (pos % tk)[:, None]
    if with_blocks:
        aux[pos, AUX_SEL - HEAD_DIM + pos // SLC_BLOCK] = 1.0
    return aux


def selected_window_attention(qt, kvr, vt, sel, oct, ngt, batch, round_src=None, tq=256):
    m = qt.shape[2]
    seq = m // batch
    nq = seq // tq
    n_slc = seq // SLC_BLOCK
    assert tq <= 256 and AUX_SEL + n_slc <= ATTN_K
    aux_s = jnp.asarray(_key_aux(seq, tq, True), BF16)
    aux_w = jnp.asarray(_key_aux(seq, tq, False), BF16)
    slopes = jnp.asarray(_alibi_slopes())
    n_g = N_KV_HEADS

    def k_spec(section):
        return pl.BlockSpec((None, seq, HEAD_DIM), lambda b, g, i: (section * n_g + g, b, 0))

    def v_spec(section):
        return pl.BlockSpec((None, HEAD_DIM, seq), lambda b, g, i: (section * n_g + g, 0, b))

    head_blk = pl.BlockSpec((Q_PER_KV, HEAD_DIM, tq), lambda b, g, i: (g, 0, b * nq + i))
    aux_blk = pl.BlockSpec((seq, ATTN_K - HEAD_DIM), lambda b, g, i: (0, 0))
    in_specs = [pl.BlockSpec(memory_space=pltpu.SMEM),
                head_blk,
                k_spec(2), v_spec(0), k_spec(3), v_spec(1),
                aux_blk, aux_blk,
                pl.BlockSpec((None, None, n_slc, tq), lambda b, g, i: (b, g, 0, i)),
                head_blk,
                pl.BlockSpec((LANES, tq), lambda b, g, i: (0, b * nq + i))]
    out_specs = [pl.BlockSpec((tq, GROUP_WIDTH), lambda b, g, i: (b * nq + i, g))]
    out_shape = [jax.ShapeDtypeStruct((m, NSA_WIDTH), BF16)]
    operands = [slopes, qt, kvr, vt, kvr, vt, aux_s, aux_w, sel, oct, ngt]
    if round_src is not None:
        steps = batch * N_KV_HEADS * nq
        rows, cols = round_src.shape
        assert rows % steps == 0 and (rows // steps) % BF16_SUBLANES == 0
        side_blk = pl.BlockSpec((rows // steps, cols), lambda b, g, i: ((b * N_KV_HEADS + g) * nq + i, 0))
        in_specs.append(side_blk)
        out_specs.append(side_blk)
        out_shape.append(jax.ShapeDtypeStruct((rows, cols), BF16))
        operands.append(round_src)
    res = pl.pallas_call(
        functools.partial(_attn_kernel, round_weights=round_src is not None),
        grid=(batch, N_KV_HEADS, nq),
        in_specs=in_specs,
        out_specs=out_specs,
        out_shape=out_shape,
        scratch_shapes=[pltpu.VMEM((seq, ATTN_K), BF16),
                        pltpu.VMEM((seq, ATTN_K), BF16),
                        pltpu.VMEM((Q_PER_KV, ATTN_K, tq), BF16),
                        pltpu.VMEM((2, Q_PER_KV, tq, tq), F32),
                        pltpu.VMEM((Q_PER_KV, tq, tq), BF16),
                        pltpu.VMEM((2, tq, tq), F32),
                        pltpu.VMEM((Q_PER_KV, 1, tq), F32),
                        pltpu.VMEM((Q_PER_KV, 1, tq), F32),
                        pltpu.VMEM((Q_PER_KV, HEAD_DIM, tq), F32),
                        pltpu.VMEM((Q_PER_KV, HEAD_DIM, tq), F32),
                        pltpu.VMEM((LANES, tq), F32)],
        compiler_params=_params("arbitrary", "arbitrary", "arbitrary"),
        name="nsa_selected_window",
    )(*operands)
    return (res[0], res[1]) if round_src is not None else (res[0], None)


def _sgu_kernel(u_ref, v_ref, lg_ref, lb_ref, ws_ref, bs_ref, o_ref):
    rows = u_ref.shape[0]
    v = _gelu(v_ref[...])
    mu = jnp.mean(v, axis=-1, keepdims=True)
    var = jnp.mean(jnp.square(v - mu), axis=-1, keepdims=True)
    vl = ((v - mu) * lax.rsqrt(var + NORM_EPS) * lg_ref[...] + lb_ref[...]).astype(BF16)
    r = lax.broadcasted_iota(jnp.int32, (SGU_CHUNK, SGU_CHUNK), 0)
    c = lax.broadcasted_iota(jnp.int32, (SGU_CHUNK, SGU_CHUNK), 1)
    gd = SGU_WIDTH // SGU_GROUPS
    for grp in range(SGU_GROUPS):
        w = jnp.where(c <= r, ws_ref[grp], 0.0).astype(BF16)
        lanes = slice(grp * gd, (grp + 1) * gd)
        for n in range(rows // SGU_CHUNK):
            rs = slice(n * SGU_CHUNK, (n + 1) * SGU_CHUNK)
            vm = _dot(w, vl[rs, lanes]) + bs_ref[:, lanes]
            o_ref[rs, lanes] = (_gelu(u_ref[rs, lanes]) * vm).astype(o_ref.dtype)


def spatial_gating(rest, ln_g, ln_b, w_s, b_s, tm=512):
    m = rest.shape[0]
    gd = SGU_WIDTH // SGU_GROUPS
    bias = jnp.repeat(b_s.T, gd, axis=1)
    return pl.pallas_call(
        _sgu_kernel,
        grid=(m // tm,),
        in_specs=[pl.BlockSpec((tm, SGU_WIDTH), lambda i: (i, 0)),
                  pl.BlockSpec((tm, SGU_WIDTH), lambda i: (i, 1)),
                  pl.BlockSpec((1, SGU_WIDTH), lambda i: (0, 0)),
                  pl.BlockSpec((1, SGU_WIDTH), lambda i: (0, 0)),
                  pl.BlockSpec((SGU_GROUPS, SGU_CHUNK, SGU_CHUNK), lambda i: (0, 0, 0)),
                  pl.BlockSpec((SGU_CHUNK, SGU_WIDTH), lambda i: (0, 0))],
        out_specs=pl.BlockSpec((tm, SGU_WIDTH), lambda i: (i, 0)),
        out_shape=jax.ShapeDtypeStruct((m, SGU_WIDTH), BF16),
        compiler_params=_params("parallel"),
        name="spatial_gating",
    )(rest, rest, ln_g.reshape(1, -1), ln_b.reshape(1, -1), w_s, bias)


def merge_branches(oa, ob, rest, w_a, w_b, layer, tm=512, tn=1024):
    m = oa.shape[0]
    d = w_a.shape[2]
    ga_blk = 2 * SGU_WIDTH // tn
    gb_blk = (2 * SGU_WIDTH + d) // tn

    def epi(accs, e_refs, o_refs):
        y = _sigmoid(e_refs[0][...]) * accs[0] + _sigmoid(e_refs[1][...]) * accs[1]
        o_refs[0][...] = y.astype(BF16)

    (y,) = mm_full(
        "merge_branches", (d // tn, m // tm),
        [(oa, (tm, oa.shape[1]), lambda j, i: (i, 0)), (ob, (tm, ob.shape[1]), lambda j, i: (i, 0))],
        [(w_a, (None, w_a.shape[1], tn), lambda j, i: (layer, 0, j)),
         (w_b, (None, w_b.shape[1], tn), lambda j, i: (layer, 0, j))],
        [(rest, (tm, tn), lambda j, i: (i, ga_blk + j)), (rest, (tm, tn), lambda j, i: (i, gb_blk + j))],
        [(jax.ShapeDtypeStruct((m, d), BF16), (tm, tn), lambda j, i: (i, j))],
        [(0, 0), (1, 1)], epi)
    return y


def residual_matmul(name, a, w, layer, x, tm=512, tn=1024):
    m, k = a.shape
    n = w.shape[2]

    def epi(accs, e_refs, o_refs):
        o_refs[0][...] = e_refs[0][...] + accs[0]

    (out,) = mm_full(
        name, (n // tn, m // tm),
        [(a, (tm, k), lambda j, i: (i, 0))],
        [(w, (None, k, tn), lambda j, i: (layer, 0, j))],
        [(x, (tm, tn), lambda j, i: (i, j))],
        [(jax.ShapeDtypeStruct((m, n), F32), (tm, tn), lambda j, i: (i, j))],
        [(0, 0)], epi)
    return out


def swiglu_up(h, w1, w3, tm=1024, tn=512):
    m, d = h.shape
    f = w1.shape[1]

    def epi(accs, e_refs, o_refs):
        o_refs[0][...] = (accs[0] * _sigmoid(accs[0]) * accs[1]).astype(BF16)

    (out,) = mm_full(
        "swiglu_up", (f // tn, m // tm),
        [(h, (tm, d), lambda j, i: (i, 0))],
        [(w1, (d, tn), lambda j, i: (0, j)), (w3, (d, tn), lambda j, i: (0, j))],
        [],
        [(jax.ShapeDtypeStruct((m, f), BF16), (tm, tn), lambda j, i: (i, j))],
        [(0, 0), (0, 1)], epi)
    return out


MOE_TILE = 512
ROW_QUARTERS = 4


def _row_options(tile):
    step = tile // ROW_QUARTERS
    return tuple(range(step, tile + 1, step))


def _route_kernel(sel_ref, w_ref, tri_ref, pos_ref, wt_ref, meta_ref, cum_ref, *, tile):
    n_e, m = sel_ref.shape
    ck = tri_ref.shape[0]
    carry = jnp.zeros((n_e, 1), F32)
    for c in range(m // ck):
        sl = slice(c * ck, (c + 1) * ck)
        cs = _dot(sel_ref[:, sl].astype(BF16), tri_ref[...]) + carry
        cum_ref[:, sl] = cs
        carry = cs[:, ck - 1:ck]
    padded = jnp.ceil(carry / tile) * tile
    sub = lax.broadcasted_iota(jnp.int32, (n_e, 1), 0)
    start = jnp.zeros((n_e, 1), F32)
    run = jnp.zeros((1, 1), F32)
    for ex in range(n_e):
        start = jnp.where(sub == ex, run, start)
        run = run + padded[ex:ex + 1, :]
    sel = sel_ref[...] > 0.5
    pos = start + cum_ref[...] - 1.0
    eidx = lax.broadcasted_iota(jnp.int32, (n_e, m), 0)
    e_lo = jnp.min(jnp.where(sel, eidx, n_e), axis=0, keepdims=True)
    e_hi = jnp.max(jnp.where(sel, eidx, -1), axis=0, keepdims=True)
    is_lo = eidx == e_lo
    is_hi = eidx == e_hi
    w = w_ref[...]
    pos_a = jnp.sum(jnp.where(is_lo, pos, 0.0), axis=0, keepdims=True)
    pos_b = jnp.sum(jnp.where(is_hi, pos, 0.0), axis=0, keepdims=True)
    w_a = jnp.sum(jnp.where(is_lo, w, 0.0), axis=0, keepdims=True)
    w_b = jnp.sum(jnp.where(is_hi, w, 0.0), axis=0, keepdims=True)
    pos_ref[...] = jnp.where(eidx == 0, pos_a, jnp.where(eidx == 1, pos_b, 0.0)).astype(jnp.int32)
    wt_ref[...] = jnp.where(eidx == 0, w_a, jnp.where(eidx == 1, w_b, 0.0)).T
    tile_lo = lax.broadcasted_iota(jnp.int32, (n_e, LANES), 1).astype(F32) * tile
    t_exp = jnp.sum(jnp.where(start + padded <= tile_lo, 1.0, 0.0), axis=0, keepdims=True)
    t_exp = jnp.minimum(t_exp, n_e - 1.0)
    r8 = lax.broadcasted_iota(jnp.int32, (n_e, LANES), 0)
    last_row = jnp.sum(jnp.where(r8.astype(F32) == t_exp, start + carry, 0.0), axis=0, keepdims=True)
    filled = jnp.clip(last_row - tile_lo[0:1], 0.0, float(tile))
    meta_ref[...] = jnp.where(r8 == 0, t_exp, jnp.where(r8 == 1, run / tile,
                                                       jnp.where(r8 == 2, filled, 0.0))).astype(jnp.int32)


def moe_route(sel, w, tile):
    n_e, m = sel.shape
    ck = 256
    tri = jnp.asarray(np.triu(np.ones((ck, ck), np.float32)), BF16)
    full = lambda shape: pl.BlockSpec(shape, lambda: tuple(0 for _ in shape))
    return pl.pallas_call(
        functools.partial(_route_kernel, tile=tile),
        in_specs=[full((n_e, m)), full((n_e, m)), full((ck, ck))],
        out_specs=[full((n_e, m)), full((m, n_e)), full((n_e, LANES))],
        out_shape=[jax.ShapeDtypeStruct((n_e, m), jnp.int32),
                   jax.ShapeDtypeStruct((m, n_e), F32),
                   jax.ShapeDtypeStruct((n_e, LANES), jnp.int32)],
        scratch_shapes=[pltpu.VMEM((n_e, m), F32)],
        compiler_params=pltpu.CompilerParams(vmem_limit_bytes=VMEM_LIMIT),
        name="moe_route",
    )(sel, w, tri)


def _row_gather_start(src_hbm, dst, rows_ref, base, n_rows, sem, unroll=8):
    def body(r, carry):
        src_row = rows_ref[base + r]
        pltpu.make_async_copy(src_hbm.at[pl.ds(src_row, 1)], dst.at[pl.ds(r, 1)], sem).start()
        return carry
    lax.fori_loop(0, n_rows, body, 0, unroll=unroll)


def _row_gather_wait(src_hbm, dst, sem):
    pltpu.make_async_copy(src_hbm.at[pl.ds(0, dst.shape[0])], dst, sem).wait()


def _moe_gather_kernel(pos_ref, meta_ref, h_hbm, zeros_hbm, o_ref, tok_ref, buf_ref, sem_ref):
    i = pl.program_id(0)
    n = meta_ref[1, 0]
    tile = o_ref.shape[0]
    m = pos_ref.shape[0] // 2

    @pl.when(i == 0)
    def _():
        clear = pltpu.make_async_copy(zeros_hbm, tok_ref, sem_ref.at[2])
        clear.start()
        clear.wait()

        def fill(t, carry):
            tok_ref[pos_ref[t]] = t
            tok_ref[pos_ref[m + t]] = t
            return carry
        lax.fori_loop(0, m, fill, 0, unroll=8)
        _row_gather_start(h_hbm, buf_ref.at[0], tok_ref, 0, tile, sem_ref.at[0])

    @pl.when(i + 1 < n)
    def _():
        nxt = (i + 1) % 2
        _row_gather_start(h_hbm, buf_ref.at[nxt], tok_ref, (i + 1) * tile, tile, sem_ref.at[nxt])

    @pl.when(i < n)
    def _():
        cur = i % 2
        _row_gather_wait(h_hbm, buf_ref.at[cur], sem_ref.at[cur])
        o_ref[...] = buf_ref[cur].astype(o_ref.dtype)

    @pl.when(i >= n)
    def _():
        o_ref[...] = jnp.zeros_like(o_ref)


def moe_gather(pos2, meta, h, n_tiles, tile):
    m, d = h.shape
    rows = n_tiles * tile
    return pl.pallas_call(
        _moe_gather_kernel,
        grid_spec=pltpu.PrefetchScalarGridSpec(
            num_scalar_prefetch=2,
            grid=(n_tiles,),
            in_specs=[pl.BlockSpec(memory_space=pl.ANY), pl.BlockSpec(memory_space=pl.ANY)],
            out_specs=pl.BlockSpec((tile, d), lambda i, pos, meta: (i, 0)),
            scratch_shapes=[pltpu.SMEM((rows,), jnp.int32),
                            pltpu.VMEM((2, tile, d), F32),
                            pltpu.SemaphoreType.DMA((3,))]),
        out_shape=jax.ShapeDtypeStruct((rows, d), BF16),
        compiler_params=_params("arbitrary"),
        name="moe_gather",
    )(pos2, meta, h, jnp.zeros((rows,), jnp.int32))


def _moe_up_kernel(meta_ref, x_ref, w1_ref, w3_ref, w2_ref, o_ref, w2b_ref):
    i = pl.program_id(1)

    def round_w2_block():
        w2b_ref[...] = w2_ref[...].astype(BF16)

    filled = meta_ref[2, i]
    tile = o_ref.shape[0]
    for rows in _row_options(tile):
        @pl.when((filled > rows - tile // ROW_QUARTERS) & (filled <= rows))
        def _(rows=rows):
            x = x_ref[0:rows, :]
            for c in range(0, o_ref.shape[1], COL_CHUNK):
                sl = slice(c, c + COL_CHUNK)
                a1 = _dot(x, w1_ref[:, sl])
                a3 = _dot(x, w3_ref[:, sl])
                o_ref[0:rows, sl] = (a1 * _sigmoid(a1) * a3).astype(o_ref.dtype)
            if rows < tile:
                o_ref[rows:tile, :] = jnp.zeros((tile - rows, o_ref.shape[1]), o_ref.dtype)
            round_w2_block()

    @pl.when(filled == 0)
    def _():
        o_ref[...] = jnp.zeros_like(o_ref)
        round_w2_block()


BF16_SUBLANES = 16


def _cast_rows_per_step(total_rows, steps):
    for rows in range(BF16_SUBLANES, total_rows + 1, BF16_SUBLANES):
        if total_rows % rows == 0 and total_rows // rows <= steps:
            return rows
    raise ValueError("weights cannot be split over the grid steps")


def moe_up(meta, xs, w1, w3, w2, tile, tn=1024):
    rows, d = xs.shape
    n_e, _, f = w1.shape
    assert f % tn == 0 and rows % tile == 0 and w1.dtype == BF16 and w3.dtype == BF16
    n_tiles = rows // tile
    w2_flat = w2.reshape(n_e * f, w2.shape[2])
    cast_rows = _cast_rows_per_step(w2_flat.shape[0], (f // tn) * n_tiles)
    n_cast = w2_flat.shape[0] // cast_rows
    last = lambda i, meta: jnp.maximum(jnp.minimum(i, meta[1, 0] - 1), 0)
    cast_blk = lambda j, i, meta: (jnp.minimum(j * n_tiles + i, n_cast - 1), 0)
    up, w2b = pl.pallas_call(
        _moe_up_kernel,
        grid_spec=pltpu.PrefetchScalarGridSpec(
            num_scalar_prefetch=1,
            grid=(f // tn, n_tiles),
            in_specs=[pl.BlockSpec((tile, d), lambda j, i, meta: (last(i, meta), 0)),
                      pl.BlockSpec((None, d, tn), lambda j, i, meta: (meta[0, i], 0, j)),
                      pl.BlockSpec((None, d, tn), lambda j, i, meta: (meta[0, i], 0, j)),
                      pl.BlockSpec((cast_rows, w2_flat.shape[1]), cast_blk)],
            out_specs=[pl.BlockSpec((tile, tn), lambda j, i, meta: (i, j)),
                       pl.BlockSpec((cast_rows, w2_flat.shape[1]), cast_blk)]),
        out_shape=[jax.ShapeDtypeStruct((rows, f), BF16),
                   jax.ShapeDtypeStruct(w2_flat.shape, BF16)],
        compiler_params=_params("arbitrary", "arbitrary"),
        name="moe_up",
    )(meta, xs, w1, w3, w2_flat)
    return up, w2b.reshape(w2.shape)


def _moe_down_kernel(meta_ref, a_ref, w_ref, o_ref):
    i = pl.program_id(0)
    k = pl.program_id(1)

    filled = meta_ref[2, i]
    tile = o_ref.shape[0]
    cn = COL_CHUNK
    for rows in _row_options(tile):
        in_range = (filled > rows - tile // ROW_QUARTERS) & (filled <= rows)

        @pl.when(in_range & (k == 0))
        def _(rows=rows):
            a = a_ref[0:rows, :]
            for c in range(0, o_ref.shape[1], cn):
                o_ref[0:rows, c:c + cn] = _dot(a, w_ref[:, c:c + cn])
            if rows < tile:
                o_ref[rows:tile, :] = jnp.zeros((tile - rows, o_ref.shape[1]), o_ref.dtype)

        @pl.when(in_range & (k > 0))
        def _(rows=rows):
            a = a_ref[0:rows, :]
            for c in range(0, o_ref.shape[1], cn):
                o_ref[0:rows, c:c + cn] += _dot(a, w_ref[:, c:c + cn])

    @pl.when((filled == 0) & (k == 0))
    def _():
        o_ref[...] = jnp.zeros_like(o_ref)


def moe_down(meta, up, w2, tile, tk=1792):
    rows, f = up.shape
    d = w2.shape[2]
    assert f % tk == 0 and rows % tile == 0
    n_tiles = rows // tile
    last = lambda i, meta: jnp.maximum(jnp.minimum(i, meta[1, 0] - 1), 0)
    return pl.pallas_call(
        _moe_down_kernel,
        grid_spec=pltpu.PrefetchScalarGridSpec(
            num_scalar_prefetch=1,
            grid=(n_tiles, f // tk),
            in_specs=[pl.BlockSpec((tile, tk), lambda i, k, meta: (last(i, meta), k)),
                      pl.BlockSpec((None, tk, d), lambda i, k, meta: (meta[0, i], k, 0))],
            out_specs=pl.BlockSpec((tile, d), lambda i, k, meta: (i, 0))),
        out_shape=jax.ShapeDtypeStruct((rows, d), F32),
        compiler_params=_params("arbitrary", "arbitrary"),
        name="moe_down",
    )(meta, up, w2)


def _moe_combine_kernel(pos_ref, x_ref, wt_ref, g_ref, y_hbm, o_ref, buf_ref, sem_ref, *, final_norm):
    i = pl.program_id(0)
    n = pl.num_programs(0)
    tc = x_ref.shape[0]
    m = pos_ref.shape[0] // 2

    def start(step, slot):
        for s in range(2):
            _row_gather_start(y_hbm, buf_ref.at[slot, s], pos_ref, s * m + step * tc, tc, sem_ref.at[slot, s])

    @pl.when(i == 0)
    def _():
        start(0, 0)

    @pl.when(i + 1 < n)
    def _():
        start(i + 1, (i + 1) % 2)

    cur = i % 2
    for s in range(2):
        _row_gather_wait(y_hbm, buf_ref.at[cur, s], sem_ref.at[cur, s])
    wt = wt_ref[...]
    out = x_ref[...] + wt[:, 0:1] * buf_ref[cur, 0] + wt[:, 1:2] * buf_ref[cur, 1]
    o_ref[...] = _rms(out, g_ref[...]) if final_norm else out


def moe_combine(pos2, x, wt, y, norm_g=None, tc=256):
    m, d = x.shape
    final_norm = norm_g is not None
    g = (norm_g if final_norm else jnp.ones((d,), F32)).reshape(1, d)
    return pl.pallas_call(
        functools.partial(_moe_combine_kernel, final_norm=final_norm),
        grid_spec=pltpu.PrefetchScalarGridSpec(
            num_scalar_prefetch=1,
            grid=(m // tc,),
            in_specs=[pl.BlockSpec((tc, d), lambda i, pos: (i, 0)),
                      pl.BlockSpec((tc, wt.shape[1]), lambda i, pos: (i, 0)),
                      pl.BlockSpec((1, d), lambda i, pos: (0, 0)),
                      pl.BlockSpec(memory_space=pl.ANY)],
            out_specs=pl.BlockSpec((tc, d), lambda i, pos: (i, 0)),
            scratch_shapes=[pltpu.VMEM((2, 2, tc, d), F32),
                            pltpu.SemaphoreType.DMA((2, 2))]),
        out_shape=jax.ShapeDtypeStruct((m, d), F32),
        compiler_params=_params("arbitrary"),
        name="moe_combine",
    )(pos2, x, wt, g, y)


def moe_layer(x, norm_g, router_w, router_b, w1, w3, w2, out_norm_g=None):
    m, d = x.shape
    n_e = router_w.shape[1]
    tile = MOE_TILE
    n_tiles = 2 * m // tile + n_e
    h, sel, w = norm_router(x, norm_g, router_w, router_b)
    pos, wt, meta = moe_route(sel, w, tile)
    pos2 = pos[:2].reshape(-1)
    xs = moe_gather(pos2, meta, h, n_tiles, tile)
    up, w2_bf16 = moe_up(meta, xs, w1, w3, w2, tile)
    y = moe_down(meta, up, w2_bf16, tile)
    return moe_combine(pos2, x, wt, y, out_norm_g)


def _mixer(x, batch, layer, norm_g, w_in_t, pe_k, pe_v, kw1, kw2, vw1, vw2, ln_g, ln_b, w_s, b_s, w_a, w_b, w_out,
           round_src=None):
    h = rmsnorm(x, norm_g, BF16)
    qt, kvr, vt, ngt, rest = in_projection(h, w_in_t, layer)
    cmp_kv = compress(kvr, batch, pe_k, pe_v, kw1, kw2, vw1, vw2)
    oct, sel = compressed_and_select(qt, cmp_kv, batch)
    oa, rounded = selected_window_attention(qt, kvr, vt, sel, oct, ngt, batch, round_src)
    ob = spatial_gating(rest, ln_g, ln_b, w_s, b_s)
    y = merge_branches(oa, ob, rest, w_a, w_b, layer)
    return residual_matmul("out_proj", y, w_out, layer, x), rounded


def kernel(x, norm_mix, w_in, cmp_pe_k, cmp_pe_v, cmp_k_w1, cmp_k_w2, cmp_v_w1, cmp_v_w2, sgu_ln_g, sgu_ln_b, sgu_w, sgu_b, w_branch_a, w_branch_b, w_out, norm_ffn, ffn_w1, ffn_w3, ffn_w2, router_w, router_b, moe_w1, moe_w3, moe_w2, norm_f):
    batch, seq, d = x.shape
    depth = norm_mix.shape[0]
    xf = x.reshape(batch * seq, d)
    w_in_t = jnp.swapaxes(w_in, 1, 2)
    normed = False
    n_moe = moe_w1.shape[0]
    moe_up_bf16 = {}
    for layer in range(depth):
        j = layer // 2
        round_src = None
        if j < n_moe and 2 * j + 1 < depth:
            w_f32 = moe_w1[j] if layer % 2 == 0 else moe_w3[j]
            round_src = w_f32.reshape(-1, w_f32.shape[-1])
        xf, rounded = _mixer(xf, batch, layer, norm_mix[layer], w_in_t, cmp_pe_k[layer], cmp_pe_v[layer],
                             cmp_k_w1[layer], cmp_k_w2[layer], cmp_v_w1[layer], cmp_v_w2[layer],
                             sgu_ln_g[layer], sgu_ln_b[layer], sgu_w[layer], sgu_b[layer],
                             w_branch_a, w_branch_b, w_out, round_src)
        if rounded is not None:
            moe_up_bf16[layer % 2] = rounded.reshape(moe_w1[j].shape)
        if layer % 2 == 0:
            h = rmsnorm(xf, norm_ffn[layer], BF16)
            up = swiglu_up(h, ffn_w1[j], ffn_w3[j])
            xf = mm_acc("ffn_down", up, ffn_w2[j], xf, 1024, 1024, up.shape[1] // 4)
        else:
            normed = layer == depth - 1
            xf = moe_layer(xf, norm_ffn[layer], router_w[j], router_b[j], moe_up_bf16[0], moe_up_bf16[1],
                           moe_w2[j], norm_f if normed else None)
    if not normed:
        xf = rmsnorm(xf, norm_f, F32)
    return xf.reshape(batch, seq, d)
```

```python
import functools

import numpy as np
import jax
import jax.numpy as jnp
from jax import lax
from jax.experimental import pallas as pl
from jax.experimental.pallas import tpu as pltpu

F32 = jnp.float32
BF16 = jnp.bfloat16

D_MODEL = 2048
N_Q_HEADS = 16
N_KV_HEADS = 4
HEAD_DIM = 64
Q_PER_KV = N_Q_HEADS // N_KV_HEADS
NSA_WIDTH = N_Q_HEADS * HEAD_DIM
KV_WIDTH = N_KV_HEADS * HEAD_DIM
GROUP_WIDTH = Q_PER_KV * HEAD_DIM
CMP_BLOCK = 32
CMP_STRIDE = 16
CMP_HIDDEN = 256
SLC_BLOCK = 64
N_SELECT = 16
WINDOW = 512
FORCED_SCORE = 1e4
SGU_WIDTH = D_MODEL // 2
SGU_GROUPS = 8
SGU_CHUNK = 128
N_EXPERTS = 8
NORM_EPS = 1e-5
NEG = -1e30
LOG2E = 1.4426950408889634
LANES = 128
VMEM_LIMIT = 56 * 1024 * 1024

Q_OFF = 0
KV_OFF = NSA_WIDTH
NG_OFF = KV_OFF + 6 * KV_WIDTH
NG_WIDTH = 3 * N_Q_HEADS
REST_OFF = NG_OFF + NG_WIDTH


def _params(*sem):
    return pltpu.CompilerParams(dimension_semantics=sem, vmem_limit_bytes=VMEM_LIMIT)


def _dot(a, b):
    return jnp.dot(a, b, preferred_element_type=F32)


def _dot_nt(a, b):
    return lax.dot_general(a, b, (((1,), (1,)), ((), ())), preferred_element_type=F32)


def _dot_split(a_f32, b_bf16):
    hi = a_f32.astype(BF16)
    lo = (a_f32 - hi.astype(F32)).astype(BF16)
    return _dot(hi, b_bf16) + _dot(lo, b_bf16)


def _sigmoid(x):
    return 1.0 / (1.0 + jnp.exp(-x))


def _gelu(x):
    return x * (0.5 * (1.0 + jnp.tanh(0.7978845608028654 * (x + 0.044715 * (x * x * x)))))


def _rms(x, g):
    return x * lax.rsqrt(jnp.mean(x * x, axis=-1, keepdims=True) + NORM_EPS) * g


def _rmsnorm_kernel(x_ref, g_ref, o_ref):
    o_ref[...] = _rms(x_ref[...], g_ref[...]).astype(o_ref.dtype)


def rmsnorm(x, g, out_dtype, tm=512):
    m, d = x.shape
    return pl.pallas_call(
        _rmsnorm_kernel,
        grid=(m // tm,),
        in_specs=[pl.BlockSpec((tm, d), lambda i: (i, 0)),
                  pl.BlockSpec((1, d), lambda i: (0, 0))],
        out_specs=pl.BlockSpec((tm, d), lambda i: (i, 0)),
        out_shape=jax.ShapeDtypeStruct((m, d), out_dtype),
        compiler_params=_params("parallel"),
        name="rmsnorm",
    )(x, g.reshape(1, d))


def _norm_router_kernel(x_ref, g_ref, rwt_ref, rb_ref, h_ref, sel_ref, w_ref):
    h = _rms(x_ref[...], g_ref[...])
    h_ref[...] = h
    logits = lax.dot_general(rwt_ref[...], h, (((1,), (1,)), ((), ())), preferred_element_type=F32,
                             precision=lax.Precision.HIGHEST) + rb_ref[...]
    n_e = logits.shape[0]
    e = lax.broadcasted_iota(jnp.int32, logits.shape, 0)
    m1 = jnp.max(logits, axis=0, keepdims=True)
    i1 = jnp.min(jnp.where(logits == m1, e, n_e), axis=0, keepdims=True)
    rest = jnp.where(e == i1, -jnp.inf, logits)
    m2 = jnp.max(rest, axis=0, keepdims=True)
    i2 = jnp.min(jnp.where(rest == m2, e, n_e), axis=0, keepdims=True)
    e2 = jnp.exp(m2 - m1)
    w1 = 1.0 / (1.0 + e2)
    w2 = e2 / (1.0 + e2)
    sel_ref[...] = jnp.where((e == i1) | (e == i2), 1.0, 0.0)
    w_ref[...] = jnp.where(e == i1, w1, 0.0) + jnp.where(e == i2, w2, 0.0)


def norm_router(x, g, router_w, router_b, tm=512):
    m, d = x.shape
    n_e = router_w.shape[1]
    return pl.pallas_call(
        _norm_router_kernel,
        grid=(m // tm,),
        in_specs=[pl.BlockSpec((tm, d), lambda i: (i, 0)),
                  pl.BlockSpec((1, d), lambda i: (0, 0)),
                  pl.BlockSpec((n_e, d), lambda i: (0, 0)),
                  pl.BlockSpec((n_e, 1), lambda i: (0, 0))],
        out_specs=[pl.BlockSpec((tm, d), lambda i: (i, 0)),
                   pl.BlockSpec((n_e, tm), lambda i: (0, i)),
                   pl.BlockSpec((n_e, tm), lambda i: (0, i))],
        out_shape=[jax.ShapeDtypeStruct((m, d), F32),
                   jax.ShapeDtypeStruct((n_e, m), F32),
                   jax.ShapeDtypeStruct((n_e, m), F32)],
        compiler_params=_params("parallel"),
        name="norm_router",
    )(x, g.reshape(1, d), router_w.T, router_b.reshape(n_e, 1))


def _mm_full_kernel(*refs, n_a, n_b, n_e, n_o, pairs, inner_axis, epilogue, b_transposed):
    a_refs = refs[:n_a]
    b_refs = refs[n_a:n_a + n_b]
    e_refs = refs[n_a + n_b:n_a + n_b + n_e]
    o_refs = refs[n_a + n_b + n_e:n_a + n_b + n_e + n_o]
    w_refs = refs[n_a + n_b + n_e + n_o:]

    @pl.when(pl.program_id(inner_axis) == 0)
    def _():
        for b_ref, w_ref in zip(b_refs, w_refs):
            w_ref[...] = b_ref[...].astype(BF16)

    dot = _dot_nt if b_transposed else _dot
    ws = [w_ref[0] if len(w_ref.shape) == 3 else w_ref[...] for w_ref in w_refs]
    accs = [dot(a_refs[ia][...], ws[ib]) for ia, ib in pairs]
    epilogue(accs, e_refs, o_refs)


def _block_dim_size(s):
    return s.block_size if isinstance(s, pl.Element) else s


def mm_full(name, grid, a_ops, b_ops, e_ops, outs, pairs, epilogue, b_transposed=False):
    ops = a_ops + b_ops + e_ops
    kernel = functools.partial(
        _mm_full_kernel, n_a=len(a_ops), n_b=len(b_ops), n_e=len(e_ops), n_o=len(outs),
        pairs=pairs, inner_axis=len(grid) - 1, epilogue=epilogue, b_transposed=b_transposed)
    scratch = [pltpu.VMEM(tuple(_block_dim_size(s) for s in blk if s is not None), BF16) for _, blk, _ in b_ops]
    res = pl.pallas_call(
        kernel,
        grid=grid,
        in_specs=[pl.BlockSpec(blk, imap) for _, blk, imap in ops],
        out_specs=[pl.BlockSpec(blk, imap) for _, blk, imap in outs],
        out_shape=[sds for sds, _, _ in outs],
        scratch_shapes=scratch,
        compiler_params=_params(*(("arbitrary",) * len(grid))),
        name=name,
    )(*[arr for arr, _, _ in ops])
    return res


COL_CHUNK = 256


def _mm_acc_kernel(a_ref, b_ref, r_ref, o_ref):
    cols = range(0, o_ref.shape[1], COL_CHUNK)

    @pl.when(pl.program_id(2) == 0)
    def _():
        a = a_ref[...]
        for c in cols:
            sl = slice(c, c + COL_CHUNK)
            o_ref[:, sl] = r_ref[:, sl] + _dot(a, b_ref[:, sl].astype(BF16))

    @pl.when(pl.program_id(2) > 0)
    def _():
        a = a_ref[...]
        for c in cols:
            sl = slice(c, c + COL_CHUNK)
            o_ref[:, sl] += _dot(a, b_ref[:, sl].astype(BF16))


def mm_acc(name, a, b, res, tm, tn, tk):
    m, k = a.shape
    n = b.shape[1]
    assert m % tm == 0 and n % tn == 0 and k % tk == 0 and tn % COL_CHUNK == 0 and tk % LANES == 0
    return pl.pallas_call(
        _mm_acc_kernel,
        grid=(m // tm, n // tn, k // tk),
        in_specs=[pl.BlockSpec((tm, tk), lambda i, j, kk: (i, kk)),
                  pl.BlockSpec((tk, tn), lambda i, j, kk: (kk, j)),
                  pl.BlockSpec((tm, tn), lambda i, j, kk: (i, j))],
        out_specs=pl.BlockSpec((tm, tn), lambda i, j, kk: (i, j)),
        out_shape=jax.ShapeDtypeStruct((m, n), F32),
        compiler_params=_params("parallel", "parallel", "arbitrary"),
        name=name,
    )(a, b, res)


def in_projection(h, w_in_t, layer):
    m, d = h.shape
    tm = 1024
    a_op = [(h, (tm, d), lambda j, i: (i, 0))]

    tq = 512
    q_heads = tq // HEAD_DIM
    q_scale = float(np.float32(HEAD_DIM ** -0.5) * np.float32(LOG2E))

    def q_epi(accs, e_refs, o_refs):
        for c in range(q_heads):
            o_refs[0][c] = (accs[0][:, c * HEAD_DIM:(c + 1) * HEAD_DIM] * q_scale).T.astype(BF16)

    (qt,) = mm_full(
        "in_proj_q", (NSA_WIDTH // tq, m // tm), a_op,
        [(w_in_t, (None, tq, d), lambda j, i: (layer, j, 0))], [],
        [(jax.ShapeDtypeStruct((N_Q_HEADS, HEAD_DIM, m), BF16), (q_heads, HEAD_DIM, tm), lambda j, i: (j, 0, i))],
        [(0, 0)], q_epi, b_transposed=True)

    sec0 = KV_OFF // KV_WIDTH

    def rows_epi(accs, e_refs, o_refs):
        for c in range(N_KV_HEADS):
            o_refs[0][c] = accs[0][:, c * HEAD_DIM:(c + 1) * HEAD_DIM].astype(BF16)

    (kvr,) = mm_full(
        "in_proj_kv_rows", (4, m // tm), a_op,
        [(w_in_t, (None, KV_WIDTH, d), lambda j, i: (layer, sec0 + j + j // 3, 0))], [],
        [(jax.ShapeDtypeStruct((4 * N_KV_HEADS, m, HEAD_DIM), BF16),
          (N_KV_HEADS, tm, HEAD_DIM), lambda j, i: (j, i, 0))],
        [(0, 0)], rows_epi, b_transposed=True)

    def cols_epi(accs, e_refs, o_refs):
        for c in range(N_KV_HEADS):
            o_refs[0][c] = accs[0][:, c * HEAD_DIM:(c + 1) * HEAD_DIM].T.astype(BF16)

    (vt,) = mm_full(
        "in_proj_v_cols", (2, m // tm), a_op,
        [(w_in_t, (None, KV_WIDTH, d), lambda j, i: (layer, sec0 + 3 + 2 * j, 0))], [],
        [(jax.ShapeDtypeStruct((2 * N_KV_HEADS, HEAD_DIM, m), BF16),
          (N_KV_HEADS, HEAD_DIM, tm), lambda j, i: (j, 0, i))],
        [(0, 0)], cols_epi, b_transposed=True)

    def f32_epi(accs, e_refs, o_refs):
        o_refs[0][...] = accs[0]

    def f32_t_epi(accs, e_refs, o_refs):
        o_refs[0][...] = accs[0].T

    assert NG_OFF % LANES == 0
    (ngt,) = mm_full(
        "in_proj_ng", (1, m // tm), a_op,
        [(w_in_t, (None, LANES, d), lambda j, i: (layer, NG_OFF // LANES, 0))], [],
        [(jax.ShapeDtypeStruct((LANES, m), F32), (LANES, tm), lambda j, i: (0, i))],
        [(0, 0)], f32_t_epi, b_transposed=True)

    n_rest = w_in_t.shape[1] - REST_OFF
    tr = 1024
    assert n_rest % tr == 0 and REST_OFF % 8 == 0
    (rest,) = mm_full(
        "in_proj_rest", (n_rest // tr, m // tm), a_op,
        [(w_in_t, (pl.Element(1), pl.Element(tr), pl.Element(d)), lambda j, i: (layer, pl.multiple_of(REST_OFF + j * tr, 8), 0))], [],
        [(jax.ShapeDtypeStruct((m, n_rest), F32), (tm, tr), lambda j, i: (i, j))],
        [(0, 0)], f32_epi, b_transposed=True)
    return qt, kvr, vt, ngt, rest


def _cmp_kernel(a_ref, w1_ref, w2_ref, pe_ref, o_ref):
    a = a_ref[...]
    rows = a.shape[0]
    half = a.shape[1]
    w1 = w1_ref[...].astype(BF16)
    p0 = _dot(a, w1[:half])
    p1 = _dot(a, w1[half:])
    bias = _dot(pe_ref[...].astype(BF16), w1)[0:1]
    hid = p0 + pltpu.roll(p1, rows - 1, 0) + bias
    o_ref[...] = _dot(_gelu(hid).astype(BF16), w2_ref[...].astype(BF16))


def compress(kv, batch, pe_k, pe_v, kw1, kw2, vw1, vw2):
    m = kv.shape[1]
    n_chunks = m // batch // CMP_STRIDE
    rows = batch * n_chunks
    feat = CMP_STRIDE * HEAD_DIM
    a = kv[:2 * N_KV_HEADS].reshape(2, N_KV_HEADS, rows, feat)
    w1 = jnp.stack([kw1, vw1])
    w2 = jnp.stack([kw2, vw2])
    pe = jnp.stack([pe_k, pe_v]).reshape(2, 1, CMP_BLOCK * HEAD_DIM)
    pe = jnp.broadcast_to(pe, (2, 8, CMP_BLOCK * HEAD_DIM))
    return pl.pallas_call(
        _cmp_kernel,
        grid=(2, N_KV_HEADS),
        in_specs=[pl.BlockSpec((None, None, rows, feat), lambda s, g: (s, g, 0, 0)),
                  pl.BlockSpec((None, 2 * feat, CMP_HIDDEN), lambda s, g: (s, 0, 0)),
                  pl.BlockSpec((None, CMP_HIDDEN, HEAD_DIM), lambda s, g: (s, 0, 0)),
                  pl.BlockSpec((None, 8, 2 * feat), lambda s, g: (s, 0, 0))],
        out_specs=pl.BlockSpec((None, None, rows, HEAD_DIM), lambda s, g: (s, g, 0, 0)),
        out_shape=jax.ShapeDtypeStruct((2, N_KV_HEADS, rows, HEAD_DIM), F32),
        compiler_params=_params("parallel", "parallel"),
        name="nsa_compress",
    )(a, w1, w2, pe)


def _alibi_slopes():
    sl = np.array([2.0 ** (-8.0 * (h + 1) / N_Q_HEADS) for h in range(N_Q_HEADS)], dtype=np.float32)
    return sl * np.float32(LOG2E)


def _overlap_t(n_cmp_pad, n_slc):
    cs = np.arange(n_cmp_pad)[None, :] * CMP_STRIDE
    ss = np.arange(n_slc)[:, None] * SLC_BLOCK
    ov = np.clip(np.minimum(cs + CMP_BLOCK, ss + SLC_BLOCK) - np.maximum(cs, ss), 0, None)
    return (ov / CMP_STRIDE).astype(np.float32)


def _sel_kernel(slopes_ref, q_ref, kc_ref, vc_ref, ovt_ref, oc_ref, sel_ref, *, n_cmp):
    g = pl.program_id(1)
    i = pl.program_id(2)
    tq = q_ref.shape[2]
    ncp = kc_ref.shape[0]
    n_slc = ovt_ref.shape[0]
    kc = kc_ref[...].astype(BF16)
    vct = vc_ref[...].T.astype(BF16)

    t = i * tq + lax.broadcasted_iota(jnp.int32, (ncp, tq), 1)
    c = lax.broadcasted_iota(jnp.int32, (ncp, tq), 0)
    dist = t - (c * CMP_STRIDE + (CMP_BLOCK - 1))
    mask = (dist >= 0) & (c < n_cmp)
    distf = dist.astype(F32)

    scores = [_dot(kc, q_ref[j]) for j in range(Q_PER_KV)]
    probs = []
    for j in range(Q_PER_KV):
        s = scores[j] - slopes_ref[g * Q_PER_KV + j] * distf
        s = jnp.where(mask, s, NEG)
        mx = jnp.max(s, axis=0, keepdims=True)
        e = jnp.where(mask, jnp.exp2(s - mx), 0.0)
        probs.append(e / jnp.maximum(jnp.sum(e, axis=0, keepdims=True), 1e-30))
    for j in range(Q_PER_KV):
        oc_ref[j] = _dot(vct, probs[j].astype(BF16))
    p_sum = functools.reduce(lambda a, b: a + b, probs)

    ovt = ovt_ref[...]
    hi = p_sum.astype(BF16)
    lo = (p_sum - hi.astype(F32)).astype(BF16)
    imp = _dot(ovt, hi) + _dot(ovt, lo)
    tt = i * tq + lax.broadcasted_iota(jnp.int32, (n_slc, tq), 1)
    blk = lax.broadcasted_iota(jnp.int32, (n_slc, tq), 0)
    cur = tt // SLC_BLOCK
    valid = blk * SLC_BLOCK <= tt
    forced = (blk == 0) | (blk == cur) | (blk == cur - 1)
    score = jnp.where(valid, imp, -1.0)
    score = jnp.where(forced, FORCED_SCORE, score)
    rank = jnp.zeros((n_slc, tq), F32)
    for mrow in range(n_slc):
        other = jnp.broadcast_to(score[mrow:mrow + 1, :], (n_slc, tq))
        beats = (other > score) | ((other == score) & (blk > mrow))
        rank = rank + jnp.where(beats, 1.0, 0.0)
    sel_ref[...] = jnp.where(rank < float(min(N_SELECT, n_slc)), 1.0, 0.0)


def compressed_and_select(qt, cmp_kv, batch, tq=256):
    m = qt.shape[2]
    seq = m // batch
    nq = seq // tq
    ncp = cmp_kv.shape[2] // batch
    n_slc = seq // SLC_BLOCK
    ovt = jnp.asarray(_overlap_t(ncp, n_slc), BF16)
    slopes = jnp.asarray(_alibi_slopes())
    kernel = functools.partial(_sel_kernel, n_cmp=ncp - 1)
    return pl.pallas_call(
        kernel,
        grid=(batch, N_KV_HEADS, nq),
        in_specs=[pl.BlockSpec(memory_space=pltpu.SMEM),
                  pl.BlockSpec((Q_PER_KV, HEAD_DIM, tq), lambda b, g, i: (g, 0, b * nq + i)),
                  pl.BlockSpec((None, None, ncp, HEAD_DIM), lambda b, g, i: (0, g, b, 0)),
                  pl.BlockSpec((None, None, ncp, HEAD_DIM), lambda b, g, i: (1, g, b, 0)),
                  pl.BlockSpec((n_slc, ncp), lambda b, g, i: (0, 0))],
        out_specs=[pl.BlockSpec((Q_PER_KV, HEAD_DIM, tq), lambda b, g, i: (g, 0, b * nq + i)),
                   pl.BlockSpec((None, None, n_slc, tq), lambda b, g, i: (b, g, 0, i))],
        out_shape=[jax.ShapeDtypeStruct((N_Q_HEADS, HEAD_DIM, m), F32),
                   jax.ShapeDtypeStruct((batch, N_KV_HEADS, n_slc, seq), F32)],
        compiler_params=_params("parallel", "parallel", "parallel"),
        name="nsa_compressed_select",
    )(slopes, qt, cmp_kv, cmp_kv, ovt)


AUX_SLOPE = HEAD_DIM
AUX_SEL = HEAD_DIM + 16
ATTN_K = 2 * HEAD_DIM
STRIP = 32


def _attn_kernel(slopes_ref, q_ref, ks_ref, vs_ref, kw_ref, vw_ref, auxs_ref, auxw_ref, sel_ref, oc_ref,
                 ng_ref, *rest, round_weights):
    if round_weights:
        wsrc_ref, o_ref, wdst_ref = rest[:3]
        scratch = rest[3:]

        def side_work():
            wdst_ref[...] = wsrc_ref[...].astype(BF16)
    else:
        o_ref = rest[0]
        scratch = rest[1:]
        side_work = lambda: None
    _attn_body(slopes_ref, q_ref, ks_ref, vs_ref, kw_ref, vw_ref, auxs_ref, auxw_ref, sel_ref, oc_ref, ng_ref,
               o_ref, *scratch, side_work=side_work)


def _attn_body(slopes_ref, q_ref, ks_ref, vs_ref, kw_ref, vw_ref, auxs_ref, auxw_ref, sel_ref, oc_ref,
               ng_ref, o_ref, ka_s_ref, ka_w_ref, qa_ref, s_ref, p_ref, mask_ref, m_ref, l_ref, acc_ref,
               os_ref, sig_ref, *, side_work):
    g = pl.program_id(1)
    i = pl.program_id(2)
    tq = q_ref.shape[2]
    tk = tq
    q0 = i * tq
    n_slc = sel_ref.shape[0]

    @pl.when(i == 0)
    def _():
        ka_s_ref[:, 0:HEAD_DIM] = ks_ref[...]
        ka_s_ref[:, HEAD_DIM:ATTN_K] = auxs_ref[...]
        ka_w_ref[:, 0:HEAD_DIM] = kw_ref[...]
        ka_w_ref[:, HEAD_DIM:ATTN_K] = auxw_ref[...]
        kr = lax.broadcasted_iota(jnp.int32, (tk, tq), 0)
        qc = lax.broadcasted_iota(jnp.int32, (tk, tq), 1)
        mask_ref[0] = jnp.where(kr <= qc, 0.0, NEG)
        mask_ref[1] = jnp.where(kr > qc, 0.0, NEG)

    sig_ref[...] = _sigmoid(ng_ref[...])
    side_work()
    sel_neg =(sel_ref[...] - 1.0) * (-NEG)
    row16 = lax.broadcasted_iota(jnp.int32, (16, tq), 0)
    pad = jnp.zeros((ATTN_K - AUX_SEL - n_slc, tq), F32)
    for j in range(Q_PER_KV):
        sl = jnp.full((16, tq), slopes_ref[g * Q_PER_KV + j], F32)
        hi = sl.astype(BF16).astype(F32)
        mid = (sl - hi).astype(BF16).astype(F32)
        lo = (sl - hi - mid).astype(BF16).astype(F32)
        pieces = jnp.where(row16 == 0, hi, jnp.where(row16 == 1, mid, jnp.where(row16 == 2, lo, 0.0)))
        qa = jnp.concatenate([q_ref[j].astype(F32), pieces, sel_neg, pad], axis=0)
        qa_ref[j] = qa.astype(BF16)

    def scores(ka_ref, k0, buf):
        ka = ka_ref[pl.ds(k0, tk), :]
        for j in range(Q_PER_KV):
            s_ref[buf, j] = _dot(ka, qa_ref[j])

    def softmax_pv(v_ref, k0, buf, mask_idx, first):
        vc = v_ref[:, pl.ds(k0, tk)]
        for j in range(Q_PER_KV):
            shift = slopes_ref[g * Q_PER_KV + j] * k0.astype(F32)
            mx = None
            for r in range(0, tk, STRIP):
                x = s_ref[buf, j, r:r + STRIP, :]
                if mask_idx is not None:
                    x = x + mask_ref[mask_idx, r:r + STRIP, :]
                    s_ref[buf, j, r:r + STRIP, :] = x
                mx = x if mx is None else jnp.maximum(mx, x)
            m_cur = jnp.max(mx, axis=0, keepdims=True) + shift
            if first:
                m_new = m_cur
            else:
                m_old = m_ref[j]
                m_new = jnp.maximum(m_old, m_cur)
                alpha = jnp.exp2(m_old - m_new)
            m_ref[j] = m_new
            m_loc = m_new - shift
            ls = None
            for r in range(0, tk, STRIP):
                p = jnp.exp2(s_ref[buf, j, r:r + STRIP, :] - m_loc)
                ls = p if ls is None else ls + p
                p_ref[j, r:r + STRIP, :] = p.astype(BF16)
            l_cur = jnp.sum(ls, axis=0, keepdims=True)
            pv = _dot(vc, p_ref[j])
            if first:
                l_ref[j] = l_cur
                acc_ref[j] = pv
            else:
                l_ref[j] = alpha * l_ref[j] + l_cur
                acc_ref[j] = alpha * acc_ref[j] + pv

    k_diag = pl.multiple_of(q0, tk)
    last = jnp.maximum(i - 1, 0)
    k_of = lambda c: pl.multiple_of(jnp.minimum(c, last) * tk, tk)
    scores(ka_s_ref, k_diag, 1)
    scores(ka_s_ref, k_of(0), 0)
    softmax_pv(vs_ref, k_diag, 1, 0, True)

    def slc_pair(cc, carry):
        c0 = 2 * cc
        scores(ka_s_ref, k_of(c0 + 1), 1)
        softmax_pv(vs_ref, k_of(c0), 0, None, False)

        @pl.when(c0 + 1 < i)
        def _():
            scores(ka_s_ref, k_of(c0 + 2), 0)
            softmax_pv(vs_ref, k_of(c0 + 1), 1, None, False)
        return carry

    lax.fori_loop(0, (i + 1) // 2, slc_pair, 0)
    for j in range(Q_PER_KV):
        os_ref[j] = acc_ref[j] / l_ref[j]

    n_back = WINDOW // tk
    back_k0 = lambda back: pl.multiple_of(jnp.maximum(q0 - back * tk, 0), tk)
    scores(ka_w_ref, k_diag, 0)
    scores(ka_w_ref, back_k0(1), 1)
    softmax_pv(vw_ref, k_diag, 0, 0, True)
    for back in range(1, n_back + 1):
        @pl.when(i >= back)
        def _(back=back):
            if back < n_back:
                scores(ka_w_ref, back_k0(back + 1), (back + 1) % 2)
            softmax_pv(vw_ref, back_k0(back), back % 2, 1 if back == n_back else None, False)

    for j in range(Q_PER_KV):
        base = (g * Q_PER_KV + j) * 3
        out_t = (sig_ref[pl.ds(base, 1), :] * oc_ref[j]
                 + sig_ref[pl.ds(base + 1, 1), :] * os_ref[j]
                 + sig_ref[pl.ds(base + 2, 1), :] * (acc_ref[j] / l_ref[j]))
        o_ref[:, j * HEAD_DIM:(j + 1) * HEAD_DIM] = out_t.T.astype(o_ref.dtype)


def _key_aux(seq, tk, with_blocks):
    aux = np.zeros((seq, ATTN_K - HEAD_DIM), np.float32)
    pos = np.arange(seq)
    aux[:, AUX_SLOPE - HEAD_DIM:AUX_SLOPE - HEAD_DIM + 3] = (pos % tk)[:, None]
    if with_blocks:
        aux[pos, AUX_SEL - HEAD_DIM + pos // SLC_BLOCK] = 1.0
    return aux


def selected_window_attention(qt, kvr, vt, sel, oct, ngt, batch, round_src=None, tq=256):
    m = qt.shape[2]
    seq = m // batch
    nq = seq // tq
    n_slc = seq // SLC_BLOCK
    assert tq <= 256 and AUX_SEL + n_slc <= ATTN_K
    aux_s = jnp.asarray(_key_aux(seq, tq, True), BF16)
    aux_w = jnp.asarray(_key_aux(seq, tq, False), BF16)
    slopes = jnp.asarray(_alibi_slopes())
    n_g = N_KV_HEADS

    def k_spec(section):
        return pl.BlockSpec((None, seq, HEAD_DIM), lambda b, g, i: (section * n_g + g, b, 0))

    def v_spec(section):
        return pl.BlockSpec((None, HEAD_DIM, seq), lambda b, g, i: (section * n_g + g, 0, b))

    head_blk = pl.BlockSpec((Q_PER_KV, HEAD_DIM, tq), lambda b, g, i: (g, 0, b * nq + i))
    aux_blk = pl.BlockSpec((seq, ATTN_K - HEAD_DIM), lambda b, g, i: (0, 0))
    in_specs = [pl.BlockSpec(memory_space=pltpu.SMEM),
                head_blk,
                k_spec(2), v_spec(0), k_spec(3), v_spec(1),
                aux_blk, aux_blk,
                pl.BlockSpec((None, None, n_slc, tq), lambda b, g, i: (b, g, 0, i)),
                head_blk,
                pl.BlockSpec((LANES, tq), lambda b, g, i: (0, b * nq + i))]
    out_specs = [pl.BlockSpec((tq, GROUP_WIDTH), lambda b, g, i: (b * nq + i, g))]
    out_shape = [jax.ShapeDtypeStruct((m, NSA_WIDTH), BF16)]
    operands = [slopes, qt, kvr, vt, kvr, vt, aux_s, aux_w, sel, oct, ngt]
    if round_src is not None:
        steps = batch * N_KV_HEADS * nq
        rows, cols = round_src.shape
        assert rows % steps == 0 and (rows // steps) % BF16_SUBLANES == 0
        side_blk = pl.BlockSpec((rows // steps, cols), lambda b, g, i: ((b * N_KV_HEADS + g) * nq + i, 0))
        in_specs.append(side_blk)
        out_specs.append(side_blk)
        out_shape.append(jax.ShapeDtypeStruct((rows, cols), BF16))
        operands.append(round_src)
    res = pl.pallas_call(
        functools.partial(_attn_kernel, round_weights=round_src is not None),
        grid=(batch, N_KV_HEADS, nq),
        in_specs=in_specs,
        out_specs=out_specs,
        out_shape=out_shape,
        scratch_shapes=[pltpu.VMEM((seq, ATTN_K), BF16),
                        pltpu.VMEM((seq, ATTN_K), BF16),
                        pltpu.VMEM((Q_PER_KV, ATTN_K, tq), BF16),
                        pltpu.VMEM((2, Q_PER_KV, tq, tq), F32),
                        pltpu.VMEM((Q_PER_KV, tq, tq), BF16),
                        pltpu.VMEM((2, tq, tq), F32),
                        pltpu.VMEM((Q_PER_KV, 1, tq), F32),
                        pltpu.VMEM((Q_PER_KV, 1, tq), F32),
                        pltpu.VMEM((Q_PER_KV, HEAD_DIM, tq), F32),
                        pltpu.VMEM((Q_PER_KV, HEAD_DIM, tq), F32),
                        pltpu.VMEM((LANES, tq), F32)],
        compiler_params=_params("arbitrary", "arbitrary", "arbitrary"),
        name="nsa_selected_window",
    )(*operands)
    return (res[0], res[1]) if round_src is not None else (res[0], None)


def _sgu_kernel(u_ref, v_ref, lg_ref, lb_ref, ws_ref, bs_ref, o_ref):
    rows = u_ref.shape[0]
    v = _gelu(v_ref[...])
    mu = jnp.mean(v, axis=-1, keepdims=True)
    var = jnp.mean(jnp.square(v - mu), axis=-1, keepdims=True)
    vl = ((v - mu) * lax.rsqrt(var + NORM_EPS) * lg_ref[...] + lb_ref[...]).astype(BF16)
    r = lax.broadcasted_iota(jnp.int32, (SGU_CHUNK, SGU_CHUNK), 0)
    c = lax.broadcasted_iota(jnp.int32, (SGU_CHUNK, SGU_CHUNK), 1)
    gd = SGU_WIDTH // SGU_GROUPS
    for grp in range(SGU_GROUPS):
        w = jnp.where(c <= r, ws_ref[grp], 0.0).astype(BF16)
        lanes = slice(grp * gd, (grp + 1) * gd)
        for n in range(rows // SGU_CHUNK):
            rs = slice(n * SGU_CHUNK, (n + 1) * SGU_CHUNK)
            vm = _dot(w, vl[rs, lanes]) + bs_ref[:, lanes]
            o_ref[rs, lanes] = (_gelu(u_ref[rs, lanes]) * vm).astype(o_ref.dtype)


def spatial_gating(rest, ln_g, ln_b, w_s, b_s, tm=512):
    m = rest.shape[0]
    gd = SGU_WIDTH // SGU_GROUPS
    bias = jnp.repeat(b_s.T, gd, axis=1)
    return pl.pallas_call(
        _sgu_kernel,
        grid=(m // tm,),
        in_specs=[pl.BlockSpec((tm, SGU_WIDTH), lambda i: (i, 0)),
                  pl.BlockSpec((tm, SGU_WIDTH), lambda i: (i, 1)),
                  pl.BlockSpec((1, SGU_WIDTH), lambda i: (0, 0)),
                  pl.BlockSpec((1, SGU_WIDTH), lambda i: (0, 0)),
                  pl.BlockSpec((SGU_GROUPS, SGU_CHUNK, SGU_CHUNK), lambda i: (0, 0, 0)),
                  pl.BlockSpec((SGU_CHUNK, SGU_WIDTH), lambda i: (0, 0))],
        out_specs=pl.BlockSpec((tm, SGU_WIDTH), lambda i: (i, 0)),
        out_shape=jax.ShapeDtypeStruct((m, SGU_WIDTH), BF16),
        compiler_params=_params("parallel"),
        name="spatial_gating",
    )(rest, rest, ln_g.reshape(1, -1), ln_b.reshape(1, -1), w_s, bias)


def merge_branches(oa, ob, rest, w_a, w_b, layer, tm=512, tn=1024):
    m = oa.shape[0]
    d = w_a.shape[2]
    ga_blk = 2 * SGU_WIDTH // tn
    gb_blk = (2 * SGU_WIDTH + d) // tn

    def epi(accs, e_refs, o_refs):
        y = _sigmoid(e_refs[0][...]) * accs[0] + _sigmoid(e_refs[1][...]) * accs[1]
        o_refs[0][...] = y.astype(BF16)

    (y,) = mm_full(
        "merge_branches", (d // tn, m // tm),
        [(oa, (tm, oa.shape[1]), lambda j, i: (i, 0)), (ob, (tm, ob.shape[1]), lambda j, i: (i, 0))],
        [(w_a, (None, w_a.shape[1], tn), lambda j, i: (layer, 0, j)),
         (w_b, (None, w_b.shape[1], tn), lambda j, i: (layer, 0, j))],
        [(rest, (tm, tn), lambda j, i: (i, ga_blk + j)), (rest, (tm, tn), lambda j, i: (i, gb_blk + j))],
        [(jax.ShapeDtypeStruct((m, d), BF16), (tm, tn), lambda j, i: (i, j))],
        [(0, 0), (1, 1)], epi)
    return y


def residual_matmul(name, a, w, layer, x, tm=512, tn=1024):
    m, k = a.shape
    n = w.shape[2]

    def epi(accs, e_refs, o_refs):
        o_refs[0][...] = e_refs[0][...] + accs[0]

    (out,) = mm_full(
        name, (n // tn, m // tm),
        [(a, (tm, k), lambda j, i: (i, 0))],
        [(w, (None, k, tn), lambda j, i: (layer, 0, j))],
        [(x, (tm, tn), lambda j, i: (i, j))],
        [(jax.ShapeDtypeStruct((m, n), F32), (tm, tn), lambda j, i: (i, j))],
        [(0, 0)], epi)
    return out


def swiglu_up(h, w1, w3, tm=1024, tn=512):
    m, d = h.shape
    f = w1.shape[1]

    def epi(accs, e_refs, o_refs):
        o_refs[0][...] = (accs[0] * _sigmoid(accs[0]) * accs[1]).astype(BF16)

    (out,) = mm_full(
        "swiglu_up", (f // tn, m // tm),
        [(h, (tm, d), lambda j, i: (i, 0))],
        [(w1, (d, tn), lambda j, i: (0, j)), (w3, (d, tn), lambda j, i: (0, j))],
        [],
        [(jax.ShapeDtypeStruct((m, f), BF16), (tm, tn), lambda j, i: (i, j))],
        [(0, 0), (0, 1)], epi)
    return out


MOE_TILE = 512
ROW_QUARTERS = 4


def _row_options(tile):
    step = tile // ROW_QUARTERS
    return tuple(range(step, tile + 1, step))


def _route_kernel(sel_ref, w_ref, tri_ref, pos_ref, wt_ref, meta_ref, cum_ref, *, tile):
    n_e, m = sel_ref.shape
    ck = tri_ref.shape[0]
    carry = jnp.zeros((n_e, 1), F32)
    for c in range(m // ck):
        sl = slice(c * ck, (c + 1) * ck)
        cs = _dot(sel_ref[:, sl].astype(BF16), tri_ref[...]) + carry
        cum_ref[:, sl] = cs
        carry = cs[:, ck - 1:ck]
    padded = jnp.ceil(carry / tile) * tile
    sub = lax.broadcasted_iota(jnp.int32, (n_e, 1), 0)
    start = jnp.zeros((n_e, 1), F32)
    run = jnp.zeros((1, 1), F32)
    for ex in range(n_e):
        start = jnp.where(sub == ex, run, start)
        run = run + padded[ex:ex + 1, :]
    sel = sel_ref[...] > 0.5
    pos = start + cum_ref[...] - 1.0
    eidx = lax.broadcasted_iota(jnp.int32, (n_e, m), 0)
    e_lo = jnp.min(jnp.where(sel, eidx, n_e), axis=0, keepdims=True)
    e_hi = jnp.max(jnp.where(sel, eidx, -1), axis=0, keepdims=True)
    is_lo = eidx == e_lo
    is_hi = eidx == e_hi
    w = w_ref[...]
    pos_a = jnp.sum(jnp.where(is_lo, pos, 0.0), axis=0, keepdims=True)
    pos_b = jnp.sum(jnp.where(is_hi, pos, 0.0), axis=0, keepdims=True)
    w_a = jnp.sum(jnp.where(is_lo, w, 0.0), axis=0, keepdims=True)
    w_b = jnp.sum(jnp.where(is_hi, w, 0.0), axis=0, keepdims=True)
    pos_ref[...] = jnp.where(eidx == 0, pos_a, jnp.where(eidx == 1, pos_b, 0.0)).astype(jnp.int32)
    wt_ref[...] = jnp.where(eidx == 0, w_a, jnp.where(eidx == 1, w_b, 0.0)).T
    tile_lo = lax.broadcasted_iota(jnp.int32, (n_e, LANES), 1).astype(F32) * tile
    t_exp = jnp.sum(jnp.where(start + padded <= tile_lo, 1.0, 0.0), axis=0, keepdims=True)
    t_exp = jnp.minimum(t_exp, n_e - 1.0)
    r8 = lax.broadcasted_iota(jnp.int32, (n_e, LANES), 0)
    last_row = jnp.sum(jnp.where(r8.astype(F32) == t_exp, start + carry, 0.0), axis=0, keepdims=True)
    filled = jnp.clip(last_row - tile_lo[0:1], 0.0, float(tile))
    meta_ref[...] = jnp.where(r8 == 0, t_exp, jnp.where(r8 == 1, run / tile,
                                                       jnp.where(r8 == 2, filled, 0.0))).astype(jnp.int32)


def moe_route(sel, w, tile):
    n_e, m = sel.shape
    ck = 256
    tri = jnp.asarray(np.triu(np.ones((ck, ck), np.float32)), BF16)
    full = lambda shape: pl.BlockSpec(shape, lambda: tuple(0 for _ in shape))
    return pl.pallas_call(
        functools.partial(_route_kernel, tile=tile),
        in_specs=[full((n_e, m)), full((n_e, m)), full((ck, ck))],
        out_specs=[full((n_e, m)), full((m, n_e)), full((n_e, LANES))],
        out_shape=[jax.ShapeDtypeStruct((n_e, m), jnp.int32),
                   jax.ShapeDtypeStruct((m, n_e), F32),
                   jax.ShapeDtypeStruct((n_e, LANES), jnp.int32)],
        scratch_shapes=[pltpu.VMEM((n_e, m), F32)],
        compiler_params=pltpu.CompilerParams(vmem_limit_bytes=VMEM_LIMIT),
        name="moe_route",
    )(sel, w, tri)


def _row_gather_start(src_hbm, dst, rows_ref, base, n_rows, sem, unroll=8):
    def body(r, carry):
        src_row = rows_ref[base + r]
        pltpu.make_async_copy(src_hbm.at[pl.ds(src_row, 1)], dst.at[pl.ds(r, 1)], sem).start()
        return carry
    lax.fori_loop(0, n_rows, body, 0, unroll=unroll)


def _row_gather_wait(src_hbm, dst, sem):
    pltpu.make_async_copy(src_hbm.at[pl.ds(0, dst.shape[0])], dst, sem).wait()


def _moe_gather_kernel(pos_ref, meta_ref, h_hbm, zeros_hbm, o_ref, tok_ref, buf_ref, sem_ref):
    i = pl.program_id(0)
    n = meta_ref[1, 0]
    tile = o_ref.shape[0]
    m = pos_ref.shape[0] // 2

    @pl.when(i == 0)
    def _():
        clear = pltpu.make_async_copy(zeros_hbm, tok_ref, sem_ref.at[2])
        clear.start()
        clear.wait()

        def fill(t, carry):
            tok_ref[pos_ref[t]] = t
            tok_ref[pos_ref[m + t]] = t
            return carry
        lax.fori_loop(0, m, fill, 0, unroll=8)
        _row_gather_start(h_hbm, buf_ref.at[0], tok_ref, 0, tile, sem_ref.at[0])

    @pl.when(i + 1 < n)
    def _():
        nxt = (i + 1) % 2
        _row_gather_start(h_hbm, buf_ref.at[nxt], tok_ref, (i + 1) * tile, tile, sem_ref.at[nxt])

    @pl.when(i < n)
    def _():
        cur = i % 2
        _row_gather_wait(h_hbm, buf_ref.at[cur], sem_ref.at[cur])
        o_ref[...] = buf_ref[cur].astype(o_ref.dtype)

    @pl.when(i >= n)
    def _():
        o_ref[...] = jnp.zeros_like(o_ref)


def moe_gather(pos2, meta, h, n_tiles, tile):
    m, d = h.shape
    rows = n_tiles * tile
    return pl.pallas_call(
        _moe_gather_kernel,
        grid_spec=pltpu.PrefetchScalarGridSpec(
            num_scalar_prefetch=2,
            grid=(n_tiles,),
            in_specs=[pl.BlockSpec(memory_space=pl.ANY), pl.BlockSpec(memory_space=pl.ANY)],
            out_specs=pl.BlockSpec((tile, d), lambda i, pos, meta: (i, 0)),
            scratch_shapes=[pltpu.SMEM((rows,), jnp.int32),
                            pltpu.VMEM((2, tile, d), F32),
                            pltpu.SemaphoreType.DMA((3,))]),
        out_shape=jax.ShapeDtypeStruct((rows, d), BF16),
        compiler_params=_params("arbitrary"),
        name="moe_gather",
    )(pos2, meta, h, jnp.zeros((rows,), jnp.int32))


def _moe_up_kernel(meta_ref, x_ref, w1_ref, w3_ref, w2_ref, o_ref, w2b_ref):
    i = pl.program_id(1)

    def round_w2_block():
        w2b_ref[...] = w2_ref[...].astype(BF16)

    filled = meta_ref[2, i]
    tile = o_ref.shape[0]
    for rows in _row_options(tile):
        @pl.when((filled > rows - tile // ROW_QUARTERS) & (filled <= rows))
        def _(rows=rows):
            x = x_ref[0:rows, :]
            for c in range(0, o_ref.shape[1], COL_CHUNK):
                sl = slice(c, c + COL_CHUNK)
                a1 = _dot(x, w1_ref[:, sl])
                a3 = _dot(x, w3_ref[:, sl])
                o_ref[0:rows, sl] = (a1 * _sigmoid(a1) * a3).astype(o_ref.dtype)
            if rows < tile:
                o_ref[rows:tile, :] = jnp.zeros((tile - rows, o_ref.shape[1]), o_ref.dtype)
            round_w2_block()

    @pl.when(filled == 0)
    def _():
        o_ref[...] = jnp.zeros_like(o_ref)
        round_w2_block()


BF16_SUBLANES = 16


def _cast_rows_per_step(total_rows, steps):
    for rows in range(BF16_SUBLANES, total_rows + 1, BF16_SUBLANES):
        if total_rows % rows == 0 and total_rows // rows <= steps:
            return rows
    raise ValueError("weights cannot be split over the grid steps")


def moe_up(meta, xs, w1, w3, w2, tile, tn=1024):
    rows, d = xs.shape
    n_e, _, f = w1.shape
    assert f % tn == 0 and rows % tile == 0 and w1.dtype == BF16 and w3.dtype == BF16
    n_tiles = rows // tile
    w2_flat = w2.reshape(n_e * f, w2.shape[2])
    cast_rows = _cast_rows_per_step(w2_flat.shape[0], (f // tn) * n_tiles)
    n_cast = w2_flat.shape[0] // cast_rows
    last = lambda i, meta: jnp.maximum(jnp.minimum(i, meta[1, 0] - 1), 0)
    cast_blk = lambda j, i, meta: (jnp.minimum(j * n_tiles + i, n_cast - 1), 0)
    up, w2b = pl.pallas_call(
        _moe_up_kernel,
        grid_spec=pltpu.PrefetchScalarGridSpec(
            num_scalar_prefetch=1,
            grid=(f // tn, n_tiles),
            in_specs=[pl.BlockSpec((tile, d), lambda j, i, meta: (last(i, meta), 0)),
                      pl.BlockSpec((None, d, tn), lambda j, i, meta: (meta[0, i], 0, j)),
                      pl.BlockSpec((None, d, tn), lambda j, i, meta: (meta[0, i], 0, j)),
                      pl.BlockSpec((cast_rows, w2_flat.shape[1]), cast_blk)],
            out_specs=[pl.BlockSpec((tile, tn), lambda j, i, meta: (i, j)),
                       pl.BlockSpec((cast_rows, w2_flat.shape[1]), cast_blk)]),
        out_shape=[jax.ShapeDtypeStruct((rows, f), BF16),
                   jax.ShapeDtypeStruct(w2_flat.shape, BF16)],
        compiler_params=_params("arbitrary", "arbitrary"),
        name="moe_up",
    )(meta, xs, w1, w3, w2_flat)
    return up, w2b.reshape(w2.shape)


def _moe_down_kernel(meta_ref, a_ref, w_ref, o_ref):
    i = pl.program_id(0)
    k = pl.program_id(1)

    filled = meta_ref[2, i]
    tile = o_ref.shape[0]
    cn = COL_CHUNK
    for rows in _row_options(tile):
        in_range = (filled > rows - tile // ROW_QUARTERS) & (filled <= rows)

        @pl.when(in_range & (k == 0))
        def _(rows=rows):
            a = a_ref[0:rows, :]
            for c in range(0, o_ref.shape[1], cn):
                o_ref[0:rows, c:c + cn] = _dot(a, w_ref[:, c:c + cn])
            if rows < tile:
                o_ref[rows:tile, :] = jnp.zeros((tile - rows, o_ref.shape[1]), o_ref.dtype)

        @pl.when(in_range & (k > 0))
        def _(rows=rows):
            a = a_ref[0:rows, :]
            for c in range(0, o_ref.shape[1], cn):
                o_ref[0:rows, c:c + cn] += _dot(a, w_ref[:, c:c + cn])

    @pl.when((filled == 0) & (k == 0))
    def _():
        o_ref[...] = jnp.zeros_like(o_ref)


def moe_down(meta, up, w2, tile, tk=1792):
    rows, f = up.shape
    d = w2.shape[2]
    assert f % tk == 0 and rows % tile == 0
    n_tiles = rows // tile
    last = lambda i, meta: jnp.maximum(jnp.minimum(i, meta[1, 0] - 1), 0)
    return pl.pallas_call(
        _moe_down_kernel,
        grid_spec=pltpu.PrefetchScalarGridSpec(
            num_scalar_prefetch=1,
            grid=(n_tiles, f // tk),
            in_specs=[pl.BlockSpec((tile, tk), lambda i, k, meta: (last(i, meta), k)),
                      pl.BlockSpec((None, tk, d), lambda i, k, meta: (meta[0, i], k, 0))],
            out_specs=pl.BlockSpec((tile, d), lambda i, k, meta: (i, 0))),
        out_shape=jax.ShapeDtypeStruct((rows, d), F32),
        compiler_params=_params("arbitrary", "arbitrary"),
        name="moe_down",
    )(meta, up, w2)


def _moe_combine_kernel(pos_ref, x_ref, wt_ref, g_ref, y_hbm, o_ref, buf_ref, sem_ref, *, final_norm):
    i = pl.program_id(0)
    n = pl.num_programs(0)
    tc = x_ref.shape[0]
    m = pos_ref.shape[0] // 2

    def start(step, slot):
        for s in range(2):
            _row_gather_start(y_hbm, buf_ref.at[slot, s], pos_ref, s * m + step * tc, tc, sem_ref.at[slot, s])

    @pl.when(i == 0)
    def _():
        start(0, 0)

    @pl.when(i + 1 < n)
    def _():
        start(i + 1, (i + 1) % 2)

    cur = i % 2
    for s in range(2):
        _row_gather_wait(y_hbm, buf_ref.at[cur, s], sem_ref.at[cur, s])
    wt = wt_ref[...]
    out = x_ref[...] + wt[:, 0:1] * buf_ref[cur, 0] + wt[:, 1:2] * buf_ref[cur, 1]
    o_ref[...] = _rms(out, g_ref[...]) if final_norm else out


def moe_combine(pos2, x, wt, y, norm_g=None, tc=256):
    m, d = x.shape
    final_norm = norm_g is not None
    g = (norm_g if final_norm else jnp.ones((d,), F32)).reshape(1, d)
    return pl.pallas_call(
        functools.partial(_moe_combine_kernel, final_norm=final_norm),
        grid_spec=pltpu.PrefetchScalarGridSpec(
            num_scalar_prefetch=1,
            grid=(m // tc,),
            in_specs=[pl.BlockSpec((tc, d), lambda i, pos: (i, 0)),
                      pl.BlockSpec((tc, wt.shape[1]), lambda i, pos: (i, 0)),
                      pl.BlockSpec((1, d), lambda i, pos: (0, 0)),
                      pl.BlockSpec(memory_space=pl.ANY)],
            out_specs=pl.BlockSpec((tc, d), lambda i, pos: (i, 0)),
            scratch_shapes=[pltpu.VMEM((2, 2, tc, d), F32),
                            pltpu.SemaphoreType.DMA((2, 2))]),
        out_shape=jax.ShapeDtypeStruct((m, d), F32),
        compiler_params=_params("arbitrary"),
        name="moe_combine",
    )(pos2, x, wt, g, y)


def moe_layer(x, norm_g, router_w, router_b, w1, w3, w2, out_norm_g=None):
    m, d = x.shape
    n_e = router_w.shape[1]
    tile = MOE_TILE
    n_tiles = 2 * m // tile + n_e
    h, sel, w = norm_router(x, norm_g, router_w, router_b)
    pos, wt, meta = moe_route(sel, w, tile)
    pos2 = pos[:2].reshape(-1)
    xs = moe_gather(pos2, meta, h, n_tiles, tile)
    up, w2_bf16 = moe_up(meta, xs, w1, w3, w2, tile)
    y = moe_down(meta, up, w2_bf16, tile)
    return moe_combine(pos2, x, wt, y, out_norm_g)


def _mixer(x, batch, layer, norm_g, w_in_t, pe_k, pe_v, kw1, kw2, vw1, vw2, ln_g, ln_b, w_s, b_s, w_a, w_b, w_out,
           round_src=None):
    h = rmsnorm(x, norm_g, BF16)
    qt, kvr, vt, ngt, rest = in_projection(h, w_in_t, layer)
    cmp_kv = compress(kvr, batch, pe_k, pe_v, kw1, kw2, vw1, vw2)
    oct, sel = compressed_and_select(qt, cmp_kv, batch)
    oa, rounded = selected_window_attention(qt, kvr, vt, sel, oct, ngt, batch, round_src)
    ob = spatial_gating(rest, ln_g, ln_b, w_s, b_s)
    y = merge_branches(oa, ob, rest, w_a, w_b, layer)
    return residual_matmul("out_proj", y, w_out, layer, x), rounded


def kernel(x, norm_mix, w_in, cmp_pe_k, cmp_pe_v, cmp_k_w1, cmp_k_w2, cmp_v_w1, cmp_v_w2, sgu_ln_g, sgu_ln_b, sgu_w, sgu_b, w_branch_a, w_branch_b, w_out, norm_ffn, ffn_w1, ffn_w3, ffn_w2, router_w, router_b, moe_w1, moe_w3, moe_w2, norm_f):
    batch, seq, d = x.shape
    depth = norm_mix.shape[0]
    xf = x.reshape(batch * seq, d)
    w_in_t = jnp.swapaxes(w_in, 1, 2)
    normed = False
    n_moe = moe_w1.shape[0]
    moe_up_bf16 = {}
    for layer in range(depth):
        j = layer // 2
        round_src = None
        if j < n_moe and 2 * j + 1 < depth:
            w_f32 = moe_w1[j] if layer % 2 == 0 else moe_w3[j]
            round_src = w_f32.reshape(-1, w_f32.shape[-1])
        xf, rounded = _mixer(xf, batch, layer, norm_mix[layer], w_in_t, cmp_pe_k[layer], cmp_pe_v[layer],
                             cmp_k_w1[layer], cmp_k_w2[layer], cmp_v_w1[layer], cmp_v_w2[layer],
                             sgu_ln_g[layer], sgu_ln_b[layer], sgu_w[layer], sgu_b[layer],
                             w_branch_a, w_branch_b, w_out, round_src)
        if rounded is not None:
            moe_up_bf16[layer % 2] = rounded.reshape(moe_w1[j].shape)
        if layer % 2 == 0:
            h = rmsnorm(xf, norm_ffn[layer], BF16)
            up = swiglu_up(h, ffn_w1[j], ffn_w3[j])
            xf = residual_matmul("ffn_down", up, ffn_w2, j, xf, tm=512, tn=512)
        else:
            normed = layer == depth - 1
            xf = moe_layer(xf, norm_ffn[layer], router_w[j], router_b[j], moe_up_bf16[0], moe_up_bf16[1],
                           moe_w2[j], norm_f if normed else None)
    if not normed:
        xf = rmsnorm(xf, norm_f, F32)
    return xf.reshape(batch, seq, d)
```

```python
import functools

import numpy as np
import jax
import jax.numpy as jnp
from jax import lax
from jax.experimental import pallas as pl
from jax.experimental.pallas import tpu as pltpu

F32 = jnp.float32
BF16 = jnp.bfloat16

D_MODEL = 2048
N_Q_HEADS = 16
N_KV_HEADS = 4
HEAD_DIM = 64
Q_PER_KV = N_Q_HEADS // N_KV_HEADS
NSA_WIDTH = N_Q_HEADS * HEAD_DIM
KV_WIDTH = N_KV_HEADS * HEAD_DIM
GROUP_WIDTH = Q_PER_KV * HEAD_DIM
CMP_BLOCK = 32
CMP_STRIDE = 16
CMP_HIDDEN = 256
SLC_BLOCK = 64
N_SELECT = 16
WINDOW = 512
FORCED_SCORE = 1e4
SGU_WIDTH = D_MODEL // 2
SGU_GROUPS = 8
SGU_CHUNK = 128
N_EXPERTS = 8
NORM_EPS = 1e-5
NEG = -1e30
LOG2E = 1.4426950408889634
LANES = 128
VMEM_LIMIT = 56 * 1024 * 1024

Q_OFF = 0
KV_OFF = NSA_WIDTH
NG_OFF = KV_OFF + 6 * KV_WIDTH
NG_WIDTH = 3 * N_Q_HEADS
REST_OFF = NG_OFF + NG_WIDTH


def _params(*sem):
    return pltpu.CompilerParams(dimension_semantics=sem, vmem_limit_bytes=VMEM_LIMIT)


def _dot(a, b):
    return jnp.dot(a, b, preferred_element_type=F32)


def _dot_nt(a, b):
    return lax.dot_general(a, b, (((1,), (1,)), ((), ())), preferred_element_type=F32)


def _dot_split(a_f32, b_bf16):
    hi = a_f32.astype(BF16)
    lo = (a_f32 - hi.astype(F32)).astype(BF16)
    return _dot(hi, b_bf16) + _dot(lo, b_bf16)


def _sigmoid(x):
    return 1.0 / (1.0 + jnp.exp(-x))


def _gelu(x):
    return x * (0.5 * (1.0 + jnp.tanh(0.7978845608028654 * (x + 0.044715 * (x * x * x)))))


def _rms(x, g):
    return x * lax.rsqrt(jnp.mean(x * x, axis=-1, keepdims=True) + NORM_EPS) * g


def _rmsnorm_kernel(x_ref, g_ref, o_ref):
    o_ref[...] = _rms(x_ref[...], g_ref[...]).astype(o_ref.dtype)


def rmsnorm(x, g, out_dtype, tm=512):
    m, d = x.shape
    return pl.pallas_call(
        _rmsnorm_kernel,
        grid=(m // tm,),
        in_specs=[pl.BlockSpec((tm, d), lambda i: (i, 0)),
                  pl.BlockSpec((1, d), lambda i: (0, 0))],
        out_specs=pl.BlockSpec((tm, d), lambda i: (i, 0)),
        out_shape=jax.ShapeDtypeStruct((m, d), out_dtype),
        compiler_params=_params("parallel"),
        name="rmsnorm",
    )(x, g.reshape(1, d))


def _norm_router_kernel(x_ref, g_ref, rwt_ref, rb_ref, h_ref, sel_ref, w_ref):
    h = _rms(x_ref[...], g_ref[...])
    h_ref[...] = h
    logits = lax.dot_general(rwt_ref[...], h, (((1,), (1,)), ((), ())), preferred_element_type=F32,
                             precision=lax.Precision.HIGHEST) + rb_ref[...]
    n_e = logits.shape[0]
    e = lax.broadcasted_iota(jnp.int32, logits.shape, 0)
    m1 = jnp.max(logits, axis=0, keepdims=True)
    i1 = jnp.min(jnp.where(logits == m1, e, n_e), axis=0, keepdims=True)
    rest = jnp.where(e == i1, -jnp.inf, logits)
    m2 = jnp.max(rest, axis=0, keepdims=True)
    i2 = jnp.min(jnp.where(rest == m2, e, n_e), axis=0, keepdims=True)
    e2 = jnp.exp(m2 - m1)
    w1 = 1.0 / (1.0 + e2)
    w2 = e2 / (1.0 + e2)
    sel_ref[...] = jnp.where((e == i1) | (e == i2), 1.0, 0.0)
    w_ref[...] = jnp.where(e == i1, w1, 0.0) + jnp.where(e == i2, w2, 0.0)


def norm_router(x, g, router_w, router_b, tm=512):
    m, d = x.shape
    n_e = router_w.shape[1]
    return pl.pallas_call(
        _norm_router_kernel,
        grid=(m // tm,),
        in_specs=[pl.BlockSpec((tm, d), lambda i: (i, 0)),
                  pl.BlockSpec((1, d), lambda i: (0, 0)),
                  pl.BlockSpec((n_e, d), lambda i: (0, 0)),
                  pl.BlockSpec((n_e, 1), lambda i: (0, 0))],
        out_specs=[pl.BlockSpec((tm, d), lambda i: (i, 0)),
                   pl.BlockSpec((n_e, tm), lambda i: (0, i)),
                   pl.BlockSpec((n_e, tm), lambda i: (0, i))],
        out_shape=[jax.ShapeDtypeStruct((m, d), F32),
                   jax.ShapeDtypeStruct((n_e, m), F32),
                   jax.ShapeDtypeStruct((n_e, m), F32)],
        compiler_params=_params("parallel"),
        name="norm_router",
    )(x, g.reshape(1, d), router_w.T, router_b.reshape(n_e, 1))


def _mm_full_kernel(*refs, n_a, n_b, n_e, n_o, pairs, inner_axis, epilogue, b_transposed):
    a_refs = refs[:n_a]
    b_refs = refs[n_a:n_a + n_b]
    e_refs = refs[n_a + n_b:n_a + n_b + n_e]
    o_refs = refs[n_a + n_b + n_e:n_a + n_b + n_e + n_o]
    w_refs = refs[n_a + n_b + n_e + n_o:]

    @pl.when(pl.program_id(inner_axis) == 0)
    def _():
        for b_ref, w_ref in zip(b_refs, w_refs):
            w_ref[...] = b_ref[...].astype(BF16)

    dot = _dot_nt if b_transposed else _dot
    ws = [w_ref[0] if len(w_ref.shape) == 3 else w_ref[...] for w_ref in w_refs]
    accs = [dot(a_refs[ia][...], ws[ib]) for ia, ib in pairs]
    epilogue(accs, e_refs, o_refs)


def _block_dim_size(s):
    return s.block_size if isinstance(s, pl.Element) else s


def mm_full(name, grid, a_ops, b_ops, e_ops, outs, pairs, epilogue, b_transposed=False):
    ops = a_ops + b_ops + e_ops
    kernel = functools.partial(
        _mm_full_kernel, n_a=len(a_ops), n_b=len(b_ops), n_e=len(e_ops), n_o=len(outs),
        pairs=pairs, inner_axis=len(grid) - 1, epilogue=epilogue, b_transposed=b_transposed)
    scratch = [pltpu.VMEM(tuple(_block_dim_size(s) for s in blk if s is not None), BF16) for _, blk, _ in b_ops]
    res = pl.pallas_call(
        kernel,
        grid=grid,
        in_specs=[pl.BlockSpec(blk, imap) for _, blk, imap in ops],
        out_specs=[pl.BlockSpec(blk, imap) for _, blk, imap in outs],
        out_shape=[sds for sds, _, _ in outs],
        scratch_shapes=scratch,
        compiler_params=_params(*(("arbitrary",) * len(grid))),
        name=name,
    )(*[arr for arr, _, _ in ops])
    return res


COL_CHUNK = 256


def in_projection(h, w_in_t, layer):
    m, d = h.shape
    tm = 1024
    a_op = [(h, (tm, d), lambda j, i: (i, 0))]

    tq = 512
    q_heads = tq // HEAD_DIM
    q_scale = float(np.float32(HEAD_DIM ** -0.5) * np.float32(LOG2E))

    def q_epi(accs, e_refs, o_refs):
        for c in range(q_heads):
            o_refs[0][c] = (accs[0][:, c * HEAD_DIM:(c + 1) * HEAD_DIM] * q_scale).T.astype(BF16)

    (qt,) = mm_full(
        "in_proj_q", (NSA_WIDTH // tq, m // tm), a_op,
        [(w_in_t, (None, tq, d), lambda j, i: (layer, j, 0))], [],
        [(jax.ShapeDtypeStruct((N_Q_HEADS, HEAD_DIM, m), BF16), (q_heads, HEAD_DIM, tm), lambda j, i: (j, 0, i))],
        [(0, 0)], q_epi, b_transposed=True)

    sec0 = KV_OFF // KV_WIDTH

    def rows_epi(accs, e_refs, o_refs):
        for c in range(N_KV_HEADS):
            o_refs[0][c] = accs[0][:, c * HEAD_DIM:(c + 1) * HEAD_DIM].astype(BF16)

    (kvr,) = mm_full(
        "in_proj_kv_rows", (4, m // tm), a_op,
        [(w_in_t, (None, KV_WIDTH, d), lambda j, i: (layer, sec0 + j + j // 3, 0))], [],
        [(jax.ShapeDtypeStruct((4 * N_KV_HEADS, m, HEAD_DIM), BF16),
          (N_KV_HEADS, tm, HEAD_DIM), lambda j, i: (j, i, 0))],
        [(0, 0)], rows_epi, b_transposed=True)

    def cols_epi(accs, e_refs, o_refs):
        for c in range(N_KV_HEADS):
            o_refs[0][c] = accs[0][:, c * HEAD_DIM:(c + 1) * HEAD_DIM].T.astype(BF16)

    (vt,) = mm_full(
        "in_proj_v_cols", (2, m // tm), a_op,
        [(w_in_t, (None, KV_WIDTH, d), lambda j, i: (layer, sec0 + 3 + 2 * j, 0))], [],
        [(jax.ShapeDtypeStruct((2 * N_KV_HEADS, HEAD_DIM, m), BF16),
          (N_KV_HEADS, HEAD_DIM, tm), lambda j, i: (j, 0, i))],
        [(0, 0)], cols_epi, b_transposed=True)

    def f32_epi(accs, e_refs, o_refs):
        o_refs[0][...] = accs[0]

    def f32_t_epi(accs, e_refs, o_refs):
        o_refs[0][...] = accs[0].T

    assert NG_OFF % LANES == 0
    (ngt,) = mm_full(
        "in_proj_ng", (1, m // tm), a_op,
        [(w_in_t, (None, LANES, d), lambda j, i: (layer, NG_OFF // LANES, 0))], [],
        [(jax.ShapeDtypeStruct((LANES, m), F32), (LANES, tm), lambda j, i: (0, i))],
        [(0, 0)], f32_t_epi, b_transposed=True)

    n_rest = w_in_t.shape[1] - REST_OFF
    tr = 1024
    assert n_rest % tr == 0 and REST_OFF % 8 == 0
    (rest,) = mm_full(
        "in_proj_rest", (n_rest // tr, m // tm), a_op,
        [(w_in_t, (pl.Element(1), pl.Element(tr), pl.Element(d)), lambda j, i: (layer, pl.multiple_of(REST_OFF + j * tr, 8), 0))], [],
        [(jax.ShapeDtypeStruct((m, n_rest), F32), (tm, tr), lambda j, i: (i, j))],
        [(0, 0)], f32_epi, b_transposed=True)
    return qt, kvr, vt, ngt, rest


def _cmp_kernel(a_ref, w1_ref, w2_ref, pe_ref, o_ref):
    a = a_ref[...]
    rows = a.shape[0]
    half = a.shape[1]
    w1 = w1_ref[...].astype(BF16)
    p0 = _dot(a, w1[:half])
    p1 = _dot(a, w1[half:])
    bias = _dot(pe_ref[...].astype(BF16), w1)[0:1]
    hid = p0 + pltpu.roll(p1, rows - 1, 0) + bias
    o_ref[...] = _dot(_gelu(hid).astype(BF16), w2_ref[...].astype(BF16))


def compress(kv, batch, pe_k, pe_v, kw1, kw2, vw1, vw2):
    m = kv.shape[1]
    n_chunks = m // batch // CMP_STRIDE
    rows = batch * n_chunks
    feat = CMP_STRIDE * HEAD_DIM
    a = kv[:2 * N_KV_HEADS].reshape(2, N_KV_HEADS, rows, feat)
    w1 = jnp.stack([kw1, vw1])
    w2 = jnp.stack([kw2, vw2])
    pe = jnp.stack([pe_k, pe_v]).reshape(2, 1, CMP_BLOCK * HEAD_DIM)
    pe = jnp.broadcast_to(pe, (2, 8, CMP_BLOCK * HEAD_DIM))
    return pl.pallas_call(
        _cmp_kernel,
        grid=(2, N_KV_HEADS),
        in_specs=[pl.BlockSpec((None, None, rows, feat), lambda s, g: (s, g, 0, 0)),
                  pl.BlockSpec((None, 2 * feat, CMP_HIDDEN), lambda s, g: (s, 0, 0)),
                  pl.BlockSpec((None, CMP_HIDDEN, HEAD_DIM), lambda s, g: (s, 0, 0)),
                  pl.BlockSpec((None, 8, 2 * feat), lambda s, g: (s, 0, 0))],
        out_specs=pl.BlockSpec((None, None, rows, HEAD_DIM), lambda s, g: (s, g, 0, 0)),
        out_shape=jax.ShapeDtypeStruct((2, N_KV_HEADS, rows, HEAD_DIM), F32),
        compiler_params=_params("parallel", "parallel"),
        name="nsa_compress",
    )(a, w1, w2, pe)


def _alibi_slopes():
    sl = np.array([2.0 ** (-8.0 * (h + 1) / N_Q_HEADS) for h in range(N_Q_HEADS)], dtype=np.float32)
    return sl * np.float32(LOG2E)


def _overlap_t(n_cmp_pad, n_slc):
    cs = np.arange(n_cmp_pad)[None, :] * CMP_STRIDE
    ss = np.arange(n_slc)[:, None] * SLC_BLOCK
    ov = np.clip(np.minimum(cs + CMP_BLOCK, ss + SLC_BLOCK) - np.maximum(cs, ss), 0, None)
    return (ov / CMP_STRIDE).astype(np.float32)


def _sel_kernel(slopes_ref, q_ref, kc_ref, vc_ref, ovt_ref, oc_ref, sel_ref, *, n_cmp):
    g = pl.program_id(1)
    i = pl.program_id(2)
    tq = q_ref.shape[2]
    ncp = kc_ref.shape[0]
    n_slc = ovt_ref.shape[0]
    kc = kc_ref[...].astype(BF16)
    vct = vc_ref[...].T.astype(BF16)

    t = i * tq + lax.broadcasted_iota(jnp.int32, (ncp, tq), 1)
    c = lax.broadcasted_iota(jnp.int32, (ncp, tq), 0)
    dist = t - (c * CMP_STRIDE + (CMP_BLOCK - 1))
    mask = (dist >= 0) & (c < n_cmp)
    distf = dist.astype(F32)

    scores = [_dot(kc, q_ref[j]) for j in range(Q_PER_KV)]
    probs = []
    for j in range(Q_PER_KV):
        s = scores[j] - slopes_ref[g * Q_PER_KV + j] * distf
        s = jnp.where(mask, s, NEG)
        mx = jnp.max(s, axis=0, keepdims=True)
        e = jnp.where(mask, jnp.exp2(s - mx), 0.0)
        probs.append(e / jnp.maximum(jnp.sum(e, axis=0, keepdims=True), 1e-30))
    for j in range(Q_PER_KV):
        oc_ref[j] = _dot(vct, probs[j].astype(BF16))
    p_sum = functools.reduce(lambda a, b: a + b, probs)

    ovt = ovt_ref[...]
    hi = p_sum.astype(BF16)
    lo = (p_sum - hi.astype(F32)).astype(BF16)
    imp = _dot(ovt, hi) + _dot(ovt, lo)
    tt = i * tq + lax.broadcasted_iota(jnp.int32, (n_slc, tq), 1)
    blk = lax.broadcasted_iota(jnp.int32, (n_slc, tq), 0)
    cur = tt // SLC_BLOCK
    valid = blk * SLC_BLOCK <= tt
    forced = (blk == 0) | (blk == cur) | (blk == cur - 1)
    score = jnp.where(valid, imp, -1.0)
    score = jnp.where(forced, FORCED_SCORE, score)
    rank = jnp.zeros((n_slc, tq), F32)
    for mrow in range(n_slc):
        other = jnp.broadcast_to(score[mrow:mrow + 1, :], (n_slc, tq))
        beats = (other > score) | ((other == score) & (blk > mrow))
        rank = rank + jnp.where(beats, 1.0, 0.0)
    sel_ref[...] = jnp.where(rank < float(min(N_SELECT, n_slc)), 1.0, 0.0)


def compressed_and_select(qt, cmp_kv, batch, tq=512):
    m = qt.shape[2]
    seq = m // batch
    nq = seq // tq
    ncp = cmp_kv.shape[2] // batch
    n_slc = seq // SLC_BLOCK
    ovt = jnp.asarray(_overlap_t(ncp, n_slc), BF16)
    slopes = jnp.asarray(_alibi_slopes())
    kernel = functools.partial(_sel_kernel, n_cmp=ncp - 1)
    return pl.pallas_call(
        kernel,
        grid=(batch, N_KV_HEADS, nq),
        in_specs=[pl.BlockSpec(memory_space=pltpu.SMEM),
                  pl.BlockSpec((Q_PER_KV, HEAD_DIM, tq), lambda b, g, i: (g, 0, b * nq + i)),
                  pl.BlockSpec((None, None, ncp, HEAD_DIM), lambda b, g, i: (0, g, b, 0)),
                  pl.BlockSpec((None, None, ncp, HEAD_DIM), lambda b, g, i: (1, g, b, 0)),
                  pl.BlockSpec((n_slc, ncp), lambda b, g, i: (0, 0))],
        out_specs=[pl.BlockSpec((Q_PER_KV, HEAD_DIM, tq), lambda b, g, i: (g, 0, b * nq + i)),
                   pl.BlockSpec((None, None, n_slc, tq), lambda b, g, i: (b, g, 0, i))],
        out_shape=[jax.ShapeDtypeStruct((N_Q_HEADS, HEAD_DIM, m), F32),
                   jax.ShapeDtypeStruct((batch, N_KV_HEADS, n_slc, seq), F32)],
        compiler_params=_params("parallel", "parallel", "parallel"),
        name="nsa_compressed_select",
    )(slopes, qt, cmp_kv, cmp_kv, ovt)


AUX_SLOPE = HEAD_DIM
AUX_SEL = HEAD_DIM + 16
ATTN_K = 2 * HEAD_DIM
STRIP = 32


def _attn_kernel(slopes_ref, q_ref, ks_ref, vs_ref, kw_ref, vw_ref, auxs_ref, auxw_ref, sel_ref, oc_ref,
                 ng_ref, *rest, round_weights):
    if round_weights:
        wsrc_ref, o_ref, wdst_ref = rest[:3]
        scratch = rest[3:]

        def side_work():
            wdst_ref[...] = wsrc_ref[...].astype(BF16)
    else:
        o_ref = rest[0]
        scratch = rest[1:]
        side_work = lambda: None
    _attn_body(slopes_ref, q_ref, ks_ref, vs_ref, kw_ref, vw_ref, auxs_ref, auxw_ref, sel_ref, oc_ref, ng_ref,
               o_ref, *scratch, side_work=side_work)


def _attn_body(slopes_ref, q_ref, ks_ref, vs_ref, kw_ref, vw_ref, auxs_ref, auxw_ref, sel_ref, oc_ref,
               ng_ref, o_ref, ka_s_ref, ka_w_ref, qa_ref, s_ref, p_ref, mask_ref, m_ref, l_ref, acc_ref,
               os_ref, sig_ref, *, side_work):
    g = pl.program_id(1)
    i = pl.program_id(2)
    tq = q_ref.shape[2]
    tk = tq
    q0 = i * tq
    n_slc = sel_ref.shape[0]

    @pl.when(i == 0)
    def _():
        ka_s_ref[:, 0:HEAD_DIM] = ks_ref[...]
        ka_s_ref[:, HEAD_DIM:ATTN_K] = auxs_ref[...]
        ka_w_ref[:, 0:HEAD_DIM] = kw_ref[...]
        ka_w_ref[:, HEAD_DIM:ATTN_K] = auxw_ref[...]
        kr = lax.broadcasted_iota(jnp.int32, (tk, tq), 0)
        qc = lax.broadcasted_iota(jnp.int32, (tk, tq), 1)
        mask_ref[0] = jnp.where(kr <= qc, 0.0, NEG)
        mask_ref[1] = jnp.where(kr > qc, 0.0, NEG)

    sig_ref[...] = _sigmoid(ng_ref[...])
    side_work()
    sel_neg =(sel_ref[...] - 1.0) * (-NEG)
    row16 = lax.broadcasted_iota(jnp.int32, (16, tq), 0)
    pad = jnp.zeros((ATTN_K - AUX_SEL - n_slc, tq), F32)
    for j in range(Q_PER_KV):
        sl = jnp.full((16, tq), slopes_ref[g * Q_PER_KV + j], F32)
        hi = sl.astype(BF16).astype(F32)
        mid = (sl - hi).astype(BF16).astype(F32)
        lo = (sl - hi - mid).astype(BF16).astype(F32)
        pieces = jnp.where(row16 == 0, hi, jnp.where(row16 == 1, mid, jnp.where(row16 == 2, lo, 0.0)))
        qa = jnp.concatenate([q_ref[j].astype(F32), pieces, sel_neg, pad], axis=0)
        qa_ref[j] = qa.astype(BF16)

    def scores(ka_ref, k0, buf):
        ka = ka_ref[pl.ds(k0, tk), :]
        for j in range(Q_PER_KV):
            s_ref[buf, j] = _dot(ka, qa_ref[j])

    def softmax_pv(v_ref, k0, buf, mask_idx, first):
        vc = v_ref[:, pl.ds(k0, tk)]
        for j in range(Q_PER_KV):
            shift = slopes_ref[g * Q_PER_KV + j] * k0.astype(F32)
            mx = None
            for r in range(0, tk, STRIP):
                x = s_ref[buf, j, r:r + STRIP, :]
                if mask_idx is not None:
                    x = x + mask_ref[mask_idx, r:r + STRIP, :]
                    s_ref[buf, j, r:r + STRIP, :] = x
                mx = x if mx is None else jnp.maximum(mx, x)
            m_cur = jnp.max(mx, axis=0, keepdims=True) + shift
            if first:
                m_new = m_cur
            else:
                m_old = m_ref[j]
                m_new = jnp.maximum(m_old, m_cur)
                alpha = jnp.exp2(m_old - m_new)
            m_ref[j] = m_new
            m_loc = m_new - shift
            ls = None
            for r in range(0, tk, STRIP):
                p = jnp.exp2(s_ref[buf, j, r:r + STRIP, :] - m_loc)
                ls = p if ls is None else ls + p
                p_ref[j, r:r + STRIP, :] = p.astype(BF16)
            l_cur = jnp.sum(ls, axis=0, keepdims=True)
            pv = _dot(vc, p_ref[j])
            if first:
                l_ref[j] = l_cur
                acc_ref[j] = pv
            else:
                l_ref[j] = alpha * l_ref[j] + l_cur
                acc_ref[j] = alpha * acc_ref[j] + pv

    k_diag = pl.multiple_of(q0, tk)
    last = jnp.maximum(i - 1, 0)
    k_of = lambda c: pl.multiple_of(jnp.minimum(c, last) * tk, tk)
    scores(ka_s_ref, k_diag, 1)
    scores(ka_s_ref, k_of(0), 0)
    softmax_pv(vs_ref, k_diag, 1, 0, True)

    def slc_pair(cc, carry):
        c0 = 2 * cc
        scores(ka_s_ref, k_of(c0 + 1), 1)
        softmax_pv(vs_ref, k_of(c0), 0, None, False)

        @pl.when(c0 + 1 < i)
        def _():
            scores(ka_s_ref, k_of(c0 + 2), 0)
            softmax_pv(vs_ref, k_of(c0 + 1), 1, None, False)
        return carry

    lax.fori_loop(0, (i + 1) // 2, slc_pair, 0)
    for j in range(Q_PER_KV):
        os_ref[j] = acc_ref[j] / l_ref[j]

    n_back = WINDOW // tk
    back_k0 = lambda back: pl.multiple_of(jnp.maximum(q0 - back * tk, 0), tk)
    scores(ka_w_ref, k_diag, 0)
    scores(ka_w_ref, back_k0(1), 1)
    softmax_pv(vw_ref, k_diag, 0, 0, True)
    for back in range(1, n_back + 1):
        @pl.when(i >= back)
        def _(back=back):
            if back < n_back:
                scores(ka_w_ref, back_k0(back + 1), (back + 1) % 2)
            softmax_pv(vw_ref, back_k0(back), back % 2, 1 if back == n_back else None, False)

    for j in range(Q_PER_KV):
        base = (g * Q_PER_KV + j) * 3
        out_t = (sig_ref[pl.ds(base, 1), :] * oc_ref[j]
                 + sig_ref[pl.ds(base + 1, 1), :] * os_ref[j]
                 + sig_ref[pl.ds(base + 2, 1), :] * (acc_ref[j] / l_ref[j]))
        o_ref[:, j * HEAD_DIM:(j + 1) * HEAD_DIM] = out_t.T.astype(o_ref.dtype)


def _key_aux(seq, tk, with_blocks):
    aux = np.zeros((seq, ATTN_K - HEAD_DIM), np.float32)
    pos = np.arange(seq)
    aux[:, AUX_SLOPE - HEAD_DIM:AUX_SLOPE - HEAD_DIM + 3] = (pos % tk)[:, None]
    if with_blocks:
        aux[pos, AUX_SEL - HEAD_DIM + pos // SLC_BLOCK] = 1.0
    return aux


def selected_window_attention(qt, kvr, vt, sel, oct, ngt, batch, round_src=None, tq=256):
    m = qt.shape[2]
    seq = m // batch
    nq = seq // tq
    n_slc = seq // SLC_BLOCK
    assert tq <= 256 and AUX_SEL + n_slc <= ATTN_K
    aux_s = jnp.asarray(_key_aux(seq, tq, True), BF16)
    aux_w = jnp.asarray(_key_aux(seq, tq, False), BF16)
    slopes = jnp.asarray(_alibi_slopes())
    n_g = N_KV_HEADS

    def k_spec(section):
        return pl.BlockSpec((None, seq, HEAD_DIM), lambda b, g, i: (section * n_g + g, b, 0))

    def v_spec(section):
        return pl.BlockSpec((None, HEAD_DIM, seq), lambda b, g, i: (section * n_g + g, 0, b))

    head_blk = pl.BlockSpec((Q_PER_KV, HEAD_DIM, tq), lambda b, g, i: (g, 0, b * nq + i))
    aux_blk = pl.BlockSpec((seq, ATTN_K - HEAD_DIM), lambda b, g, i: (0, 0))
    in_specs = [pl.BlockSpec(memory_space=pltpu.SMEM),
                head_blk,
                k_spec(2), v_spec(0), k_spec(3), v_spec(1),
                aux_blk, aux_blk,
                pl.BlockSpec((None, None, n_slc, tq), lambda b, g, i: (b, g, 0, i)),
                head_blk,
                pl.BlockSpec((LANES, tq), lambda b, g, i: (0, b * nq + i))]
    out_specs = [pl.BlockSpec((tq, GROUP_WIDTH), lambda b, g, i: (b * nq + i, g))]
    out_shape = [jax.ShapeDtypeStruct((m, NSA_WIDTH), BF16)]
    operands = [slopes, qt, kvr, vt, kvr, vt, aux_s, aux_w, sel, oct, ngt]
    if round_src is not None:
        steps = batch * N_KV_HEADS * nq
        rows, cols = round_src.shape
        assert rows % steps == 0 and (rows // steps) % BF16_SUBLANES == 0
        side_blk = pl.BlockSpec((rows // steps, cols), lambda b, g, i: ((b * N_KV_HEADS + g) * nq + i, 0))
        in_specs.append(side_blk)
        out_specs.append(side_blk)
        out_shape.append(jax.ShapeDtypeStruct((rows, cols), BF16))
        operands.append(round_src)
    res = pl.pallas_call(
        functools.partial(_attn_kernel, round_weights=round_src is not None),
        grid=(batch, N_KV_HEADS, nq),
        in_specs=in_specs,
        out_specs=out_specs,
        out_shape=out_shape,
        scratch_shapes=[pltpu.VMEM((seq, ATTN_K), BF16),
                        pltpu.VMEM((seq, ATTN_K), BF16),
                        pltpu.VMEM((Q_PER_KV, ATTN_K, tq), BF16),
                        pltpu.VMEM((2, Q_PER_KV, tq, tq), F32),
                        pltpu.VMEM((Q_PER_KV, tq, tq), BF16),
                        pltpu.VMEM((2, tq, tq), F32),
                        pltpu.VMEM((Q_PER_KV, 1, tq), F32),
                        pltpu.VMEM((Q_PER_KV, 1, tq), F32),
                        pltpu.VMEM((Q_PER_KV, HEAD_DIM, tq), F32),
                        pltpu.VMEM((Q_PER_KV, HEAD_DIM, tq), F32),
                        pltpu.VMEM((LANES, tq), F32)],
        compiler_params=_params("arbitrary", "arbitrary", "arbitrary"),
        name="nsa_selected_window",
    )(*operands)
    return (res[0], res[1]) if round_src is not None else (res[0], None)


def _sgu_kernel(u_ref, v_ref, lg_ref, lb_ref, ws_ref, bs_ref, o_ref):
    rows = u_ref.shape[0]
    v = _gelu(v_ref[...])
    mu = jnp.mean(v, axis=-1, keepdims=True)
    var = jnp.mean(jnp.square(v - mu), axis=-1, keepdims=True)
    vl = ((v - mu) * lax.rsqrt(var + NORM_EPS) * lg_ref[...] + lb_ref[...]).astype(BF16)
    r = lax.broadcasted_iota(jnp.int32, (SGU_CHUNK, SGU_CHUNK), 0)
    c = lax.broadcasted_iota(jnp.int32, (SGU_CHUNK, SGU_CHUNK), 1)
    gd = SGU_WIDTH // SGU_GROUPS
    for grp in range(SGU_GROUPS):
        w = jnp.where(c <= r, ws_ref[grp], 0.0).astype(BF16)
        lanes = slice(grp * gd, (grp + 1) * gd)
        for n in range(rows // SGU_CHUNK):
            rs = slice(n * SGU_CHUNK, (n + 1) * SGU_CHUNK)
            vm = _dot(w, vl[rs, lanes]) + bs_ref[:, lanes]
            o_ref[rs, lanes] = (_gelu(u_ref[rs, lanes]) * vm).astype(o_ref.dtype)


def spatial_gating(rest, ln_g, ln_b, w_s, b_s, tm=512):
    m = rest.shape[0]
    gd = SGU_WIDTH // SGU_GROUPS
    bias = jnp.repeat(b_s.T, gd, axis=1)
    return pl.pallas_call(
        _sgu_kernel,
        grid=(m // tm,),
        in_specs=[pl.BlockSpec((tm, SGU_WIDTH), lambda i: (i, 0)),
                  pl.BlockSpec((tm, SGU_WIDTH), lambda i: (i, 1)),
                  pl.BlockSpec((1, SGU_WIDTH), lambda i: (0, 0)),
                  pl.BlockSpec((1, SGU_WIDTH), lambda i: (0, 0)),
                  pl.BlockSpec((SGU_GROUPS, SGU_CHUNK, SGU_CHUNK), lambda i: (0, 0, 0)),
                  pl.BlockSpec((SGU_CHUNK, SGU_WIDTH), lambda i: (0, 0))],
        out_specs=pl.BlockSpec((tm, SGU_WIDTH), lambda i: (i, 0)),
        out_shape=jax.ShapeDtypeStruct((m, SGU_WIDTH), BF16),
        compiler_params=_params("parallel"),
        name="spatial_gating",
    )(rest, rest, ln_g.reshape(1, -1), ln_b.reshape(1, -1), w_s, bias)


def merge_branches(oa, ob, rest, w_a, w_b, layer, tm=512, tn=1024):
    m = oa.shape[0]
    d = w_a.shape[2]
    ga_blk = 2 * SGU_WIDTH // tn
    gb_blk = (2 * SGU_WIDTH + d) // tn

    def epi(accs, e_refs, o_refs):
        y = _sigmoid(e_refs[0][...]) * accs[0] + _sigmoid(e_refs[1][...]) * accs[1]
        o_refs[0][...] = y.astype(BF16)

    (y,) = mm_full(
        "merge_branches", (d // tn, m // tm),
        [(oa, (tm, oa.shape[1]), lambda j, i: (i, 0)), (ob, (tm, ob.shape[1]), lambda j, i: (i, 0))],
        [(w_a, (None, w_a.shape[1], tn), lambda j, i: (layer, 0, j)),
         (w_b, (None, w_b.shape[1], tn), lambda j, i: (layer, 0, j))],
        [(rest, (tm, tn), lambda j, i: (i, ga_blk + j)), (rest, (tm, tn), lambda j, i: (i, gb_blk + j))],
        [(jax.ShapeDtypeStruct((m, d), BF16), (tm, tn), lambda j, i: (i, j))],
        [(0, 0), (1, 1)], epi)
    return y


def residual_matmul(name, a, w, layer, x, tm=512, tn=1024):
    m, k = a.shape
    n = w.shape[2]

    def epi(accs, e_refs, o_refs):
        o_refs[0][...] = e_refs[0][...] + accs[0]

    (out,) = mm_full(
        name, (n // tn, m // tm),
        [(a, (tm, k), lambda j, i: (i, 0))],
        [(w, (None, k, tn), lambda j, i: (layer, 0, j))],
        [(x, (tm, tn), lambda j, i: (i, j))],
        [(jax.ShapeDtypeStruct((m, n), F32), (tm, tn), lambda j, i: (i, j))],
        [(0, 0)], epi)
    return out


def swiglu_up(h, w1, w3, tm=1024, tn=512):
    m, d = h.shape
    f = w1.shape[1]

    def epi(accs, e_refs, o_refs):
        o_refs[0][...] = (accs[0] * _sigmoid(accs[0]) * accs[1]).astype(BF16)

    (out,) = mm_full(
        "swiglu_up", (f // tn, m // tm),
        [(h, (tm, d), lambda j, i: (i, 0))],
        [(w1, (d, tn), lambda j, i: (0, j)), (w3, (d, tn), lambda j, i: (0, j))],
        [],
        [(jax.ShapeDtypeStruct((m, f), BF16), (tm, tn), lambda j, i: (i, j))],
        [(0, 0), (0, 1)], epi)
    return out


MOE_TILE = 512
ROW_QUARTERS = 4


def _row_options(tile):
    step = tile // ROW_QUARTERS
    return tuple(range(step, tile + 1, step))


def _route_kernel(sel_ref, w_ref, tri_ref, pos_ref, wt_ref, meta_ref, cum_ref, *, tile):
    n_e, m = sel_ref.shape
    ck = tri_ref.shape[0]
    carry = jnp.zeros((n_e, 1), F32)
    for c in range(m // ck):
        sl = slice(c * ck, (c + 1) * ck)
        cs = _dot(sel_ref[:, sl].astype(BF16), tri_ref[...]) + carry
        cum_ref[:, sl] = cs
        carry = cs[:, ck - 1:ck]
    padded = jnp.ceil(carry / tile) * tile
    sub = lax.broadcasted_iota(jnp.int32, (n_e, 1), 0)
    start = jnp.zeros((n_e, 1), F32)
    run = jnp.zeros((1, 1), F32)
    for ex in range(n_e):
        start = jnp.where(sub == ex, run, start)
        run = run + padded[ex:ex + 1, :]
    sel = sel_ref[...] > 0.5
    pos = start + cum_ref[...] - 1.0
    eidx = lax.broadcasted_iota(jnp.int32, (n_e, m), 0)
    e_lo = jnp.min(jnp.where(sel, eidx, n_e), axis=0, keepdims=True)
    e_hi = jnp.max(jnp.where(sel, eidx, -1), axis=0, keepdims=True)
    is_lo = eidx == e_lo
    is_hi = eidx == e_hi
    w = w_ref[...]
    pos_a = jnp.sum(jnp.where(is_lo, pos, 0.0), axis=0, keepdims=True)
    pos_b = jnp.sum(jnp.where(is_hi, pos, 0.0), axis=0, keepdims=True)
    w_a = jnp.sum(jnp.where(is_lo, w, 0.0), axis=0, keepdims=True)
    w_b = jnp.sum(jnp.where(is_hi, w, 0.0), axis=0, keepdims=True)
    pos_ref[...] = jnp.where(eidx == 0, pos_a, jnp.where(eidx == 1, pos_b, 0.0)).astype(jnp.int32)
    wt_ref[...] = jnp.where(eidx == 0, w_a, jnp.where(eidx == 1, w_b, 0.0)).T
    tile_lo = lax.broadcasted_iota(jnp.int32, (n_e, LANES), 1).astype(F32) * tile
    t_exp = jnp.sum(jnp.where(start + padded <= tile_lo, 1.0, 0.0), axis=0, keepdims=True)
    t_exp = jnp.minimum(t_exp, n_e - 1.0)
    r8 = lax.broadcasted_iota(jnp.int32, (n_e, LANES), 0)
    last_row = jnp.sum(jnp.where(r8.astype(F32) == t_exp, start + carry, 0.0), axis=0, keepdims=True)
    filled = jnp.clip(last_row - tile_lo[0:1], 0.0, float(tile))
    meta_ref[...] = jnp.where(r8 == 0, t_exp, jnp.where(r8 == 1, run / tile,
                                                       jnp.where(r8 == 2, filled, 0.0))).astype(jnp.int32)


def moe_route(sel, w, tile):
    n_e, m = sel.shape
    ck = 256
    tri = jnp.asarray(np.triu(np.ones((ck, ck), np.float32)), BF16)
    full = lambda shape: pl.BlockSpec(shape, lambda: tuple(0 for _ in shape))
    return pl.pallas_call(
        functools.partial(_route_kernel, tile=tile),
        in_specs=[full((n_e, m)), full((n_e, m)), full((ck, ck))],
        out_specs=[full((n_e, m)), full((m, n_e)), full((n_e, LANES))],
        out_shape=[jax.ShapeDtypeStruct((n_e, m), jnp.int32),
                   jax.ShapeDtypeStruct((m, n_e), F32),
                   jax.ShapeDtypeStruct((n_e, LANES), jnp.int32)],
        scratch_shapes=[pltpu.VMEM((n_e, m), F32)],
        compiler_params=pltpu.CompilerParams(vmem_limit_bytes=VMEM_LIMIT),
        name="moe_route",
    )(sel, w, tri)


def _row_gather_start(src_hbm, dst, rows_ref, base, n_rows, sem, unroll=8):
    def body(r, carry):
        src_row = rows_ref[base + r]
        pltpu.make_async_copy(src_hbm.at[pl.ds(src_row, 1)], dst.at[pl.ds(r, 1)], sem).start()
        return carry
    lax.fori_loop(0, n_rows, body, 0, unroll=unroll)


def _row_gather_wait(src_hbm, dst, sem):
    pltpu.make_async_copy(src_hbm.at[pl.ds(0, dst.shape[0])], dst, sem).wait()


def _moe_gather_kernel(pos_ref, meta_ref, h_hbm, zeros_hbm, o_ref, tok_ref, buf_ref, sem_ref):
    i = pl.program_id(0)
    n = meta_ref[1, 0]
    tile = o_ref.shape[0]
    m = pos_ref.shape[0] // 2

    @pl.when(i == 0)
    def _():
        clear = pltpu.make_async_copy(zeros_hbm, tok_ref, sem_ref.at[2])
        clear.start()
        clear.wait()

        def fill(t, carry):
            tok_ref[pos_ref[t]] = t
            tok_ref[pos_ref[m + t]] = t
            return carry
        lax.fori_loop(0, m, fill, 0, unroll=8)
        _row_gather_start(h_hbm, buf_ref.at[0], tok_ref, 0, tile, sem_ref.at[0])

    @pl.when(i + 1 < n)
    def _():
        nxt = (i + 1) % 2
        _row_gather_start(h_hbm, buf_ref.at[nxt], tok_ref, (i + 1) * tile, tile, sem_ref.at[nxt])

    @pl.when(i < n)
    def _():
        cur = i % 2
        _row_gather_wait(h_hbm, buf_ref.at[cur], sem_ref.at[cur])
        o_ref[...] = buf_ref[cur].astype(o_ref.dtype)

    @pl.when(i >= n)
    def _():
        o_ref[...] = jnp.zeros_like(o_ref)


def moe_gather(pos2, meta, h, n_tiles, tile):
    m, d = h.shape
    rows = n_tiles * tile
    return pl.pallas_call(
        _moe_gather_kernel,
        grid_spec=pltpu.PrefetchScalarGridSpec(
            num_scalar_prefetch=2,
            grid=(n_tiles,),
            in_specs=[pl.BlockSpec(memory_space=pl.ANY), pl.BlockSpec(memory_space=pl.ANY)],
            out_specs=pl.BlockSpec((tile, d), lambda i, pos, meta: (i, 0)),
            scratch_shapes=[pltpu.SMEM((rows,), jnp.int32),
                            pltpu.VMEM((2, tile, d), F32),
                            pltpu.SemaphoreType.DMA((3,))]),
        out_shape=jax.ShapeDtypeStruct((rows, d), BF16),
        compiler_params=_params("arbitrary"),
        name="moe_gather",
    )(pos2, meta, h, jnp.zeros((rows,), jnp.int32))


def _moe_up_kernel(meta_ref, x_ref, w1_ref, w3_ref, w2_ref, o_ref, w2b_ref):
    i = pl.program_id(1)

    def round_w2_block():
        w2b_ref[...] = w2_ref[...].astype(BF16)

    filled = meta_ref[2, i]
    tile = o_ref.shape[0]
    for rows in _row_options(tile):
        @pl.when((filled > rows - tile // ROW_QUARTERS) & (filled <= rows))
        def _(rows=rows):
            x = x_ref[0:rows, :]
            for c in range(0, o_ref.shape[1], COL_CHUNK):
                sl = slice(c, c + COL_CHUNK)
                a1 = _dot(x, w1_ref[:, sl])
                a3 = _dot(x, w3_ref[:, sl])
                o_ref[0:rows, sl] = (a1 * _sigmoid(a1) * a3).astype(o_ref.dtype)
            if rows < tile:
                o_ref[rows:tile, :] = jnp.zeros((tile - rows, o_ref.shape[1]), o_ref.dtype)
            round_w2_block()

    @pl.when(filled == 0)
    def _():
        o_ref[...] = jnp.zeros_like(o_ref)
        round_w2_block()


BF16_SUBLANES = 16


def _cast_rows_per_step(total_rows, steps):
    for rows in range(BF16_SUBLANES, total_rows + 1, BF16_SUBLANES):
        if total_rows % rows == 0 and total_rows // rows <= steps:
            return rows
    raise ValueError("weights cannot be split over the grid steps")


def moe_up(meta, xs, w1, w3, w2, tile, tn=1792):
    rows, d = xs.shape
    n_e, _, f = w1.shape
    assert f % tn == 0 and rows % tile == 0 and w1.dtype == BF16 and w3.dtype == BF16
    n_tiles = rows // tile
    w2_flat = w2.reshape(n_e * f, w2.shape[2])
    cast_rows = _cast_rows_per_step(w2_flat.shape[0], (f // tn) * n_tiles)
    n_cast = w2_flat.shape[0] // cast_rows
    last = lambda i, meta: jnp.maximum(jnp.minimum(i, meta[1, 0] - 1), 0)
    cast_blk = lambda j, i, meta: (jnp.minimum(j * n_tiles + i, n_cast - 1), 0)
    up, w2b = pl.pallas_call(
        _moe_up_kernel,
        grid_spec=pltpu.PrefetchScalarGridSpec(
            num_scalar_prefetch=1,
            grid=(f // tn, n_tiles),
            in_specs=[pl.BlockSpec((tile, d), lambda j, i, meta: (last(i, meta), 0)),
                      pl.BlockSpec((None, d, tn), lambda j, i, meta: (meta[0, i], 0, j)),
                      pl.BlockSpec((None, d, tn), lambda j, i, meta: (meta[0, i], 0, j)),
                      pl.BlockSpec((cast_rows, w2_flat.shape[1]), cast_blk)],
            out_specs=[pl.BlockSpec((tile, tn), lambda j, i, meta: (i, j)),
                       pl.BlockSpec((cast_rows, w2_flat.shape[1]), cast_blk)]),
        out_shape=[jax.ShapeDtypeStruct((rows, f), BF16),
                   jax.ShapeDtypeStruct(w2_flat.shape, BF16)],
        compiler_params=_params("arbitrary", "arbitrary"),
        name="moe_up",
    )(meta, xs, w1, w3, w2_flat)
    return up, w2b.reshape(w2.shape)


def _moe_down_kernel(meta_ref, a_ref, w_ref, o_ref):
    i = pl.program_id(0)
    k = pl.program_id(1)

    filled = meta_ref[2, i]
    tile = o_ref.shape[0]
    cn = COL_CHUNK
    for rows in _row_options(tile):
        in_range = (filled > rows - tile // ROW_QUARTERS) & (filled <= rows)

        @pl.when(in_range & (k == 0))
        def _(rows=rows):
            a = a_ref[0:rows, :]
            for c in range(0, o_ref.shape[1], cn):
                o_ref[0:rows, c:c + cn] = _dot(a, w_ref[:, c:c + cn])
            if rows < tile:
                o_ref[rows:tile, :] = jnp.zeros((tile - rows, o_ref.shape[1]), o_ref.dtype)

        @pl.when(in_range & (k > 0))
        def _(rows=rows):
            a = a_ref[0:rows, :]
            for c in range(0, o_ref.shape[1], cn):
                o_ref[0:rows, c:c + cn] += _dot(a, w_ref[:, c:c + cn])

    @pl.when((filled == 0) & (k == 0))
    def _():
        o_ref[...] = jnp.zeros_like(o_ref)


def moe_down(meta, up, w2, tile, tk=3584):
    rows, f = up.shape
    d = w2.shape[2]
    assert f % tk == 0 and rows % tile == 0
    n_tiles = rows // tile
    last = lambda i, meta: jnp.maximum(jnp.minimum(i, meta[1, 0] - 1), 0)
    return pl.pallas_call(
        _moe_down_kernel,
        grid_spec=pltpu.PrefetchScalarGridSpec(
            num_scalar_prefetch=1,
            grid=(n_tiles, f // tk),
            in_specs=[pl.BlockSpec((tile, tk), lambda i, k, meta: (last(i, meta), k)),
                      pl.BlockSpec((None, tk, d), lambda i, k, meta: (meta[0, i], k, 0))],
            out_specs=pl.BlockSpec((tile, d), lambda i, k, meta: (i, 0))),
        out_shape=jax.ShapeDtypeStruct((rows, d), F32),
        compiler_params=_params("arbitrary", "arbitrary"),
        name="moe_down",
    )(meta, up, w2)


def _moe_combine_kernel(pos_ref, x_ref, wt_ref, g_ref, y_hbm, o_ref, buf_ref, sem_ref, *, final_norm):
    i = pl.program_id(0)
    n = pl.num_programs(0)
    tc = x_ref.shape[0]
    m = pos_ref.shape[0] // 2

    def start(step, slot):
        for s in range(2):
            _row_gather_start(y_hbm, buf_ref.at[slot, s], pos_ref, s * m + step * tc, tc, sem_ref.at[slot, s])

    @pl.when(i == 0)
    def _():
        start(0, 0)

    @pl.when(i + 1 < n)
    def _():
        start(i + 1, (i + 1) % 2)

    cur = i % 2
    for s in range(2):
        _row_gather_wait(y_hbm, buf_ref.at[cur, s], sem_ref.at[cur, s])
    wt = wt_ref[...]
    out = x_ref[...] + wt[:, 0:1] * buf_ref[cur, 0] + wt[:, 1:2] * buf_ref[cur, 1]
    o_ref[...] = _rms(out, g_ref[...]) if final_norm else out


def moe_combine(pos2, x, wt, y, norm_g=None, tc=256):
    m, d = x.shape
    final_norm = norm_g is not None
    g = (norm_g if final_norm else jnp.ones((d,), F32)).reshape(1, d)
    return pl.pallas_call(
        functools.partial(_moe_combine_kernel, final_norm=final_norm),
        grid_spec=pltpu.PrefetchScalarGridSpec(
            num_scalar_prefetch=1,
            grid=(m // tc,),
            in_specs=[pl.BlockSpec((tc, d), lambda i, pos: (i, 0)),
                      pl.BlockSpec((tc, wt.shape[1]), lambda i, pos: (i, 0)),
                      pl.BlockSpec((1, d), lambda i, pos: (0, 0)),
                      pl.BlockSpec(memory_space=pl.ANY)],
            out_specs=pl.BlockSpec((tc, d), lambda i, pos: (i, 0)),
            scratch_shapes=[pltpu.VMEM((2, 2, tc, d), F32),
                            pltpu.SemaphoreType.DMA((2, 2))]),
        out_shape=jax.ShapeDtypeStruct((m, d), F32),
        compiler_params=_params("arbitrary"),
        name="moe_combine",
    )(pos2, x, wt, g, y)


def moe_layer(x, norm_g, router_w, router_b, w1, w3, w2, out_norm_g=None):
    m, d = x.shape
    n_e = router_w.shape[1]
    tile = MOE_TILE
    n_tiles = 2 * m // tile + n_e
    h, sel, w = norm_router(x, norm_g, router_w, router_b)
    pos, wt, meta = moe_route(sel, w, tile)
    pos2 = pos[:2].reshape(-1)
    xs = moe_gather(pos2, meta, h, n_tiles, tile)
    up, w2_bf16 = moe_up(meta, xs, w1, w3, w2, tile)
    y = moe_down(meta, up, w2_bf16, tile)
    return moe_combine(pos2, x, wt, y, out_norm_g)


def _mixer(x, batch, layer, norm_g, w_in_t, pe_k, pe_v, kw1, kw2, vw1, vw2, ln_g, ln_b, w_s, b_s, w_a, w_b, w_out,
           round_src=None):
    h = rmsnorm(x, norm_g, BF16)
    qt, kvr, vt, ngt, rest = in_projection(h, w_in_t, layer)
    cmp_kv = compress(kvr, batch, pe_k, pe_v, kw1, kw2, vw1, vw2)
    oct, sel = compressed_and_select(qt, cmp_kv, batch)
    oa, rounded = selected_window_attention(qt, kvr, vt, sel, oct, ngt, batch, round_src)
    ob = spatial_gating(rest, ln_g, ln_b, w_s, b_s)
    y = merge_branches(oa, ob, rest, w_a, w_b, layer)
    return residual_matmul("out_proj", y, w_out, layer, x), rounded


def kernel(x, norm_mix, w_in, cmp_pe_k, cmp_pe_v, cmp_k_w1, cmp_k_w2, cmp_v_w1, cmp_v_w2, sgu_ln_g, sgu_ln_b, sgu_w, sgu_b, w_branch_a, w_branch_b, w_out, norm_ffn, ffn_w1, ffn_w3, ffn_w2, router_w, router_b, moe_w1, moe_w3, moe_w2, norm_f):
    batch, seq, d = x.shape
    depth = norm_mix.shape[0]
    xf = x.reshape(batch * seq, d)
    w_in_t = jnp.swapaxes(w_in, 1, 2)
    normed = False
    n_moe = moe_w1.shape[0]
    moe_up_bf16 = {}
    for layer in range(depth):
        j = layer // 2
        round_src = None
        if j < n_moe and 2 * j + 1 < depth:
            w_f32 = moe_w1[j] if layer % 2 == 0 else moe_w3[j]
            round_src = w_f32.reshape(-1, w_f32.shape[-1])
        xf, rounded = _mixer(xf, batch, layer, norm_mix[layer], w_in_t, cmp_pe_k[layer], cmp_pe_v[layer],
                             cmp_k_w1[layer], cmp_k_w2[layer], cmp_v_w1[layer], cmp_v_w2[layer],
                             sgu_ln_g[layer], sgu_ln_b[layer], sgu_w[layer], sgu_b[layer],
                             w_branch_a, w_branch_b, w_out, round_src)
        if rounded is not None:
            moe_up_bf16[layer % 2] = rounded.reshape(moe_w1[j].shape)
        if layer % 2 == 0:
            h = rmsnorm(xf, norm_ffn[layer], BF16)
            up = swiglu_up(h, ffn_w1[j], ffn_w3[j])
            xf = residual_matmul("ffn_down", up, ffn_w2, j, xf, tm=512, tn=512)
        else:
            normed = layer == depth - 1
            xf = moe_layer(xf, norm_ffn[layer], router_w[j], router_b[j], moe_up_bf16[0], moe_up_bf16[1],
                           moe_w2[j], norm_f if normed else None)
    if not normed:
        xf = rmsnorm(xf, norm_f, F32)
    return xf.reshape(batch, seq, d)
```

```python
import functools

import numpy as np
import jax
import jax.numpy as jnp
from jax import lax
from jax.experimental import pallas as pl
from jax.experimental.pallas import tpu as pltpu

F32 = jnp.float32
BF16 = jnp.bfloat16

D_MODEL = 2048
N_Q_HEADS = 16
N_KV_HEADS = 4
HEAD_DIM = 64
Q_PER_KV = N_Q_HEADS // N_KV_HEADS
NSA_WIDTH = N_Q_HEADS * HEAD_DIM
KV_WIDTH = N_KV_HEADS * HEAD_DIM
GROUP_WIDTH = Q_PER_KV * HEAD_DIM
CMP_BLOCK = 32
CMP_STRIDE = 16
CMP_HIDDEN = 256
SLC_BLOCK = 64
N_SELECT = 16
WINDOW = 512
FORCED_SCORE = 1e4
SGU_WIDTH = D_MODEL // 2
SGU_GROUPS = 8
SGU_CHUNK = 128
N_EXPERTS = 8
NORM_EPS = 1e-5
NEG = -1e30
LOG2E = 1.4426950408889634
LANES = 128
VMEM_LIMIT = 56 * 1024 * 1024

Q_OFF = 0
KV_OFF = NSA_WIDTH
NG_OFF = KV_OFF + 6 * KV_WIDTH
NG_WIDTH = 3 * N_Q_HEADS
REST_OFF = NG_OFF + NG_WIDTH


def _params(*sem):
    return pltpu.CompilerParams(dimension_semantics=sem, vmem_limit_bytes=VMEM_LIMIT)


def _dot(a, b):
    return jnp.dot(a, b, preferred_element_type=F32)


def _dot_nt(a, b):
    return lax.dot_general(a, b, (((1,), (1,)), ((), ())), preferred_element_type=F32)


def _dot_split(a_f32, b_bf16):
    hi = a_f32.astype(BF16)
    lo = (a_f32 - hi.astype(F32)).astype(BF16)
    return _dot(hi, b_bf16) + _dot(lo, b_bf16)


def _sigmoid(x):
    return 1.0 / (1.0 + jnp.exp(-x))


def _gelu(x):
    return x * (0.5 * (1.0 + jnp.tanh(0.7978845608028654 * (x + 0.044715 * (x * x * x)))))


def _rms(x, g):
    return x * lax.rsqrt(jnp.mean(x * x, axis=-1, keepdims=True) + NORM_EPS) * g


def _rmsnorm_kernel(x_ref, g_ref, o_ref):
    o_ref[...] = _rms(x_ref[...], g_ref[...]).astype(o_ref.dtype)


def rmsnorm(x, g, out_dtype, tm=1024):
    m, d = x.shape
    return pl.pallas_call(
        _rmsnorm_kernel,
        grid=(m // tm,),
        in_specs=[pl.BlockSpec((tm, d), lambda i: (i, 0)),
                  pl.BlockSpec((1, d), lambda i: (0, 0))],
        out_specs=pl.BlockSpec((tm, d), lambda i: (i, 0)),
        out_shape=jax.ShapeDtypeStruct((m, d), out_dtype),
        compiler_params=_params("parallel"),
        name="rmsnorm",
    )(x, g.reshape(1, d))


def _norm_router_kernel(x_ref, g_ref, rwt_ref, rb_ref, h_ref, sel_ref, w_ref):
    h = _rms(x_ref[...], g_ref[...])
    h_ref[...] = h
    logits = lax.dot_general(rwt_ref[...], h, (((1,), (1,)), ((), ())), preferred_element_type=F32,
                             precision=lax.Precision.HIGHEST) + rb_ref[...]
    n_e = logits.shape[0]
    e = lax.broadcasted_iota(jnp.int32, logits.shape, 0)
    m1 = jnp.max(logits, axis=0, keepdims=True)
    i1 = jnp.min(jnp.where(logits == m1, e, n_e), axis=0, keepdims=True)
    rest = jnp.where(e == i1, -jnp.inf, logits)
    m2 = jnp.max(rest, axis=0, keepdims=True)
    i2 = jnp.min(jnp.where(rest == m2, e, n_e), axis=0, keepdims=True)
    e2 = jnp.exp(m2 - m1)
    w1 = 1.0 / (1.0 + e2)
    w2 = e2 / (1.0 + e2)
    sel_ref[...] = jnp.where((e == i1) | (e == i2), 1.0, 0.0)
    w_ref[...] = jnp.where(e == i1, w1, 0.0) + jnp.where(e == i2, w2, 0.0)


def norm_router(x, g, router_w, router_b, tm=512):
    m, d = x.shape
    n_e = router_w.shape[1]
    return pl.pallas_call(
        _norm_router_kernel,
        grid=(m // tm,),
        in_specs=[pl.BlockSpec((tm, d), lambda i: (i, 0)),
                  pl.BlockSpec((1, d), lambda i: (0, 0)),
                  pl.BlockSpec((n_e, d), lambda i: (0, 0)),
                  pl.BlockSpec((n_e, 1), lambda i: (0, 0))],
        out_specs=[pl.BlockSpec((tm, d), lambda i: (i, 0)),
                   pl.BlockSpec((n_e, tm), lambda i: (0, i)),
                   pl.BlockSpec((n_e, tm), lambda i: (0, i))],
        out_shape=[jax.ShapeDtypeStruct((m, d), F32),
                   jax.ShapeDtypeStruct((n_e, m), F32),
                   jax.ShapeDtypeStruct((n_e, m), F32)],
        compiler_params=_params("parallel"),
        name="norm_router",
    )(x, g.reshape(1, d), router_w.T, router_b.reshape(n_e, 1))


def _mm_full_kernel(*refs, n_a, n_b, n_e, n_o, pairs, inner_axis, epilogue, b_transposed):
    a_refs = refs[:n_a]
    b_refs = refs[n_a:n_a + n_b]
    e_refs = refs[n_a + n_b:n_a + n_b + n_e]
    o_refs = refs[n_a + n_b + n_e:n_a + n_b + n_e + n_o]
    w_refs = refs[n_a + n_b + n_e + n_o:]

    @pl.when(pl.program_id(inner_axis) == 0)
    def _():
        for b_ref, w_ref in zip(b_refs, w_refs):
            w_ref[...] = b_ref[...].astype(BF16)

    dot = _dot_nt if b_transposed else _dot
    ws = [w_ref[0] if len(w_ref.shape) == 3 else w_ref[...] for w_ref in w_refs]
    accs = [dot(a_refs[ia][...], ws[ib]) for ia, ib in pairs]
    epilogue(accs, e_refs, o_refs)


def _block_dim_size(s):
    return s.block_size if isinstance(s, pl.Element) else s


def mm_full(name, grid, a_ops, b_ops, e_ops, outs, pairs, epilogue, b_transposed=False):
    ops = a_ops + b_ops + e_ops
    kernel = functools.partial(
        _mm_full_kernel, n_a=len(a_ops), n_b=len(b_ops), n_e=len(e_ops), n_o=len(outs),
        pairs=pairs, inner_axis=len(grid) - 1, epilogue=epilogue, b_transposed=b_transposed)
    scratch = [pltpu.VMEM(tuple(_block_dim_size(s) for s in blk if s is not None), BF16) for _, blk, _ in b_ops]
    res = pl.pallas_call(
        kernel,
        grid=grid,
        in_specs=[pl.BlockSpec(blk, imap) for _, blk, imap in ops],
        out_specs=[pl.BlockSpec(blk, imap) for _, blk, imap in outs],
        out_shape=[sds for sds, _, _ in outs],
        scratch_shapes=scratch,
        compiler_params=_params(*(("arbitrary",) * len(grid))),
        name=name,
    )(*[arr for arr, _, _ in ops])
    return res


COL_CHUNK = 256


def in_projection(h, w_in_t, layer):
    m, d = h.shape
    tm = 1024
    a_op = [(h, (tm, d), lambda j, i: (i, 0))]

    tq = NSA_WIDTH
    q_heads = tq // HEAD_DIM
    q_scale = float(np.float32(HEAD_DIM ** -0.5) * np.float32(LOG2E))

    def q_epi(accs, e_refs, o_refs):
        for c in range(q_heads):
            o_refs[0][c] = (accs[0][:, c * HEAD_DIM:(c + 1) * HEAD_DIM] * q_scale).T.astype(BF16)

    (qt,) = mm_full(
        "in_proj_q", (NSA_WIDTH // tq, m // tm), a_op,
        [(w_in_t, (None, tq, d), lambda j, i: (layer, j, 0))], [],
        [(jax.ShapeDtypeStruct((N_Q_HEADS, HEAD_DIM, m), BF16), (q_heads, HEAD_DIM, tm), lambda j, i: (j, 0, i))],
        [(0, 0)], q_epi, b_transposed=True)

    sec0 = KV_OFF // KV_WIDTH

    def rows_epi(accs, e_refs, o_refs):
        for c in range(N_KV_HEADS):
            o_refs[0][c] = accs[0][:, c * HEAD_DIM:(c + 1) * HEAD_DIM].astype(BF16)

    (kvr,) = mm_full(
        "in_proj_kv_rows", (4, m // tm), a_op,
        [(w_in_t, (None, KV_WIDTH, d), lambda j, i: (layer, sec0 + j + j // 3, 0))], [],
        [(jax.ShapeDtypeStruct((4 * N_KV_HEADS, m, HEAD_DIM), BF16),
          (N_KV_HEADS, tm, HEAD_DIM), lambda j, i: (j, i, 0))],
        [(0, 0)], rows_epi, b_transposed=True)

    def cols_epi(accs, e_refs, o_refs):
        for c in range(N_KV_HEADS):
            o_refs[0][c] = accs[0][:, c * HEAD_DIM:(c + 1) * HEAD_DIM].T.astype(BF16)

    (vt,) = mm_full(
        "in_proj_v_cols", (2, m // tm), a_op,
        [(w_in_t, (None, KV_WIDTH, d), lambda j, i: (layer, sec0 + 3 + 2 * j, 0))], [],
        [(jax.ShapeDtypeStruct((2 * N_KV_HEADS, HEAD_DIM, m), BF16),
          (N_KV_HEADS, HEAD_DIM, tm), lambda j, i: (j, 0, i))],
        [(0, 0)], cols_epi, b_transposed=True)

    def f32_epi(accs, e_refs, o_refs):
        o_refs[0][...] = accs[0]

    def f32_t_epi(accs, e_refs, o_refs):
        o_refs[0][...] = accs[0].T

    assert NG_OFF % LANES == 0
    (ngt,) = mm_full(
        "in_proj_ng", (1, m // tm), a_op,
        [(w_in_t, (None, LANES, d), lambda j, i: (layer, NG_OFF // LANES, 0))], [],
        [(jax.ShapeDtypeStruct((LANES, m), F32), (LANES, tm), lambda j, i: (0, i))],
        [(0, 0)], f32_t_epi, b_transposed=True)

    n_rest = w_in_t.shape[1] - REST_OFF
    tr = 1024
    assert n_rest % tr == 0 and REST_OFF % 8 == 0
    (rest,) = mm_full(
        "in_proj_rest", (n_rest // tr, m // tm), a_op,
        [(w_in_t, (pl.Element(1), pl.Element(tr), pl.Element(d)), lambda j, i: (layer, pl.multiple_of(REST_OFF + j * tr, 8), 0))], [],
        [(jax.ShapeDtypeStruct((m, n_rest), F32), (tm, tr), lambda j, i: (i, j))],
        [(0, 0)], f32_epi, b_transposed=True)
    return qt, kvr, vt, ngt, rest


def _cmp_kernel(a_ref, w1_ref, w2_ref, pe_ref, o_ref):
    a = a_ref[...]
    rows = a.shape[0]
    half = a.shape[1]
    w1 = w1_ref[...].astype(BF16)
    p0 = _dot(a, w1[:half])
    p1 = _dot(a, w1[half:])
    bias = _dot(pe_ref[...].astype(BF16), w1)[0:1]
    hid = p0 + pltpu.roll(p1, rows - 1, 0) + bias
    o_ref[...] = _dot(_gelu(hid).astype(BF16), w2_ref[...].astype(BF16))


def compress(kv, batch, pe_k, pe_v, kw1, kw2, vw1, vw2):
    m = kv.shape[1]
    n_chunks = m // batch // CMP_STRIDE
    rows = batch * n_chunks
    feat = CMP_STRIDE * HEAD_DIM
    a = kv[:2 * N_KV_HEADS].reshape(2, N_KV_HEADS, rows, feat)
    w1 = jnp.stack([kw1, vw1])
    w2 = jnp.stack([kw2, vw2])
    pe = jnp.stack([pe_k, pe_v]).reshape(2, 1, CMP_BLOCK * HEAD_DIM)
    pe = jnp.broadcast_to(pe, (2, 8, CMP_BLOCK * HEAD_DIM))
    return pl.pallas_call(
        _cmp_kernel,
        grid=(2, N_KV_HEADS),
        in_specs=[pl.BlockSpec((None, None, rows, feat), lambda s, g: (s, g, 0, 0)),
                  pl.BlockSpec((None, 2 * feat, CMP_HIDDEN), lambda s, g: (s, 0, 0)),
                  pl.BlockSpec((None, CMP_HIDDEN, HEAD_DIM), lambda s, g: (s, 0, 0)),
                  pl.BlockSpec((None, 8, 2 * feat), lambda s, g: (s, 0, 0))],
        out_specs=pl.BlockSpec((None, None, rows, HEAD_DIM), lambda s, g: (s, g, 0, 0)),
        out_shape=jax.ShapeDtypeStruct((2, N_KV_HEADS, rows, HEAD_DIM), F32),
        compiler_params=_params("parallel", "parallel"),
        name="nsa_compress",
    )(a, w1, w2, pe)


def _alibi_slopes():
    sl = np.array([2.0 ** (-8.0 * (h + 1) / N_Q_HEADS) for h in range(N_Q_HEADS)], dtype=np.float32)
    return sl * np.float32(LOG2E)


def _overlap_t(n_cmp_pad, n_slc):
    cs = np.arange(n_cmp_pad)[None, :] * CMP_STRIDE
    ss = np.arange(n_slc)[:, None] * SLC_BLOCK
    ov = np.clip(np.minimum(cs + CMP_BLOCK, ss + SLC_BLOCK) - np.maximum(cs, ss), 0, None)
    return (ov / CMP_STRIDE).astype(np.float32)


def _sel_kernel(slopes_ref, q_ref, kc_ref, vc_ref, ovt_ref, oc_ref, sel_ref, *, n_cmp):
    g = pl.program_id(1)
    i = pl.program_id(2)
    tq = q_ref.shape[2]
    ncp = kc_ref.shape[0]
    n_slc = ovt_ref.shape[0]
    kc = kc_ref[...].astype(BF16)
    vct = vc_ref[...].T.astype(BF16)

    t = i * tq + lax.broadcasted_iota(jnp.int32, (ncp, tq), 1)
    c = lax.broadcasted_iota(jnp.int32, (ncp, tq), 0)
    dist = t - (c * CMP_STRIDE + (CMP_BLOCK - 1))
    mask = (dist >= 0) & (c < n_cmp)
    distf = dist.astype(F32)

    scores = [_dot(kc, q_ref[j]) for j in range(Q_PER_KV)]
    probs = []
    for j in range(Q_PER_KV):
        s = scores[j] - slopes_ref[g * Q_PER_KV + j] * distf
        s = jnp.where(mask, s, NEG)
        mx = jnp.max(s, axis=0, keepdims=True)
        e = jnp.where(mask, jnp.exp2(s - mx), 0.0)
        probs.append(e / jnp.maximum(jnp.sum(e, axis=0, keepdims=True), 1e-30))
    for j in range(Q_PER_KV):
        oc_ref[j] = _dot(vct, probs[j].astype(BF16))
    p_sum = functools.reduce(lambda a, b: a + b, probs)

    ovt = ovt_ref[...]
    hi = p_sum.astype(BF16)
    lo = (p_sum - hi.astype(F32)).astype(BF16)
    imp = _dot(ovt, hi) + _dot(ovt, lo)
    tt = i * tq + lax.broadcasted_iota(jnp.int32, (n_slc, tq), 1)
    blk = lax.broadcasted_iota(jnp.int32, (n_slc, tq), 0)
    cur = tt // SLC_BLOCK
    valid = blk * SLC_BLOCK <= tt
    forced = (blk == 0) | (blk == cur) | (blk == cur - 1)
    score = jnp.where(valid, imp, -1.0)
    score = jnp.where(forced, FORCED_SCORE, score)
    rank = jnp.zeros((n_slc, tq), F32)
    for mrow in range(n_slc):
        other = jnp.broadcast_to(score[mrow:mrow + 1, :], (n_slc, tq))
        beats = (other > score) | ((other == score) & (blk > mrow))
        rank = rank + jnp.where(beats, 1.0, 0.0)
    sel_ref[...] = jnp.where(rank < float(min(N_SELECT, n_slc)), 1.0, 0.0)


def compressed_and_select(qt, cmp_kv, batch, tq=512):
    m = qt.shape[2]
    seq = m // batch
    nq = seq // tq
    ncp = cmp_kv.shape[2] // batch
    n_slc = seq // SLC_BLOCK
    ovt = jnp.asarray(_overlap_t(ncp, n_slc), BF16)
    slopes = jnp.asarray(_alibi_slopes())
    kernel = functools.partial(_sel_kernel, n_cmp=ncp - 1)
    return pl.pallas_call(
        kernel,
        grid=(batch, N_KV_HEADS, nq),
        in_specs=[pl.BlockSpec(memory_space=pltpu.SMEM),
                  pl.BlockSpec((Q_PER_KV, HEAD_DIM, tq), lambda b, g, i: (g, 0, b * nq + i)),
                  pl.BlockSpec((None, None, ncp, HEAD_DIM), lambda b, g, i: (0, g, b, 0)),
                  pl.BlockSpec((None, None, ncp, HEAD_DIM), lambda b, g, i: (1, g, b, 0)),
                  pl.BlockSpec((n_slc, ncp), lambda b, g, i: (0, 0))],
        out_specs=[pl.BlockSpec((Q_PER_KV, HEAD_DIM, tq), lambda b, g, i: (g, 0, b * nq + i)),
                   pl.BlockSpec((None, None, n_slc, tq), lambda b, g, i: (b, g, 0, i))],
        out_shape=[jax.ShapeDtypeStruct((N_Q_HEADS, HEAD_DIM, m), F32),
                   jax.ShapeDtypeStruct((batch, N_KV_HEADS, n_slc, seq), F32)],
        compiler_params=_params("parallel", "parallel", "parallel"),
        name="nsa_compressed_select",
    )(slopes, qt, cmp_kv, cmp_kv, ovt)


AUX_SLOPE = HEAD_DIM
AUX_SEL = HEAD_DIM + 16
ATTN_K = 2 * HEAD_DIM
STRIP = 32


def _attn_kernel(slopes_ref, q_ref, ks_ref, vs_ref, kw_ref, vw_ref, auxs_ref, auxw_ref, sel_ref, oc_ref,
                 ng_ref, *rest, round_weights):
    if round_weights:
        wsrc_ref, o_ref, wdst_ref = rest[:3]
        scratch = rest[3:]

        def side_work():
            wdst_ref[...] = wsrc_ref[...].astype(BF16)
    else:
        o_ref = rest[0]
        scratch = rest[1:]
        side_work = lambda: None
    _attn_body(slopes_ref, q_ref, ks_ref, vs_ref, kw_ref, vw_ref, auxs_ref, auxw_ref, sel_ref, oc_ref, ng_ref,
               o_ref, *scratch, side_work=side_work)


def _attn_body(slopes_ref, q_ref, ks_ref, vs_ref, kw_ref, vw_ref, auxs_ref, auxw_ref, sel_ref, oc_ref,
               ng_ref, o_ref, ka_s_ref, ka_w_ref, qa_ref, s_ref, p_ref, mask_ref, m_ref, l_ref, acc_ref,
               os_ref, sig_ref, *, side_work):
    g = pl.program_id(1)
    i = pl.program_id(2)
    tq = q_ref.shape[2]
    tk = tq
    q0 = i * tq
    n_slc = sel_ref.shape[0]

    @pl.when(i == 0)
    def _():
        ka_s_ref[:, 0:HEAD_DIM] = ks_ref[...]
        ka_s_ref[:, HEAD_DIM:ATTN_K] = auxs_ref[...]
        ka_w_ref[:, 0:HEAD_DIM] = kw_ref[...]
        ka_w_ref[:, HEAD_DIM:ATTN_K] = auxw_ref[...]
        kr = lax.broadcasted_iota(jnp.int32, (tk, tq), 0)
        qc = lax.broadcasted_iota(jnp.int32, (tk, tq), 1)
        mask_ref[0] = jnp.where(kr <= qc, 0.0, NEG)
        mask_ref[1] = jnp.where(kr > qc, 0.0, NEG)

    sig_ref[...] = _sigmoid(ng_ref[...])
    side_work()
    sel_neg =(sel_ref[...] - 1.0) * (-NEG)
    row16 = lax.broadcasted_iota(jnp.int32, (16, tq), 0)
    pad = jnp.zeros((ATTN_K - AUX_SEL - n_slc, tq), F32)
    for j in range(Q_PER_KV):
        sl = jnp.full((16, tq), slopes_ref[g * Q_PER_KV + j], F32)
        hi = sl.astype(BF16).astype(F32)
        mid = (sl - hi).astype(BF16).astype(F32)
        lo = (sl - hi - mid).astype(BF16).astype(F32)
        pieces = jnp.where(row16 == 0, hi, jnp.where(row16 == 1, mid, jnp.where(row16 == 2, lo, 0.0)))
        qa = jnp.concatenate([q_ref[j].astype(F32), pieces, sel_neg, pad], axis=0)
        qa_ref[j] = qa.astype(BF16)

    def scores(ka_ref, k0, buf):
        ka = ka_ref[pl.ds(k0, tk), :]
        for j in range(Q_PER_KV):
            s_ref[buf, j] = _dot(ka, qa_ref[j])

    def softmax_pv(v_ref, k0, buf, mask_idx, first):
        vc = v_ref[:, pl.ds(k0, tk)]
        for j in range(Q_PER_KV):
            shift = slopes_ref[g * Q_PER_KV + j] * k0.astype(F32)
            mx = None
            for r in range(0, tk, STRIP):
                x = s_ref[buf, j, r:r + STRIP, :]
                if mask_idx is not None:
                    x = x + mask_ref[mask_idx, r:r + STRIP, :]
                    s_ref[buf, j, r:r + STRIP, :] = x
                mx = x if mx is None else jnp.maximum(mx, x)
            m_cur = jnp.max(mx, axis=0, keepdims=True) + shift
            if first:
                m_new = m_cur
            else:
                m_old = m_ref[j]
                m_new = jnp.maximum(m_old, m_cur)
                alpha = jnp.exp2(m_old - m_new)
            m_ref[j] = m_new
            m_loc = m_new - shift
            ls = None
            for r in range(0, tk, STRIP):
                p = jnp.exp2(s_ref[buf, j, r:r + STRIP, :] - m_loc)
                ls = p if ls is None else ls + p
                p_ref[j, r:r + STRIP, :] = p.astype(BF16)
            l_cur = jnp.sum(ls, axis=0, keepdims=True)
            pv = _dot(vc, p_ref[j])
            if first:
                l_ref[j] = l_cur
                acc_ref[j] = pv
            else:
                l_ref[j] = alpha * l_ref[j] + l_cur
                acc_ref[j] = alpha * acc_ref[j] + pv

    k_diag = pl.multiple_of(q0, tk)
    last = jnp.maximum(i - 1, 0)
    k_of = lambda c: pl.multiple_of(jnp.minimum(c, last) * tk, tk)
    scores(ka_s_ref, k_diag, 1)
    scores(ka_s_ref, k_of(0), 0)
    softmax_pv(vs_ref, k_diag, 1, 0, True)

    def slc_pair(cc, carry):
        c0 = 2 * cc
        scores(ka_s_ref, k_of(c0 + 1), 1)
        softmax_pv(vs_ref, k_of(c0), 0, None, False)

        @pl.when(c0 + 1 < i)
        def _():
            scores(ka_s_ref, k_of(c0 + 2), 0)
            softmax_pv(vs_ref, k_of(c0 + 1), 1, None, False)
        return carry

    lax.fori_loop(0, (i + 1) // 2, slc_pair, 0)
    for j in range(Q_PER_KV):
        os_ref[j] = acc_ref[j] / l_ref[j]

    n_back = WINDOW // tk
    back_k0 = lambda back: pl.multiple_of(jnp.maximum(q0 - back * tk, 0), tk)
    scores(ka_w_ref, k_diag, 0)
    scores(ka_w_ref, back_k0(1), 1)
    softmax_pv(vw_ref, k_diag, 0, 0, True)
    for back in range(1, n_back + 1):
        @pl.when(i >= back)
        def _(back=back):
            if back < n_back:
                scores(ka_w_ref, back_k0(back + 1), (back + 1) % 2)
            softmax_pv(vw_ref, back_k0(back), back % 2, 1 if back == n_back else None, False)

    for j in range(Q_PER_KV):
        base = (g * Q_PER_KV + j) * 3
        out_t = (sig_ref[pl.ds(base, 1), :] * oc_ref[j]
                 + sig_ref[pl.ds(base + 1, 1), :] * os_ref[j]
                 + sig_ref[pl.ds(base + 2, 1), :] * (acc_ref[j] / l_ref[j]))
        o_ref[:, j * HEAD_DIM:(j + 1) * HEAD_DIM] = out_t.T.astype(o_ref.dtype)


def _key_aux(seq, tk, with_blocks):
    aux = np.zeros((seq, ATTN_K - HEAD_DIM), np.float32)
    pos = np.arange(seq)
    aux[:, AUX_SLOPE - HEAD_DIM:AUX_SLOPE - HEAD_DIM + 3] = (pos % tk)[:, None]
    if with_blocks:
        aux[pos, AUX_SEL - HEAD_DIM + pos // SLC_BLOCK] = 1.0
    return aux


def selected_window_attention(qt, kvr, vt, sel, oct, ngt, batch, round_src=None, tq=256):
    m = qt.shape[2]
    seq = m // batch
    nq = seq // tq
    n_slc = seq // SLC_BLOCK
    assert tq <= 256 and AUX_SEL + n_slc <= ATTN_K
    aux_s = jnp.asarray(_key_aux(seq, tq, True), BF16)
    aux_w = jnp.asarray(_key_aux(seq, tq, False), BF16)
    slopes = jnp.asarray(_alibi_slopes())
    n_g = N_KV_HEADS

    def k_spec(section):
        return pl.BlockSpec((None, seq, HEAD_DIM), lambda b, g, i: (section * n_g + g, b, 0))

    def v_spec(section):
        return pl.BlockSpec((None, HEAD_DIM, seq), lambda b, g, i: (section * n_g + g, 0, b))

    head_blk = pl.BlockSpec((Q_PER_KV, HEAD_DIM, tq), lambda b, g, i: (g, 0, b * nq + i))
    aux_blk = pl.BlockSpec((seq, ATTN_K - HEAD_DIM), lambda b, g, i: (0, 0))
    in_specs = [pl.BlockSpec(memory_space=pltpu.SMEM),
                head_blk,
                k_spec(2), v_spec(0), k_spec(3), v_spec(1),
                aux_blk, aux_blk,
                pl.BlockSpec((None, None, n_slc, tq), lambda b, g, i: (b, g, 0, i)),
                head_blk,
                pl.BlockSpec((LANES, tq), lambda b, g, i: (0, b * nq + i))]
    out_specs = [pl.BlockSpec((tq, GROUP_WIDTH), lambda b, g, i: (b * nq + i, g))]
    out_shape = [jax.ShapeDtypeStruct((m, NSA_WIDTH), BF16)]
    operands = [slopes, qt, kvr, vt, kvr, vt, aux_s, aux_w, sel, oct, ngt]
    if round_src is not None:
        steps = batch * N_KV_HEADS * nq
        rows, cols = round_src.shape
        assert rows % steps == 0 and (rows // steps) % BF16_SUBLANES == 0
        side_blk = pl.BlockSpec((rows // steps, cols), lambda b, g, i: ((b * N_KV_HEADS + g) * nq + i, 0))
        in_specs.append(side_blk)
        out_specs.append(side_blk)
        out_shape.append(jax.ShapeDtypeStruct((rows, cols), BF16))
        operands.append(round_src)
    res = pl.pallas_call(
        functools.partial(_attn_kernel, round_weights=round_src is not None),
        grid=(batch, N_KV_HEADS, nq),
        in_specs=in_specs,
        out_specs=out_specs,
        out_shape=out_shape,
        scratch_shapes=[pltpu.VMEM((seq, ATTN_K), BF16),
                        pltpu.VMEM((seq, ATTN_K), BF16),
                        pltpu.VMEM((Q_PER_KV, ATTN_K, tq), BF16),
                        pltpu.VMEM((2, Q_PER_KV, tq, tq), F32),
                        pltpu.VMEM((Q_PER_KV, tq, tq), BF16),
                        pltpu.VMEM((2, tq, tq), F32),
                        pltpu.VMEM((Q_PER_KV, 1, tq), F32),
                        pltpu.VMEM((Q_PER_KV, 1, tq), F32),
                        pltpu.VMEM((Q_PER_KV, HEAD_DIM, tq), F32),
                        pltpu.VMEM((Q_PER_KV, HEAD_DIM, tq), F32),
                        pltpu.VMEM((LANES, tq), F32)],
        compiler_params=_params("arbitrary", "arbitrary", "arbitrary"),
        name="nsa_selected_window",
    )(*operands)
    return (res[0], res[1]) if round_src is not None else (res[0], None)


def _sgu_kernel(u_ref, v_ref, lg_ref, lb_ref, ws_ref, bs_ref, o_ref):
    rows = u_ref.shape[0]
    v = _gelu(v_ref[...])
    mu = jnp.mean(v, axis=-1, keepdims=True)
    var = jnp.mean(jnp.square(v - mu), axis=-1, keepdims=True)
    vl = ((v - mu) * lax.rsqrt(var + NORM_EPS) * lg_ref[...] + lb_ref[...]).astype(BF16)
    r = lax.broadcasted_iota(jnp.int32, (SGU_CHUNK, SGU_CHUNK), 0)
    c = lax.broadcasted_iota(jnp.int32, (SGU_CHUNK, SGU_CHUNK), 1)
    gd = SGU_WIDTH // SGU_GROUPS
    for grp in range(SGU_GROUPS):
        w = jnp.where(c <= r, ws_ref[grp], 0.0).astype(BF16)
        lanes = slice(grp * gd, (grp + 1) * gd)
        for n in range(rows // SGU_CHUNK):
            rs = slice(n * SGU_CHUNK, (n + 1) * SGU_CHUNK)
            vm = _dot(w, vl[rs, lanes]) + bs_ref[:, lanes]
            o_ref[rs, lanes] = (_gelu(u_ref[rs, lanes]) * vm).astype(o_ref.dtype)


def spatial_gating(rest, ln_g, ln_b, w_s, b_s, tm=512):
    m = rest.shape[0]
    gd = SGU_WIDTH // SGU_GROUPS
    bias = jnp.repeat(b_s.T, gd, axis=1)
    return pl.pallas_call(
        _sgu_kernel,
        grid=(m // tm,),
        in_specs=[pl.BlockSpec((tm, SGU_WIDTH), lambda i: (i, 0)),
                  pl.BlockSpec((tm, SGU_WIDTH), lambda i: (i, 1)),
                  pl.BlockSpec((1, SGU_WIDTH), lambda i: (0, 0)),
                  pl.BlockSpec((1, SGU_WIDTH), lambda i: (0, 0)),
                  pl.BlockSpec((SGU_GROUPS, SGU_CHUNK, SGU_CHUNK), lambda i: (0, 0, 0)),
                  pl.BlockSpec((SGU_CHUNK, SGU_WIDTH), lambda i: (0, 0))],
        out_specs=pl.BlockSpec((tm, SGU_WIDTH), lambda i: (i, 0)),
        out_shape=jax.ShapeDtypeStruct((m, SGU_WIDTH), BF16),
        compiler_params=_params("parallel"),
        name="spatial_gating",
    )(rest, rest, ln_g.reshape(1, -1), ln_b.reshape(1, -1), w_s, bias)


def merge_branches(oa, ob, rest, w_a, w_b, layer, tm=512, tn=1024):
    m = oa.shape[0]
    d = w_a.shape[2]
    ga_blk = 2 * SGU_WIDTH // tn
    gb_blk = (2 * SGU_WIDTH + d) // tn

    def epi(accs, e_refs, o_refs):
        y = _sigmoid(e_refs[0][...]) * accs[0] + _sigmoid(e_refs[1][...]) * accs[1]
        o_refs[0][...] = y.astype(BF16)

    (y,) = mm_full(
        "merge_branches", (d // tn, m // tm),
        [(oa, (tm, oa.shape[1]), lambda j, i: (i, 0)), (ob, (tm, ob.shape[1]), lambda j, i: (i, 0))],
        [(w_a, (None, w_a.shape[1], tn), lambda j, i: (layer, 0, j)),
         (w_b, (None, w_b.shape[1], tn), lambda j, i: (layer, 0, j))],
        [(rest, (tm, tn), lambda j, i: (i, ga_blk + j)), (rest, (tm, tn), lambda j, i: (i, gb_blk + j))],
        [(jax.ShapeDtypeStruct((m, d), BF16), (tm, tn), lambda j, i: (i, j))],
        [(0, 0), (1, 1)], epi)
    return y


def residual_matmul(name, a, w, layer, x, tm=512, tn=1024):
    m, k = a.shape
    n = w.shape[2]

    def epi(accs, e_refs, o_refs):
        o_refs[0][...] = e_refs[0][...] + accs[0]

    (out,) = mm_full(
        name, (n // tn, m // tm),
        [(a, (tm, k), lambda j, i: (i, 0))],
        [(w, (None, k, tn), lambda j, i: (layer, 0, j))],
        [(x, (tm, tn), lambda j, i: (i, j))],
        [(jax.ShapeDtypeStruct((m, n), F32), (tm, tn), lambda j, i: (i, j))],
        [(0, 0)], epi)
    return out


def swiglu_up(h, w1, w3, tm=1024, tn=512):
    m, d = h.shape
    f = w1.shape[1]

    def epi(accs, e_refs, o_refs):
        o_refs[0][...] = (accs[0] * _sigmoid(accs[0]) * accs[1]).astype(BF16)

    (out,) = mm_full(
        "swiglu_up", (f // tn, m // tm),
        [(h, (tm, d), lambda j, i: (i, 0))],
        [(w1, (d, tn), lambda j, i: (0, j)), (w3, (d, tn), lambda j, i: (0, j))],
        [],
        [(jax.ShapeDtypeStruct((m, f), BF16), (tm, tn), lambda j, i: (i, j))],
        [(0, 0), (0, 1)], epi)
    return out


MOE_TILE = 512
ROW_QUARTERS = 4


def _row_options(tile):
    step = tile // ROW_QUARTERS
    return tuple(range(step, tile + 1, step))


def _route_kernel(sel_ref, w_ref, tri_ref, pos_ref, wt_ref, meta_ref, cum_ref, *, tile):
    n_e, m = sel_ref.shape
    ck = tri_ref.shape[0]
    carry = jnp.zeros((n_e, 1), F32)
    for c in range(m // ck):
        sl = slice(c * ck, (c + 1) * ck)
        cs = _dot(sel_ref[:, sl].astype(BF16), tri_ref[...]) + carry
        cum_ref[:, sl] = cs
        carry = cs[:, ck - 1:ck]
    padded = jnp.ceil(carry / tile) * tile
    sub = lax.broadcasted_iota(jnp.int32, (n_e, 1), 0)
    start = jnp.zeros((n_e, 1), F32)
    run = jnp.zeros((1, 1), F32)
    for ex in range(n_e):
        start = jnp.where(sub == ex, run, start)
        run = run + padded[ex:ex + 1, :]
    sel = sel_ref[...] > 0.5
    pos = start + cum_ref[...] - 1.0
    eidx = lax.broadcasted_iota(jnp.int32, (n_e, m), 0)
    e_lo = jnp.min(jnp.where(sel, eidx, n_e), axis=0, keepdims=True)
    e_hi = jnp.max(jnp.where(sel, eidx, -1), axis=0, keepdims=True)
    is_lo = eidx == e_lo
    is_hi = eidx == e_hi
    w = w_ref[...]
    pos_a = jnp.sum(jnp.where(is_lo, pos, 0.0), axis=0, keepdims=True)
    pos_b = jnp.sum(jnp.where(is_hi, pos, 0.0), axis=0, keepdims=True)
    w_a = jnp.sum(jnp.where(is_lo, w, 0.0), axis=0, keepdims=True)
    w_b = jnp.sum(jnp.where(is_hi, w, 0.0), axis=0, keepdims=True)
    pos_ref[...] = jnp.where(eidx == 0, pos_a, jnp.where(eidx == 1, pos_b, 0.0)).astype(jnp.int32)
    wt_ref[...] = jnp.where(eidx == 0, w_a, jnp.where(eidx == 1, w_b, 0.0)).T
    tile_lo = lax.broadcasted_iota(jnp.int32, (n_e, LANES), 1).astype(F32) * tile
    t_exp = jnp.sum(jnp.where(start + padded <= tile_lo, 1.0, 0.0), axis=0, keepdims=True)
    t_exp = jnp.minimum(t_exp, n_e - 1.0)
    r8 = lax.broadcasted_iota(jnp.int32, (n_e, LANES), 0)
    last_row = jnp.sum(jnp.where(r8.astype(F32) == t_exp, start + carry, 0.0), axis=0, keepdims=True)
    filled = jnp.clip(last_row - tile_lo[0:1], 0.0, float(tile))
    meta_ref[...] = jnp.where(r8 == 0, t_exp, jnp.where(r8 == 1, run / tile,
                                                       jnp.where(r8 == 2, filled, 0.0))).astype(jnp.int32)


def moe_route(sel, w, tile):
    n_e, m = sel.shape
    ck = 256
    tri = jnp.asarray(np.triu(np.ones((ck, ck), np.float32)), BF16)
    full = lambda shape: pl.BlockSpec(shape, lambda: tuple(0 for _ in shape))
    return pl.pallas_call(
        functools.partial(_route_kernel, tile=tile),
        in_specs=[full((n_e, m)), full((n_e, m)), full((ck, ck))],
        out_specs=[full((n_e, m)), full((m, n_e)), full((n_e, LANES))],
        out_shape=[jax.ShapeDtypeStruct((n_e, m), jnp.int32),
                   jax.ShapeDtypeStruct((m, n_e), F32),
                   jax.ShapeDtypeStruct((n_e, LANES), jnp.int32)],
        scratch_shapes=[pltpu.VMEM((n_e, m), F32)],
        compiler_params=pltpu.CompilerParams(vmem_limit_bytes=VMEM_LIMIT),
        name="moe_route",
    )(sel, w, tri)


def _row_gather_start(src_hbm, dst, rows_ref, base, n_rows, sem, unroll=8):
    def body(r, carry):
        src_row = rows_ref[base + r]
        pltpu.make_async_copy(src_hbm.at[pl.ds(src_row, 1)], dst.at[pl.ds(r, 1)], sem).start()
        return carry
    lax.fori_loop(0, n_rows, body, 0, unroll=unroll)


def _row_gather_wait(src_hbm, dst, sem):
    pltpu.make_async_copy(src_hbm.at[pl.ds(0, dst.shape[0])], dst, sem).wait()


def _moe_gather_kernel(pos_ref, meta_ref, h_hbm, zeros_hbm, o_ref, tok_ref, buf_ref, sem_ref):
    i = pl.program_id(0)
    n = meta_ref[1, 0]
    tile = o_ref.shape[0]
    m = pos_ref.shape[0] // 2

    @pl.when(i == 0)
    def _():
        clear = pltpu.make_async_copy(zeros_hbm, tok_ref, sem_ref.at[2])
        clear.start()
        clear.wait()

        def fill(t, carry):
            tok_ref[pos_ref[t]] = t
            tok_ref[pos_ref[m + t]] = t
            return carry
        lax.fori_loop(0, m, fill, 0, unroll=8)
        _row_gather_start(h_hbm, buf_ref.at[0], tok_ref, 0, tile, sem_ref.at[0])

    @pl.when(i + 1 < n)
    def _():
        nxt = (i + 1) % 2
        _row_gather_start(h_hbm, buf_ref.at[nxt], tok_ref, (i + 1) * tile, tile, sem_ref.at[nxt])

    @pl.when(i < n)
    def _():
        cur = i % 2
        _row_gather_wait(h_hbm, buf_ref.at[cur], sem_ref.at[cur])
        o_ref[...] = buf_ref[cur].astype(o_ref.dtype)

    @pl.when(i >= n)
    def _():
        o_ref[...] = jnp.zeros_like(o_ref)


def moe_gather(pos2, meta, h, n_tiles, tile):
    m, d = h.shape
    rows = n_tiles * tile
    return pl.pallas_call(
        _moe_gather_kernel,
        grid_spec=pltpu.PrefetchScalarGridSpec(
            num_scalar_prefetch=2,
            grid=(n_tiles,),
            in_specs=[pl.BlockSpec(memory_space=pl.ANY), pl.BlockSpec(memory_space=pl.ANY)],
            out_specs=pl.BlockSpec((tile, d), lambda i, pos, meta: (i, 0)),
            scratch_shapes=[pltpu.SMEM((rows,), jnp.int32),
                            pltpu.VMEM((2, tile, d), F32),
                            pltpu.SemaphoreType.DMA((3,))]),
        out_shape=jax.ShapeDtypeStruct((rows, d), BF16),
        compiler_params=_params("arbitrary"),
        name="moe_gather",
    )(pos2, meta, h, jnp.zeros((rows,), jnp.int32))


def _moe_up_kernel(meta_ref, x_ref, w1_ref, w3_ref, w2_ref, o_ref, w2b_ref):
    i = pl.program_id(1)

    def round_w2_block():
        w2b_ref[...] = w2_ref[...].astype(BF16)

    filled = meta_ref[2, i]
    tile = o_ref.shape[0]
    for rows in _row_options(tile):
        @pl.when((filled > rows - tile // ROW_QUARTERS) & (filled <= rows))
        def _(rows=rows):
            x = x_ref[0:rows, :]
            for c in range(0, o_ref.shape[1], COL_CHUNK):
                sl = slice(c, c + COL_CHUNK)
                a1 = _dot(x, w1_ref[:, sl])
                a3 = _dot(x, w3_ref[:, sl])
                o_ref[0:rows, sl] = (a1 * _sigmoid(a1) * a3).astype(o_ref.dtype)
            if rows < tile:
                o_ref[rows:tile, :] = jnp.zeros((tile - rows, o_ref.shape[1]), o_ref.dtype)
            round_w2_block()

    @pl.when(filled == 0)
    def _():
        o_ref[...] = jnp.zeros_like(o_ref)
        round_w2_block()


BF16_SUBLANES = 16


def _cast_rows_per_step(total_rows, steps):
    for rows in range(BF16_SUBLANES, total_rows + 1, BF16_SUBLANES):
        if total_rows % rows == 0 and total_rows // rows <= steps:
            return rows
    raise ValueError("weights cannot be split over the grid steps")


def moe_up(meta, xs, w1, w3, w2, tile, tn=1792):
    rows, d = xs.shape
    n_e, _, f = w1.shape
    assert f % tn == 0 and rows % tile == 0 and w1.dtype == BF16 and w3.dtype == BF16
    n_tiles = rows // tile
    w2_flat = w2.reshape(n_e * f, w2.shape[2])
    cast_rows = _cast_rows_per_step(w2_flat.shape[0], (f // tn) * n_tiles)
    n_cast = w2_flat.shape[0] // cast_rows
    last = lambda i, meta: jnp.maximum(jnp.minimum(i, meta[1, 0] - 1), 0)
    cast_blk = lambda j, i, meta: (jnp.minimum(j * n_tiles + i, n_cast - 1), 0)
    up, w2b = pl.pallas_call(
        _moe_up_kernel,
        grid_spec=pltpu.PrefetchScalarGridSpec(
            num_scalar_prefetch=1,
            grid=(f // tn, n_tiles),
            in_specs=[pl.BlockSpec((tile, d), lambda j, i, meta: (last(i, meta), 0)),
                      pl.BlockSpec((None, d, tn), lambda j, i, meta: (meta[0, i], 0, j)),
                      pl.BlockSpec((None, d, tn), lambda j, i, meta: (meta[0, i], 0, j)),
                      pl.BlockSpec((cast_rows, w2_flat.shape[1]), cast_blk)],
            out_specs=[pl.BlockSpec((tile, tn), lambda j, i, meta: (i, j)),
                       pl.BlockSpec((cast_rows, w2_flat.shape[1]), cast_blk)]),
        out_shape=[jax.ShapeDtypeStruct((rows, f), BF16),
                   jax.ShapeDtypeStruct(w2_flat.shape, BF16)],
        compiler_params=_params("arbitrary", "arbitrary"),
        name="moe_up",
    )(meta, xs, w1, w3, w2_flat)
    return up, w2b.reshape(w2.shape)


def _moe_down_kernel(meta_ref, a_ref, w_ref, o_ref):
    i = pl.program_id(0)
    k = pl.program_id(1)

    filled = meta_ref[2, i]
    tile = o_ref.shape[0]
    cn = COL_CHUNK
    for rows in _row_options(tile):
        in_range = (filled > rows - tile // ROW_QUARTERS) & (filled <= rows)

        @pl.when(in_range & (k == 0))
        def _(rows=rows):
            a = a_ref[0:rows, :]
            for c in range(0, o_ref.shape[1], cn):
                o_ref[0:rows, c:c + cn] = _dot(a, w_ref[:, c:c + cn])
            if rows < tile:
                o_ref[rows:tile, :] = jnp.zeros((tile - rows, o_ref.shape[1]), o_ref.dtype)

        @pl.when(in_range & (k > 0))
        def _(rows=rows):
            a = a_ref[0:rows, :]
            for c in range(0, o_ref.shape[1], cn):
                o_ref[0:rows, c:c + cn] += _dot(a, w_ref[:, c:c + cn])

    @pl.when((filled == 0) & (k == 0))
    def _():
        o_ref[...] = jnp.zeros_like(o_ref)


def moe_down(meta, up, w2, tile, tk=3584):
    rows, f = up.shape
    d = w2.shape[2]
    assert f % tk == 0 and rows % tile == 0
    n_tiles = rows // tile
    last = lambda i, meta: jnp.maximum(jnp.minimum(i, meta[1, 0] - 1), 0)
    return pl.pallas_call(
        _moe_down_kernel,
        grid_spec=pltpu.PrefetchScalarGridSpec(
            num_scalar_prefetch=1,
            grid=(n_tiles, f // tk),
            in_specs=[pl.BlockSpec((tile, tk), lambda i, k, meta: (last(i, meta), k)),
                      pl.BlockSpec((None, tk, d), lambda i, k, meta: (meta[0, i], k, 0))],
            out_specs=pl.BlockSpec((tile, d), lambda i, k, meta: (i, 0))),
        out_shape=jax.ShapeDtypeStruct((rows, d), F32),
        compiler_params=_params("arbitrary", "arbitrary"),
        name="moe_down",
    )(meta, up, w2)


def _moe_combine_kernel(pos_ref, x_ref, wt_ref, g_ref, y_hbm, o_ref, buf_ref, sem_ref, *, final_norm):
    i = pl.program_id(0)
    n = pl.num_programs(0)
    tc = x_ref.shape[0]
    m = pos_ref.shape[0] // 2

    def start(step, slot):
        for s in range(2):
            _row_gather_start(y_hbm, buf_ref.at[slot, s], pos_ref, s * m + step * tc, tc, sem_ref.at[slot, s])

    @pl.when(i == 0)
    def _():
        start(0, 0)

    @pl.when(i + 1 < n)
    def _():
        start(i + 1, (i + 1) % 2)

    cur = i % 2
    for s in range(2):
        _row_gather_wait(y_hbm, buf_ref.at[cur, s], sem_ref.at[cur, s])
    wt = wt_ref[...]
    out = x_ref[...] + wt[:, 0:1] * buf_ref[cur, 0] + wt[:, 1:2] * buf_ref[cur, 1]
    o_ref[...] = _rms(out, g_ref[...]) if final_norm else out


def moe_combine(pos2, x, wt, y, norm_g=None, tc=512):
    m, d = x.shape
    final_norm = norm_g is not None
    g = (norm_g if final_norm else jnp.ones((d,), F32)).reshape(1, d)
    return pl.pallas_call(
        functools.partial(_moe_combine_kernel, final_norm=final_norm),
        grid_spec=pltpu.PrefetchScalarGridSpec(
            num_scalar_prefetch=1,
            grid=(m // tc,),
            in_specs=[pl.BlockSpec((tc, d), lambda i, pos: (i, 0)),
                      pl.BlockSpec((tc, wt.shape[1]), lambda i, pos: (i, 0)),
                      pl.BlockSpec((1, d), lambda i, pos: (0, 0)),
                      pl.BlockSpec(memory_space=pl.ANY)],
            out_specs=pl.BlockSpec((tc, d), lambda i, pos: (i, 0)),
            scratch_shapes=[pltpu.VMEM((2, 2, tc, d), F32),
                            pltpu.SemaphoreType.DMA((2, 2))]),
        out_shape=jax.ShapeDtypeStruct((m, d), F32),
        compiler_params=_params("arbitrary"),
        name="moe_combine",
    )(pos2, x, wt, g, y)


def moe_layer(x, norm_g, router_w, router_b, w1, w3, w2, out_norm_g=None):
    m, d = x.shape
    n_e = router_w.shape[1]
    tile = MOE_TILE
    n_tiles = 2 * m // tile + n_e
    h, sel, w = norm_router(x, norm_g, router_w, router_b)
    pos, wt, meta = moe_route(sel, w, tile)
    pos2 = pos[:2].reshape(-1)
    xs = moe_gather(pos2, meta, h, n_tiles, tile)
    up, w2_bf16 = moe_up(meta, xs, w1, w3, w2, tile)
    y = moe_down(meta, up, w2_bf16, tile)
    return moe_combine(pos2, x, wt, y, out_norm_g)


def _mixer(x, batch, layer, norm_g, w_in_t, pe_k, pe_v, kw1, kw2, vw1, vw2, ln_g, ln_b, w_s, b_s, w_a, w_b, w_out,
           round_src=None):
    h = rmsnorm(x, norm_g, BF16)
    qt, kvr, vt, ngt, rest = in_projection(h, w_in_t, layer)
    cmp_kv = compress(kvr, batch, pe_k, pe_v, kw1, kw2, vw1, vw2)
    oct, sel = compressed_and_select(qt, cmp_kv, batch)
    oa, rounded = selected_window_attention(qt, kvr, vt, sel, oct, ngt, batch, round_src)
    ob = spatial_gating(rest, ln_g, ln_b, w_s, b_s)
    y = merge_branches(oa, ob, rest, w_a, w_b, layer)
    return residual_matmul("out_proj", y, w_out, layer, x), rounded


def kernel(x, norm_mix, w_in, cmp_pe_k, cmp_pe_v, cmp_k_w1, cmp_k_w2, cmp_v_w1, cmp_v_w2, sgu_ln_g, sgu_ln_b, sgu_w, sgu_b, w_branch_a, w_branch_b, w_out, norm_ffn, ffn_w1, ffn_w3, ffn_w2, router_w, router_b, moe_w1, moe_w3, moe_w2, norm_f):
    batch, seq, d = x.shape
    depth = norm_mix.shape[0]
    xf = x.reshape(batch * seq, d)
    w_in_t = jnp.swapaxes(w_in, 1, 2)
    normed = False
    n_moe = moe_w1.shape[0]
    moe_up_bf16 = {}
    for layer in range(depth):
        j = layer // 2
        round_src = None
        if j < n_moe and 2 * j + 1 < depth:
            w_f32 = moe_w1[j] if layer % 2 == 0 else moe_w3[j]
            round_src = w_f32.reshape(-1, w_f32.shape[-1])
        xf, rounded = _mixer(xf, batch, layer, norm_mix[layer], w_in_t, cmp_pe_k[layer], cmp_pe_v[layer],
                             cmp_k_w1[layer], cmp_k_w2[layer], cmp_v_w1[layer], cmp_v_w2[layer],
                             sgu_ln_g[layer], sgu_ln_b[layer], sgu_w[layer], sgu_b[layer],
                             w_branch_a, w_branch_b, w_out, round_src)
        if rounded is not None:
            moe_up_bf16[layer % 2] = rounded.reshape(moe_w1[j].shape)
        if layer % 2 == 0:
            h = rmsnorm(xf, norm_ffn[layer], BF16)
            up = swiglu_up(h, ffn_w1[j], ffn_w3[j])
            xf = residual_matmul("ffn_down", up, ffn_w2, j, xf, tm=512, tn=512)
        else:
            normed = layer == depth - 1
            xf = moe_layer(xf, norm_ffn[layer], router_w[j], router_b[j], moe_up_bf16[0], moe_up_bf16[1],
                           moe_w2[j], norm_f if normed else None)
    if not normed:
        xf = rmsnorm(xf, norm_f, F32)
    return xf.reshape(batch, seq, d)
```

```python
import functools

import numpy as np
import jax
import jax.numpy as jnp
from jax import lax
from jax.experimental import pallas as pl
from jax.experimental.pallas import tpu as pltpu

F32 = jnp.float32
BF16 = jnp.bfloat16

D_MODEL = 2048
N_Q_HEADS = 16
N_KV_HEADS = 4
HEAD_DIM = 64
Q_PER_KV = N_Q_HEADS // N_KV_HEADS
NSA_WIDTH = N_Q_HEADS * HEAD_DIM
KV_WIDTH = N_KV_HEADS * HEAD_DIM
GROUP_WIDTH = Q_PER_KV * HEAD_DIM
CMP_BLOCK = 32
CMP_STRIDE = 16
CMP_HIDDEN = 256
SLC_BLOCK = 64
N_SELECT = 16
WINDOW = 512
FORCED_SCORE = 1e4
SGU_WIDTH = D_MODEL // 2
SGU_GROUPS = 8
SGU_CHUNK = 128
N_EXPERTS = 8
NORM_EPS = 1e-5
NEG = -1e30
LOG2E = 1.4426950408889634
LANES = 128
VMEM_LIMIT = 56 * 1024 * 1024

Q_OFF = 0
KV_OFF = NSA_WIDTH
NG_OFF = KV_OFF + 6 * KV_WIDTH
NG_WIDTH = 3 * N_Q_HEADS
REST_OFF = NG_OFF + NG_WIDTH


def _params(*sem):
    return pltpu.CompilerParams(dimension_semantics=sem, vmem_limit_bytes=VMEM_LIMIT)


def _dot(a, b):
    return jnp.dot(a, b, preferred_element_type=F32)


def _dot_nt(a, b):
    return lax.dot_general(a, b, (((1,), (1,)), ((), ())), preferred_element_type=F32)


def _dot_split(a_f32, b_bf16):
    hi = a_f32.astype(BF16)
    lo = (a_f32 - hi.astype(F32)).astype(BF16)
    return _dot(hi, b_bf16) + _dot(lo, b_bf16)


def _sigmoid(x):
    return 1.0 / (1.0 + jnp.exp(-x))


def _gelu(x):
    return x * (0.5 * (1.0 + jnp.tanh(0.7978845608028654 * (x + 0.044715 * (x * x * x)))))


def _rms(x, g):
    return x * lax.rsqrt(jnp.mean(x * x, axis=-1, keepdims=True) + NORM_EPS) * g


def _rmsnorm_kernel(x_ref, g_ref, o_ref):
    o_ref[...] = _rms(x_ref[...], g_ref[...]).astype(o_ref.dtype)


def rmsnorm(x, g, out_dtype, tm=1024):
    m, d = x.shape
    return pl.pallas_call(
        _rmsnorm_kernel,
        grid=(m // tm,),
        in_specs=[pl.BlockSpec((tm, d), lambda i: (i, 0)),
                  pl.BlockSpec((1, d), lambda i: (0, 0))],
        out_specs=pl.BlockSpec((tm, d), lambda i: (i, 0)),
        out_shape=jax.ShapeDtypeStruct((m, d), out_dtype),
        compiler_params=_params("parallel"),
        name="rmsnorm",
    )(x, g.reshape(1, d))


def _norm_router_kernel(x_ref, g_ref, rwt_ref, rb_ref, h_ref, sel_ref, w_ref):
    h = _rms(x_ref[...], g_ref[...])
    h_ref[...] = h
    logits = lax.dot_general(rwt_ref[...], h, (((1,), (1,)), ((), ())), preferred_element_type=F32,
                             precision=lax.Precision.HIGHEST) + rb_ref[...]
    n_e = logits.shape[0]
    e = lax.broadcasted_iota(jnp.int32, logits.shape, 0)
    m1 = jnp.max(logits, axis=0, keepdims=True)
    i1 = jnp.min(jnp.where(logits == m1, e, n_e), axis=0, keepdims=True)
    rest = jnp.where(e == i1, -jnp.inf, logits)
    m2 = jnp.max(rest, axis=0, keepdims=True)
    i2 = jnp.min(jnp.where(rest == m2, e, n_e), axis=0, keepdims=True)
    e2 = jnp.exp(m2 - m1)
    w1 = 1.0 / (1.0 + e2)
    w2 = e2 / (1.0 + e2)
    sel_ref[...] = jnp.where((e == i1) | (e == i2), 1.0, 0.0)
    w_ref[...] = jnp.where(e == i1, w1, 0.0) + jnp.where(e == i2, w2, 0.0)


def norm_router(x, g, router_w, router_b, tm=1024):
    m, d = x.shape
    n_e = router_w.shape[1]
    return pl.pallas_call(
        _norm_router_kernel,
        grid=(m // tm,),
        in_specs=[pl.BlockSpec((tm, d), lambda i: (i, 0)),
                  pl.BlockSpec((1, d), lambda i: (0, 0)),
                  pl.BlockSpec((n_e, d), lambda i: (0, 0)),
                  pl.BlockSpec((n_e, 1), lambda i: (0, 0))],
        out_specs=[pl.BlockSpec((tm, d), lambda i: (i, 0)),
                   pl.BlockSpec((n_e, tm), lambda i: (0, i)),
                   pl.BlockSpec((n_e, tm), lambda i: (0, i))],
        out_shape=[jax.ShapeDtypeStruct((m, d), F32),
                   jax.ShapeDtypeStruct((n_e, m), F32),
                   jax.ShapeDtypeStruct((n_e, m), F32)],
        compiler_params=_params("parallel"),
        name="norm_router",
    )(x, g.reshape(1, d), router_w.T, router_b.reshape(n_e, 1))


def _mm_full_kernel(*refs, n_a, n_b, n_e, n_o, pairs, inner_axis, epilogue, b_transposed):
    a_refs = refs[:n_a]
    b_refs = refs[n_a:n_a + n_b]
    e_refs = refs[n_a + n_b:n_a + n_b + n_e]
    o_refs = refs[n_a + n_b + n_e:n_a + n_b + n_e + n_o]
    w_refs = refs[n_a + n_b + n_e + n_o:]

    @pl.when(pl.program_id(inner_axis) == 0)
    def _():
        for b_ref, w_ref in zip(b_refs, w_refs):
            w_ref[...] = b_ref[...].astype(BF16)

    dot = _dot_nt if b_transposed else _dot
    ws = [w_ref[0] if len(w_ref.shape) == 3 else w_ref[...] for w_ref in w_refs]
    accs = [dot(a_refs[ia][...], ws[ib]) for ia, ib in pairs]
    epilogue(accs, e_refs, o_refs)


def _block_dim_size(s):
    return s.block_size if isinstance(s, pl.Element) else s


def mm_full(name, grid, a_ops, b_ops, e_ops, outs, pairs, epilogue, b_transposed=False):
    ops = a_ops + b_ops + e_ops
    kernel = functools.partial(
        _mm_full_kernel, n_a=len(a_ops), n_b=len(b_ops), n_e=len(e_ops), n_o=len(outs),
        pairs=pairs, inner_axis=len(grid) - 1, epilogue=epilogue, b_transposed=b_transposed)
    scratch = [pltpu.VMEM(tuple(_block_dim_size(s) for s in blk if s is not None), BF16) for _, blk, _ in b_ops]
    res = pl.pallas_call(
        kernel,
        grid=grid,
        in_specs=[pl.BlockSpec(blk, imap) for _, blk, imap in ops],
        out_specs=[pl.BlockSpec(blk, imap) for _, blk, imap in outs],
        out_shape=[sds for sds, _, _ in outs],
        scratch_shapes=scratch,
        compiler_params=_params(*(("arbitrary",) * len(grid))),
        name=name,
    )(*[arr for arr, _, _ in ops])
    return res


COL_CHUNK = 256


def in_projection(h, w_in_t, layer):
    m, d = h.shape
    tm = 1024
    a_op = [(h, (tm, d), lambda j, i: (i, 0))]
    tm_narrow = 2048
    a_op_narrow = [(h, (tm_narrow, d), lambda j, i: (i, 0))]

    tq = NSA_WIDTH
    q_heads = tq // HEAD_DIM
    q_scale = float(np.float32(HEAD_DIM ** -0.5) * np.float32(LOG2E))

    def q_epi(accs, e_refs, o_refs):
        for c in range(q_heads):
            o_refs[0][c] = (accs[0][:, c * HEAD_DIM:(c + 1) * HEAD_DIM] * q_scale).T.astype(BF16)

    (qt,) = mm_full(
        "in_proj_q", (NSA_WIDTH // tq, m // tm), a_op,
        [(w_in_t, (None, tq, d), lambda j, i: (layer, j, 0))], [],
        [(jax.ShapeDtypeStruct((N_Q_HEADS, HEAD_DIM, m), BF16), (q_heads, HEAD_DIM, tm), lambda j, i: (j, 0, i))],
        [(0, 0)], q_epi, b_transposed=True)

    sec0 = KV_OFF // KV_WIDTH

    def rows_epi(accs, e_refs, o_refs):
        for c in range(N_KV_HEADS):
            o_refs[0][c] = accs[0][:, c * HEAD_DIM:(c + 1) * HEAD_DIM].astype(BF16)

    (kvr,) = mm_full(
        "in_proj_kv_rows", (4, m // tm_narrow), a_op_narrow,
        [(w_in_t, (None, KV_WIDTH, d), lambda j, i: (layer, sec0 + j + j // 3, 0))], [],
        [(jax.ShapeDtypeStruct((4 * N_KV_HEADS, m, HEAD_DIM), BF16),
          (N_KV_HEADS, tm_narrow, HEAD_DIM), lambda j, i: (j, i, 0))],
        [(0, 0)], rows_epi, b_transposed=True)

    def cols_epi(accs, e_refs, o_refs):
        for c in range(N_KV_HEADS):
            o_refs[0][c] = accs[0][:, c * HEAD_DIM:(c + 1) * HEAD_DIM].T.astype(BF16)

    (vt,) = mm_full(
        "in_proj_v_cols", (2, m // tm_narrow), a_op_narrow,
        [(w_in_t, (None, KV_WIDTH, d), lambda j, i: (layer, sec0 + 3 + 2 * j, 0))], [],
        [(jax.ShapeDtypeStruct((2 * N_KV_HEADS, HEAD_DIM, m), BF16),
          (N_KV_HEADS, HEAD_DIM, tm_narrow), lambda j, i: (j, 0, i))],
        [(0, 0)], cols_epi, b_transposed=True)

    def f32_epi(accs, e_refs, o_refs):
        o_refs[0][...] = accs[0]

    def f32_t_epi(accs, e_refs, o_refs):
        o_refs[0][...] = accs[0].T

    assert NG_OFF % LANES == 0
    (ngt,) = mm_full(
        "in_proj_ng", (1, m // tm_narrow), a_op_narrow,
        [(w_in_t, (None, LANES, d), lambda j, i: (layer, NG_OFF // LANES, 0))], [],
        [(jax.ShapeDtypeStruct((LANES, m), F32), (LANES, tm_narrow), lambda j, i: (0, i))],
        [(0, 0)], f32_t_epi, b_transposed=True)

    n_rest = w_in_t.shape[1] - REST_OFF
    tr = 1024
    assert n_rest % tr == 0 and REST_OFF % 8 == 0
    (rest,) = mm_full(
        "in_proj_rest", (n_rest // tr, m // tm), a_op,
        [(w_in_t, (pl.Element(1), pl.Element(tr), pl.Element(d)), lambda j, i: (layer, pl.multiple_of(REST_OFF + j * tr, 8), 0))], [],
        [(jax.ShapeDtypeStruct((m, n_rest), F32), (tm, tr), lambda j, i: (i, j))],
        [(0, 0)], f32_epi, b_transposed=True)
    return qt, kvr, vt, ngt, rest


def _cmp_kernel(a_ref, w1_ref, w2_ref, pe_ref, o_ref):
    a = a_ref[...]
    rows = a.shape[0]
    half = a.shape[1]
    w1 = w1_ref[...].astype(BF16)
    p0 = _dot(a, w1[:half])
    p1 = _dot(a, w1[half:])
    bias = _dot(pe_ref[...].astype(BF16), w1)[0:1]
    hid = p0 + pltpu.roll(p1, rows - 1, 0) + bias
    o_ref[...] = _dot(_gelu(hid).astype(BF16), w2_ref[...].astype(BF16))


def compress(kv, batch, pe_k, pe_v, kw1, kw2, vw1, vw2):
    m = kv.shape[1]
    n_chunks = m // batch // CMP_STRIDE
    rows = batch * n_chunks
    feat = CMP_STRIDE * HEAD_DIM
    a = kv[:2 * N_KV_HEADS].reshape(2, N_KV_HEADS, rows, feat)
    w1 = jnp.stack([kw1, vw1])
    w2 = jnp.stack([kw2, vw2])
    pe = jnp.stack([pe_k, pe_v]).reshape(2, 1, CMP_BLOCK * HEAD_DIM)
    pe = jnp.broadcast_to(pe, (2, 8, CMP_BLOCK * HEAD_DIM))
    return pl.pallas_call(
        _cmp_kernel,
        grid=(2, N_KV_HEADS),
        in_specs=[pl.BlockSpec((None, None, rows, feat), lambda s, g: (s, g, 0, 0)),
                  pl.BlockSpec((None, 2 * feat, CMP_HIDDEN), lambda s, g: (s, 0, 0)),
                  pl.BlockSpec((None, CMP_HIDDEN, HEAD_DIM), lambda s, g: (s, 0, 0)),
                  pl.BlockSpec((None, 8, 2 * feat), lambda s, g: (s, 0, 0))],
        out_specs=pl.BlockSpec((None, None, rows, HEAD_DIM), lambda s, g: (s, g, 0, 0)),
        out_shape=jax.ShapeDtypeStruct((2, N_KV_HEADS, rows, HEAD_DIM), F32),
        compiler_params=_params("parallel", "parallel"),
        name="nsa_compress",
    )(a, w1, w2, pe)


def _alibi_slopes():
    sl = np.array([2.0 ** (-8.0 * (h + 1) / N_Q_HEADS) for h in range(N_Q_HEADS)], dtype=np.float32)
    return sl * np.float32(LOG2E)


def _overlap_t(n_cmp_pad, n_slc):
    cs = np.arange(n_cmp_pad)[None, :] * CMP_STRIDE
    ss = np.arange(n_slc)[:, None] * SLC_BLOCK
    ov = np.clip(np.minimum(cs + CMP_BLOCK, ss + SLC_BLOCK) - np.maximum(cs, ss), 0, None)
    return (ov / CMP_STRIDE).astype(np.float32)


def _sel_kernel(slopes_ref, q_ref, kc_ref, vc_ref, ovt_ref, oc_ref, sel_ref, *, n_cmp):
    g = pl.program_id(1)
    i = pl.program_id(2)
    tq = q_ref.shape[2]
    ncp = kc_ref.shape[0]
    n_slc = ovt_ref.shape[0]
    kc = kc_ref[...].astype(BF16)
    vct = vc_ref[...].T.astype(BF16)

    t = i * tq + lax.broadcasted_iota(jnp.int32, (ncp, tq), 1)
    c = lax.broadcasted_iota(jnp.int32, (ncp, tq), 0)
    dist = t - (c * CMP_STRIDE + (CMP_BLOCK - 1))
    mask = (dist >= 0) & (c < n_cmp)
    distf = dist.astype(F32)

    scores = [_dot(kc, q_ref[j]) for j in range(Q_PER_KV)]
    probs = []
    for j in range(Q_PER_KV):
        s = scores[j] - slopes_ref[g * Q_PER_KV + j] * distf
        s = jnp.where(mask, s, NEG)
        mx = jnp.max(s, axis=0, keepdims=True)
        e = jnp.where(mask, jnp.exp2(s - mx), 0.0)
        probs.append(e / jnp.maximum(jnp.sum(e, axis=0, keepdims=True), 1e-30))
    for j in range(Q_PER_KV):
        oc_ref[j] = _dot(vct, probs[j].astype(BF16))
    p_sum = functools.reduce(lambda a, b: a + b, probs)

    ovt = ovt_ref[...]
    hi = p_sum.astype(BF16)
    lo = (p_sum - hi.astype(F32)).astype(BF16)
    imp = _dot(ovt, hi) + _dot(ovt, lo)
    tt = i * tq + lax.broadcasted_iota(jnp.int32, (n_slc, tq), 1)
    blk = lax.broadcasted_iota(jnp.int32, (n_slc, tq), 0)
    cur = tt // SLC_BLOCK
    valid = blk * SLC_BLOCK <= tt
    forced = (blk == 0) | (blk == cur) | (blk == cur - 1)
    score = jnp.where(valid, imp, -1.0)
    score = jnp.where(forced, FORCED_SCORE, score)
    rank = jnp.zeros((n_slc, tq), F32)
    for mrow in range(n_slc):
        other = jnp.broadcast_to(score[mrow:mrow + 1, :], (n_slc, tq))
        beats = (other > score) | ((other == score) & (blk > mrow))
        rank = rank + jnp.where(beats, 1.0, 0.0)
    sel_ref[...] = jnp.where(rank < float(min(N_SELECT, n_slc)), 1.0, 0.0)


def compressed_and_select(qt, cmp_kv, batch, tq=512):
    m = qt.shape[2]
    seq = m // batch
    nq = seq // tq
    ncp = cmp_kv.shape[2] // batch
    n_slc = seq // SLC_BLOCK
    ovt = jnp.asarray(_overlap_t(ncp, n_slc), BF16)
    slopes = jnp.asarray(_alibi_slopes())
    kernel = functools.partial(_sel_kernel, n_cmp=ncp - 1)
    return pl.pallas_call(
        kernel,
        grid=(batch, N_KV_HEADS, nq),
        in_specs=[pl.BlockSpec(memory_space=pltpu.SMEM),
                  pl.BlockSpec((Q_PER_KV, HEAD_DIM, tq), lambda b, g, i: (g, 0, b * nq + i)),
                  pl.BlockSpec((None, None, ncp, HEAD_DIM), lambda b, g, i: (0, g, b, 0)),
                  pl.BlockSpec((None, None, ncp, HEAD_DIM), lambda b, g, i: (1, g, b, 0)),
                  pl.BlockSpec((n_slc, ncp), lambda b, g, i: (0, 0))],
        out_specs=[pl.BlockSpec((Q_PER_KV, HEAD_DIM, tq), lambda b, g, i: (g, 0, b * nq + i)),
                   pl.BlockSpec((None, None, n_slc, tq), lambda b, g, i: (b, g, 0, i))],
        out_shape=[jax.ShapeDtypeStruct((N_Q_HEADS, HEAD_DIM, m), F32),
                   jax.ShapeDtypeStruct((batch, N_KV_HEADS, n_slc, seq), F32)],
        compiler_params=_params("parallel", "parallel", "parallel"),
        name="nsa_compressed_select",
    )(slopes, qt, cmp_kv, cmp_kv, ovt)


AUX_SLOPE = HEAD_DIM
AUX_SEL = HEAD_DIM + 16
ATTN_K = 2 * HEAD_DIM
STRIP = 32


def _attn_kernel(slopes_ref, q_ref, ks_ref, vs_ref, kw_ref, vw_ref, auxs_ref, auxw_ref, sel_ref, oc_ref,
                 ng_ref, *rest, round_weights):
    if round_weights:
        wsrc_ref, o_ref, wdst_ref = rest[:3]
        scratch = rest[3:]

        def side_work():
            wdst_ref[...] = wsrc_ref[...].astype(BF16)
    else:
        o_ref = rest[0]
        scratch = rest[1:]
        side_work = lambda: None
    _attn_body(slopes_ref, q_ref, ks_ref, vs_ref, kw_ref, vw_ref, auxs_ref, auxw_ref, sel_ref, oc_ref, ng_ref,
               o_ref, *scratch, side_work=side_work)


def _attn_body(slopes_ref, q_ref, ks_ref, vs_ref, kw_ref, vw_ref, auxs_ref, auxw_ref, sel_ref, oc_ref,
               ng_ref, o_ref, ka_s_ref, ka_w_ref, qa_ref, s_ref, p_ref, mask_ref, m_ref, l_ref, acc_ref,
               os_ref, sig_ref, *, side_work):
    g = pl.program_id(1)
    i = pl.program_id(2)
    tq = q_ref.shape[2]
    tk = tq
    q0 = i * tq
    n_slc = sel_ref.shape[0]

    @pl.when(i == 0)
    def _():
        ka_s_ref[:, 0:HEAD_DIM] = ks_ref[...]
        ka_s_ref[:, HEAD_DIM:ATTN_K] = auxs_ref[...]
        ka_w_ref[:, 0:HEAD_DIM] = kw_ref[...]
        ka_w_ref[:, HEAD_DIM:ATTN_K] = auxw_ref[...]
        kr = lax.broadcasted_iota(jnp.int32, (tk, tq), 0)
        qc = lax.broadcasted_iota(jnp.int32, (tk, tq), 1)
        mask_ref[0] = jnp.where(kr <= qc, 0.0, NEG)
        mask_ref[1] = jnp.where(kr > qc, 0.0, NEG)

    sig_ref[...] = _sigmoid(ng_ref[...])
    side_work()
    sel_neg =(sel_ref[...] - 1.0) * (-NEG)
    row16 = lax.broadcasted_iota(jnp.int32, (16, tq), 0)
    pad = jnp.zeros((ATTN_K - AUX_SEL - n_slc, tq), F32)
    for j in range(Q_PER_KV):
        sl = jnp.full((16, tq), slopes_ref[g * Q_PER_KV + j], F32)
        hi = sl.astype(BF16).astype(F32)
        mid = (sl - hi).astype(BF16).astype(F32)
        lo = (sl - hi - mid).astype(BF16).astype(F32)
        pieces = jnp.where(row16 == 0, hi, jnp.where(row16 == 1, mid, jnp.where(row16 == 2, lo, 0.0)))
        qa = jnp.concatenate([q_ref[j].astype(F32), pieces, sel_neg, pad], axis=0)
        qa_ref[j] = qa.astype(BF16)

    def scores(ka_ref, k0, buf):
        ka = ka_ref[pl.ds(k0, tk), :]
        for j in range(Q_PER_KV):
            s_ref[buf, j] = _dot(ka, qa_ref[j])

    def softmax_pv(v_ref, k0, buf, mask_idx, first):
        vc = v_ref[:, pl.ds(k0, tk)]
        for j in range(Q_PER_KV):
            shift = slopes_ref[g * Q_PER_KV + j] * k0.astype(F32)
            mx = None
            for r in range(0, tk, STRIP):
                x = s_ref[buf, j, r:r + STRIP, :]
                if mask_idx is not None:
                    x = x + mask_ref[mask_idx, r:r + STRIP, :]
                    s_ref[buf, j, r:r + STRIP, :] = x
                mx = x if mx is None else jnp.maximum(mx, x)
            m_cur = jnp.max(mx, axis=0, keepdims=True) + shift
            if first:
                m_new = m_cur
            else:
                m_old = m_ref[j]
                m_new = jnp.maximum(m_old, m_cur)
                alpha = jnp.exp2(m_old - m_new)
            m_ref[j] = m_new
            m_loc = m_new - shift
            ls = None
            for r in range(0, tk, STRIP):
                p = jnp.exp2(s_ref[buf, j, r:r + STRIP, :] - m_loc)
                ls = p if ls is None else ls + p
                p_ref[j, r:r + STRIP, :] = p.astype(BF16)
            l_cur = jnp.sum(ls, axis=0, keepdims=True)
            pv = _dot(vc, p_ref[j])
            if first:
                l_ref[j] = l_cur
                acc_ref[j] = pv
            else:
                l_ref[j] = alpha * l_ref[j] + l_cur
                acc_ref[j] = alpha * acc_ref[j] + pv

    k_diag = pl.multiple_of(q0, tk)
    last = jnp.maximum(i - 1, 0)
    k_of = lambda c: pl.multiple_of(jnp.minimum(c, last) * tk, tk)
    scores(ka_s_ref, k_diag, 1)
    scores(ka_s_ref, k_of(0), 0)
    softmax_pv(vs_ref, k_diag, 1, 0, True)

    def slc_pair(cc, carry):
        c0 = 2 * cc
        scores(ka_s_ref, k_of(c0 + 1), 1)
        softmax_pv(vs_ref, k_of(c0), 0, None, False)

        @pl.when(c0 + 1 < i)
        def _():
            scores(ka_s_ref, k_of(c0 + 2), 0)
            softmax_pv(vs_ref, k_of(c0 + 1), 1, None, False)
        return carry

    lax.fori_loop(0, (i + 1) // 2, slc_pair, 0)
    for j in range(Q_PER_KV):
        os_ref[j] = acc_ref[j] / l_ref[j]

    n_back = WINDOW // tk
    back_k0 = lambda back: pl.multiple_of(jnp.maximum(q0 - back * tk, 0), tk)
    scores(ka_w_ref, k_diag, 0)
    scores(ka_w_ref, back_k0(1), 1)
    softmax_pv(vw_ref, k_diag, 0, 0, True)
    for back in range(1, n_back + 1):
        @pl.when(i >= back)
        def _(back=back):
            if back < n_back:
                scores(ka_w_ref, back_k0(back + 1), (back + 1) % 2)
            softmax_pv(vw_ref, back_k0(back), back % 2, 1 if back == n_back else None, False)

    for j in range(Q_PER_KV):
        base = (g * Q_PER_KV + j) * 3
        out_t = (sig_ref[pl.ds(base, 1), :] * oc_ref[j]
                 + sig_ref[pl.ds(base + 1, 1), :] * os_ref[j]
                 + sig_ref[pl.ds(base + 2, 1), :] * (acc_ref[j] / l_ref[j]))
        o_ref[:, j * HEAD_DIM:(j + 1) * HEAD_DIM] = out_t.T.astype(o_ref.dtype)


def _key_aux(seq, tk, with_blocks):
    aux = np.zeros((seq, ATTN_K - HEAD_DIM), np.float32)
    pos = np.arange(seq)
    aux[:, AUX_SLOPE - HEAD_DIM:AUX_SLOPE - HEAD_DIM + 3] = (pos % tk)[:, None]
    if with_blocks:
        aux[pos, AUX_SEL - HEAD_DIM + pos // SLC_BLOCK] = 1.0
    return aux


def selected_window_attention(qt, kvr, vt, sel, oct, ngt, batch, round_src=None, tq=256):
    m = qt.shape[2]
    seq = m // batch
    nq = seq // tq
    n_slc = seq // SLC_BLOCK
    assert tq <= 256 and AUX_SEL + n_slc <= ATTN_K
    aux_s = jnp.asarray(_key_aux(seq, tq, True), BF16)
    aux_w = jnp.asarray(_key_aux(seq, tq, False), BF16)
    slopes = jnp.asarray(_alibi_slopes())
    n_g = N_KV_HEADS

    def k_spec(section):
        return pl.BlockSpec((None, seq, HEAD_DIM), lambda b, g, i: (section * n_g + g, b, 0))

    def v_spec(section):
        return pl.BlockSpec((None, HEAD_DIM, seq), lambda b, g, i: (section * n_g + g, 0, b))

    head_blk = pl.BlockSpec((Q_PER_KV, HEAD_DIM, tq), lambda b, g, i: (g, 0, b * nq + i))
    aux_blk = pl.BlockSpec((seq, ATTN_K - HEAD_DIM), lambda b, g, i: (0, 0))
    in_specs = [pl.BlockSpec(memory_space=pltpu.SMEM),
                head_blk,
                k_spec(2), v_spec(0), k_spec(3), v_spec(1),
                aux_blk, aux_blk,
                pl.BlockSpec((None, None, n_slc, tq), lambda b, g, i: (b, g, 0, i)),
                head_blk,
                pl.BlockSpec((LANES, tq), lambda b, g, i: (0, b * nq + i))]
    out_specs = [pl.BlockSpec((tq, GROUP_WIDTH), lambda b, g, i: (b * nq + i, g))]
    out_shape = [jax.ShapeDtypeStruct((m, NSA_WIDTH), BF16)]
    operands = [slopes, qt, kvr, vt, kvr, vt, aux_s, aux_w, sel, oct, ngt]
    if round_src is not None:
        steps = batch * N_KV_HEADS * nq
        rows, cols = round_src.shape
        assert rows % steps == 0 and (rows // steps) % BF16_SUBLANES == 0
        side_blk = pl.BlockSpec((rows // steps, cols), lambda b, g, i: ((b * N_KV_HEADS + g) * nq + i, 0))
        in_specs.append(side_blk)
        out_specs.append(side_blk)
        out_shape.append(jax.ShapeDtypeStruct((rows, cols), BF16))
        operands.append(round_src)
    res = pl.pallas_call(
        functools.partial(_attn_kernel, round_weights=round_src is not None),
        grid=(batch, N_KV_HEADS, nq),
        in_specs=in_specs,
        out_specs=out_specs,
        out_shape=out_shape,
        scratch_shapes=[pltpu.VMEM((seq, ATTN_K), BF16),
                        pltpu.VMEM((seq, ATTN_K), BF16),
                        pltpu.VMEM((Q_PER_KV, ATTN_K, tq), BF16),
                        pltpu.VMEM((2, Q_PER_KV, tq, tq), F32),
                        pltpu.VMEM((Q_PER_KV, tq, tq), BF16),
                        pltpu.VMEM((2, tq, tq), F32),
                        pltpu.VMEM((Q_PER_KV, 1, tq), F32),
                        pltpu.VMEM((Q_PER_KV, 1, tq), F32),
                        pltpu.VMEM((Q_PER_KV, HEAD_DIM, tq), F32),
                        pltpu.VMEM((Q_PER_KV, HEAD_DIM, tq), F32),
                        pltpu.VMEM((LANES, tq), F32)],
        compiler_params=_params("arbitrary", "arbitrary", "arbitrary"),
        name="nsa_selected_window",
    )(*operands)
    return (res[0], res[1]) if round_src is not None else (res[0], None)


def _sgu_kernel(u_ref, v_ref, lg_ref, lb_ref, ws_ref, bs_ref, o_ref):
    rows = u_ref.shape[0]
    v = _gelu(v_ref[...])
    mu = jnp.mean(v, axis=-1, keepdims=True)
    var = jnp.mean(jnp.square(v - mu), axis=-1, keepdims=True)
    vl = ((v - mu) * lax.rsqrt(var + NORM_EPS) * lg_ref[...] + lb_ref[...]).astype(BF16)
    r = lax.broadcasted_iota(jnp.int32, (SGU_CHUNK, SGU_CHUNK), 0)
    c = lax.broadcasted_iota(jnp.int32, (SGU_CHUNK, SGU_CHUNK), 1)
    gd = SGU_WIDTH // SGU_GROUPS
    for grp in range(SGU_GROUPS):
        w = jnp.where(c <= r, ws_ref[grp], 0.0).astype(BF16)
        lanes = slice(grp * gd, (grp + 1) * gd)
        for n in range(rows // SGU_CHUNK):
            rs = slice(n * SGU_CHUNK, (n + 1) * SGU_CHUNK)
            vm = _dot(w, vl[rs, lanes]) + bs_ref[:, lanes]
            o_ref[rs, lanes] = (_gelu(u_ref[rs, lanes]) * vm).astype(o_ref.dtype)


def spatial_gating(rest, ln_g, ln_b, w_s, b_s, tm=1024):
    m = rest.shape[0]
    gd = SGU_WIDTH // SGU_GROUPS
    bias = jnp.repeat(b_s.T, gd, axis=1)
    return pl.pallas_call(
        _sgu_kernel,
        grid=(m // tm,),
        in_specs=[pl.BlockSpec((tm, SGU_WIDTH), lambda i: (i, 0)),
                  pl.BlockSpec((tm, SGU_WIDTH), lambda i: (i, 1)),
                  pl.BlockSpec((1, SGU_WIDTH), lambda i: (0, 0)),
                  pl.BlockSpec((1, SGU_WIDTH), lambda i: (0, 0)),
                  pl.BlockSpec((SGU_GROUPS, SGU_CHUNK, SGU_CHUNK), lambda i: (0, 0, 0)),
                  pl.BlockSpec((SGU_CHUNK, SGU_WIDTH), lambda i: (0, 0))],
        out_specs=pl.BlockSpec((tm, SGU_WIDTH), lambda i: (i, 0)),
        out_shape=jax.ShapeDtypeStruct((m, SGU_WIDTH), BF16),
        compiler_params=_params("parallel"),
        name="spatial_gating",
    )(rest, rest, ln_g.reshape(1, -1), ln_b.reshape(1, -1), w_s, bias)


def merge_branches(oa, ob, rest, w_a, w_b, layer, tm=512, tn=1024):
    m = oa.shape[0]
    d = w_a.shape[2]
    ga_blk = 2 * SGU_WIDTH // tn
    gb_blk = (2 * SGU_WIDTH + d) // tn

    def epi(accs, e_refs, o_refs):
        y = _sigmoid(e_refs[0][...]) * accs[0] + _sigmoid(e_refs[1][...]) * accs[1]
        o_refs[0][...] = y.astype(BF16)

    (y,) = mm_full(
        "merge_branches", (d // tn, m // tm),
        [(oa, (tm, oa.shape[1]), lambda j, i: (i, 0)), (ob, (tm, ob.shape[1]), lambda j, i: (i, 0))],
        [(w_a, (None, w_a.shape[1], tn), lambda j, i: (layer, 0, j)),
         (w_b, (None, w_b.shape[1], tn), lambda j, i: (layer, 0, j))],
        [(rest, (tm, tn), lambda j, i: (i, ga_blk + j)), (rest, (tm, tn), lambda j, i: (i, gb_blk + j))],
        [(jax.ShapeDtypeStruct((m, d), BF16), (tm, tn), lambda j, i: (i, j))],
        [(0, 0), (1, 1)], epi)
    return y


def residual_matmul(name, a, w, layer, x, tm=512, tn=1024):
    m, k = a.shape
    n = w.shape[2]

    def epi(accs, e_refs, o_refs):
        o_refs[0][...] = e_refs[0][...] + accs[0]

    (out,) = mm_full(
        name, (n // tn, m // tm),
        [(a, (tm, k), lambda j, i: (i, 0))],
        [(w, (None, k, tn), lambda j, i: (layer, 0, j))],
        [(x, (tm, tn), lambda j, i: (i, j))],
        [(jax.ShapeDtypeStruct((m, n), F32), (tm, tn), lambda j, i: (i, j))],
        [(0, 0)], epi)
    return out


def swiglu_up(h, w1, w3, tm=2048, tn=512):
    m, d = h.shape
    f = w1.shape[1]

    def epi(accs, e_refs, o_refs):
        o_refs[0][...] = (accs[0] * _sigmoid(accs[0]) * accs[1]).astype(BF16)

    (out,) = mm_full(
        "swiglu_up", (f // tn, m // tm),
        [(h, (tm, d), lambda j, i: (i, 0))],
        [(w1, (d, tn), lambda j, i: (0, j)), (w3, (d, tn), lambda j, i: (0, j))],
        [],
        [(jax.ShapeDtypeStruct((m, f), BF16), (tm, tn), lambda j, i: (i, j))],
        [(0, 0), (0, 1)], epi)
    return out


MOE_TILE = 512
ROW_QUARTERS = 4


def _row_options(tile):
    step = tile // ROW_QUARTERS
    return tuple(range(step, tile + 1, step))


def _route_kernel(sel_ref, w_ref, tri_ref, pos_ref, wt_ref, meta_ref, cum_ref, *, tile):
    n_e, m = sel_ref.shape
    ck = tri_ref.shape[0]
    carry = jnp.zeros((n_e, 1), F32)
    for c in range(m // ck):
        sl = slice(c * ck, (c + 1) * ck)
        cs = _dot(sel_ref[:, sl].astype(BF16), tri_ref[...]) + carry
        cum_ref[:, sl] = cs
        carry = cs[:, ck - 1:ck]
    padded = jnp.ceil(carry / tile) * tile
    sub = lax.broadcasted_iota(jnp.int32, (n_e, 1), 0)
    start = jnp.zeros((n_e, 1), F32)
    run = jnp.zeros((1, 1), F32)
    for ex in range(n_e):
        start = jnp.where(sub == ex, run, start)
        run = run + padded[ex:ex + 1, :]
    sel = sel_ref[...] > 0.5
    pos = start + cum_ref[...] - 1.0
    eidx = lax.broadcasted_iota(jnp.int32, (n_e, m), 0)
    e_lo = jnp.min(jnp.where(sel, eidx, n_e), axis=0, keepdims=True)
    e_hi = jnp.max(jnp.where(sel, eidx, -1), axis=0, keepdims=True)
    is_lo = eidx == e_lo
    is_hi = eidx == e_hi
    w = w_ref[...]
    pos_a = jnp.sum(jnp.where(is_lo, pos, 0.0), axis=0, keepdims=True)
    pos_b = jnp.sum(jnp.where(is_hi, pos, 0.0), axis=0, keepdims=True)
    w_a = jnp.sum(jnp.where(is_lo, w, 0.0), axis=0, keepdims=True)
    w_b = jnp.sum(jnp.where(is_hi, w, 0.0), axis=0, keepdims=True)
    pos_ref[...] = jnp.where(eidx == 0, pos_a, jnp.where(eidx == 1, pos_b, 0.0)).astype(jnp.int32)
    wt_ref[...] = jnp.where(eidx == 0, w_a, jnp.where(eidx == 1, w_b, 0.0)).T
    tile_lo = lax.broadcasted_iota(jnp.int32, (n_e, LANES), 1).astype(F32) * tile
    t_exp = jnp.sum(jnp.where(start + padded <= tile_lo, 1.0, 0.0), axis=0, keepdims=True)
    t_exp = jnp.minimum(t_exp, n_e - 1.0)
    r8 = lax.broadcasted_iota(jnp.int32, (n_e, LANES), 0)
    last_row = jnp.sum(jnp.where(r8.astype(F32) == t_exp, start + carry, 0.0), axis=0, keepdims=True)
    filled = jnp.clip(last_row - tile_lo[0:1], 0.0, float(tile))
    meta_ref[...] = jnp.where(r8 == 0, t_exp, jnp.where(r8 == 1, run / tile,
                                                       jnp.where(r8 == 2, filled, 0.0))).astype(jnp.int32)


def moe_route(sel, w, tile):
    n_e, m = sel.shape
    ck = 256
    tri = jnp.asarray(np.triu(np.ones((ck, ck), np.float32)), BF16)
    full = lambda shape: pl.BlockSpec(shape, lambda: tuple(0 for _ in shape))
    return pl.pallas_call(
        functools.partial(_route_kernel, tile=tile),
        in_specs=[full((n_e, m)), full((n_e, m)), full((ck, ck))],
        out_specs=[full((n_e, m)), full((m, n_e)), full((n_e, LANES))],
        out_shape=[jax.ShapeDtypeStruct((n_e, m), jnp.int32),
                   jax.ShapeDtypeStruct((m, n_e), F32),
                   jax.ShapeDtypeStruct((n_e, LANES), jnp.int32)],
        scratch_shapes=[pltpu.VMEM((n_e, m), F32)],
        compiler_params=pltpu.CompilerParams(vmem_limit_bytes=VMEM_LIMIT),
        name="moe_route",
    )(sel, w, tri)


def _row_gather_start(src_hbm, dst, rows_ref, base, n_rows, sem, unroll=8):
    def body(r, carry):
        src_row = rows_ref[base + r]
        pltpu.make_async_copy(src_hbm.at[pl.ds(src_row, 1)], dst.at[pl.ds(r, 1)], sem).start()
        return carry
    lax.fori_loop(0, n_rows, body, 0, unroll=unroll)


def _row_gather_wait(src_hbm, dst, sem):
    pltpu.make_async_copy(src_hbm.at[pl.ds(0, dst.shape[0])], dst, sem).wait()


def _moe_gather_kernel(pos_ref, meta_ref, h_hbm, zeros_hbm, o_ref, tok_ref, buf_ref, sem_ref):
    i = pl.program_id(0)
    n = meta_ref[1, 0]
    tile = o_ref.shape[0]
    m = pos_ref.shape[0] // 2

    @pl.when(i == 0)
    def _():
        clear = pltpu.make_async_copy(zeros_hbm, tok_ref, sem_ref.at[2])
        clear.start()
        clear.wait()

        def fill(t, carry):
            tok_ref[pos_ref[t]] = t
            tok_ref[pos_ref[m + t]] = t
            return carry
        lax.fori_loop(0, m, fill, 0, unroll=8)
        _row_gather_start(h_hbm, buf_ref.at[0], tok_ref, 0, tile, sem_ref.at[0])

    @pl.when(i + 1 < n)
    def _():
        nxt = (i + 1) % 2
        _row_gather_start(h_hbm, buf_ref.at[nxt], tok_ref, (i + 1) * tile, tile, sem_ref.at[nxt])

    @pl.when(i < n)
    def _():
        cur = i % 2
        _row_gather_wait(h_hbm, buf_ref.at[cur], sem_ref.at[cur])
        o_ref[...] = buf_ref[cur].astype(o_ref.dtype)

    @pl.when(i >= n)
    def _():
        o_ref[...] = jnp.zeros_like(o_ref)


def moe_gather(pos2, meta, h, n_tiles, tile):
    m, d = h.shape
    rows = n_tiles * tile
    return pl.pallas_call(
        _moe_gather_kernel,
        grid_spec=pltpu.PrefetchScalarGridSpec(
            num_scalar_prefetch=2,
            grid=(n_tiles,),
            in_specs=[pl.BlockSpec(memory_space=pl.ANY), pl.BlockSpec(memory_space=pl.ANY)],
            out_specs=pl.BlockSpec((tile, d), lambda i, pos, meta: (i, 0)),
            scratch_shapes=[pltpu.SMEM((rows,), jnp.int32),
                            pltpu.VMEM((2, tile, d), F32),
                            pltpu.SemaphoreType.DMA((3,))]),
        out_shape=jax.ShapeDtypeStruct((rows, d), BF16),
        compiler_params=_params("arbitrary"),
        name="moe_gather",
    )(pos2, meta, h, jnp.zeros((rows,), jnp.int32))


def _moe_up_kernel(meta_ref, x_ref, w1_ref, w3_ref, w2_ref, o_ref, w2b_ref):
    i = pl.program_id(1)

    def round_w2_block():
        w2b_ref[...] = w2_ref[...].astype(BF16)

    filled = meta_ref[2, i]
    tile = o_ref.shape[0]
    for rows in _row_options(tile):
        @pl.when((filled > rows - tile // ROW_QUARTERS) & (filled <= rows))
        def _(rows=rows):
            x = x_ref[0:rows, :]
            for c in range(0, o_ref.shape[1], COL_CHUNK):
                sl = slice(c, c + COL_CHUNK)
                a1 = _dot(x, w1_ref[:, sl])
                a3 = _dot(x, w3_ref[:, sl])
                o_ref[0:rows, sl] = (a1 * _sigmoid(a1) * a3).astype(o_ref.dtype)
            if rows < tile:
                o_ref[rows:tile, :] = jnp.zeros((tile - rows, o_ref.shape[1]), o_ref.dtype)
            round_w2_block()

    @pl.when(filled == 0)
    def _():
        o_ref[...] = jnp.zeros_like(o_ref)
        round_w2_block()


BF16_SUBLANES = 16


def _cast_rows_per_step(total_rows, steps):
    for rows in range(BF16_SUBLANES, total_rows + 1, BF16_SUBLANES):
        if total_rows % rows == 0 and total_rows // rows <= steps:
            return rows
    raise ValueError("weights cannot be split over the grid steps")


def moe_up(meta, xs, w1, w3, w2, tile, tn=1792):
    rows, d = xs.shape
    n_e, _, f = w1.shape
    assert f % tn == 0 and rows % tile == 0 and w1.dtype == BF16 and w3.dtype == BF16
    n_tiles = rows // tile
    w2_flat = w2.reshape(n_e * f, w2.shape[2])
    cast_rows = _cast_rows_per_step(w2_flat.shape[0], (f // tn) * n_tiles)
    n_cast = w2_flat.shape[0] // cast_rows
    last = lambda i, meta: jnp.maximum(jnp.minimum(i, meta[1, 0] - 1), 0)
    cast_blk = lambda j, i, meta: (jnp.minimum(j * n_tiles + i, n_cast - 1), 0)
    up, w2b = pl.pallas_call(
        _moe_up_kernel,
        grid_spec=pltpu.PrefetchScalarGridSpec(
            num_scalar_prefetch=1,
            grid=(f // tn, n_tiles),
            in_specs=[pl.BlockSpec((tile, d), lambda j, i, meta: (last(i, meta), 0)),
                      pl.BlockSpec((None, d, tn), lambda j, i, meta: (meta[0, i], 0, j)),
                      pl.BlockSpec((None, d, tn), lambda j, i, meta: (meta[0, i], 0, j)),
                      pl.BlockSpec((cast_rows, w2_flat.shape[1]), cast_blk)],
            out_specs=[pl.BlockSpec((tile, tn), lambda j, i, meta: (i, j)),
                       pl.BlockSpec((cast_rows, w2_flat.shape[1]), cast_blk)]),
        out_shape=[jax.ShapeDtypeStruct((rows, f), BF16),
                   jax.ShapeDtypeStruct(w2_flat.shape, BF16)],
        compiler_params=_params("arbitrary", "arbitrary"),
        name="moe_up",
    )(meta, xs, w1, w3, w2_flat)
    return up, w2b.reshape(w2.shape)


def _moe_down_kernel(meta_ref, a_ref, w_ref, o_ref):
    i = pl.program_id(0)
    k = pl.program_id(1)

    filled = meta_ref[2, i]
    tile = o_ref.shape[0]
    cn = COL_CHUNK
    for rows in _row_options(tile):
        in_range = (filled > rows - tile // ROW_QUARTERS) & (filled <= rows)

        @pl.when(in_range & (k == 0))
        def _(rows=rows):
            a = a_ref[0:rows, :]
            for c in range(0, o_ref.shape[1], cn):
                o_ref[0:rows, c:c + cn] = _dot(a, w_ref[:, c:c + cn])
            if rows < tile:
                o_ref[rows:tile, :] = jnp.zeros((tile - rows, o_ref.shape[1]), o_ref.dtype)

        @pl.when(in_range & (k > 0))
        def _(rows=rows):
            a = a_ref[0:rows, :]
            for c in range(0, o_ref.shape[1], cn):
                o_ref[0:rows, c:c + cn] += _dot(a, w_ref[:, c:c + cn])

    @pl.when((filled == 0) & (k == 0))
    def _():
        o_ref[...] = jnp.zeros_like(o_ref)


def moe_down(meta, up, w2, tile, tk=3584):
    rows, f = up.shape
    d = w2.shape[2]
    assert f % tk == 0 and rows % tile == 0
    n_tiles = rows // tile
    last = lambda i, meta: jnp.maximum(jnp.minimum(i, meta[1, 0] - 1), 0)
    return pl.pallas_call(
        _moe_down_kernel,
        grid_spec=pltpu.PrefetchScalarGridSpec(
            num_scalar_prefetch=1,
            grid=(n_tiles, f // tk),
            in_specs=[pl.BlockSpec((tile, tk), lambda i, k, meta: (last(i, meta), k)),
                      pl.BlockSpec((None, tk, d), lambda i, k, meta: (meta[0, i], k, 0))],
            out_specs=pl.BlockSpec((tile, d), lambda i, k, meta: (i, 0))),
        out_shape=jax.ShapeDtypeStruct((rows, d), F32),
        compiler_params=_params("arbitrary", "arbitrary"),
        name="moe_down",
    )(meta, up, w2)


def _moe_combine_kernel(pos_ref, x_ref, wt_ref, g_ref, y_hbm, o_ref, buf_ref, sem_ref, *, final_norm):
    i = pl.program_id(0)
    n = pl.num_programs(0)
    tc = x_ref.shape[0]
    m = pos_ref.shape[0] // 2

    def start(step, slot):
        for s in range(2):
            _row_gather_start(y_hbm, buf_ref.at[slot, s], pos_ref, s * m + step * tc, tc, sem_ref.at[slot, s])

    @pl.when(i == 0)
    def _():
        start(0, 0)

    @pl.when(i + 1 < n)
    def _():
        start(i + 1, (i + 1) % 2)

    cur = i % 2
    for s in range(2):
        _row_gather_wait(y_hbm, buf_ref.at[cur, s], sem_ref.at[cur, s])
    wt = wt_ref[...]
    out = x_ref[...] + wt[:, 0:1] * buf_ref[cur, 0] + wt[:, 1:2] * buf_ref[cur, 1]
    o_ref[...] = _rms(out, g_ref[...]) if final_norm else out


def moe_combine(pos2, x, wt, y, norm_g=None, tc=512):
    m, d = x.shape
    final_norm = norm_g is not None
    g = (norm_g if final_norm else jnp.ones((d,), F32)).reshape(1, d)
    return pl.pallas_call(
        functools.partial(_moe_combine_kernel, final_norm=final_norm),
        grid_spec=pltpu.PrefetchScalarGridSpec(
            num_scalar_prefetch=1,
            grid=(m // tc,),
            in_specs=[pl.BlockSpec((tc, d), lambda i, pos: (i, 0)),
                      pl.BlockSpec((tc, wt.shape[1]), lambda i, pos: (i, 0)),
                      pl.BlockSpec((1, d), lambda i, pos: (0, 0)),
                      pl.BlockSpec(memory_space=pl.ANY)],
            out_specs=pl.BlockSpec((tc, d), lambda i, pos: (i, 0)),
            scratch_shapes=[pltpu.VMEM((2, 2, tc, d), F32),
                            pltpu.SemaphoreType.DMA((2, 2))]),
        out_shape=jax.ShapeDtypeStruct((m, d), F32),
        compiler_params=_params("arbitrary"),
        name="moe_combine",
    )(pos2, x, wt, g, y)


def moe_layer(x, norm_g, router_w, router_b, w1, w3, w2, out_norm_g=None):
    m, d = x.shape
    n_e = router_w.shape[1]
    tile = MOE_TILE
    n_tiles = 2 * m // tile + n_e
    h, sel, w = norm_router(x, norm_g, router_w, router_b)
    pos, wt, meta = moe_route(sel, w, tile)
    pos2 = pos[:2].reshape(-1)
    xs = moe_gather(pos2, meta, h, n_tiles, tile)
    up, w2_bf16 = moe_up(meta, xs, w1, w3, w2, tile)
    y = moe_down(meta, up, w2_bf16, tile)
    return moe_combine(pos2, x, wt, y, out_norm_g)


def _mixer(x, batch, layer, norm_g, w_in_t, pe_k, pe_v, kw1, kw2, vw1, vw2, ln_g, ln_b, w_s, b_s, w_a, w_b, w_out,
           round_src=None):
    h = rmsnorm(x, norm_g, BF16)
    qt, kvr, vt, ngt, rest = in_projection(h, w_in_t, layer)
    cmp_kv = compress(kvr, batch, pe_k, pe_v, kw1, kw2, vw1, vw2)
    oct, sel = compressed_and_select(qt, cmp_kv, batch)
    oa, rounded = selected_window_attention(qt, kvr, vt, sel, oct, ngt, batch, round_src)
    ob = spatial_gating(rest, ln_g, ln_b, w_s, b_s)
    y = merge_branches(oa, ob, rest, w_a, w_b, layer)
    return residual_matmul("out_proj", y, w_out, layer, x), rounded


def kernel(x, norm_mix, w_in, cmp_pe_k, cmp_pe_v, cmp_k_w1, cmp_k_w2, cmp_v_w1, cmp_v_w2, sgu_ln_g, sgu_ln_b, sgu_w, sgu_b, w_branch_a, w_branch_b, w_out, norm_ffn, ffn_w1, ffn_w3, ffn_w2, router_w, router_b, moe_w1, moe_w3, moe_w2, norm_f):
    batch, seq, d = x.shape
    depth = norm_mix.shape[0]
    xf = x.reshape(batch * seq, d)
    w_in_t = jnp.swapaxes(w_in, 1, 2)
    normed = False
    n_moe = moe_w1.shape[0]
    moe_up_bf16 = {}
    for layer in range(depth):
        j = layer // 2
        round_src = None
        if j < n_moe and 2 * j + 1 < depth:
            w_f32 = moe_w1[j] if layer % 2 == 0 else moe_w3[j]
            round_src = w_f32.reshape(-1, w_f32.shape[-1])
        xf, rounded = _mixer(xf, batch, layer, norm_mix[layer], w_in_t, cmp_pe_k[layer], cmp_pe_v[layer],
                             cmp_k_w1[layer], cmp_k_w2[layer], cmp_v_w1[layer], cmp_v_w2[layer],
                             sgu_ln_g[layer], sgu_ln_b[layer], sgu_w[layer], sgu_b[layer],
                             w_branch_a, w_branch_b, w_out, round_src)
        if rounded is not None:
            moe_up_bf16[layer % 2] = rounded.reshape(moe_w1[j].shape)
        if layer % 2 == 0:
            h = rmsnorm(xf, norm_ffn[layer], BF16)
            up = swiglu_up(h, ffn_w1[j], ffn_w3[j])
            xf = residual_matmul("ffn_down", up, ffn_w2, j, xf, tm=512, tn=512)
        else:
            normed = layer == depth - 1
            xf = moe_layer(xf, norm_ffn[layer], router_w[j], router_b[j], moe_up_bf16[0], moe_up_bf16[1],
                           moe_w2[j], norm_f if normed else None)
    if not normed:
        xf = rmsnorm(xf, norm_f, F32)
    return xf.reshape(batch, seq, d)
```

```python
import functools

import numpy as np
import jax
import jax.numpy as jnp
from jax import lax
from jax.experimental import pallas as pl
from jax.experimental.pallas import tpu as pltpu

F32 = jnp.float32
BF16 = jnp.bfloat16

D_MODEL = 2048
N_Q_HEADS = 16
N_KV_HEADS = 4
HEAD_DIM = 64
Q_PER_KV = N_Q_HEADS // N_KV_HEADS
NSA_WIDTH = N_Q_HEADS * HEAD_DIM
KV_WIDTH = N_KV_HEADS * HEAD_DIM
GROUP_WIDTH = Q_PER_KV * HEAD_DIM
CMP_BLOCK = 32
CMP_STRIDE = 16
CMP_HIDDEN = 256
SLC_BLOCK = 64
N_SELECT = 16
WINDOW = 512
FORCED_SCORE = 1e4
SGU_WIDTH = D_MODEL // 2
SGU_GROUPS = 8
SGU_CHUNK = 128
N_EXPERTS = 8
NORM_EPS = 1e-5
NEG = -1e30
LOG2E = 1.4426950408889634
LANES = 128
VMEM_LIMIT = 56 * 1024 * 1024

Q_OFF = 0
KV_OFF = NSA_WIDTH
NG_OFF = KV_OFF + 6 * KV_WIDTH
NG_WIDTH = 3 * N_Q_HEADS
REST_OFF = NG_OFF + NG_WIDTH


def _params(*sem):
    return pltpu.CompilerParams(dimension_semantics=sem, vmem_limit_bytes=VMEM_LIMIT)


def _dot(a, b):
    return jnp.dot(a, b, preferred_element_type=F32)


def _dot_nt(a, b):
    return lax.dot_general(a, b, (((1,), (1,)), ((), ())), preferred_element_type=F32)


def _dot_split(a_f32, b_bf16):
    hi = a_f32.astype(BF16)
    lo = (a_f32 - hi.astype(F32)).astype(BF16)
    return _dot(hi, b_bf16) + _dot(lo, b_bf16)


def _sigmoid(x):
    return 1.0 / (1.0 + jnp.exp(-x))


def _gelu(x):
    return x * (0.5 * (1.0 + jnp.tanh(0.7978845608028654 * (x + 0.044715 * (x * x * x)))))


def _rms(x, g):
    return x * lax.rsqrt(jnp.mean(x * x, axis=-1, keepdims=True) + NORM_EPS) * g


def _rmsnorm_kernel(x_ref, g_ref, o_ref):
    o_ref[...] = _rms(x_ref[...], g_ref[...]).astype(o_ref.dtype)


def rmsnorm(x, g, out_dtype, tm=1024):
    m, d = x.shape
    return pl.pallas_call(
        _rmsnorm_kernel,
        grid=(m // tm,),
        in_specs=[pl.BlockSpec((tm, d), lambda i: (i, 0)),
                  pl.BlockSpec((1, d), lambda i: (0, 0))],
        out_specs=pl.BlockSpec((tm, d), lambda i: (i, 0)),
        out_shape=jax.ShapeDtypeStruct((m, d), out_dtype),
        compiler_params=_params("parallel"),
        name="rmsnorm",
    )(x, g.reshape(1, d))


def _norm_router_kernel(x_ref, g_ref, rwt_ref, rb_ref, h_ref, sel_ref, w_ref):
    h = _rms(x_ref[...], g_ref[...])
    h_ref[...] = h
    logits = lax.dot_general(rwt_ref[...], h, (((1,), (1,)), ((), ())), preferred_element_type=F32,
                             precision=lax.Precision.HIGHEST) + rb_ref[...]
    n_e = logits.shape[0]
    e = lax.broadcasted_iota(jnp.int32, logits.shape, 0)
    m1 = jnp.max(logits, axis=0, keepdims=True)
    i1 = jnp.min(jnp.where(logits == m1, e, n_e), axis=0, keepdims=True)
    rest = jnp.where(e == i1, -jnp.inf, logits)
    m2 = jnp.max(rest, axis=0, keepdims=True)
    i2 = jnp.min(jnp.where(rest == m2, e, n_e), axis=0, keepdims=True)
    e2 = jnp.exp(m2 - m1)
    w1 = 1.0 / (1.0 + e2)
    w2 = e2 / (1.0 + e2)
    sel_ref[...] = jnp.where((e == i1) | (e == i2), 1.0, 0.0)
    w_ref[...] = jnp.where(e == i1, w1, 0.0) + jnp.where(e == i2, w2, 0.0)


def norm_router(x, g, router_w, router_b, tm=1024):
    m, d = x.shape
    n_e = router_w.shape[1]
    return pl.pallas_call(
        _norm_router_kernel,
        grid=(m // tm,),
        in_specs=[pl.BlockSpec((tm, d), lambda i: (i, 0)),
                  pl.BlockSpec((1, d), lambda i: (0, 0)),
                  pl.BlockSpec((n_e, d), lambda i: (0, 0)),
                  pl.BlockSpec((n_e, 1), lambda i: (0, 0))],
        out_specs=[pl.BlockSpec((tm, d), lambda i: (i, 0)),
                   pl.BlockSpec((n_e, tm), lambda i: (0, i)),
                   pl.BlockSpec((n_e, tm), lambda i: (0, i))],
        out_shape=[jax.ShapeDtypeStruct((m, d), F32),
                   jax.ShapeDtypeStruct((n_e, m), F32),
                   jax.ShapeDtypeStruct((n_e, m), F32)],
        compiler_params=_params("parallel"),
        name="norm_router",
    )(x, g.reshape(1, d), router_w.T, router_b.reshape(n_e, 1))


def _mm_full_kernel(*refs, n_a, n_b, n_e, n_o, pairs, inner_axis, epilogue, b_transposed):
    a_refs = refs[:n_a]
    b_refs = refs[n_a:n_a + n_b]
    e_refs = refs[n_a + n_b:n_a + n_b + n_e]
    o_refs = refs[n_a + n_b + n_e:n_a + n_b + n_e + n_o]
    w_refs = refs[n_a + n_b + n_e + n_o:]

    @pl.when(pl.program_id(inner_axis) == 0)
    def _():
        for b_ref, w_ref in zip(b_refs, w_refs):
            w_ref[...] = b_ref[...].astype(BF16)

    dot = _dot_nt if b_transposed else _dot
    ws = [w_ref[0] if len(w_ref.shape) == 3 else w_ref[...] for w_ref in w_refs]
    accs = [dot(a_refs[ia][...], ws[ib]) for ia, ib in pairs]
    epilogue(accs, e_refs, o_refs)


def _block_dim_size(s):
    return s.block_size if isinstance(s, pl.Element) else s


def mm_full(name, grid, a_ops, b_ops, e_ops, outs, pairs, epilogue, b_transposed=False):
    ops = a_ops + b_ops + e_ops
    kernel = functools.partial(
        _mm_full_kernel, n_a=len(a_ops), n_b=len(b_ops), n_e=len(e_ops), n_o=len(outs),
        pairs=pairs, inner_axis=len(grid) - 1, epilogue=epilogue, b_transposed=b_transposed)
    scratch = [pltpu.VMEM(tuple(_block_dim_size(s) for s in blk if s is not None), BF16) for _, blk, _ in b_ops]
    res = pl.pallas_call(
        kernel,
        grid=grid,
        in_specs=[pl.BlockSpec(blk, imap) for _, blk, imap in ops],
        out_specs=[pl.BlockSpec(blk, imap) for _, blk, imap in outs],
        out_shape=[sds for sds, _, _ in outs],
        scratch_shapes=scratch,
        compiler_params=_params(*(("arbitrary",) * len(grid))),
        name=name,
    )(*[arr for arr, _, _ in ops])
    return res


COL_CHUNK = 256


def in_projection(h, w_in_t, layer):
    m, d = h.shape
    tm = 1024
    a_op = [(h, (tm, d), lambda j, i: (i, 0))]
    tm_narrow = 2048
    a_op_narrow = [(h, (tm_narrow, d), lambda j, i: (i, 0))]

    tq = NSA_WIDTH
    q_heads = tq // HEAD_DIM
    q_scale = float(np.float32(HEAD_DIM ** -0.5) * np.float32(LOG2E))

    def q_epi(accs, e_refs, o_refs):
        for c in range(q_heads):
            o_refs[0][c] = (accs[0][:, c * HEAD_DIM:(c + 1) * HEAD_DIM] * q_scale).T.astype(BF16)

    (qt,) = mm_full(
        "in_proj_q", (NSA_WIDTH // tq, m // tm), a_op,
        [(w_in_t, (None, tq, d), lambda j, i: (layer, j, 0))], [],
        [(jax.ShapeDtypeStruct((N_Q_HEADS, HEAD_DIM, m), BF16), (q_heads, HEAD_DIM, tm), lambda j, i: (j, 0, i))],
        [(0, 0)], q_epi, b_transposed=True)

    sec0 = KV_OFF // KV_WIDTH

    def rows_epi(accs, e_refs, o_refs):
        for c in range(N_KV_HEADS):
            o_refs[0][c] = accs[0][:, c * HEAD_DIM:(c + 1) * HEAD_DIM].astype(BF16)

    (kvr,) = mm_full(
        "in_proj_kv_rows", (4, m // tm_narrow), a_op_narrow,
        [(w_in_t, (None, KV_WIDTH, d), lambda j, i: (layer, sec0 + j + j // 3, 0))], [],
        [(jax.ShapeDtypeStruct((4 * N_KV_HEADS, m, HEAD_DIM), BF16),
          (N_KV_HEADS, tm_narrow, HEAD_DIM), lambda j, i: (j, i, 0))],
        [(0, 0)], rows_epi, b_transposed=True)

    def cols_epi(accs, e_refs, o_refs):
        for c in range(N_KV_HEADS):
            o_refs[0][c] = accs[0][:, c * HEAD_DIM:(c + 1) * HEAD_DIM].T.astype(BF16)

    (vt,) = mm_full(
        "in_proj_v_cols", (2, m // tm_narrow), a_op_narrow,
        [(w_in_t, (None, KV_WIDTH, d), lambda j, i: (layer, sec0 + 3 + 2 * j, 0))], [],
        [(jax.ShapeDtypeStruct((2 * N_KV_HEADS, HEAD_DIM, m), BF16),
          (N_KV_HEADS, HEAD_DIM, tm_narrow), lambda j, i: (j, 0, i))],
        [(0, 0)], cols_epi, b_transposed=True)

    def f32_epi(accs, e_refs, o_refs):
        o_refs[0][...] = accs[0]

    def f32_t_epi(accs, e_refs, o_refs):
        o_refs[0][...] = accs[0].T

    assert NG_OFF % LANES == 0
    (ngt,) = mm_full(
        "in_proj_ng", (1, m // tm_narrow), a_op_narrow,
        [(w_in_t, (None, LANES, d), lambda j, i: (layer, NG_OFF // LANES, 0))], [],
        [(jax.ShapeDtypeStruct((LANES, m), F32), (LANES, tm_narrow), lambda j, i: (0, i))],
        [(0, 0)], f32_t_epi, b_transposed=True)

    n_rest = w_in_t.shape[1] - REST_OFF
    tr = 1024
    assert n_rest % tr == 0 and REST_OFF % 8 == 0
    (rest,) = mm_full(
        "in_proj_rest", (n_rest // tr, m // tm), a_op,
        [(w_in_t, (pl.Element(1), pl.Element(tr), pl.Element(d)), lambda j, i: (layer, pl.multiple_of(REST_OFF + j * tr, 8), 0))], [],
        [(jax.ShapeDtypeStruct((m, n_rest), F32), (tm, tr), lambda j, i: (i, j))],
        [(0, 0)], f32_epi, b_transposed=True)
    return qt, kvr, vt, ngt, rest


def _cmp_kernel(a_ref, w1_ref, w2_ref, pe_ref, o_ref):
    a = a_ref[...]
    rows = a.shape[0]
    half = a.shape[1]
    w1 = w1_ref[...].astype(BF16)
    p0 = _dot(a, w1[:half])
    p1 = _dot(a, w1[half:])
    bias = _dot(pe_ref[...].astype(BF16), w1)[0:1]
    hid = p0 + pltpu.roll(p1, rows - 1, 0) + bias
    o_ref[...] = _dot(_gelu(hid).astype(BF16), w2_ref[...].astype(BF16))


def compress(kv, batch, pe_k, pe_v, kw1, kw2, vw1, vw2):
    m = kv.shape[1]
    n_chunks = m // batch // CMP_STRIDE
    rows = batch * n_chunks
    feat = CMP_STRIDE * HEAD_DIM
    a = kv[:2 * N_KV_HEADS].reshape(2, N_KV_HEADS, rows, feat)
    w1 = jnp.stack([kw1, vw1])
    w2 = jnp.stack([kw2, vw2])
    pe = jnp.stack([pe_k, pe_v]).reshape(2, 1, CMP_BLOCK * HEAD_DIM)
    pe = jnp.broadcast_to(pe, (2, 8, CMP_BLOCK * HEAD_DIM))
    return pl.pallas_call(
        _cmp_kernel,
        grid=(2, N_KV_HEADS),
        in_specs=[pl.BlockSpec((None, None, rows, feat), lambda s, g: (s, g, 0, 0)),
                  pl.BlockSpec((None, 2 * feat, CMP_HIDDEN), lambda s, g: (s, 0, 0)),
                  pl.BlockSpec((None, CMP_HIDDEN, HEAD_DIM), lambda s, g: (s, 0, 0)),
                  pl.BlockSpec((None, 8, 2 * feat), lambda s, g: (s, 0, 0))],
        out_specs=pl.BlockSpec((None, None, rows, HEAD_DIM), lambda s, g: (s, g, 0, 0)),
        out_shape=jax.ShapeDtypeStruct((2, N_KV_HEADS, rows, HEAD_DIM), F32),
        compiler_params=_params("parallel", "parallel"),
        name="nsa_compress",
    )(a, w1, w2, pe)


def _alibi_slopes():
    sl = np.array([2.0 ** (-8.0 * (h + 1) / N_Q_HEADS) for h in range(N_Q_HEADS)], dtype=np.float32)
    return sl * np.float32(LOG2E)


def _overlap_t(n_cmp_pad, n_slc):
    cs = np.arange(n_cmp_pad)[None, :] * CMP_STRIDE
    ss = np.arange(n_slc)[:, None] * SLC_BLOCK
    ov = np.clip(np.minimum(cs + CMP_BLOCK, ss + SLC_BLOCK) - np.maximum(cs, ss), 0, None)
    return (ov / CMP_STRIDE).astype(np.float32)


def _sel_kernel(slopes_ref, q_ref, kc_ref, vc_ref, ovt_ref, oc_ref, sel_ref, *, n_cmp):
    g = pl.program_id(1)
    i = pl.program_id(2)
    tq = q_ref.shape[2]
    ncp = kc_ref.shape[0]
    n_slc = ovt_ref.shape[0]
    kc = kc_ref[...].astype(BF16)
    vct = vc_ref[...].T.astype(BF16)

    t = i * tq + lax.broadcasted_iota(jnp.int32, (ncp, tq), 1)
    c = lax.broadcasted_iota(jnp.int32, (ncp, tq), 0)
    dist = t - (c * CMP_STRIDE + (CMP_BLOCK - 1))
    mask = (dist >= 0) & (c < n_cmp)
    distf = dist.astype(F32)

    scores = [_dot(kc, q_ref[j]) for j in range(Q_PER_KV)]
    probs = []
    for j in range(Q_PER_KV):
        s = scores[j] - slopes_ref[g * Q_PER_KV + j] * distf
        s = jnp.where(mask, s, NEG)
        mx = jnp.max(s, axis=0, keepdims=True)
        e = jnp.where(mask, jnp.exp2(s - mx), 0.0)
        inv = 1.0 / jnp.maximum(jnp.sum(e, axis=0, keepdims=True), 1e-30)
        probs.append(e * inv)
    for j in range(Q_PER_KV):
        oc_ref[j] = _dot(vct, probs[j].astype(BF16))
    p_sum = functools.reduce(lambda a, b: a + b, probs)

    ovt = ovt_ref[...]
    hi = p_sum.astype(BF16)
    lo = (p_sum - hi.astype(F32)).astype(BF16)
    imp = _dot(ovt, hi) + _dot(ovt, lo)
    tt = i * tq + lax.broadcasted_iota(jnp.int32, (n_slc, tq), 1)
    blk = lax.broadcasted_iota(jnp.int32, (n_slc, tq), 0)
    cur = tt // SLC_BLOCK
    valid = blk * SLC_BLOCK <= tt
    forced = (blk == 0) | (blk == cur) | (blk == cur - 1)
    score = jnp.where(valid, imp, -1.0)
    score = jnp.where(forced, FORCED_SCORE, score)
    rank = jnp.zeros((n_slc, tq), F32)
    for mrow in range(n_slc):
        other = jnp.broadcast_to(score[mrow:mrow + 1, :], (n_slc, tq))
        beats = (other > score) | ((other == score) & (blk > mrow))
        rank = rank + jnp.where(beats, 1.0, 0.0)
    sel_ref[...] = jnp.where(rank < float(min(N_SELECT, n_slc)), 1.0, 0.0)


def compressed_and_select(qt, cmp_kv, batch, tq=512):
    m = qt.shape[2]
    seq = m // batch
    nq = seq // tq
    ncp = cmp_kv.shape[2] // batch
    n_slc = seq // SLC_BLOCK
    ovt = jnp.asarray(_overlap_t(ncp, n_slc), BF16)
    slopes = jnp.asarray(_alibi_slopes())
    kernel = functools.partial(_sel_kernel, n_cmp=ncp - 1)
    return pl.pallas_call(
        kernel,
        grid=(batch, N_KV_HEADS, nq),
        in_specs=[pl.BlockSpec(memory_space=pltpu.SMEM),
                  pl.BlockSpec((Q_PER_KV, HEAD_DIM, tq), lambda b, g, i: (g, 0, b * nq + i)),
                  pl.BlockSpec((None, None, ncp, HEAD_DIM), lambda b, g, i: (0, g, b, 0)),
                  pl.BlockSpec((None, None, ncp, HEAD_DIM), lambda b, g, i: (1, g, b, 0)),
                  pl.BlockSpec((n_slc, ncp), lambda b, g, i: (0, 0))],
        out_specs=[pl.BlockSpec((Q_PER_KV, HEAD_DIM, tq), lambda b, g, i: (g, 0, b * nq + i)),
                   pl.BlockSpec((None, None, n_slc, tq), lambda b, g, i: (b, g, 0, i))],
        out_shape=[jax.ShapeDtypeStruct((N_Q_HEADS, HEAD_DIM, m), F32),
                   jax.ShapeDtypeStruct((batch, N_KV_HEADS, n_slc, seq), F32)],
        compiler_params=_params("parallel", "parallel", "parallel"),
        name="nsa_compressed_select",
    )(slopes, qt, cmp_kv, cmp_kv, ovt)


AUX_SLOPE = HEAD_DIM
AUX_SEL = HEAD_DIM + 16
ATTN_K = 2 * HEAD_DIM
STRIP = 32


def _attn_kernel(slopes_ref, q_ref, ks_ref, vs_ref, kw_ref, vw_ref, auxs_ref, auxw_ref, sel_ref, oc_ref,
                 ng_ref, *rest, round_weights):
    if round_weights:
        wsrc_ref, o_ref, wdst_ref = rest[:3]
        scratch = rest[3:]

        def side_work():
            wdst_ref[...] = wsrc_ref[...].astype(BF16)
    else:
        o_ref = rest[0]
        scratch = rest[1:]
        side_work = lambda: None
    _attn_body(slopes_ref, q_ref, ks_ref, vs_ref, kw_ref, vw_ref, auxs_ref, auxw_ref, sel_ref, oc_ref, ng_ref,
               o_ref, *scratch, side_work=side_work)


def _attn_body(slopes_ref, q_ref, ks_ref, vs_ref, kw_ref, vw_ref, auxs_ref, auxw_ref, sel_ref, oc_ref,
               ng_ref, o_ref, ka_s_ref, ka_w_ref, qa_ref, s_ref, p_ref, mask_ref, m_ref, l_ref, acc_ref,
               os_ref, sig_ref, *, side_work):
    g = pl.program_id(1)
    i = pl.program_id(2)
    tq = q_ref.shape[2]
    tk = tq
    q0 = i * tq
    n_slc = sel_ref.shape[0]

    @pl.when(i == 0)
    def _():
        ka_s_ref[:, 0:HEAD_DIM] = ks_ref[...]
        ka_s_ref[:, HEAD_DIM:ATTN_K] = auxs_ref[...]
        ka_w_ref[:, 0:HEAD_DIM] = kw_ref[...]
        ka_w_ref[:, HEAD_DIM:ATTN_K] = auxw_ref[...]
        kr = lax.broadcasted_iota(jnp.int32, (tk, tq), 0)
        qc = lax.broadcasted_iota(jnp.int32, (tk, tq), 1)
        mask_ref[0] = jnp.where(kr <= qc, 0.0, NEG)
        mask_ref[1] = jnp.where(kr > qc, 0.0, NEG)

    sig_ref[...] = _sigmoid(ng_ref[...])
    side_work()
    sel_neg =(sel_ref[...] - 1.0) * (-NEG)
    row16 = lax.broadcasted_iota(jnp.int32, (16, tq), 0)
    pad = jnp.zeros((ATTN_K - AUX_SEL - n_slc, tq), F32)
    for j in range(Q_PER_KV):
        sl = jnp.full((16, tq), slopes_ref[g * Q_PER_KV + j], F32)
        hi = sl.astype(BF16).astype(F32)
        mid = (sl - hi).astype(BF16).astype(F32)
        lo = (sl - hi - mid).astype(BF16).astype(F32)
        pieces = jnp.where(row16 == 0, hi, jnp.where(row16 == 1, mid, jnp.where(row16 == 2, lo, 0.0)))
        qa = jnp.concatenate([q_ref[j].astype(F32), pieces, sel_neg, pad], axis=0)
        qa_ref[j] = qa.astype(BF16)

    def scores(ka_ref, k0, buf):
        ka = ka_ref[pl.ds(k0, tk), :]
        for j in range(Q_PER_KV):
            s_ref[buf, j] = _dot(ka, qa_ref[j])

    def softmax_pv(v_ref, k0, buf, mask_idx, first):
        vc = v_ref[:, pl.ds(k0, tk)]
        for j in range(Q_PER_KV):
            shift = slopes_ref[g * Q_PER_KV + j] * k0.astype(F32)
            mx = None
            for r in range(0, tk, STRIP):
                x = s_ref[buf, j, r:r + STRIP, :]
                if mask_idx is not None:
                    x = x + mask_ref[mask_idx, r:r + STRIP, :]
                    s_ref[buf, j, r:r + STRIP, :] = x
                mx = x if mx is None else jnp.maximum(mx, x)
            m_cur = jnp.max(mx, axis=0, keepdims=True) + shift
            if first:
                m_new = m_cur
            else:
                m_old = m_ref[j]
                m_new = jnp.maximum(m_old, m_cur)
                alpha = jnp.exp2(m_old - m_new)
            m_ref[j] = m_new
            m_loc = m_new - shift
            ls = None
            for r in range(0, tk, STRIP):
                p = jnp.exp2(s_ref[buf, j, r:r + STRIP, :] - m_loc)
                ls = p if ls is None else ls + p
                p_ref[j, r:r + STRIP, :] = p.astype(BF16)
            l_cur = jnp.sum(ls, axis=0, keepdims=True)
            pv = _dot(vc, p_ref[j])
            if first:
                l_ref[j] = l_cur
                acc_ref[j] = pv
            else:
                l_ref[j] = alpha * l_ref[j] + l_cur
                acc_ref[j] = alpha * acc_ref[j] + pv

    k_diag = pl.multiple_of(q0, tk)
    last = jnp.maximum(i - 1, 0)
    k_of = lambda c: pl.multiple_of(jnp.minimum(c, last) * tk, tk)
    scores(ka_s_ref, k_diag, 1)
    scores(ka_s_ref, k_of(0), 0)
    softmax_pv(vs_ref, k_diag, 1, 0, True)

    def slc_pair(cc, carry):
        c0 = 2 * cc
        scores(ka_s_ref, k_of(c0 + 1), 1)
        softmax_pv(vs_ref, k_of(c0), 0, None, False)

        @pl.when(c0 + 1 < i)
        def _():
            scores(ka_s_ref, k_of(c0 + 2), 0)
            softmax_pv(vs_ref, k_of(c0 + 1), 1, None, False)
        return carry

    lax.fori_loop(0, (i + 1) // 2, slc_pair, 0)
    for j in range(Q_PER_KV):
        os_ref[j] = acc_ref[j] / l_ref[j]

    n_back = WINDOW // tk
    back_k0 = lambda back: pl.multiple_of(jnp.maximum(q0 - back * tk, 0), tk)
    scores(ka_w_ref, k_diag, 0)
    scores(ka_w_ref, back_k0(1), 1)
    softmax_pv(vw_ref, k_diag, 0, 0, True)
    for back in range(1, n_back + 1):
        @pl.when(i >= back)
        def _(back=back):
            if back < n_back:
                scores(ka_w_ref, back_k0(back + 1), (back + 1) % 2)
            softmax_pv(vw_ref, back_k0(back), back % 2, 1 if back == n_back else None, False)

    for j in range(Q_PER_KV):
        base = (g * Q_PER_KV + j) * 3
        out_t = (sig_ref[pl.ds(base, 1), :] * oc_ref[j]
                 + sig_ref[pl.ds(base + 1, 1), :] * os_ref[j]
                 + sig_ref[pl.ds(base + 2, 1), :] * (acc_ref[j] / l_ref[j]))
        o_ref[:, j * HEAD_DIM:(j + 1) * HEAD_DIM] = out_t.T.astype(o_ref.dtype)


def _key_aux(seq, tk, with_blocks):
    aux = np.zeros((seq, ATTN_K - HEAD_DIM), np.float32)
    pos = np.arange(seq)
    aux[:, AUX_SLOPE - HEAD_DIM:AUX_SLOPE - HEAD_DIM + 3] = (pos % tk)[:, None]
    if with_blocks:
        aux[pos, AUX_SEL - HEAD_DIM + pos // SLC_BLOCK] = 1.0
    return aux


def selected_window_attention(qt, kvr, vt, sel, oct, ngt, batch, round_src=None, tq=256):
    m = qt.shape[2]
    seq = m // batch
    nq = seq // tq
    n_slc = seq // SLC_BLOCK
    assert tq <= 256 and AUX_SEL + n_slc <= ATTN_K
    aux_s = jnp.asarray(_key_aux(seq, tq, True), BF16)
    aux_w = jnp.asarray(_key_aux(seq, tq, False), BF16)
    slopes = jnp.asarray(_alibi_slopes())
    n_g = N_KV_HEADS

    def k_spec(section):
        return pl.BlockSpec((None, seq, HEAD_DIM), lambda b, g, i: (section * n_g + g, b, 0))

    def v_spec(section):
        return pl.BlockSpec((None, HEAD_DIM, seq), lambda b, g, i: (section * n_g + g, 0, b))

    head_blk = pl.BlockSpec((Q_PER_KV, HEAD_DIM, tq), lambda b, g, i: (g, 0, b * nq + i))
    aux_blk = pl.BlockSpec((seq, ATTN_K - HEAD_DIM), lambda b, g, i: (0, 0))
    in_specs = [pl.BlockSpec(memory_space=pltpu.SMEM),
                head_blk,
                k_spec(2), v_spec(0), k_spec(3), v_spec(1),
                aux_blk, aux_blk,
                pl.BlockSpec((None, None, n_slc, tq), lambda b, g, i: (b, g, 0, i)),
                head_blk,
                pl.BlockSpec((LANES, tq), lambda b, g, i: (0, b * nq + i))]
    out_specs = [pl.BlockSpec((tq, GROUP_WIDTH), lambda b, g, i: (b * nq + i, g))]
    out_shape = [jax.ShapeDtypeStruct((m, NSA_WIDTH), BF16)]
    operands = [slopes, qt, kvr, vt, kvr, vt, aux_s, aux_w, sel, oct, ngt]
    if round_src is not None:
        steps = batch * N_KV_HEADS * nq
        rows, cols = round_src.shape
        assert rows % steps == 0 and (rows // steps) % BF16_SUBLANES == 0
        side_blk = pl.BlockSpec((rows // steps, cols), lambda b, g, i: ((b * N_KV_HEADS + g) * nq + i, 0))
        in_specs.append(side_blk)
        out_specs.append(side_blk)
        out_shape.append(jax.ShapeDtypeStruct((rows, cols), BF16))
        operands.append(round_src)
    res = pl.pallas_call(
        functools.partial(_attn_kernel, round_weights=round_src is not None),
        grid=(batch, N_KV_HEADS, nq),
        in_specs=in_specs,
        out_specs=out_specs,
        out_shape=out_shape,
        scratch_shapes=[pltpu.VMEM((seq, ATTN_K), BF16),
                        pltpu.VMEM((seq, ATTN_K), BF16),
                        pltpu.VMEM((Q_PER_KV, ATTN_K, tq), BF16),
                        pltpu.VMEM((2, Q_PER_KV, tq, tq), F32),
                        pltpu.VMEM((Q_PER_KV, tq, tq), BF16),
                        pltpu.VMEM((2, tq, tq), F32),
                        pltpu.VMEM((Q_PER_KV, 1, tq), F32),
                        pltpu.VMEM((Q_PER_KV, 1, tq), F32),
                        pltpu.VMEM((Q_PER_KV, HEAD_DIM, tq), F32),
                        pltpu.VMEM((Q_PER_KV, HEAD_DIM, tq), F32),
                        pltpu.VMEM((LANES, tq), F32)],
        compiler_params=_params("arbitrary", "arbitrary", "arbitrary"),
        name="nsa_selected_window",
    )(*operands)
    return (res[0], res[1]) if round_src is not None else (res[0], None)


def _sgu_kernel(u_ref, v_ref, lg_ref, lb_ref, ws_ref, bs_ref, o_ref):
    rows = u_ref.shape[0]
    v = _gelu(v_ref[...])
    mu = jnp.mean(v, axis=-1, keepdims=True)
    var = jnp.mean(jnp.square(v - mu), axis=-1, keepdims=True)
    vl = ((v - mu) * lax.rsqrt(var + NORM_EPS) * lg_ref[...] + lb_ref[...]).astype(BF16)
    r = lax.broadcasted_iota(jnp.int32, (SGU_CHUNK, SGU_CHUNK), 0)
    c = lax.broadcasted_iota(jnp.int32, (SGU_CHUNK, SGU_CHUNK), 1)
    gd = SGU_WIDTH // SGU_GROUPS
    for grp in range(SGU_GROUPS):
        w = jnp.where(c <= r, ws_ref[grp], 0.0).astype(BF16)
        lanes = slice(grp * gd, (grp + 1) * gd)
        for n in range(rows // SGU_CHUNK):
            rs = slice(n * SGU_CHUNK, (n + 1) * SGU_CHUNK)
            vm = _dot(w, vl[rs, lanes]) + bs_ref[:, lanes]
            o_ref[rs, lanes] = (_gelu(u_ref[rs, lanes]) * vm).astype(o_ref.dtype)


def spatial_gating(rest, ln_g, ln_b, w_s, b_s, tm=1024):
    m = rest.shape[0]
    gd = SGU_WIDTH // SGU_GROUPS
    bias = jnp.repeat(b_s.T, gd, axis=1)
    return pl.pallas_call(
        _sgu_kernel,
        grid=(m // tm,),
        in_specs=[pl.BlockSpec((tm, SGU_WIDTH), lambda i: (i, 0)),
                  pl.BlockSpec((tm, SGU_WIDTH), lambda i: (i, 1)),
                  pl.BlockSpec((1, SGU_WIDTH), lambda i: (0, 0)),
                  pl.BlockSpec((1, SGU_WIDTH), lambda i: (0, 0)),
                  pl.BlockSpec((SGU_GROUPS, SGU_CHUNK, SGU_CHUNK), lambda i: (0, 0, 0)),
                  pl.BlockSpec((SGU_CHUNK, SGU_WIDTH), lambda i: (0, 0))],
        out_specs=pl.BlockSpec((tm, SGU_WIDTH), lambda i: (i, 0)),
        out_shape=jax.ShapeDtypeStruct((m, SGU_WIDTH), BF16),
        compiler_params=_params("parallel"),
        name="spatial_gating",
    )(rest, rest, ln_g.reshape(1, -1), ln_b.reshape(1, -1), w_s, bias)


def merge_branches(oa, ob, rest, w_a, w_b, layer, tm=512, tn=1024):
    m = oa.shape[0]
    d = w_a.shape[2]
    ga_blk = 2 * SGU_WIDTH // tn
    gb_blk = (2 * SGU_WIDTH + d) // tn

    def epi(accs, e_refs, o_refs):
        y = _sigmoid(e_refs[0][...]) * accs[0] + _sigmoid(e_refs[1][...]) * accs[1]
        o_refs[0][...] = y.astype(BF16)

    (y,) = mm_full(
        "merge_branches", (d // tn, m // tm),
        [(oa, (tm, oa.shape[1]), lambda j, i: (i, 0)), (ob, (tm, ob.shape[1]), lambda j, i: (i, 0))],
        [(w_a, (None, w_a.shape[1], tn), lambda j, i: (layer, 0, j)),
         (w_b, (None, w_b.shape[1], tn), lambda j, i: (layer, 0, j))],
        [(rest, (tm, tn), lambda j, i: (i, ga_blk + j)), (rest, (tm, tn), lambda j, i: (i, gb_blk + j))],
        [(jax.ShapeDtypeStruct((m, d), BF16), (tm, tn), lambda j, i: (i, j))],
        [(0, 0), (1, 1)], epi)
    return y


def residual_matmul(name, a, w, layer, x, tm=512, tn=1024):
    m, k = a.shape
    n = w.shape[2]

    def epi(accs, e_refs, o_refs):
        o_refs[0][...] = e_refs[0][...] + accs[0]

    (out,) = mm_full(
        name, (n // tn, m // tm),
        [(a, (tm, k), lambda j, i: (i, 0))],
        [(w, (None, k, tn), lambda j, i: (layer, 0, j))],
        [(x, (tm, tn), lambda j, i: (i, j))],
        [(jax.ShapeDtypeStruct((m, n), F32), (tm, tn), lambda j, i: (i, j))],
        [(0, 0)], epi)
    return out


def swiglu_up(h, w1, w3, tm=2048, tn=512):
    m, d = h.shape
    f = w1.shape[1]

    def epi(accs, e_refs, o_refs):
        o_refs[0][...] = (accs[0] * _sigmoid(accs[0]) * accs[1]).astype(BF16)

    (out,) = mm_full(
        "swiglu_up", (f // tn, m // tm),
        [(h, (tm, d), lambda j, i: (i, 0))],
        [(w1, (d, tn), lambda j, i: (0, j)), (w3, (d, tn), lambda j, i: (0, j))],
        [],
        [(jax.ShapeDtypeStruct((m, f), BF16), (tm, tn), lambda j, i: (i, j))],
        [(0, 0), (0, 1)], epi)
    return out


MOE_TILE = 512
ROW_QUARTERS = 4


def _row_options(tile):
    step = tile // ROW_QUARTERS
    return tuple(range(step, tile + 1, step))


def _route_kernel(sel_ref, w_ref, tri_ref, pos_ref, wt_ref, meta_ref, cum_ref, *, tile):
    n_e, m = sel_ref.shape
    ck = tri_ref.shape[0]
    carry = jnp.zeros((n_e, 1), F32)
    for c in range(m // ck):
        sl = slice(c * ck, (c + 1) * ck)
        cs = _dot(sel_ref[:, sl].astype(BF16), tri_ref[...]) + carry
        cum_ref[:, sl] = cs
        carry = cs[:, ck - 1:ck]
    padded = jnp.ceil(carry / tile) * tile
    sub = lax.broadcasted_iota(jnp.int32, (n_e, 1), 0)
    start = jnp.zeros((n_e, 1), F32)
    run = jnp.zeros((1, 1), F32)
    for ex in range(n_e):
        start = jnp.where(sub == ex, run, start)
        run = run + padded[ex:ex + 1, :]
    sel = sel_ref[...] > 0.5
    pos = start + cum_ref[...] - 1.0
    eidx = lax.broadcasted_iota(jnp.int32, (n_e, m), 0)
    e_lo = jnp.min(jnp.where(sel, eidx, n_e), axis=0, keepdims=True)
    e_hi = jnp.max(jnp.where(sel, eidx, -1), axis=0, keepdims=True)
    is_lo = eidx == e_lo
    is_hi = eidx == e_hi
    w = w_ref[...]
    pos_a = jnp.sum(jnp.where(is_lo, pos, 0.0), axis=0, keepdims=True)
    pos_b = jnp.sum(jnp.where(is_hi, pos, 0.0), axis=0, keepdims=True)
    w_a = jnp.sum(jnp.where(is_lo, w, 0.0), axis=0, keepdims=True)
    w_b = jnp.sum(jnp.where(is_hi, w, 0.0), axis=0, keepdims=True)
    pos_ref[...] = jnp.where(eidx == 0, pos_a, jnp.where(eidx == 1, pos_b, 0.0)).astype(jnp.int32)
    wt_ref[...] = jnp.where(eidx == 0, w_a, jnp.where(eidx == 1, w_b, 0.0)).T
    tile_lo = lax.broadcasted_iota(jnp.int32, (n_e, LANES), 1).astype(F32) * tile
    t_exp = jnp.sum(jnp.where(start + padded <= tile_lo, 1.0, 0.0), axis=0, keepdims=True)
    t_exp = jnp.minimum(t_exp, n_e - 1.0)
    r8 = lax.broadcasted_iota(jnp.int32, (n_e, LANES), 0)
    last_row = jnp.sum(jnp.where(r8.astype(F32) == t_exp, start + carry, 0.0), axis=0, keepdims=True)
    filled = jnp.clip(last_row - tile_lo[0:1], 0.0, float(tile))
    meta_ref[...] = jnp.where(r8 == 0, t_exp, jnp.where(r8 == 1, run / tile,
                                                       jnp.where(r8 == 2, filled, 0.0))).astype(jnp.int32)


def moe_route(sel, w, tile):
    n_e, m = sel.shape
    ck = 256
    tri = jnp.asarray(np.triu(np.ones((ck, ck), np.float32)), BF16)
    full = lambda shape: pl.BlockSpec(shape, lambda: tuple(0 for _ in shape))
    return pl.pallas_call(
        functools.partial(_route_kernel, tile=tile),
        in_specs=[full((n_e, m)), full((n_e, m)), full((ck, ck))],
        out_specs=[full((n_e, m)), full((m, n_e)), full((n_e, LANES))],
        out_shape=[jax.ShapeDtypeStruct((n_e, m), jnp.int32),
                   jax.ShapeDtypeStruct((m, n_e), F32),
                   jax.ShapeDtypeStruct((n_e, LANES), jnp.int32)],
        scratch_shapes=[pltpu.VMEM((n_e, m), F32)],
        compiler_params=pltpu.CompilerParams(vmem_limit_bytes=VMEM_LIMIT),
        name="moe_route",
    )(sel, w, tri)


def _row_gather_start(src_hbm, dst, rows_ref, base, n_rows, sem, unroll=8):
    def body(r, carry):
        src_row = rows_ref[base + r]
        pltpu.make_async_copy(src_hbm.at[pl.ds(src_row, 1)], dst.at[pl.ds(r, 1)], sem).start()
        return carry
    lax.fori_loop(0, n_rows, body, 0, unroll=unroll)


def _row_gather_wait(src_hbm, dst, sem):
    pltpu.make_async_copy(src_hbm.at[pl.ds(0, dst.shape[0])], dst, sem).wait()


def _moe_gather_kernel(pos_ref, meta_ref, h_hbm, zeros_hbm, o_ref, tok_ref, buf_ref, sem_ref):
    i = pl.program_id(0)
    n = meta_ref[1, 0]
    tile = o_ref.shape[0]
    m = pos_ref.shape[0] // 2

    @pl.when(i == 0)
    def _():
        clear = pltpu.make_async_copy(zeros_hbm, tok_ref, sem_ref.at[2])
        clear.start()
        clear.wait()

        def fill(t, carry):
            tok_ref[pos_ref[t]] = t
            tok_ref[pos_ref[m + t]] = t
            return carry
        lax.fori_loop(0, m, fill, 0, unroll=8)
        _row_gather_start(h_hbm, buf_ref.at[0], tok_ref, 0, tile, sem_ref.at[0])

    @pl.when(i + 1 < n)
    def _():
        nxt = (i + 1) % 2
        _row_gather_start(h_hbm, buf_ref.at[nxt], tok_ref, (i + 1) * tile, tile, sem_ref.at[nxt])

    @pl.when(i < n)
    def _():
        cur = i % 2
        _row_gather_wait(h_hbm, buf_ref.at[cur], sem_ref.at[cur])
        o_ref[...] = buf_ref[cur].astype(o_ref.dtype)

    @pl.when(i >= n)
    def _():
        o_ref[...] = jnp.zeros_like(o_ref)


def moe_gather(pos2, meta, h, n_tiles, tile):
    m, d = h.shape
    rows = n_tiles * tile
    return pl.pallas_call(
        _moe_gather_kernel,
        grid_spec=pltpu.PrefetchScalarGridSpec(
            num_scalar_prefetch=2,
            grid=(n_tiles,),
            in_specs=[pl.BlockSpec(memory_space=pl.ANY), pl.BlockSpec(memory_space=pl.ANY)],
            out_specs=pl.BlockSpec((tile, d), lambda i, pos, meta: (i, 0)),
            scratch_shapes=[pltpu.SMEM((rows,), jnp.int32),
                            pltpu.VMEM((2, tile, d), F32),
                            pltpu.SemaphoreType.DMA((3,))]),
        out_shape=jax.ShapeDtypeStruct((rows, d), BF16),
        compiler_params=_params("arbitrary"),
        name="moe_gather",
    )(pos2, meta, h, jnp.zeros((rows,), jnp.int32))


def _moe_up_kernel(meta_ref, x_ref, w1_ref, w3_ref, w2_ref, o_ref, w2b_ref):
    i = pl.program_id(1)

    def round_w2_block():
        w2b_ref[...] = w2_ref[...].astype(BF16)

    filled = meta_ref[2, i]
    tile = o_ref.shape[0]
    for rows in _row_options(tile):
        @pl.when((filled > rows - tile // ROW_QUARTERS) & (filled <= rows))
        def _(rows=rows):
            x = x_ref[0:rows, :]
            for c in range(0, o_ref.shape[1], COL_CHUNK):
                sl = slice(c, c + COL_CHUNK)
                a1 = _dot(x, w1_ref[:, sl])
                a3 = _dot(x, w3_ref[:, sl])
                o_ref[0:rows, sl] = (a1 * _sigmoid(a1) * a3).astype(o_ref.dtype)
            if rows < tile:
                o_ref[rows:tile, :] = jnp.zeros((tile - rows, o_ref.shape[1]), o_ref.dtype)
            round_w2_block()

    @pl.when(filled == 0)
    def _():
        o_ref[...] = jnp.zeros_like(o_ref)
        round_w2_block()


BF16_SUBLANES = 16


def _cast_rows_per_step(total_rows, steps):
    for rows in range(BF16_SUBLANES, total_rows + 1, BF16_SUBLANES):
        if total_rows % rows == 0 and total_rows // rows <= steps:
            return rows
    raise ValueError("weights cannot be split over the grid steps")


def moe_up(meta, xs, w1, w3, w2, tile, tn=1792):
    rows, d = xs.shape
    n_e, _, f = w1.shape
    assert f % tn == 0 and rows % tile == 0 and w1.dtype == BF16 and w3.dtype == BF16
    n_tiles = rows // tile
    w2_flat = w2.reshape(n_e * f, w2.shape[2])
    cast_rows = _cast_rows_per_step(w2_flat.shape[0], (f // tn) * n_tiles)
    n_cast = w2_flat.shape[0] // cast_rows
    last = lambda i, meta: jnp.maximum(jnp.minimum(i, meta[1, 0] - 1), 0)
    cast_blk = lambda j, i, meta: (jnp.minimum(j * n_tiles + i, n_cast - 1), 0)
    up, w2b = pl.pallas_call(
        _moe_up_kernel,
        grid_spec=pltpu.PrefetchScalarGridSpec(
            num_scalar_prefetch=1,
            grid=(f // tn, n_tiles),
            in_specs=[pl.BlockSpec((tile, d), lambda j, i, meta: (last(i, meta), 0)),
                      pl.BlockSpec((None, d, tn), lambda j, i, meta: (meta[0, i], 0, j)),
                      pl.BlockSpec((None, d, tn), lambda j, i, meta: (meta[0, i], 0, j)),
                      pl.BlockSpec((cast_rows, w2_flat.shape[1]), cast_blk)],
            out_specs=[pl.BlockSpec((tile, tn), lambda j, i, meta: (i, j)),
                       pl.BlockSpec((cast_rows, w2_flat.shape[1]), cast_blk)]),
        out_shape=[jax.ShapeDtypeStruct((rows, f), BF16),
                   jax.ShapeDtypeStruct(w2_flat.shape, BF16)],
        compiler_params=_params("arbitrary", "arbitrary"),
        name="moe_up",
    )(meta, xs, w1, w3, w2_flat)
    return up, w2b.reshape(w2.shape)


def _moe_down_kernel(meta_ref, a_ref, w_ref, o_ref):
    i = pl.program_id(0)
    k = pl.program_id(1)

    filled = meta_ref[2, i]
    tile = o_ref.shape[0]
    cn = COL_CHUNK
    for rows in _row_options(tile):
        in_range = (filled > rows - tile // ROW_QUARTERS) & (filled <= rows)

        @pl.when(in_range & (k == 0))
        def _(rows=rows):
            a = a_ref[0:rows, :]
            for c in range(0, o_ref.shape[1], cn):
                o_ref[0:rows, c:c + cn] = _dot(a, w_ref[:, c:c + cn])
            if rows < tile:
                o_ref[rows:tile, :] = jnp.zeros((tile - rows, o_ref.shape[1]), o_ref.dtype)

        @pl.when(in_range & (k > 0))
        def _(rows=rows):
            a = a_ref[0:rows, :]
            for c in range(0, o_ref.shape[1], cn):
                o_ref[0:rows, c:c + cn] += _dot(a, w_ref[:, c:c + cn])

    @pl.when((filled == 0) & (k == 0))
    def _():
        o_ref[...] = jnp.zeros_like(o_ref)


def moe_down(meta, up, w2, tile, tk=3584):
    rows, f = up.shape
    d = w2.shape[2]
    assert f % tk == 0 and rows % tile == 0
    n_tiles = rows // tile
    last = lambda i, meta: jnp.maximum(jnp.minimum(i, meta[1, 0] - 1), 0)
    return pl.pallas_call(
        _moe_down_kernel,
        grid_spec=pltpu.PrefetchScalarGridSpec(
            num_scalar_prefetch=1,
            grid=(n_tiles, f // tk),
            in_specs=[pl.BlockSpec((tile, tk), lambda i, k, meta: (last(i, meta), k)),
                      pl.BlockSpec((None, tk, d), lambda i, k, meta: (meta[0, i], k, 0))],
            out_specs=pl.BlockSpec((tile, d), lambda i, k, meta: (i, 0))),
        out_shape=jax.ShapeDtypeStruct((rows, d), F32),
        compiler_params=_params("arbitrary", "arbitrary"),
        name="moe_down",
    )(meta, up, w2)


def _moe_combine_kernel(pos_ref, x_ref, wt_ref, g_ref, y_hbm, o_ref, buf_ref, sem_ref, *, final_norm):
    i = pl.program_id(0)
    n = pl.num_programs(0)
    tc = x_ref.shape[0]
    m = pos_ref.shape[0] // 2

    def start(step, slot):
        for s in range(2):
            _row_gather_start(y_hbm, buf_ref.at[slot, s], pos_ref, s * m + step * tc, tc, sem_ref.at[slot, s])

    @pl.when(i == 0)
    def _():
        start(0, 0)

    @pl.when(i + 1 < n)
    def _():
        start(i + 1, (i + 1) % 2)

    cur = i % 2
    for s in range(2):
        _row_gather_wait(y_hbm, buf_ref.at[cur, s], sem_ref.at[cur, s])
    wt = wt_ref[...]
    out = x_ref[...] + wt[:, 0:1] * buf_ref[cur, 0] + wt[:, 1:2] * buf_ref[cur, 1]
    o_ref[...] = _rms(out, g_ref[...]) if final_norm else out


def moe_combine(pos2, x, wt, y, norm_g=None, tc=512):
    m, d = x.shape
    final_norm = norm_g is not None
    g = (norm_g if final_norm else jnp.ones((d,), F32)).reshape(1, d)
    return pl.pallas_call(
        functools.partial(_moe_combine_kernel, final_norm=final_norm),
        grid_spec=pltpu.PrefetchScalarGridSpec(
            num_scalar_prefetch=1,
            grid=(m // tc,),
            in_specs=[pl.BlockSpec((tc, d), lambda i, pos: (i, 0)),
                      pl.BlockSpec((tc, wt.shape[1]), lambda i, pos: (i, 0)),
                      pl.BlockSpec((1, d), lambda i, pos: (0, 0)),
                      pl.BlockSpec(memory_space=pl.ANY)],
            out_specs=pl.BlockSpec((tc, d), lambda i, pos: (i, 0)),
            scratch_shapes=[pltpu.VMEM((2, 2, tc, d), F32),
                            pltpu.SemaphoreType.DMA((2, 2))]),
        out_shape=jax.ShapeDtypeStruct((m, d), F32),
        compiler_params=_params("arbitrary"),
        name="moe_combine",
    )(pos2, x, wt, g, y)


def moe_layer(x, norm_g, router_w, router_b, w1, w3, w2, out_norm_g=None):
    m, d = x.shape
    n_e = router_w.shape[1]
    tile = MOE_TILE
    n_tiles = 2 * m // tile + n_e
    h, sel, w = norm_router(x, norm_g, router_w, router_b)
    pos, wt, meta = moe_route(sel, w, tile)
    pos2 = pos[:2].reshape(-1)
    xs = moe_gather(pos2, meta, h, n_tiles, tile)
    up, w2_bf16 = moe_up(meta, xs, w1, w3, w2, tile)
    y = moe_down(meta, up, w2_bf16, tile)
    return moe_combine(pos2, x, wt, y, out_norm_g)


def _mixer(x, batch, layer, norm_g, w_in_t, pe_k, pe_v, kw1, kw2, vw1, vw2, ln_g, ln_b, w_s, b_s, w_a, w_b, w_out,
           round_src=None):
    h = rmsnorm(x, norm_g, BF16)
    qt, kvr, vt, ngt, rest = in_projection(h, w_in_t, layer)
    cmp_kv = compress(kvr, batch, pe_k, pe_v, kw1, kw2, vw1, vw2)
    oct, sel = compressed_and_select(qt, cmp_kv, batch)
    oa, rounded = selected_window_attention(qt, kvr, vt, sel, oct, ngt, batch, round_src)
    ob = spatial_gating(rest, ln_g, ln_b, w_s, b_s)
    y = merge_branches(oa, ob, rest, w_a, w_b, layer)
    return residual_matmul("out_proj", y, w_out, layer, x), rounded


def kernel(x, norm_mix, w_in, cmp_pe_k, cmp_pe_v, cmp_k_w1, cmp_k_w2, cmp_v_w1, cmp_v_w2, sgu_ln_g, sgu_ln_b, sgu_w, sgu_b, w_branch_a, w_branch_b, w_out, norm_ffn, ffn_w1, ffn_w3, ffn_w2, router_w, router_b, moe_w1, moe_w3, moe_w2, norm_f):
    batch, seq, d = x.shape
    depth = norm_mix.shape[0]
    xf = x.reshape(batch * seq, d)
    w_in_t = jnp.swapaxes(w_in, 1, 2)
    normed = False
    n_moe = moe_w1.shape[0]
    moe_up_bf16 = {}
    for layer in range(depth):
        j = layer // 2
        round_src = None
        if j < n_moe and 2 * j + 1 < depth:
            w_f32 = moe_w1[j] if layer % 2 == 0 else moe_w3[j]
            round_src = w_f32.reshape(-1, w_f32.shape[-1])
        xf, rounded = _mixer(xf, batch, layer, norm_mix[layer], w_in_t, cmp_pe_k[layer], cmp_pe_v[layer],
                             cmp_k_w1[layer], cmp_k_w2[layer], cmp_v_w1[layer], cmp_v_w2[layer],
                             sgu_ln_g[layer], sgu_ln_b[layer], sgu_w[layer], sgu_b[layer],
                             w_branch_a, w_branch_b, w_out, round_src)
        if rounded is not None:
            moe_up_bf16[layer % 2] = rounded.reshape(moe_w1[j].shape)
        if layer % 2 == 0:
            h = rmsnorm(xf, norm_ffn[layer], BF16)
            up = swiglu_up(h, ffn_w1[j], ffn_w3[j])
            xf = residual_matmul("ffn_down", up, ffn_w2, j, xf, tm=512, tn=512)
        else:
            normed = layer == depth - 1
            xf = moe_layer(xf, norm_ffn[layer], router_w[j], router_b[j], moe_up_bf16[0], moe_up_bf16[1],
                           moe_w2[j], norm_f if normed else None)
    if not normed:
        xf = rmsnorm(xf, norm_f, F32)
    return xf.reshape(batch, seq, d)
```
